```python
import math, functools
import jax, jax.numpy as jnp
from jax import lax
import numpy as np

D_MODEL = 1024
BATCH = 8
SEQ = 2048
DEPTH = 4
DEC_BATCH = 32
DEC_SEQ = 1
PAST_LEN = 8192
PAGE_SIZE = 128

N_A_LAYERS = DEPTH // 2
N_B_LAYERS = DEPTH - N_A_LAYERS
HEAD_DIM = 64
D_MEM = D_MODEL // 4
N_MEM_HEADS = D_MEM // HEAD_DIM
N_MEM_TOK = 256
D_RNN = D_MODEL - D_MEM
N_LRU_BLOCKS = D_RNN // HEAD_DIM
LRU_BLOCK = D_RNN // N_LRU_BLOCKS
CONV_W = 4
LRU_C = 8.0
N_HEADS = D_RNN // HEAD_DIM
N_KV_GROUPS = 4
HEADS_PER_GROUP = N_HEADS // N_KV_GROUPS
CMP_LEN = 32
CMP_STRIDE = 16
CMP_HID = 2 * HEAD_DIM
SEL_LEN = 64
N_SEL = 16
WINDOW = 512
N_BUCKETS = 32
MAX_DISTANCE = 1024
N_GROUPS = 4
EXPERTS_PER_GROUP = 8
N_EXPERTS = N_GROUPS * EXPERTS_PER_GROUP
D_EXPERT = 128
TOP_K_IN_GROUP = 2
ALPHA = (2 * DEPTH) ** 0.25
BETA = (8 * DEPTH) ** -0.25
Q_BLOCK = 128
LN_EPS = 1e-5
NEG = -1e30
FORCE = 1e9
SCALE = HEAD_DIM ** -0.5
D_IN_A = 2 * D_RNN + D_MEM
D_IN_B = N_HEADS * HEAD_DIM + 3 * N_HEADS + D_MEM
D_KV_SHARED = 3 * 2 * N_KV_GROUPS * HEAD_DIM

kernel_name = 'yoco_hawk_nsa_hmoe_decoder_step'


def layer_norm(x, g, b):
    xf = x.astype(jnp.float32)
    mu = jnp.mean(xf, axis=-1, keepdims=True)
    var = jnp.mean(jnp.square(xf - mu), axis=-1, keepdims=True)
    y = (xf - mu) * lax.rsqrt(var + LN_EPS) * g.astype(jnp.float32) + b.astype(jnp.float32)
    return y.astype(x.dtype)


def post_norm(x, sub, g, b):
    return layer_norm(ALPHA * x + sub.astype(x.dtype), g, b)


def masked_softmax(logits, mask):
    p = jax.nn.softmax(jnp.where(mask, logits, NEG), axis=-1)
    return jnp.where(mask, p, 0.0)


def rel_bucket(dist):
    n = jnp.maximum(dist, 0)
    max_exact = N_BUCKETS // 2
    nf = jnp.maximum(n, 1).astype(jnp.float32)
    large = max_exact + (jnp.log(nf / max_exact) / math.log(MAX_DISTANCE / max_exact)
                         * (N_BUCKETS - max_exact)).astype(jnp.int32)
    large = jnp.minimum(large, N_BUCKETS - 1)
    return jnp.where(n < max_exact, n, large)


def bias_qk(tab, dist):
    q_len, k_len = dist.shape
    b = tab[rel_bucket(dist)].reshape(q_len, k_len, N_KV_GROUPS, HEADS_PER_GROUP)
    return b.transpose(0, 2, 3, 1).astype(jnp.float32)


def mem_attn(qm, mk, mv):
    bsz, t_len, _ = qm.shape
    q = qm.reshape(bsz, t_len, N_MEM_HEADS, HEAD_DIM)
    logits = jnp.einsum('bthd,bmhd->bhtm', q, mk).astype(jnp.float32) * SCALE
    p = jax.nn.softmax(logits, axis=-1).astype(mv.dtype)
    return jnp.einsum('bhtm,bmhd->bthd', p, mv).reshape(bsz, t_len, D_MEM)


def rglru(xr, cw, cb, wrg, brg, wig, big, lam, conv_buf, h0):
    bsz, t_len, _ = xr.shape
    xp = jnp.concatenate([conv_buf.astype(xr.dtype), xr], axis=1)
    xc = cb + sum(xp[:, k:k + t_len] * cw[k] for k in range(CONV_W))
    xb = xc.reshape(bsz, t_len, N_LRU_BLOCKS, LRU_BLOCK)
    r = jax.nn.sigmoid((jnp.einsum('btni,nij->btnj', xb, wrg).reshape(bsz, t_len, D_RNN) + brg).astype(jnp.float32))
    i = jax.nn.sigmoid((jnp.einsum('btni,nij->btnj', xb, wig).reshape(bsz, t_len, D_RNN) + big).astype(jnp.float32))
    log_a = -LRU_C * r * jax.nn.softplus(-lam.astype(jnp.float32))
    a = jnp.exp(log_a)
    u = jnp.sqrt(-jnp.expm1(2.0 * log_a)) * i * xc.astype(jnp.float32)

    def step(h, au):
        a_t, u_t = au
        h = a_t * h + u_t
        return h, h

    h_t, hs = lax.scan(step, h0.astype(jnp.float32), (jnp.swapaxes(a, 0, 1), jnp.swapaxes(u, 0, 1)))
    return jnp.swapaxes(hs, 0, 1).astype(xr.dtype), h_t, xp[:, t_len:]


def mixer_a(x, w_in, cw, cb, wrg, brg, wig, big, lam, conv_buf, h0, mk, mv, wo):
    proj = x @ w_in
    gate = jax.nn.gelu(proj[..., :D_RNN])
    y, h_t, buf = rglru(proj[..., D_RNN:2 * D_RNN], cw, cb, wrg, brg, wig, big, lam, conv_buf, h0)
    mix = jnp.concatenate([gate * y, mem_attn(proj[..., 2 * D_RNN:], mk, mv)], axis=-1)
    return mix @ wo, h_t, buf


def shared_kv(h, w_kv):
    bsz, t_len, _ = h.shape
    kv = (h @ w_kv).reshape(bsz, t_len, 3, 2, N_KV_GROUPS, HEAD_DIM)
    return kv[:, :, 0], kv[:, :, 1], kv[:, :, 2]


def compress(rows, pe, w1, b1, w2):
    bsz, t_len = rows.shape[:2]
    n_chunk = t_len // CMP_STRIDE
    r = CMP_LEN // CMP_STRIDE
    nc = n_chunk - r + 1
    chunks = rows[:, :n_chunk * CMP_STRIDE].reshape(bsz, n_chunk, CMP_STRIDE, N_KV_GROUPS, HEAD_DIM)
    blocks = jnp.concatenate([chunks[:, m:m + nc] for m in range(r)], axis=2)
    blocks = blocks + pe[None, None, :, None, :]
    flat = blocks.transpose(0, 1, 3, 2, 4).reshape(bsz, nc, N_KV_GROUPS, CMP_LEN * HEAD_DIM)
    return jax.nn.gelu(flat @ w1 + b1) @ w2


def to_sel_blocks(rows):
    bsz, t_len = rows.shape[:2]
    ns = -(-t_len // SEL_LEN)
    rows = jnp.pad(rows, ((0, 0), (0, ns * SEL_LEN - t_len), (0, 0), (0, 0)))
    return rows.reshape(bsz, ns, SEL_LEN, N_KV_GROUPS, HEAD_DIM).transpose(0, 3, 1, 2, 4)


def cmp_branch(q, q_pos, comp_k, comp_v, tab):
    nc = comp_k.shape[1]
    c_end = jnp.arange(nc) * CMP_STRIDE + (CMP_LEN - 1)
    dist = q_pos[:, None] - c_end[None, :]
    logits = jnp.einsum('bqghd,bcgd->bqghc', q, comp_k).astype(jnp.float32) * SCALE + bias_qk(tab, dist)[None]
    p = masked_softmax(logits, (dist >= 0)[None, :, None, None, :])
    return jnp.einsum('bqghc,bcgd->bqghd', p.astype(comp_v.dtype), comp_v), p


def sel_branch(q, q_pos, p_cmp, ks_g, vs_g, tab):
    bsz, q_len = q.shape[:2]
    nc = p_cmp.shape[-1]
    ns = ks_g.shape[2]
    c_start = jnp.arange(nc) * CMP_STRIDE
    s_start = jnp.arange(ns) * SEL_LEN
    cover = ((c_start[:, None] < s_start[None, :] + SEL_LEN)
             & (c_start[:, None] + CMP_LEN > s_start[None, :])).astype(jnp.float32)
    imp = jnp.einsum('bqgc,cs->bqgs', p_cmp.sum(3), cover)
    cur = q_pos // SEL_LEN
    blk = jnp.arange(ns)
    forced = (blk[None] == 0) | (blk[None] == cur[:, None]) | (blk[None] == cur[:, None] - 1)
    valid = s_start[None] <= q_pos[:, None]
    score = jnp.where(forced[None, :, None, :], FORCE, imp)
    score = jnp.where(valid[None, :, None, :], score, NEG)
    k_sel = min(N_SEL, ns)
    top_val, idx = lax.top_k(score, k_sel)
    idx = idx.transpose(0, 2, 1, 3)
    ok = (top_val > 0.5 * NEG).transpose(0, 2, 1, 3)
    gather = jax.vmap(jax.vmap(lambda blocks, ix: blocks[ix]))
    flat = idx.reshape(bsz, N_KV_GROUPS, q_len * k_sel)
    k_g = gather(ks_g, flat).reshape(bsz, N_KV_GROUPS, q_len, k_sel, SEL_LEN, HEAD_DIM)
    v_g = gather(vs_g, flat).reshape(bsz, N_KV_GROUPS, q_len, k_sel, SEL_LEN, HEAD_DIM)
    pos = idx[..., None] * SEL_LEN + jnp.arange(SEL_LEN)
    dist = q_pos[None, None, :, None, None] - pos
    mask = (dist >= 0) & ok[..., None]
    tab_g = tab.reshape(N_BUCKETS, N_KV_GROUPS, HEADS_PER_GROUP).transpose(1, 0, 2)
    bias = tab_g[jnp.arange(N_KV_GROUPS)[None, :, None, None, None], rel_bucket(dist)]
    logits = (jnp.einsum('bqghd,bgqksd->bgqhks', q, k_g).astype(jnp.float32) * SCALE
              + bias.transpose(0, 1, 2, 5, 3, 4).astype(jnp.float32))
    shp = logits.shape
    p = masked_softmax(logits.reshape(shp[0], shp[1], shp[2], shp[3], -1),
                       mask.reshape(bsz, N_KV_GROUPS, q_len, 1, -1)).reshape(shp)
    return jnp.einsum('bgqhks,bgqksd->bqghd', p.astype(v_g.dtype), v_g)


def win_branch(q, q_pos, kw, vw, kw_pos, tab):
    dist = q_pos[:, None] - kw_pos[None, :]
    mask = (dist >= 0) & (dist < WINDOW) & (kw_pos[None, :] >= 0)
    logits = jnp.einsum('bqghd,bkgd->bqghk', q, kw).astype(jnp.float32) * SCALE + bias_qk(tab, dist)[None]
    p = masked_softmax(logits, mask[None, :, None, None, :])
    return jnp.einsum('bqghk,bkgd->bqghd', p.astype(vw.dtype), vw)


def nsa_core(q, gate_logits, q_pos, comp_k, comp_v, ks_g, vs_g, kw, vw, kw_pos, tab):
    o_c, p_c = cmp_branch(q, q_pos, comp_k, comp_v, tab)
    o_s = sel_branch(q, q_pos, p_c, ks_g, vs_g, tab)
    o_w = win_branch(q, q_pos, kw, vw, kw_pos, tab)
    bsz, q_len = q.shape[:2]
    g = jax.nn.sigmoid(gate_logits.astype(jnp.float32)).reshape(bsz, q_len, N_KV_GROUPS, HEADS_PER_GROUP, 3)
    o = g[..., 0:1] * o_c + g[..., 1:2] * o_s + g[..., 2:3] * o_w
    return o.reshape(bsz, q_len, N_HEADS * HEAD_DIM).astype(q.dtype)


def nsa_prompt(q, gl, comp_k, comp_v, ks_g, vs_g, kw_pad, tab):
    bsz, t_len = q.shape[:2]

    def block(i):
        qs = i * Q_BLOCK
        qb = lax.dynamic_slice_in_dim(q, qs, Q_BLOCK, axis=1)
        gb = lax.dynamic_slice_in_dim(gl, qs, Q_BLOCK, axis=1)
        kwb = lax.dynamic_slice_in_dim(kw_pad, qs, WINDOW + Q_BLOCK, axis=1)
        q_pos = qs + jnp.arange(Q_BLOCK)
        kw_pos = qs - WINDOW + jnp.arange(WINDOW + Q_BLOCK)
        return nsa_core(qb, gb, q_pos, comp_k, comp_v, ks_g, vs_g, kwb[:, :, 0], kwb[:, :, 1], kw_pos, tab)

    out = lax.map(block, jnp.arange(t_len // Q_BLOCK))
    return out.transpose(1, 0, 2, 3).reshape(bsz, t_len, N_HEADS * HEAD_DIM)


def nsa_decode(q, gl, q_pos, comp_k, comp_v, ks_g, vs_g, kw, kw_pos, tab):
    return nsa_core(q, gl, q_pos, comp_k, comp_v, ks_g, vs_g, kw[:, :, 0], kw[:, :, 1], kw_pos, tab)


def mixer_b(x, w_in, nsa_fn, mk, mv, wo):
    bsz, t_len, _ = x.shape
    proj = x @ w_in
    nq = N_HEADS * HEAD_DIM
    q = proj[..., :nq].reshape(bsz, t_len, N_KV_GROUPS, HEADS_PER_GROUP, HEAD_DIM)
    gl = proj[..., nq:nq + 3 * N_HEADS]
    mix = jnp.concatenate([nsa_fn(q, gl), mem_attn(proj[..., nq + 3 * N_HEADS:], mk, mv)], axis=-1)
    return mix @ wo


def hmoe(x, wg, bg, we, be, w_gate, w_up, w_down):
    shp = x.shape
    xf = x.reshape(-1, shp[-1])
    lg = (xf @ wg + bg).astype(jnp.float32)
    grp = jnp.argmax(lg, axis=-1)
    g_w = jnp.take_along_axis(jax.nn.softmax(lg, axis=-1), grp[:, None], axis=1)
    le = (xf @ we + be).astype(jnp.float32).reshape(-1, N_GROUPS, EXPERTS_PER_GROUP)
    le_g = jnp.take_along_axis(le, grp[:, None, None], axis=1)[:, 0]
    tv, ti = lax.top_k(le_g, TOP_K_IN_GROUP)
    w = g_w * jax.nn.softmax(tv, axis=-1)
    e_idx = grp[:, None] * EXPERTS_PER_GROUP + ti
    combine = jnp.einsum('nk,nke->ne', w, jax.nn.one_hot(e_idx, N_EXPERTS, dtype=jnp.float32))
    hdn = jax.nn.silu(jnp.einsum('nd,edf->nef', xf, w_gate)) * jnp.einsum('nd,edf->nef', xf, w_up)
    out = jnp.einsum('nef,efd->nd', hdn * combine[..., None].astype(hdn.dtype), w_down)
    return out.reshape(shp)


def gather_pages(pool, page_table):
    g = pool[page_table]
    return g.reshape((g.shape[0], g.shape[1] * g.shape[2]) + g.shape[3:])


def setup_inputs(seed: int = 0) -> dict:
    key = jax.random.key(seed)
    keys = iter(jax.random.split(key, 48))
    f32 = jnp.float32

    def nrm(shape, scale):
        return jax.random.normal(next(keys), shape, f32) * scale

    n_pages = PAST_LEN // PAGE_SIZE
    n_used = DEC_BATCH * n_pages
    n_pool = n_used + max(1, n_used // 4)
    perm = jax.random.permutation(next(keys), n_pool)
    page_table = perm[:n_used].reshape(DEC_BATCH, n_pages).astype(jnp.int32)
    w_buf = min(WINDOW, PAST_LEN)
    kv_scale = jnp.array([1.0, BETA], f32)
    col_a = jnp.concatenate([jnp.ones((D_RNN,), f32), jnp.full((D_RNN,), BETA, f32), jnp.ones((D_MEM,), f32)])
    u = jax.random.uniform(next(keys), (N_A_LAYERS, D_RNN), f32, 0.9, 0.999)
    s = u ** (1.0 / LRU_C)
    lru_lambda = jnp.log(s) - jnp.log1p(-s)
    dm = D_MODEL ** -0.5
    return {
        'x_prompt': nrm((BATCH, SEQ, D_MODEL), 1.0),
        'x_sample': nrm((DEC_BATCH, DEC_SEQ, D_MODEL), 1.0),
        'mem_prompt': nrm((BATCH, N_MEM_TOK, D_MODEL), 1.0),
        'cache_cmp_kv': nrm((n_pool, PAGE_SIZE, 2, N_KV_GROUPS, HEAD_DIM), 1.0),
        'cache_slc_kv': nrm((n_pool, PAGE_SIZE, 2, N_KV_GROUPS, HEAD_DIM), 1.0),
        'cache_win_kv': nrm((DEC_BATCH, w_buf, 2, N_KV_GROUPS, HEAD_DIM), 1.0),
        'cache_mem_kv': nrm((DEPTH, DEC_BATCH, N_MEM_TOK, 2, N_MEM_HEADS, HEAD_DIM), 1.0),
        'state_lru_h': nrm((N_A_LAYERS, DEC_BATCH, D_RNN), 0.5),
        'state_conv': nrm((N_A_LAYERS, DEC_BATCH, CONV_W - 1, D_RNN), 1.0),
        'page_table': page_table,
        'w_in_a': nrm((N_A_LAYERS, D_MODEL, D_IN_A), dm) * col_a,
        'conv_w': nrm((N_A_LAYERS, CONV_W, D_RNN), CONV_W ** -0.5),
        'conv_b': nrm((N_A_LAYERS, D_RNN), 0.01),
        'w_rg': nrm((N_A_LAYERS, N_LRU_BLOCKS, LRU_BLOCK, LRU_BLOCK), LRU_BLOCK ** -0.5),
        'b_rg': nrm((N_A_LAYERS, D_RNN), 0.01),
        'w_ig': nrm((N_A_LAYERS, N_LRU_BLOCKS, LRU_BLOCK, LRU_BLOCK), LRU_BLOCK ** -0.5),
        'b_ig': nrm((N_A_LAYERS, D_RNN), 0.01),
        'lru_lambda': lru_lambda,
        'w_in_b': nrm((N_B_LAYERS, D_MODEL, D_IN_B), dm),
        'w_kv_shared': (nrm((D_MODEL, 3, 2, N_KV_GROUPS * HEAD_DIM), dm) * kv_scale[None, None, :, None]).reshape(D_MODEL, D_KV_SHARED),
        'cmp_pe': nrm((2, CMP_LEN, HEAD_DIM), 0.1),
        'cmp_w1': nrm((2, CMP_LEN * HEAD_DIM, CMP_HID), (CMP_LEN * HEAD_DIM) ** -0.5),
        'cmp_b1': nrm((2, CMP_HID), 0.01),
        'cmp_w2': nrm((2, CMP_HID, HEAD_DIM), CMP_HID ** -0.5),
        'rel_bias': nrm((N_BUCKETS, N_HEADS), 0.5),
        'w_mem_kv': (nrm((DEPTH, D_MODEL, 2, D_MEM), dm) * kv_scale[:, None]).reshape(DEPTH, D_MODEL, 2 * D_MEM),
        'w_out': nrm((DEPTH, D_MODEL, D_MODEL), dm * BETA),
        'ln1_g': 1.0 + nrm((DEPTH, D_MODEL), 0.02),
        'ln1_b': nrm((DEPTH, D_MODEL), 0.02),
        'ln2_g': 1.0 + nrm((DEPTH, D_MODEL), 0.02),
        'ln2_b': nrm((DEPTH, D_MODEL), 0.02),
        'w_router_g': nrm((DEPTH, D_MODEL, N_GROUPS), dm),
        'b_router_g': nrm((DEPTH, N_GROUPS), 0.01),
        'w_router_e': nrm((DEPTH, D_MODEL, N_EXPERTS), dm),
        'b_router_e': nrm((DEPTH, N_EXPERTS), 0.01),
        'w_exp_gate': nrm((DEPTH, N_EXPERTS, D_MODEL, D_EXPERT), dm),
        'w_exp_up': nrm((DEPTH, N_EXPERTS, D_MODEL, D_EXPERT), dm),
        'w_exp_down': nrm((DEPTH, N_EXPERTS, D_EXPERT, D_MODEL), D_EXPERT ** -0.5 * BETA),
    }


def reference(x_prompt, x_sample, mem_prompt, cache_cmp_kv, cache_slc_kv, cache_win_kv, cache_mem_kv,
              state_lru_h, state_conv, page_table, w_in_a, conv_w, conv_b, w_rg, b_rg, w_ig, b_ig, lru_lambda,
              w_in_b, w_kv_shared, cmp_pe, cmp_w1, cmp_b1, cmp_w2, rel_bias, w_mem_kv, w_out,
              ln1_g, ln1_b, ln2_g, ln2_b, w_router_g, b_router_g, w_router_e, b_router_e,
              w_exp_gate, w_exp_up, w_exp_down):
    bp, t_len, _ = x_prompt.shape
    p_mem_kv = jnp.einsum('bmd,lde->lbme', mem_prompt, w_mem_kv).reshape(
        DEPTH, bp, mem_prompt.shape[1], 2, N_MEM_HEADS, HEAD_DIM)
    x = x_prompt
    lru_p, conv_p = [], []
    for l in range(DEPTH):
        mk, mv = p_mem_kv[l, :, :, 0], p_mem_kv[l, :, :, 1]
        if l < N_A_LAYERS:
            sub, h_t, buf = mixer_a(x, w_in_a[l], conv_w[l], conv_b[l], w_rg[l], b_rg[l], w_ig[l], b_ig[l],
                                    lru_lambda[l], jnp.zeros((bp, CONV_W - 1, D_RNN), x.dtype),
                                    jnp.zeros((bp, D_RNN), jnp.float32), mk, mv, w_out[l])
            lru_p.append(h_t.astype(x_prompt.dtype))
            conv_p.append(buf)
        else:
            if l == N_A_LAYERS:
                p_cmp_kv, p_slc_kv, p_win_rows = shared_kv(x, w_kv_shared)
                nsa_p = functools.partial(
                    nsa_prompt,
                    comp_k=compress(p_cmp_kv[:, :, 0], cmp_pe[0], cmp_w1[0], cmp_b1[0], cmp_w2[0]),
                    comp_v=compress(p_cmp_kv[:, :, 1], cmp_pe[1], cmp_w1[1], cmp_b1[1], cmp_w2[1]),
                    ks_g=to_sel_blocks(p_slc_kv[:, :, 0]), vs_g=to_sel_blocks(p_slc_kv[:, :, 1]),
                    kw_pad=jnp.pad(p_win_rows, ((0, 0), (WINDOW, 0), (0, 0), (0, 0), (0, 0))), tab=rel_bias)
            sub = mixer_b(x, w_in_b[l - N_A_LAYERS], nsa_p, mk, mv, w_out[l])
        x = post_norm(x, sub, ln1_g[l], ln1_b[l])
        x = post_norm(x, hmoe(x, w_router_g[l], b_router_g[l], w_router_e[l], b_router_e[l],
                              w_exp_gate[l], w_exp_up[l], w_exp_down[l]), ln2_g[l], ln2_b[l])
    y_prompt = x
    p_win_kv = p_win_rows[:, t_len - min(WINDOW, t_len):]
    p_lru_h = jnp.stack(lru_p)
    p_conv = jnp.stack(conv_p)

    bd, s_len, _ = x_sample.shape
    past = page_table.shape[1] * cache_cmp_kv.shape[1]
    w_buf = cache_win_kv.shape[1]
    q_pos = past + jnp.arange(s_len)
    x = x_sample
    lru_s, conv_s = [], []
    for l in range(DEPTH):
        mk, mv = cache_mem_kv[l, :, :, 0], cache_mem_kv[l, :, :, 1]
        if l < N_A_LAYERS:
            sub, h_t, buf = mixer_a(x, w_in_a[l], conv_w[l], conv_b[l], w_rg[l], b_rg[l], w_ig[l], b_ig[l],
                                    lru_lambda[l], state_conv[l], state_lru_h[l], mk, mv, w_out[l])
            lru_s.append(h_t.astype(state_lru_h.dtype))
            conv_s.append(buf)
        else:
            if l == N_A_LAYERS:
                s_cmp_kv, s_slc_kv, s_win_rows = shared_kv(x, w_kv_shared)
                cmp_all = jnp.concatenate([gather_pages(cache_cmp_kv, page_table), s_cmp_kv.astype(cache_cmp_kv.dtype)], axis=1)
                slc_all = jnp.concatenate([gather_pages(cache_slc_kv, page_table), s_slc_kv.astype(cache_slc_kv.dtype)], axis=1)
                kw_all = jnp.concatenate([cache_win_kv, s_win_rows.astype(cache_win_kv.dtype)], axis=1)
                nsa_s = functools.partial(
                    nsa_decode, q_pos=q_pos,
                    comp_k=compress(cmp_all[:, :, 0], cmp_pe[0], cmp_w1[0], cmp_b1[0], cmp_w2[0]),
                    comp_v=compress(cmp_all[:, :, 1], cmp_pe[1], cmp_w1[1], cmp_b1[1], cmp_w2[1]),
                    ks_g=to_sel_blocks(slc_all[:, :, 0]), vs_g=to_sel_blocks(slc_all[:, :, 1]),
                    kw=kw_all, kw_pos=past - w_buf + jnp.arange(w_buf + s_len), tab=rel_bias)
            sub = mixer_b(x, w_in_b[l - N_A_LAYERS], nsa_s, mk, mv, w_out[l])
        x = post_norm(x, sub, ln1_g[l], ln1_b[l])
        x = post_norm(x, hmoe(x, w_router_g[l], b_router_g[l], w_router_e[l], b_router_e[l],
                              w_exp_gate[l], w_exp_up[l], w_exp_down[l]), ln2_g[l], ln2_b[l])
    y_sample = x
    s_win_kv = kw_all[:, kw_all.shape[1] - w_buf:]
    s_lru_h = jnp.stack(lru_s)
    s_conv = jnp.stack(conv_s)
    return (y_prompt, y_sample, p_cmp_kv, p_slc_kv, p_win_kv, p_mem_kv, p_lru_h, p_conv,
            s_cmp_kv, s_slc_kv, s_win_kv, s_lru_h, s_conv)
```

```python
import functools
import math

import jax
import jax.numpy as jnp
from jax import lax
from jax.experimental import pallas as pl
from jax.experimental.pallas import tpu as pltpu

F32 = jnp.float32
BF16 = jnp.bfloat16
I32 = jnp.int32

HEAD_DIM = 64
N_MEM_HEADS = 4
D_MEM = N_MEM_HEADS * HEAD_DIM
N_HEADS = 12
D_RNN = N_HEADS * HEAD_DIM
N_KV_GROUPS = 4
HEADS_PER_GROUP = N_HEADS // N_KV_GROUPS
D_KV = 2 * N_KV_GROUPS * HEAD_DIM
CONV_W = 4
LRU_C = 8.0
CMP_LEN = 32
CMP_STRIDE = 16
CMP_HID = 2 * HEAD_DIM
SEL_LEN = 64
N_SEL = 16
WINDOW = 512
N_BUCKETS = 32
MAX_DISTANCE = 1024
N_GROUPS = 4
EXPERTS_PER_GROUP = 8
N_EXPERTS = N_GROUPS * EXPERTS_PER_GROUP
D_EXPERT = 128
Q_BLOCK = 128
LN_EPS = 1e-5
NEG = -1e30
FORCE = 1e9
SCALE = HEAD_DIM ** -0.5

LANES = 128
SUBLANES = 8
VMEM_LIMIT_BYTES = 56 * 1024 * 1024


def _cparams(n_axes):
    return pltpu.CompilerParams(dimension_semantics=("arbitrary",) * n_axes,
                                vmem_limit_bytes=VMEM_LIMIT_BYTES)


def _dot(a, b):
    return jnp.dot(a, b, preferred_element_type=F32)


def _dot_nt(a, b):
    return lax.dot_general(a, b, (((1,), (1,)), ((), ())), preferred_element_type=F32)


def _split3(x):
    hi = x.astype(BF16)
    r1 = x - hi.astype(F32)
    mid = r1.astype(BF16)
    lo = (r1 - mid.astype(F32)).astype(BF16)
    return hi, mid, lo


def _layer_norm(v, g, b):
    mu = jnp.mean(v, axis=-1, keepdims=True)
    d = v - mu
    var = jnp.mean(d * d, axis=-1, keepdims=True)
    return d * lax.rsqrt(var + LN_EPS) * g + b


def _rel_bucket(dist):
    n = jnp.maximum(dist, 0)
    max_exact = N_BUCKETS // 2
    nf = jnp.maximum(n, 1).astype(F32)
    large = max_exact + (jnp.log(nf / max_exact) / math.log(MAX_DISTANCE / max_exact)
                         * (N_BUCKETS - max_exact)).astype(I32)
    large = jnp.minimum(large, N_BUCKETS - 1)
    return jnp.where(n < max_exact, n, large)


def _bias_from_bucket(bucket, tab_ref, heads):
    masks = [bucket == k for k in range(1, N_BUCKETS)]
    out = []
    for h in heads:
        acc = jnp.full(bucket.shape, tab_ref[0, h], F32)
        for k in range(1, N_BUCKETS):
            acc = jnp.where(masks[k - 1], tab_ref[k, h], acc)
        out.append(acc)
    return out


def _proj_kernel(x_ref, w_ref, *o_refs, cols):
    y = _dot(x_ref[...].astype(BF16), w_ref[...])
    for o_ref, (off, n) in zip(o_refs, cols):
        o_ref[...] = y[:, off:off + n].astype(o_ref.dtype)


def _proj(x, w, outs, tm, name):
    m, k = x.shape
    n = w.shape[1]
    tm = min(tm, m)
    assert m % tm == 0 and all(off + wd <= n for off, wd, _ in outs)
    return pl.pallas_call(
        functools.partial(_proj_kernel, cols=tuple((off, wd) for off, wd, _ in outs)),
        grid=(m // tm,),
        in_specs=[pl.BlockSpec((tm, k), lambda i: (i, 0)),
                  pl.BlockSpec((k, n), lambda i: (0, 0))],
        out_specs=[pl.BlockSpec((tm, wd), lambda i: (i, 0)) for _, wd, _ in outs],
        out_shape=[jax.ShapeDtypeStruct((m, wd), dt) for _, wd, dt in outs],
        compiler_params=_cparams(1),
        name=name,
    )(x, w)


def _softplus(z):
    return jnp.maximum(z, 0.0) + jnp.log1p(jnp.exp(-jnp.abs(z)))


def _lru_gates(xc, wr_ref, wi_ref, br, bi, lam):
    xcb = xc.astype(BF16)
    nb = D_RNN // LANES
    r_l = jnp.concatenate([_dot(xcb[:, j * LANES:(j + 1) * LANES], wr_ref[j]) for j in range(nb)], axis=1)
    i_l = jnp.concatenate([_dot(xcb[:, j * LANES:(j + 1) * LANES], wi_ref[j]) for j in range(nb)], axis=1)
    r = jax.nn.sigmoid(r_l + br)
    i = jax.nn.sigmoid(i_l + bi)
    log_a = -LRU_C * r * _softplus(-lam)
    a = jnp.exp(log_a)
    u = jnp.sqrt(1.0 - a * a) * i * xc
    return a, u


def _rglru_prompt_kernel(gate_ref, xr_ref, cw_ref, cb_ref, wr_ref, wi_ref, br_ref, bi_ref, lam_ref,
                         y_ref, ht_ref, buf_ref, xp_ref, *, t_len, chunk):
    d = D_RNN
    pad = SUBLANES
    xp_ref[0:pad, :] = jnp.zeros((pad, d), F32)
    xp_ref[pad:pad + t_len, :] = xr_ref[0]
    buf_ref[0] = xr_ref[0, t_len - (CONV_W - 1):t_len, :]
    cw = cw_ref[...]
    cb = cb_ref[...]
    br = br_ref[...]
    bi = bi_ref[...]
    lam = lam_ref[...]
    row_in_tile = lax.broadcasted_iota(I32, (chunk, d), 0) & (SUBLANES - 1)
    h = jnp.zeros((1, d), F32)
    for c in range(t_len // chunk):
        base = c * chunk
        xc = cb
        for k in range(CONV_W):
            lo = pad - (CONV_W - 1) + k + base
            xc = xc + xp_ref[lo:lo + chunk, :] * cw[k:k + 1, :]
        a, u = _lru_gates(xc, wr_ref, wi_ref, br, bi, lam)
        for k in (1, 2, 4):
            a_s = pltpu.roll(a, k, axis=0)
            u_s = pltpu.roll(u, k, axis=0)
            m = row_in_tile >= k
            u = jnp.where(m, a * u_s + u, u)
            a = jnp.where(m, a * a_s, a)
        hs = []
        for j in range(chunk // SUBLANES):
            h_t = a[j * SUBLANES:(j + 1) * SUBLANES] * h + u[j * SUBLANES:(j + 1) * SUBLANES]
            h = h_t[SUBLANES - 1:SUBLANES]
            hs.append(h_t)
        hfull = jnp.concatenate(hs, axis=0)
        g = jax.nn.gelu(gate_ref[0, base:base + chunk, :])
        y_ref[0, base:base + chunk, :] = (g * hfull).astype(y_ref.dtype)
    ht_ref[0] = h


def _rglru_prompt(gate, xr, cw, cb, wr_bd, wi_bd, br, bi, lam):
    b, t_len, d = xr.shape
    chunk = min(256, t_len)
    full = lambda shape: pl.BlockSpec(shape, lambda i: (0,) * len(shape))
    return pl.pallas_call(
        functools.partial(_rglru_prompt_kernel, t_len=t_len, chunk=chunk),
        grid=(b,),
        in_specs=[pl.BlockSpec((1, t_len, d), lambda i: (i, 0, 0)),
                  pl.BlockSpec((1, t_len, d), lambda i: (i, 0, 0)),
                  full((CONV_W, d)), full((1, d)), full(wr_bd.shape), full(wi_bd.shape),
                  full((1, d)), full((1, d)), full((1, d))],
        out_specs=[pl.BlockSpec((1, t_len, d), lambda i: (i, 0, 0)),
                   pl.BlockSpec((1, 1, d), lambda i: (i, 0, 0)),
                   pl.BlockSpec((1, CONV_W - 1, d), lambda i: (i, 0, 0))],
        out_shape=[jax.ShapeDtypeStruct((b, t_len, d), BF16),
                   jax.ShapeDtypeStruct((b, 1, d), F32),
                   jax.ShapeDtypeStruct((b, CONV_W - 1, d), F32)],
        scratch_shapes=[pltpu.VMEM((t_len + SUBLANES, d), F32)],
        compiler_params=_cparams(1),
        name="rglru_prompt",
    )(gate, xr, cw, cb, wr_bd, wi_bd, br, bi, lam)


def _rglru_step_kernel(gate_ref, xr_ref, conv_ref, h0_ref, cw_ref, cb_ref, wr_ref, wi_ref, br_ref, bi_ref,
                       lam_ref, y_ref, ht_ref, buf_ref):
    xr = xr_ref[...]
    cw = cw_ref[...]
    xc = cb_ref[...] + xr * cw[CONV_W - 1:CONV_W, :]
    for k in range(CONV_W - 1):
        xc = xc + conv_ref[k] * cw[k:k + 1, :]
    a, u = _lru_gates(xc, wr_ref, wi_ref, br_ref[...], bi_ref[...], lam_ref[...])
    h = a * h0_ref[...] + u
    ht_ref[...] = h
    y_ref[...] = (jax.nn.gelu(gate_ref[...]) * h).astype(y_ref.dtype)
    for k in range(CONV_W - 2):
        buf_ref[k] = conv_ref[k + 1]
    buf_ref[CONV_W - 2] = xr


def _rglru_step(gate, xr, conv_t, h0, cw, cb, wr_bd, wi_bd, br, bi, lam):
    b, d = xr.shape
    return pl.pallas_call(
        _rglru_step_kernel,
        out_shape=[jax.ShapeDtypeStruct((b, d), BF16),
                   jax.ShapeDtypeStruct((b, d), F32),
                   jax.ShapeDtypeStruct((CONV_W - 1, b, d), F32)],
        compiler_params=pltpu.CompilerParams(vmem_limit_bytes=VMEM_LIMIT_BYTES),
        name="rglru_step",
    )(gate, xr, conv_t, h0, cw, cb, wr_bd, wi_bd, br, bi, lam)


def _mem_attn_kernel(q_ref, kv_ref, o_ref, *, rows):
    q = q_ref[0]
    if rows < SUBLANES:
        q = jnp.broadcast_to(q, (SUBLANES, D_MEM))
    kv = kv_ref[0, 0].astype(BF16)
    outs = []
    for h in range(N_MEM_HEADS):
        qh = q[:, h * HEAD_DIM:(h + 1) * HEAD_DIM]
        kh = kv[:, h * HEAD_DIM:(h + 1) * HEAD_DIM]
        vh = kv[:, D_MEM + h * HEAD_DIM:D_MEM + (h + 1) * HEAD_DIM]
        s = _dot_nt(qh, kh) * SCALE
        m = jnp.max(s, axis=-1, keepdims=True)
        e = jnp.exp(s - m)
        p = e / jnp.sum(e, axis=-1, keepdims=True)
        outs.append(_dot(p.astype(BF16), vh))
    o = jnp.concatenate(outs, axis=1)
    o_ref[0] = o[:rows].astype(o_ref.dtype)


def _mem_attn(q, mem_kv, layer):
    b, t_len, _ = q.shape
    n_mem = mem_kv.shape[2]
    tq = min(t_len, 512)
    return pl.pallas_call(
        functools.partial(_mem_attn_kernel, rows=tq),
        grid=(b, t_len // tq),
        in_specs=[pl.BlockSpec((1, tq, D_MEM), lambda i, j: (i, j, 0)),
                  pl.BlockSpec((1, 1, n_mem, 2 * D_MEM), lambda i, j: (layer, i, 0, 0))],
        out_specs=pl.BlockSpec((1, tq, D_MEM), lambda i, j: (i, j, 0)),
        out_shape=jax.ShapeDtypeStruct((b, t_len, D_MEM), BF16),
        compiler_params=_cparams(2),
        name="mem_attn",
    )(q, mem_kv)


def _out_ln_kernel(x_ref, ma_ref, mm_ref, wa_ref, wm_ref, g_ref, b_ref, o_ref, *, alpha):
    sub = _dot(ma_ref[...], wa_ref[...]) + _dot(mm_ref[...], wm_ref[...])
    o_ref[...] = _layer_norm(alpha * x_ref[...] + sub, g_ref[...], b_ref[...])


def _out_ln(x, mix_a, mix_m, wo_a, wo_m, g, b, alpha, tm):
    m, d = x.shape
    da = mix_a.shape[1]
    dm = mix_m.shape[1]
    return pl.pallas_call(
        functools.partial(_out_ln_kernel, alpha=alpha),
        grid=(m // tm,),
        in_specs=[pl.BlockSpec((tm, d), lambda i: (i, 0)),
                  pl.BlockSpec((tm, da), lambda i: (i, 0)),
                  pl.BlockSpec((tm, dm), lambda i: (i, 0)),
                  pl.BlockSpec((da, d), lambda i: (0, 0)),
                  pl.BlockSpec((dm, d), lambda i: (0, 0)),
                  pl.BlockSpec((1, d), lambda i: (0, 0)),
                  pl.BlockSpec((1, d), lambda i: (0, 0))],
        out_specs=pl.BlockSpec((tm, d), lambda i: (i, 0)),
        out_shape=jax.ShapeDtypeStruct((m, d), F32),
        compiler_params=_cparams(1),
        name="out_ln",
    )(x, mix_a, mix_m, wo_a, wo_m, g, b)


def _route(logits):
    lane = lax.broadcasted_iota(I32, logits.shape, 1)
    big = jnp.int32(4 * LANES)
    is_g = lane < N_GROUPS
    lg = jnp.where(is_g, logits, -jnp.inf)
    gmax = jnp.max(lg, axis=-1, keepdims=True)
    grp = jnp.min(jnp.where(lg == gmax, lane, big), axis=-1, keepdims=True)
    gsum = jnp.sum(jnp.where(is_g, jnp.exp(logits - gmax), 0.0), axis=-1, keepdims=True)
    g_w = 1.0 / gsum
    e_idx = lane - N_GROUPS
    in_grp = (lane >= N_GROUPS) & (lane < N_GROUPS + N_EXPERTS) & ((e_idx >> 3) == grp)
    v = jnp.where(in_grp, logits, -jnp.inf)
    v1 = jnp.max(v, axis=-1, keepdims=True)
    i1 = jnp.min(jnp.where(v == v1, lane, big), axis=-1, keepdims=True)
    vr = jnp.where(lane == i1, -jnp.inf, v)
    v2 = jnp.max(vr, axis=-1, keepdims=True)
    i2 = jnp.min(jnp.where(vr == v2, lane, big), axis=-1, keepdims=True)
    e2 = jnp.exp(v2 - v1)
    w1 = g_w / (1.0 + e2)
    w2 = g_w * e2 / (1.0 + e2)
    return jnp.where(lane == i1, w1, 0.0) + jnp.where(lane == i2, w2, 0.0)


def _moe_kernel(x_ref, wrh_ref, wrl_ref, rb_ref, wgu_ref, wd_ref, g_ref, b_ref, o_ref,
                xb_ref, comb_ref, acc_ref, *, alpha):
    grp = pl.program_id(1)
    hid = EXPERTS_PER_GROUP * D_EXPERT

    @pl.when(grp == 0)
    def _():
        x = x_ref[...]
        xh = x.astype(BF16)
        xl = (x - xh.astype(F32)).astype(BF16)
        logits = _dot(xh, wrh_ref[...]) + _dot(xl, wrh_ref[...]) + _dot(xh, wrl_ref[...]) + rb_ref[...]
        xb_ref[...] = xh
        comb_ref[...] = _route(logits)
        acc_ref[...] = jnp.zeros(acc_ref.shape, F32)

    xb = xb_ref[...]
    gu = _dot(xb, wgu_ref[0])
    gate = gu[:, :hid]
    up = gu[:, hid:]
    hdn = gate * jax.nn.sigmoid(gate) * up
    comb = comb_ref[...]
    lane = lax.broadcasted_iota(I32, comb.shape, 1)
    scale = []
    for e in range(EXPERTS_PER_GROUP):
        col = jnp.sum(jnp.where(lane == N_GROUPS + grp * EXPERTS_PER_GROUP + e, comb, 0.0),
                      axis=-1, keepdims=True)
        scale.append(jnp.broadcast_to(col, (comb.shape[0], D_EXPERT)))
    hdn = hdn * jnp.concatenate(scale, axis=1)
    acc_ref[...] += _dot(hdn.astype(BF16), wd_ref[0])

    @pl.when(grp == N_GROUPS - 1)
    def _():
        o_ref[...] = _layer_norm(alpha * x_ref[...] + acc_ref[...], g_ref[...], b_ref[...])


def _moe_ln(x, wr_hi, wr_lo, rb, wgu, wd, g, b, alpha, tm):
    m, d = x.shape
    hid = EXPERTS_PER_GROUP * D_EXPERT
    return pl.pallas_call(
        functools.partial(_moe_kernel, alpha=alpha),
        grid=(m // tm, N_GROUPS),
        in_specs=[pl.BlockSpec((tm, d), lambda i, j: (i, 0)),
                  pl.BlockSpec((d, LANES), lambda i, j: (0, 0)),
                  pl.BlockSpec((d, LANES), lambda i, j: (0, 0)),
                  pl.BlockSpec((1, LANES), lambda i, j: (0, 0)),
                  pl.BlockSpec((1, d, 2 * hid), lambda i, j: (j, 0, 0)),
                  pl.BlockSpec((1, hid, d), lambda i, j: (j, 0, 0)),
                  pl.BlockSpec((1, d), lambda i, j: (0, 0)),
                  pl.BlockSpec((1, d), lambda i, j: (0, 0))],
        out_specs=pl.BlockSpec((tm, d), lambda i, j: (i, 0)),
        out_shape=jax.ShapeDtypeStruct((m, d), F32),
        scratch_shapes=[pltpu.VMEM((tm, d), BF16), pltpu.VMEM((tm, LANES), F32), pltpu.VMEM((tm, d), F32)],
        compiler_params=_cparams(2),
        name="moe_ln",
    )(x, wr_hi, wr_lo, rb, wgu, wd, g, b)


CMP_MBLK = 128
ROW_TILES = D_KV // LANES
PAIR_HID = 2 * CMP_HID


def _compress_weights(pe, w1, b1, w2):
    w1 = w1.reshape(2, CMP_LEN, HEAD_DIM, CMP_HID)
    z1 = jnp.zeros_like(w1)
    w1p = jnp.concatenate([jnp.concatenate([w1, z1], axis=3), jnp.concatenate([z1, w1], axis=3)], axis=2)
    z2 = jnp.zeros_like(w2)
    w2p = jnp.concatenate([jnp.concatenate([w2, z2], axis=2), jnp.concatenate([z2, w2], axis=2)], axis=1)
    return (jnp.concatenate([pe, pe], axis=2), w1p.astype(BF16),
            jnp.concatenate([b1, b1], axis=1)[:, None], w2p.astype(BF16))


def _compress_rows(x_ref, n_chunk, pe_ref, w1_ref, b1_ref, w2_ref, out_ref, a_ref, b_ref):
    n_blk = n_chunk // CMP_MBLK
    step = CMP_STRIDE * ROW_TILES
    for kv in range(2):
        for pair in range(N_KV_GROUPS // 2):
            tile = kv * (N_KV_GROUPS // 2) + pair

            def fill(mb, carry, kv=kv, tile=tile):
                row0 = pl.multiple_of(mb * (CMP_MBLK * step), CMP_MBLK * step)
                acc_a = jnp.zeros((CMP_MBLK, PAIR_HID), F32)
                acc_b = jnp.zeros((CMP_MBLK, PAIR_HID), F32)
                for r in range(CMP_STRIDE):
                    xg = x_ref[pl.ds(row0 + r * ROW_TILES + tile, CMP_MBLK, stride=step), :]
                    acc_a += _dot((xg + pe_ref[kv, r:r + 1, :]).astype(BF16), w1_ref[kv, r])
                    acc_b += _dot((xg + pe_ref[kv, CMP_STRIDE + r:CMP_STRIDE + r + 1, :]).astype(BF16),
                                  w1_ref[kv, CMP_STRIDE + r])
                m0 = pl.multiple_of(mb * CMP_MBLK, CMP_MBLK)
                a_ref[pl.ds(m0, CMP_MBLK), :] = acc_a
                b_ref[pl.ds(m0, CMP_MBLK), :] = acc_b
                return carry

            lax.fori_loop(0, n_blk, fill, 0)
            b_ref[n_chunk:n_chunk + SUBLANES, :] = jnp.zeros((SUBLANES, PAIR_HID), F32)
            for mb in range(n_blk):
                m0 = mb * CMP_MBLK
                hid = jax.nn.gelu(a_ref[m0:m0 + CMP_MBLK, :] + b_ref[m0 + 1:m0 + 1 + CMP_MBLK, :] + b1_ref[kv])
                out_ref[m0:m0 + CMP_MBLK, tile * LANES:(tile + 1) * LANES] = _dot(hid.astype(BF16), w2_ref[kv])


def _compress_prompt_kernel(x_ref, pe_ref, w1_ref, b1_ref, w2_ref, o_ref, a_ref, b_ref, *, n_chunk):
    _compress_rows(x_ref.at[0], n_chunk, pe_ref, w1_ref, b1_ref, w2_ref, o_ref.at[0], a_ref, b_ref)


def _compress_prompt(rows, pe, w1, b1, w2):
    b, t_len, _ = rows.shape
    n_chunk = t_len // CMP_STRIDE
    assert n_chunk % CMP_MBLK == 0
    full = lambda a: pl.BlockSpec(a.shape, lambda i: (0,) * a.ndim)
    return pl.pallas_call(
        functools.partial(_compress_prompt_kernel, n_chunk=n_chunk),
        grid=(b,),
        in_specs=[pl.BlockSpec((1, t_len * ROW_TILES, LANES), lambda i: (i, 0, 0)),
                  full(pe), full(w1), full(b1), full(w2)],
        out_specs=pl.BlockSpec((1, n_chunk, D_KV), lambda i: (i, 0, 0)),
        out_shape=jax.ShapeDtypeStruct((b, n_chunk, D_KV), F32),
        scratch_shapes=[pltpu.VMEM((n_chunk, PAIR_HID), F32), pltpu.VMEM((n_chunk + SUBLANES, PAIR_HID), F32)],
        compiler_params=_cparams(1),
        name="compress_prompt",
    )(rows.reshape(b, t_len * ROW_TILES, LANES), pe, w1, b1, w2)


def _masked_softmax_rows(s, mask):
    sm = jnp.where(mask, s, NEG)
    m = jnp.max(sm, axis=-1, keepdims=True)
    e = jnp.where(mask, jnp.exp(sm - m), 0.0)
    tot = jnp.sum(e, axis=-1, keepdims=True)
    return e * jnp.where(tot > 0.0, 1.0 / tot, 0.0)


def _topk_select(score, blk, n_blocks):
    cnt = jnp.zeros(score.shape, F32)
    for s in range(n_blocks):
        col = score[:, s:s + 1]
        beats = (col > score) | ((col == score) & (blk > s))
        cnt = cnt + jnp.where(beats, 1.0, 0.0)
    return (cnt < float(N_SEL)) & (score > 0.5 * NEG)


def _attend_tiles(qg, kv_ref, c0, j_lo, j_hi, i_blk, bias_ref, heads, madd_fn):
    rows = qg.shape[0]
    reps = rows // Q_BLOCK

    def body(j, carry):
        m, l, acc = carry
        k0 = pl.multiple_of(j * Q_BLOCK, Q_BLOCK)
        kt = kv_ref[pl.ds(k0, Q_BLOCK), c0:c0 + HEAD_DIM]
        vt = kv_ref[pl.ds(k0, Q_BLOCK), D_KV // 2 + c0:D_KV // 2 + c0 + HEAD_DIM]
        delta = i_blk - j
        bias = jnp.concatenate([bias_ref[delta * N_HEADS + h] for h in heads], axis=0)
        madd = madd_fn(j, delta)
        s = _dot_nt(qg, kt) * SCALE + bias + jnp.concatenate([madd] * reps, axis=0)
        m_new = jnp.maximum(m, jnp.max(s, axis=-1, keepdims=True))
        alpha = jnp.exp(m - m_new)
        p = jnp.exp(s - m_new)
        l = alpha * l + jnp.sum(p, axis=-1, keepdims=True)
        acc = alpha * acc + _dot(p.astype(BF16), vt)
        return m_new, l, acc

    init = (jnp.full((rows, 1), NEG, F32), jnp.zeros((rows, 1), F32), jnp.zeros((rows, HEAD_DIM), F32))
    m, l, acc = lax.fori_loop(j_lo, j_hi, body, init)
    return acc / l


def _nsa_prompt_kernel(tab_ref, q_ref, gl_ref, comp_ref, slc_ref, win_ref, o_ref, bias_ref, sm_ref, *, n_tiles):
    b = pl.program_id(0)
    i = pl.program_id(1)
    ii = lax.broadcasted_iota(I32, (Q_BLOCK, Q_BLOCK), 0)
    jj = lax.broadcasted_iota(I32, (Q_BLOCK, Q_BLOCK), 1)
    dd = ii - jj

    @pl.when((b == 0) & (i == 0))
    def _():
        def build(delta, carry):
            biases = _bias_from_bucket(_rel_bucket(delta * Q_BLOCK + dd), tab_ref, range(N_HEADS))
            for h in range(N_HEADS):
                bias_ref[delta * N_HEADS + h] = biases[h]
            return carry
        lax.fori_loop(0, n_tiles, build, 0)

    qs = i * Q_BLOCK
    q_all = q_ref[0]
    gates = jax.nn.sigmoid(gl_ref[0])
    comp = comp_ref[0].astype(BF16)
    n_cmp = comp.shape[0]
    n_sel_blocks = 2 * n_tiles

    dist_c = qs + ii - (jj * CMP_STRIDE + (CMP_LEN - 1))
    mask_c = (dist_c >= 0) & (jj < n_cmp - 1)
    bucket_c = _rel_bucket(dist_c)
    c_start = ii * CMP_STRIDE
    s_start = jj * SEL_LEN
    cover = jnp.where((c_start < s_start + SEL_LEN) & (c_start + CMP_LEN > s_start)
                      & (jj < n_sel_blocks) & (ii < n_cmp - 1), 1.0, 0.0).astype(BF16)
    q_pos = qs + ii
    cur = q_pos // SEL_LEN
    forced = (jj == 0) | (jj == cur) | (jj == cur - 1)
    valid = (jj * SEL_LEN <= q_pos) & (jj < n_sel_blocks)

    outs = []
    for g in range(N_KV_GROUPS):
        heads = [g * HEADS_PER_GROUP + hp for hp in range(HEADS_PER_GROUP)]
        c0 = g * HEAD_DIM
        qg = jnp.concatenate([q_all[:, h * HEAD_DIM:(h + 1) * HEAD_DIM] for h in heads], axis=0)

        bias_c = jnp.concatenate(_bias_from_bucket(bucket_c, tab_ref, heads), axis=0)
        ck = comp[:, c0:c0 + HEAD_DIM]
        cv = comp[:, D_KV // 2 + c0:D_KV // 2 + c0 + HEAD_DIM]
        s_c = _dot_nt(qg, ck) * SCALE + bias_c
        p_c = _masked_softmax_rows(s_c, jnp.concatenate([mask_c] * HEADS_PER_GROUP, axis=0))
        o_c = _dot(p_c.astype(BF16), cv)
        p_sum = p_c[0:Q_BLOCK]
        for hp in range(1, HEADS_PER_GROUP):
            p_sum = p_sum + p_c[hp * Q_BLOCK:(hp + 1) * Q_BLOCK]

        hi, mid, lo = _split3(p_sum)
        imp = _dot(hi, cover) + _dot(mid, cover) + _dot(lo, cover)
        score = jnp.where(forced, FORCE, imp)
        score = jnp.where(valid, score, NEG)
        sel = _topk_select(score, jj, n_sel_blocks)
        sel_f = jnp.where(sel, 1.0, 0.0)
        for j in range(n_tiles):
            tile = jnp.where(jj < SEL_LEN, sel_f[:, 2 * j:2 * j + 1], sel_f[:, 2 * j + 1:2 * j + 2])
            sm_ref[j] = jnp.where(tile > 0.5, 0.0, NEG)

        def madd_sel(j, delta):
            return sm_ref[j] + jnp.where(dd + delta * Q_BLOCK >= 0, 0.0, NEG)
        o_s = _attend_tiles(qg, slc_ref.at[0], c0, 0, i + 1, i, bias_ref, heads, madd_sel)

        def madd_win(j, delta):
            dist = dd + delta * Q_BLOCK
            return jnp.where((dist >= 0) & (dist < WINDOW), 0.0, NEG)
        o_w = _attend_tiles(qg, win_ref.at[0], c0, jnp.maximum(i - WINDOW // Q_BLOCK, 0), i + 1, i,
                            bias_ref, heads, madd_win)

        for hp, h in enumerate(heads):
            r0 = hp * Q_BLOCK
            outs.append(gates[:, 3 * h:3 * h + 1] * o_c[r0:r0 + Q_BLOCK]
                        + gates[:, 3 * h + 1:3 * h + 2] * o_s[r0:r0 + Q_BLOCK]
                        + gates[:, 3 * h + 2:3 * h + 3] * o_w[r0:r0 + Q_BLOCK])
    o_ref[0] = jnp.concatenate(outs, axis=1).astype(o_ref.dtype)


def _nsa_prompt(tab, q, gl, comp, slc_bf, win_bf):
    b, t_len, dq = q.shape
    n_tiles = t_len // Q_BLOCK
    assert comp.shape[1] == Q_BLOCK and t_len % Q_BLOCK == 0
    return pl.pallas_call(
        functools.partial(_nsa_prompt_kernel, n_tiles=n_tiles),
        grid=(b, n_tiles),
        in_specs=[pl.BlockSpec(memory_space=pltpu.SMEM),
                  pl.BlockSpec((1, Q_BLOCK, dq), lambda i, j: (i, j, 0)),
                  pl.BlockSpec((1, Q_BLOCK, LANES), lambda i, j: (i, j, 0)),
                  pl.BlockSpec((1, Q_BLOCK, D_KV), lambda i, j: (i, 0, 0)),
                  pl.BlockSpec((1, t_len, D_KV), lambda i, j: (i, 0, 0)),
                  pl.BlockSpec((1, t_len, D_KV), lambda i, j: (i, 0, 0))],
        out_specs=pl.BlockSpec((1, Q_BLOCK, dq), lambda i, j: (i, j, 0)),
        out_shape=jax.ShapeDtypeStruct((b, t_len, dq), BF16),
        scratch_shapes=[pltpu.VMEM((n_tiles * N_HEADS, Q_BLOCK, Q_BLOCK), F32),
                        pltpu.VMEM((n_tiles, Q_BLOCK, Q_BLOCK), F32)],
        compiler_params=_cparams(2),
        name="nsa_prompt",
    )(tab, q, gl, comp, slc_bf, win_bf)


def _page_copy(pt_ref, pool_ref, buf_ref, sem, b, j, page):
    return pltpu.make_async_copy(pool_ref.at[pt_ref[b, j]], buf_ref.at[pl.ds(j * page, page)], sem)


def _gather_pages_start(pt_ref, pool_ref, buf_ref, sem, b, n_pages, page):
    def go(j, carry):
        _page_copy(pt_ref, pool_ref, buf_ref, sem, b, j, page).start()
        return carry
    lax.fori_loop(0, n_pages, go, 0)


def _gather_pages_wait(pt_ref, pool_ref, buf_ref, sem, b, n_pages, page):
    def go(j, carry):
        _page_copy(pt_ref, pool_ref, buf_ref, sem, b, j, page).wait()
        return carry
    lax.fori_loop(0, n_pages, go, 0)


def _compress_pages_kernel(pt_ref, pool_ref, pe_ref, w1_ref, b1_ref, w2_ref, o_ref, x_ref, a_ref, b_ref, sem,
                           *, n_pages, page, n_chunk):
    b = pl.program_id(0)
    _gather_pages_start(pt_ref, pool_ref, x_ref, sem, b, n_pages, page)
    _gather_pages_wait(pt_ref, pool_ref, x_ref, sem, b, n_pages, page)
    _compress_rows(x_ref, n_chunk, pe_ref, w1_ref, b1_ref, w2_ref, o_ref.at[0], a_ref, b_ref)


def _compress_pages(page_table, pool, pe, w1, b1, w2):
    b, n_pages = page_table.shape
    n_pool, page, _ = pool.shape
    past = n_pages * page
    n_chunk = past // CMP_STRIDE
    assert n_chunk % CMP_MBLK == 0
    full = lambda a: pl.BlockSpec(a.shape, lambda i: (0,) * a.ndim)
    return pl.pallas_call(
        functools.partial(_compress_pages_kernel, n_pages=n_pages, page=page * ROW_TILES, n_chunk=n_chunk),
        grid=(b,),
        in_specs=[pl.BlockSpec(memory_space=pltpu.SMEM), pl.BlockSpec(memory_space=pl.ANY),
                  full(pe), full(w1), full(b1), full(w2)],
        out_specs=pl.BlockSpec((1, n_chunk, D_KV), lambda i: (i, 0, 0)),
        out_shape=jax.ShapeDtypeStruct((b, n_chunk, D_KV), F32),
        scratch_shapes=[pltpu.VMEM((past * ROW_TILES, LANES), F32),
                        pltpu.VMEM((n_chunk, PAIR_HID), F32), pltpu.VMEM((n_chunk + SUBLANES, PAIR_HID), F32),
                        pltpu.SemaphoreType.DMA(())],
        compiler_params=_cparams(1),
        name="compress_pages",
    )(page_table, pool.reshape(n_pool, page * ROW_TILES, LANES), pe, w1, b1, w2)


DEC_TILE = 512
DEC_ROWS = 16


def _row_bias(dist_row, tab_ref, bias_ref, col0):
    length = dist_row.shape[1]
    biases = _bias_from_bucket(_rel_bucket(dist_row), tab_ref, range(N_HEADS))
    zero = jnp.zeros((DEC_ROWS - HEADS_PER_GROUP, length), F32)
    for g in range(N_KV_GROUPS):
        rows = biases[g * HEADS_PER_GROUP:(g + 1) * HEADS_PER_GROUP] + [zero]
        bias_ref[g * DEC_ROWS:(g + 1) * DEC_ROWS, pl.ds(col0, length)] = jnp.concatenate(rows, axis=0)


def _nsa_step_kernel(tab_ref, pt_ref, q_ref, gl_ref, comp_ref, pool_ref, slc_new_ref, win_ref, win_new_ref, o_ref,
                     slc_buf, win_buf, bias_c_ref, bias_s_ref, bias_w_ref, sem,
                     *, n_pages, page, past, n_cmp, w_buf):
    b = pl.program_id(0)
    n_tiles = past // DEC_TILE + 1
    n_blocks = past // SEL_LEN + 1
    blk_lanes = -(-n_blocks // LANES) * LANES
    _gather_pages_start(pt_ref, pool_ref, slc_buf, sem, b, n_pages, page)

    lane_t = lax.broadcasted_iota(I32, (1, DEC_TILE), 1)

    @pl.when(b == 0)
    def _():
        lane_c = lax.broadcasted_iota(I32, (1, n_cmp), 1)
        _row_bias(past - (lane_c * CMP_STRIDE + (CMP_LEN - 1)), tab_ref, bias_c_ref, 0)

        def tile_bias(t, carry):
            k0 = pl.multiple_of(t * DEC_TILE, DEC_TILE)
            _row_bias(past - (k0 + lane_t), tab_ref, bias_s_ref, k0)
            return carry
        lax.fori_loop(0, n_tiles, tile_bias, 0)
        lane_w = lax.broadcasted_iota(I32, (1, win_buf.shape[0]), 1)
        _row_bias(w_buf - lane_w, tab_ref, bias_w_ref, 0)

    q_row = q_ref[0]
    gates = jax.nn.sigmoid(gl_ref[0])
    pad_rows = jnp.zeros((DEC_ROWS - HEADS_PER_GROUP, HEAD_DIM), BF16)
    qgs = []
    for g in range(N_KV_GROUPS):
        rows = [q_row[:, (g * HEADS_PER_GROUP + hp) * HEAD_DIM:(g * HEADS_PER_GROUP + hp + 1) * HEAD_DIM]
                for hp in range(HEADS_PER_GROUP)]
        qgs.append(jnp.concatenate(rows + [pad_rows], axis=0))

    comp = comp_ref[0].astype(BF16)
    lane_c8 = lax.broadcasted_iota(I32, (DEC_ROWS, n_cmp), 1)
    mask_c = lane_c8 < n_cmp - 1
    o_cs, p_sums = [], []
    for g in range(N_KV_GROUPS):
        c0 = g * HEAD_DIM
        s_c = _dot_nt(qgs[g], comp[:, c0:c0 + HEAD_DIM]) * SCALE + bias_c_ref[g * DEC_ROWS:(g + 1) * DEC_ROWS, :]
        p_c = _masked_softmax_rows(s_c, mask_c)
        o_cs.append(_dot(p_c.astype(BF16), comp[:, D_KV // 2 + c0:D_KV // 2 + c0 + HEAD_DIM]))
        p_sums.append(jnp.sum(p_c[0:HEADS_PER_GROUP], axis=0, keepdims=True))
    p_sum = jnp.concatenate(p_sums + [jnp.zeros((DEC_ROWS - N_KV_GROUPS, n_cmp), F32)], axis=0)

    c_start = lax.broadcasted_iota(I32, (n_cmp, blk_lanes), 0) * CMP_STRIDE
    s_start = lax.broadcasted_iota(I32, (n_cmp, blk_lanes), 1) * SEL_LEN
    c_idx = lax.broadcasted_iota(I32, (n_cmp, blk_lanes), 0)
    cover = jnp.where((c_start < s_start + SEL_LEN) & (c_start + CMP_LEN > s_start) & (c_idx < n_cmp - 1),
                      1.0, 0.0).astype(BF16)
    hi, mid, lo = _split3(p_sum)
    imp = _dot(hi, cover) + _dot(mid, cover) + _dot(lo, cover)
    blk = lax.broadcasted_iota(I32, (DEC_ROWS, blk_lanes), 1)
    cur = past // SEL_LEN
    forced = (blk == 0) | (blk == cur) | (blk == cur - 1)
    score = jnp.where(forced, FORCE, imp)
    score = jnp.where(blk < n_blocks, score, NEG)
    sel = jnp.zeros(score.shape, F32)
    big = jnp.int32(4 * blk_lanes)
    for _ in range(N_SEL):
        mx = jnp.max(score, axis=-1, keepdims=True)
        idx = jnp.min(jnp.where(score == mx, blk, big), axis=-1, keepdims=True)
        hit = blk == idx
        sel = jnp.where(hit & (mx > 0.5 * NEG), 1.0, sel)
        score = jnp.where(hit, -jnp.inf, score)
    sel_bf = sel.astype(BF16)

    _gather_pages_wait(pt_ref, pool_ref, slc_buf, sem, b, n_pages, page)
    slc_buf[past:past + DEC_TILE, :] = jnp.zeros((DEC_TILE, D_KV), F32)
    slc_buf[past:past + 1, :] = slc_new_ref[0]
    e_row = lax.broadcasted_iota(I32, (blk_lanes, DEC_TILE), 0)
    e_col = lax.broadcasted_iota(I32, (blk_lanes, DEC_TILE), 1) // SEL_LEN

    def sel_tile(t, carry):
        k0 = pl.multiple_of(t * DEC_TILE, DEC_TILE)
        expand = jnp.where(e_row == e_col + t * (DEC_TILE // SEL_LEN), 1.0, 0.0).astype(BF16)
        key_sel = _dot(sel_bf, expand)
        causal = jnp.where(k0 + lane_t <= past, 0.0, NEG)
        new = []
        for g in range(N_KV_GROUPS):
            m, l, acc = carry[g]
            c0 = g * HEAD_DIM
            kt = slc_buf[pl.ds(k0, DEC_TILE), c0:c0 + HEAD_DIM].astype(BF16)
            vt = slc_buf[pl.ds(k0, DEC_TILE), D_KV // 2 + c0:D_KV // 2 + c0 + HEAD_DIM].astype(BF16)
            madd = jnp.where(key_sel[g:g + 1] > 0.5, 0.0, NEG) + causal
            s = (_dot_nt(qgs[g], kt) * SCALE + bias_s_ref[g * DEC_ROWS:(g + 1) * DEC_ROWS, pl.ds(k0, DEC_TILE)]
                 + madd)
            m_new = jnp.maximum(m, jnp.max(s, axis=-1, keepdims=True))
            alpha = jnp.exp(m - m_new)
            p = jnp.exp(s - m_new)
            new.append((m_new, alpha * l + jnp.sum(p, axis=-1, keepdims=True), alpha * acc + _dot(p.astype(BF16), vt)))
        return tuple(new)

    init = tuple((jnp.full((DEC_ROWS, 1), NEG, F32), jnp.zeros((DEC_ROWS, 1), F32),
                  jnp.zeros((DEC_ROWS, HEAD_DIM), F32)) for _ in range(N_KV_GROUPS))
    fin = lax.fori_loop(0, n_tiles, sel_tile, init)
    o_ss = [acc / l for (_, l, acc) in fin]

    n_win = win_buf.shape[0]
    win_buf[0:w_buf, :] = win_ref[0]
    win_buf[w_buf:n_win, :] = jnp.zeros((n_win - w_buf, D_KV), F32)
    win_buf[w_buf:w_buf + 1, :] = win_new_ref[0]
    kw = win_buf[...].astype(BF16)
    dist_w = w_buf - lax.broadcasted_iota(I32, (DEC_ROWS, n_win), 1)
    mask_w = (dist_w >= 0) & (dist_w < WINDOW)
    o_ws = []
    for g in range(N_KV_GROUPS):
        c0 = g * HEAD_DIM
        s_w = _dot_nt(qgs[g], kw[:, c0:c0 + HEAD_DIM]) * SCALE + bias_w_ref[g * DEC_ROWS:(g + 1) * DEC_ROWS, :]
        p_w = _masked_softmax_rows(s_w, mask_w)
        o_ws.append(_dot(p_w.astype(BF16), kw[:, D_KV // 2 + c0:D_KV // 2 + c0 + HEAD_DIM]))

    outs = []
    for g in range(N_KV_GROUPS):
        for hp in range(HEADS_PER_GROUP):
            h = g * HEADS_PER_GROUP + hp
            outs.append(gates[:, 3 * h:3 * h + 1] * o_cs[g][hp:hp + 1]
                        + gates[:, 3 * h + 1:3 * h + 2] * o_ss[g][hp:hp + 1]
                        + gates[:, 3 * h + 2:3 * h + 3] * o_ws[g][hp:hp + 1])
    o_ref[0] = jnp.concatenate(outs, axis=1).astype(o_ref.dtype)


def _nsa_step(tab, page_table, q, gl, comp, pool, slc_new, win_cache, win_new):
    b, n_pages = page_table.shape
    page = pool.shape[1]
    past = n_pages * page
    n_cmp = comp.shape[1]
    w_buf = win_cache.shape[1]
    dq = q.shape[2]
    assert past % DEC_TILE == 0 and past % SEL_LEN == 0 and w_buf == WINDOW and n_cmp == past // CMP_STRIDE
    n_keys = past + DEC_TILE
    n_win = w_buf + LANES
    row3 = lambda w: pl.BlockSpec((1, 1, w), lambda i: (i, 0, 0))
    return pl.pallas_call(
        functools.partial(_nsa_step_kernel, n_pages=n_pages, page=page, past=past, n_cmp=n_cmp, w_buf=w_buf),
        grid=(b,),
        in_specs=[pl.BlockSpec(memory_space=pltpu.SMEM), pl.BlockSpec(memory_space=pltpu.SMEM),
                  row3(dq), row3(LANES),
                  pl.BlockSpec((1, n_cmp, D_KV), lambda i: (i, 0, 0)),
                  pl.BlockSpec(memory_space=pl.ANY),
                  row3(D_KV),
                  pl.BlockSpec((1, w_buf, D_KV), lambda i: (i, 0, 0)),
                  row3(D_KV)],
        out_specs=row3(dq),
        out_shape=jax.ShapeDtypeStruct((b, 1, dq), BF16),
        scratch_shapes=[pltpu.VMEM((n_keys, D_KV), F32),
                        pltpu.VMEM((n_win, D_KV), F32),
                        pltpu.VMEM((N_KV_GROUPS * DEC_ROWS, n_cmp), F32),
                        pltpu.VMEM((N_KV_GROUPS * DEC_ROWS, n_keys), F32),
                        pltpu.VMEM((N_KV_GROUPS * DEC_ROWS, n_win), F32),
                        pltpu.SemaphoreType.DMA(())],
        compiler_params=_cparams(1),
        name="nsa_step",
    )(tab, page_table, q, gl, comp, pool, slc_new, win_cache, win_new)


def _block_diag_pairs(w):
    n, blk, _ = w.shape
    w = w.reshape(n // 2, 2, blk, blk)
    z = jnp.zeros((n // 2, blk, blk), w.dtype)
    top = jnp.concatenate([w[:, 0], z], axis=2)
    bot = jnp.concatenate([z, w[:, 1]], axis=2)
    return jnp.concatenate([top, bot], axis=1).astype(BF16)


def _prep_weights(w_in_a, conv_w, conv_b, w_rg, b_rg, w_ig, b_ig, lru_lambda, w_in_b, w_kv_shared, cmp_pe, cmp_w1,
                  cmp_b1, cmp_w2, w_mem_kv, w_out, ln1_g, ln1_b, ln2_g, ln2_b, w_router_g, b_router_g,
                  w_router_e, b_router_e, w_exp_gate, w_exp_up, w_exp_down):
    depth, d_model, _ = w_out.shape
    n_a = w_in_a.shape[0]
    nq = N_HEADS * HEAD_DIM
    n_gl = 3 * N_HEADS
    hid = EXPERTS_PER_GROUP * D_EXPERT
    p = {}
    p["w_in_a"] = w_in_a.astype(BF16)
    p["w_in_b"] = jnp.concatenate(
        [w_in_b[:, :, :nq], w_in_b[:, :, nq + n_gl:], w_in_b[:, :, nq:nq + n_gl],
         jnp.zeros((w_in_b.shape[0], d_model, LANES - n_gl), w_in_b.dtype)], axis=2).astype(BF16)
    p["w_kv"] = w_kv_shared.astype(BF16)
    p["w_mem"] = w_mem_kv.transpose(1, 0, 2).reshape(d_model, depth * 2 * D_MEM).astype(BF16)
    p["wo_a"] = w_out[:, :D_RNN].astype(BF16)
    p["wo_m"] = w_out[:, D_RNN:].astype(BF16)
    p["conv_w"] = conv_w
    p["conv_b"] = conv_b[:, None]
    p["w_rg"] = jnp.stack([_block_diag_pairs(w_rg[l]) for l in range(n_a)])
    p["w_ig"] = jnp.stack([_block_diag_pairs(w_ig[l]) for l in range(n_a)])
    p["b_rg"] = b_rg[:, None]
    p["b_ig"] = b_ig[:, None]
    p["lam"] = lru_lambda[:, None]
    p["pe"], p["cmp_w1"], p["cmp_b1"], p["cmp_w2"] = _compress_weights(cmp_pe, cmp_w1, cmp_b1, cmp_w2)
    p["ln1_g"], p["ln1_b"], p["ln2_g"], p["ln2_b"] = ln1_g[:, None], ln1_b[:, None], ln2_g[:, None], ln2_b[:, None]
    n_r = N_GROUPS + N_EXPERTS
    wr = jnp.concatenate([w_router_g, w_router_e, jnp.zeros((depth, d_model, LANES - n_r), F32)], axis=2)
    p["wr_hi"] = wr.astype(BF16)
    p["wr_lo"] = (wr - p["wr_hi"].astype(F32)).astype(BF16)
    p["rb"] = jnp.concatenate([b_router_g, b_router_e, jnp.zeros((depth, LANES - n_r), F32)], axis=1)[:, None]

    def by_group(w):
        w = w.astype(BF16).reshape(depth, N_GROUPS, EXPERTS_PER_GROUP, d_model, D_EXPERT)
        return w.transpose(0, 1, 3, 2, 4).reshape(depth, N_GROUPS, d_model, hid)
    p["w_gu"] = jnp.concatenate([by_group(w_exp_gate), by_group(w_exp_up)], axis=3)
    p["w_dn"] = w_exp_down.astype(BF16).reshape(depth, N_GROUPS, hid, d_model)
    return p


def _trunk_tail(x, mix_a, mix_m, p, l, alpha, tm):
    x = _out_ln(x, mix_a, mix_m, p["wo_a"][l], p["wo_m"][l], p["ln1_g"][l], p["ln1_b"][l], alpha, tm)
    return _moe_ln(x, p["wr_hi"][l], p["wr_lo"][l], p["rb"][l], p["w_gu"][l], p["w_dn"][l],
                   p["ln2_g"][l], p["ln2_b"][l], alpha, tm)


def kernel(x_prompt, x_sample, mem_prompt, cache_cmp_kv, cache_slc_kv, cache_win_kv, cache_mem_kv, state_lru_h, state_conv, page_table, w_in_a, conv_w, conv_b, w_rg, b_rg, w_ig, b_ig, lru_lambda, w_in_b, w_kv_shared, cmp_pe, cmp_w1, cmp_b1, cmp_w2, rel_bias, w_mem_kv, w_out, ln1_g, ln1_b, ln2_g, ln2_b, w_router_g, b_router_g, w_router_e, b_router_e, w_exp_gate, w_exp_up, w_exp_down):
    p = _prep_weights(w_in_a, conv_w, conv_b, w_rg, b_rg, w_ig, b_ig, lru_lambda, w_in_b, w_kv_shared, cmp_pe,
                      cmp_w1, cmp_b1, cmp_w2, w_mem_kv, w_out, ln1_g, ln1_b, ln2_g, ln2_b, w_router_g,
                      b_router_g, w_router_e, b_router_e, w_exp_gate, w_exp_up, w_exp_down)
    depth, d_model, _ = w_out.shape
    n_a = w_in_a.shape[0]
    alpha = (2 * depth) ** 0.25
    kv_shape = (2, N_KV_GROUPS, HEAD_DIM)
    a_outs = [(0, D_RNN, F32), (D_RNN, D_RNN, F32), (2 * D_RNN, D_MEM, BF16)]
    b_outs = [(0, D_RNN, BF16), (D_RNN, D_MEM, BF16), (D_RNN + D_MEM, LANES, F32)]
    kv_outs = [(0, D_KV, F32), (D_KV, D_KV, F32), (2 * D_KV, D_KV, F32)]

    bp, t_len, _ = x_prompt.shape
    n_mem = mem_prompt.shape[1]
    tm = 512
    mem_rows = _proj(mem_prompt.reshape(bp * n_mem, d_model), p["w_mem"],
                     [(l * 2 * D_MEM, 2 * D_MEM, F32) for l in range(depth)], tm, "mem_kv_proj")
    p_mem = jnp.stack(mem_rows).reshape(depth, bp, n_mem, 2 * D_MEM)
    x = x_prompt.reshape(bp * t_len, d_model)
    lru_p, conv_p = [], []
    for l in range(depth):
        if l < n_a:
            gate, xr, mq = _proj(x, p["w_in_a"][l], a_outs, tm, "in_proj_a")
            mix_a, h_t, buf = _rglru_prompt(gate.reshape(bp, t_len, D_RNN), xr.reshape(bp, t_len, D_RNN),
                                            p["conv_w"][l], p["conv_b"][l], p["w_rg"][l], p["w_ig"][l],
                                            p["b_rg"][l], p["b_ig"][l], p["lam"][l])
            lru_p.append(h_t[:, 0])
            conv_p.append(buf)
        else:
            if l == n_a:
                cmp_rows, slc_rows, win_rows, slc_bf, win_bf = _proj(
                    x, p["w_kv"], kv_outs + [(D_KV, D_KV, BF16), (2 * D_KV, D_KV, BF16)], tm, "kv_proj")
                comp = _compress_prompt(cmp_rows.reshape(bp, t_len, D_KV), p["pe"], p["cmp_w1"], p["cmp_b1"],
                                        p["cmp_w2"])
            q, mq, gl = _proj(x, p["w_in_b"][l - n_a], b_outs, tm, "in_proj_b")
            mix_a = _nsa_prompt(rel_bias, q.reshape(bp, t_len, D_RNN), gl.reshape(bp, t_len, LANES), comp,
                                slc_bf.reshape(bp, t_len, D_KV), win_bf.reshape(bp, t_len, D_KV))
        mix_m = _mem_attn(mq.reshape(bp, t_len, D_MEM), p_mem, l)
        x = _trunk_tail(x, mix_a.reshape(bp * t_len, D_RNN), mix_m.reshape(bp * t_len, D_MEM), p, l, alpha, tm)
    y_prompt = x.reshape(bp, t_len, d_model)
    p_cmp_kv = cmp_rows.reshape((bp, t_len) + kv_shape)
    p_slc_kv = slc_rows.reshape((bp, t_len) + kv_shape)
    w_keep = min(WINDOW, t_len)
    p_win_kv = win_rows.reshape((bp, t_len) + kv_shape)[:, t_len - w_keep:]
    p_mem_kv = p_mem.reshape(depth, bp, n_mem, 2, N_MEM_HEADS, HEAD_DIM)
    p_lru_h = jnp.stack(lru_p)
    p_conv = jnp.stack(conv_p)

    bd, s_len, _ = x_sample.shape
    assert s_len == 1
    n_pool, page = cache_cmp_kv.shape[:2]
    w_buf = cache_win_kv.shape[1]
    mem_cache = cache_mem_kv.reshape(depth, bd, cache_mem_kv.shape[2], 2 * D_MEM)
    x = x_sample.reshape(bd, d_model)
    lru_s, conv_s = [], []
    for l in range(depth):
        if l < n_a:
            gate, xr, mq = _proj(x, p["w_in_a"][l], a_outs, bd, "in_proj_a_step")
            mix_a, h_t, buf = _rglru_step(gate, xr, state_conv[l].transpose(1, 0, 2), state_lru_h[l],
                                          p["conv_w"][l], p["conv_b"][l], p["w_rg"][l], p["w_ig"][l],
                                          p["b_rg"][l], p["b_ig"][l], p["lam"][l])
            lru_s.append(h_t)
            conv_s.append(buf.transpose(1, 0, 2))
        else:
            if l == n_a:
                s_cmp, s_slc, s_win = _proj(x, p["w_kv"], kv_outs, bd, "kv_proj_step")
                comp_s = _compress_pages(page_table, cache_cmp_kv.reshape(n_pool, page, D_KV), p["pe"],
                                         p["cmp_w1"], p["cmp_b1"], p["cmp_w2"])
            q, mq, gl = _proj(x, p["w_in_b"][l - n_a], b_outs, bd, "in_proj_b_step")
            mix_a = _nsa_step(rel_bias, page_table, q.reshape(bd, 1, D_RNN), gl.reshape(bd, 1, LANES), comp_s,
                              cache_slc_kv.reshape(n_pool, page, D_KV), s_slc.reshape(bd, 1, D_KV),
                              cache_win_kv.reshape(bd, w_buf, D_KV), s_win.reshape(bd, 1, D_KV))
            mix_a = mix_a.reshape(bd, D_RNN)
        mix_m = _mem_attn(mq.reshape(bd, 1, D_MEM), mem_cache, l).reshape(bd, D_MEM)
        x = _trunk_tail(x, mix_a, mix_m, p, l, alpha, bd)
    y_sample = x.reshape(bd, 1, d_model)
    s_cmp_kv = s_cmp.reshape((bd, 1) + kv_shape)
    s_slc_kv = s_slc.reshape((bd, 1) + kv_shape)
    s_win_kv = jnp.concatenate([cache_win_kv, s_win.reshape((bd, 1) + kv_shape)], axis=1)[:, 1:]
    s_lru_h = jnp.stack(lru_s)
    s_conv = jnp.stack(conv_s)
    return (y_prompt, y_sample, p_cmp_kv, p_slc_kv, p_win_kv, p_mem_kv, p_lru_h, p_conv,
            s_cmp_kv, s_slc_kv, s_win_kv, s_lru_h, s_conv)
```

```python
import functools
import math

import jax
import jax.numpy as jnp
from jax import lax
from jax.experimental import pallas as pl
from jax.experimental.pallas import tpu as pltpu

F32 = jnp.float32
BF16 = jnp.bfloat16
I32 = jnp.int32

HEAD_DIM = 64
N_MEM_HEADS = 4
D_MEM = N_MEM_HEADS * HEAD_DIM
N_HEADS = 12
D_RNN = N_HEADS * HEAD_DIM
N_KV_GROUPS = 4
HEADS_PER_GROUP = N_HEADS // N_KV_GROUPS
D_KV = 2 * N_KV_GROUPS * HEAD_DIM
CONV_W = 4
LRU_C = 8.0
CMP_LEN = 32
CMP_STRIDE = 16
CMP_HID = 2 * HEAD_DIM
SEL_LEN = 64
N_SEL = 16
WINDOW = 512
N_BUCKETS = 32
MAX_DISTANCE = 1024
N_GROUPS = 4
EXPERTS_PER_GROUP = 8
N_EXPERTS = N_GROUPS * EXPERTS_PER_GROUP
D_EXPERT = 128
Q_BLOCK = 128
LN_EPS = 1e-5
NEG = -1e30
FORCE = 1e9
SCALE = HEAD_DIM ** -0.5

LANES = 128
SUBLANES = 8
VMEM_LIMIT_BYTES = 56 * 1024 * 1024


def _cparams(n_axes):
    return pltpu.CompilerParams(dimension_semantics=("arbitrary",) * n_axes,
                                vmem_limit_bytes=VMEM_LIMIT_BYTES)


def _dot(a, b):
    return jnp.dot(a, b, preferred_element_type=F32)


def _dot_nt(a, b):
    return lax.dot_general(a, b, (((1,), (1,)), ((), ())), preferred_element_type=F32)


def _layer_norm(v, g, b):
    mu = jnp.mean(v, axis=-1, keepdims=True)
    d = v - mu
    var = jnp.mean(d * d, axis=-1, keepdims=True)
    return d * lax.rsqrt(var + LN_EPS) * g + b


def _rel_bucket(dist):
    n = jnp.maximum(dist, 0)
    max_exact = N_BUCKETS // 2
    nf = jnp.maximum(n, 1).astype(F32)
    large = max_exact + (jnp.log(nf / max_exact) / math.log(MAX_DISTANCE / max_exact)
                         * (N_BUCKETS - max_exact)).astype(I32)
    large = jnp.minimum(large, N_BUCKETS - 1)
    return jnp.where(n < max_exact, n, large)


def _bias_from_bucket(bucket, tab_ref, heads):
    masks = [bucket == k for k in range(1, N_BUCKETS)]
    out = []
    for h in heads:
        acc = jnp.full(bucket.shape, tab_ref[0, h], F32)
        for k in range(1, N_BUCKETS):
            acc = jnp.where(masks[k - 1], tab_ref[k, h], acc)
        out.append(acc)
    return out


def _proj_kernel(x_ref, w_ref, *o_refs, cols):
    y = _dot(x_ref[...].astype(BF16), w_ref[...])
    for o_ref, (off, n) in zip(o_refs, cols):
        o_ref[...] = y[:, off:off + n].astype(o_ref.dtype)


def _proj(x, w, outs, tm, name):
    m, k = x.shape
    n = w.shape[1]
    tm = min(tm, m)
    assert m % tm == 0 and all(off + wd <= n for off, wd, _ in outs)
    return pl.pallas_call(
        functools.partial(_proj_kernel, cols=tuple((off, wd) for off, wd, _ in outs)),
        grid=(m // tm,),
        in_specs=[pl.BlockSpec((tm, k), lambda i: (i, 0)),
                  pl.BlockSpec((k, n), lambda i: (0, 0))],
        out_specs=[pl.BlockSpec((tm, wd), lambda i: (i, 0)) for _, wd, _ in outs],
        out_shape=[jax.ShapeDtypeStruct((m, wd), dt) for _, wd, dt in outs],
        compiler_params=_cparams(1),
        name=name,
    )(x, w)


def _softplus(z):
    return jnp.maximum(z, 0.0) + jnp.log1p(jnp.exp(-jnp.abs(z)))


def _lru_gates(xc, wr_ref, wi_ref, br, bi, lam):
    xcb = xc.astype(BF16)
    nb = D_RNN // LANES
    r_l = jnp.concatenate([_dot(xcb[:, j * LANES:(j + 1) * LANES], wr_ref[j]) for j in range(nb)], axis=1)
    i_l = jnp.concatenate([_dot(xcb[:, j * LANES:(j + 1) * LANES], wi_ref[j]) for j in range(nb)], axis=1)
    r = jax.nn.sigmoid(r_l + br)
    i = jax.nn.sigmoid(i_l + bi)
    log_a = -LRU_C * r * _softplus(-lam)
    a = jnp.exp(log_a)
    u = jnp.sqrt(1.0 - a * a) * i * xc
    return a, u


def _rglru_prompt_kernel(gate_ref, xr_ref, cw_ref, cb_ref, wr_ref, wi_ref, br_ref, bi_ref, lam_ref,
                         y_ref, ht_ref, buf_ref, xp_ref, *, t_len, chunk):
    d = D_RNN
    pad = SUBLANES
    xp_ref[0:pad, :] = jnp.zeros((pad, d), F32)
    xp_ref[pad:pad + t_len, :] = xr_ref[0]
    buf_ref[0] = xr_ref[0, t_len - (CONV_W - 1):t_len, :]
    cw = cw_ref[...]
    cb = cb_ref[...]
    br = br_ref[...]
    bi = bi_ref[...]
    lam = lam_ref[...]
    row_in_tile = lax.broadcasted_iota(I32, (chunk, d), 0) & (SUBLANES - 1)
    h = jnp.zeros((1, d), F32)
    for c in range(t_len // chunk):
        base = c * chunk
        xc = cb
        for k in range(CONV_W):
            lo = pad - (CONV_W - 1) + k + base
            xc = xc + xp_ref[lo:lo + chunk, :] * cw[k:k + 1, :]
        a, u = _lru_gates(xc, wr_ref, wi_ref, br, bi, lam)
        for k in (1, 2, 4):
            a_s = pltpu.roll(a, k, axis=0)
            u_s = pltpu.roll(u, k, axis=0)
            m = row_in_tile >= k
            u = jnp.where(m, a * u_s + u, u)
            a = jnp.where(m, a * a_s, a)
        hs = []
        for j in range(chunk // SUBLANES):
            h_t = a[j * SUBLANES:(j + 1) * SUBLANES] * h + u[j * SUBLANES:(j + 1) * SUBLANES]
            h = h_t[SUBLANES - 1:SUBLANES]
            hs.append(h_t)
        hfull = jnp.concatenate(hs, axis=0)
        g = jax.nn.gelu(gate_ref[0, base:base + chunk, :])
        y_ref[0, base:base + chunk, :] = (g * hfull).astype(y_ref.dtype)
    ht_ref[0] = h


def _rglru_prompt(gate, xr, cw, cb, wr_bd, wi_bd, br, bi, lam):
    b, t_len, d = xr.shape
    chunk = min(256, t_len)
    full = lambda shape: pl.BlockSpec(shape, lambda i: (0,) * len(shape))
    return pl.pallas_call(
        functools.partial(_rglru_prompt_kernel, t_len=t_len, chunk=chunk),
        grid=(b,),
        in_specs=[pl.BlockSpec((1, t_len, d), lambda i: (i, 0, 0)),
                  pl.BlockSpec((1, t_len, d), lambda i: (i, 0, 0)),
                  full((CONV_W, d)), full((1, d)), full(wr_bd.shape), full(wi_bd.shape),
                  full((1, d)), full((1, d)), full((1, d))],
        out_specs=[pl.BlockSpec((1, t_len, d), lambda i: (i, 0, 0)),
                   pl.BlockSpec((1, 1, d), lambda i: (i, 0, 0)),
                   pl.BlockSpec((1, CONV_W - 1, d), lambda i: (i, 0, 0))],
        out_shape=[jax.ShapeDtypeStruct((b, t_len, d), BF16),
                   jax.ShapeDtypeStruct((b, 1, d), F32),
                   jax.ShapeDtypeStruct((b, CONV_W - 1, d), F32)],
        scratch_shapes=[pltpu.VMEM((t_len + SUBLANES, d), F32)],
        compiler_params=_cparams(1),
        name="rglru_prompt",
    )(gate, xr, cw, cb, wr_bd, wi_bd, br, bi, lam)


def _rglru_step_kernel(gate_ref, xr_ref, conv_ref, h0_ref, cw_ref, cb_ref, wr_ref, wi_ref, br_ref, bi_ref,
                       lam_ref, y_ref, ht_ref, buf_ref):
    xr = xr_ref[...]
    cw = cw_ref[...]
    xc = cb_ref[...] + xr * cw[CONV_W - 1:CONV_W, :]
    for k in range(CONV_W - 1):
        xc = xc + conv_ref[k] * cw[k:k + 1, :]
    a, u = _lru_gates(xc, wr_ref, wi_ref, br_ref[...], bi_ref[...], lam_ref[...])
    h = a * h0_ref[...] + u
    ht_ref[...] = h
    y_ref[...] = (jax.nn.gelu(gate_ref[...]) * h).astype(y_ref.dtype)
    for k in range(CONV_W - 2):
        buf_ref[k] = conv_ref[k + 1]
    buf_ref[CONV_W - 2] = xr


def _rglru_step(gate, xr, conv_t, h0, cw, cb, wr_bd, wi_bd, br, bi, lam):
    b, d = xr.shape
    return pl.pallas_call(
        _rglru_step_kernel,
        out_shape=[jax.ShapeDtypeStruct((b, d), BF16),
                   jax.ShapeDtypeStruct((b, d), F32),
                   jax.ShapeDtypeStruct((CONV_W - 1, b, d), F32)],
        compiler_params=pltpu.CompilerParams(vmem_limit_bytes=VMEM_LIMIT_BYTES),
        name="rglru_step",
    )(gate, xr, conv_t, h0, cw, cb, wr_bd, wi_bd, br, bi, lam)


def _mem_attn_kernel(q_ref, kv_ref, o_ref, *, rows):
    q = q_ref[0]
    if rows < SUBLANES:
        q = jnp.broadcast_to(q, (SUBLANES, D_MEM))
    kv = kv_ref[0, 0].astype(BF16)
    outs = []
    for h in range(N_MEM_HEADS):
        qh = q[:, h * HEAD_DIM:(h + 1) * HEAD_DIM]
        kh = kv[:, h * HEAD_DIM:(h + 1) * HEAD_DIM]
        vh = kv[:, D_MEM + h * HEAD_DIM:D_MEM + (h + 1) * HEAD_DIM]
        s = _dot_nt(qh, kh) * SCALE
        m = jnp.max(s, axis=-1, keepdims=True)
        e = jnp.exp(s - m)
        p = e / jnp.sum(e, axis=-1, keepdims=True)
        outs.append(_dot(p.astype(BF16), vh))
    o = jnp.concatenate(outs, axis=1)
    o_ref[0] = o[:rows].astype(o_ref.dtype)


def _mem_attn(q, mem_kv, layer):
    b, t_len, _ = q.shape
    n_mem = mem_kv.shape[2]
    tq = min(t_len, 512)
    return pl.pallas_call(
        functools.partial(_mem_attn_kernel, rows=tq),
        grid=(b, t_len // tq),
        in_specs=[pl.BlockSpec((1, tq, D_MEM), lambda i, j: (i, j, 0)),
                  pl.BlockSpec((1, 1, n_mem, 2 * D_MEM), lambda i, j: (layer, i, 0, 0))],
        out_specs=pl.BlockSpec((1, tq, D_MEM), lambda i, j: (i, j, 0)),
        out_shape=jax.ShapeDtypeStruct((b, t_len, D_MEM), BF16),
        compiler_params=_cparams(2),
        name="mem_attn",
    )(q, mem_kv)


def _out_ln_kernel(x_ref, ma_ref, mm_ref, wa_ref, wm_ref, g_ref, b_ref, o_ref, *, alpha):
    sub = _dot(ma_ref[...], wa_ref[...]) + _dot(mm_ref[...], wm_ref[...])
    o_ref[...] = _layer_norm(alpha * x_ref[...] + sub, g_ref[...], b_ref[...])


def _out_ln(x, mix_a, mix_m, wo_a, wo_m, g, b, alpha, tm):
    m, d = x.shape
    da = mix_a.shape[1]
    dm = mix_m.shape[1]
    return pl.pallas_call(
        functools.partial(_out_ln_kernel, alpha=alpha),
        grid=(m // tm,),
        in_specs=[pl.BlockSpec((tm, d), lambda i: (i, 0)),
                  pl.BlockSpec((tm, da), lambda i: (i, 0)),
                  pl.BlockSpec((tm, dm), lambda i: (i, 0)),
                  pl.BlockSpec((da, d), lambda i: (0, 0)),
                  pl.BlockSpec((dm, d), lambda i: (0, 0)),
                  pl.BlockSpec((1, d), lambda i: (0, 0)),
                  pl.BlockSpec((1, d), lambda i: (0, 0))],
        out_specs=pl.BlockSpec((tm, d), lambda i: (i, 0)),
        out_shape=jax.ShapeDtypeStruct((m, d), F32),
        compiler_params=_cparams(1),
        name="out_ln",
    )(x, mix_a, mix_m, wo_a, wo_m, g, b)


def _route(logits):
    lane = lax.broadcasted_iota(I32, logits.shape, 1)
    big = jnp.int32(4 * LANES)
    is_g = lane < N_GROUPS
    lg = jnp.where(is_g, logits, -jnp.inf)
    gmax = jnp.max(lg, axis=-1, keepdims=True)
    grp = jnp.min(jnp.where(lg == gmax, lane, big), axis=-1, keepdims=True)
    gsum = jnp.sum(jnp.where(is_g, jnp.exp(logits - gmax), 0.0), axis=-1, keepdims=True)
    g_w = 1.0 / gsum
    e_idx = lane - N_GROUPS
    in_grp = (lane >= N_GROUPS) & (lane < N_GROUPS + N_EXPERTS) & ((e_idx >> 3) == grp)
    v = jnp.where(in_grp, logits, -jnp.inf)
    v1 = jnp.max(v, axis=-1, keepdims=True)
    i1 = jnp.min(jnp.where(v == v1, lane, big), axis=-1, keepdims=True)
    vr = jnp.where(lane == i1, -jnp.inf, v)
    v2 = jnp.max(vr, axis=-1, keepdims=True)
    i2 = jnp.min(jnp.where(vr == v2, lane, big), axis=-1, keepdims=True)
    e2 = jnp.exp(v2 - v1)
    w1 = g_w / (1.0 + e2)
    w2 = g_w * e2 / (1.0 + e2)
    return jnp.where(lane == i1, w1, 0.0) + jnp.where(lane == i2, w2, 0.0)


def _moe_kernel(x_ref, wr_ref, rb_ref, wgu_ref, wd_ref, g_ref, b_ref, o_ref,
                xb_ref, comb_ref, acc_ref, *, alpha):
    grp = pl.program_id(1)
    hid = EXPERTS_PER_GROUP * D_EXPERT

    @pl.when(grp == 0)
    def _():
        xh = x_ref[...].astype(BF16)
        logits = _dot(xh, wr_ref[...]) + rb_ref[...]
        xb_ref[...] = xh
        comb_ref[...] = _route(logits)
        acc_ref[...] = jnp.zeros(acc_ref.shape, F32)

    xb = xb_ref[...]
    gu = _dot(xb, wgu_ref[0])
    gate = gu[:, :hid]
    up = gu[:, hid:]
    hdn = gate * jax.nn.sigmoid(gate) * up
    comb = comb_ref[...]
    lane = lax.broadcasted_iota(I32, comb.shape, 1)
    scale = []
    for e in range(EXPERTS_PER_GROUP):
        col = jnp.sum(jnp.where(lane == N_GROUPS + grp * EXPERTS_PER_GROUP + e, comb, 0.0),
                      axis=-1, keepdims=True)
        scale.append(jnp.broadcast_to(col, (comb.shape[0], D_EXPERT)))
    hdn = hdn * jnp.concatenate(scale, axis=1)
    acc_ref[...] += _dot(hdn.astype(BF16), wd_ref[0])

    @pl.when(grp == N_GROUPS - 1)
    def _():
        o_ref[...] = _layer_norm(alpha * x_ref[...] + acc_ref[...], g_ref[...], b_ref[...])


def _moe_ln(x, wr, rb, wgu, wd, g, b, alpha, tm):
    m, d = x.shape
    hid = EXPERTS_PER_GROUP * D_EXPERT
    return pl.pallas_call(
        functools.partial(_moe_kernel, alpha=alpha),
        grid=(m // tm, N_GROUPS),
        in_specs=[pl.BlockSpec((tm, d), lambda i, j: (i, 0)),
                  pl.BlockSpec((d, LANES), lambda i, j: (0, 0)),
                  pl.BlockSpec((1, LANES), lambda i, j: (0, 0)),
                  pl.BlockSpec((1, d, 2 * hid), lambda i, j: (j, 0, 0)),
                  pl.BlockSpec((1, hid, d), lambda i, j: (j, 0, 0)),
                  pl.BlockSpec((1, d), lambda i, j: (0, 0)),
                  pl.BlockSpec((1, d), lambda i, j: (0, 0))],
        out_specs=pl.BlockSpec((tm, d), lambda i, j: (i, 0)),
        out_shape=jax.ShapeDtypeStruct((m, d), F32),
        scratch_shapes=[pltpu.VMEM((tm, d), BF16), pltpu.VMEM((tm, LANES), F32), pltpu.VMEM((tm, d), F32)],
        compiler_params=_cparams(2),
        name="moe_ln",
    )(x, wr, rb, wgu, wd, g, b)


CMP_MBLK = 128
ROW_TILES = D_KV // LANES
PAIR_HID = 2 * CMP_HID


def _compress_weights(pe, w1, b1, w2):
    w1 = w1.reshape(2, CMP_LEN, HEAD_DIM, CMP_HID)
    z1 = jnp.zeros_like(w1)
    w1p = jnp.concatenate([jnp.concatenate([w1, z1], axis=3), jnp.concatenate([z1, w1], axis=3)], axis=2)
    z2 = jnp.zeros_like(w2)
    w2p = jnp.concatenate([jnp.concatenate([w2, z2], axis=2), jnp.concatenate([z2, w2], axis=2)], axis=1)
    return (jnp.concatenate([pe, pe], axis=2), w1p.astype(BF16),
            jnp.concatenate([b1, b1], axis=1)[:, None], w2p.astype(BF16))


def _compress_rows(x_ref, n_chunk, pe_ref, w1_ref, b1_ref, w2_ref, out_ref, a_ref, b_ref):
    n_blk = n_chunk // CMP_MBLK
    step = CMP_STRIDE * ROW_TILES
    for kv in range(2):
        for pair in range(N_KV_GROUPS // 2):
            tile = kv * (N_KV_GROUPS // 2) + pair

            def fill(mb, carry, kv=kv, tile=tile):
                row0 = pl.multiple_of(mb * (CMP_MBLK * step), CMP_MBLK * step)
                acc_a = jnp.zeros((CMP_MBLK, PAIR_HID), F32)
                acc_b = jnp.zeros((CMP_MBLK, PAIR_HID), F32)
                for r in range(CMP_STRIDE):
                    xg = x_ref[pl.ds(row0 + r * ROW_TILES + tile, CMP_MBLK, stride=step), :]
                    acc_a += _dot((xg + pe_ref[kv, r:r + 1, :]).astype(BF16), w1_ref[kv, r])
                    acc_b += _dot((xg + pe_ref[kv, CMP_STRIDE + r:CMP_STRIDE + r + 1, :]).astype(BF16),
                                  w1_ref[kv, CMP_STRIDE + r])
                m0 = pl.multiple_of(mb * CMP_MBLK, CMP_MBLK)
                a_ref[pl.ds(m0, CMP_MBLK), :] = acc_a
                b_ref[pl.ds(m0, CMP_MBLK), :] = acc_b
                return carry

            lax.fori_loop(0, n_blk, fill, 0)
            b_ref[n_chunk:n_chunk + SUBLANES, :] = jnp.zeros((SUBLANES, PAIR_HID), F32)
            for mb in range(n_blk):
                m0 = mb * CMP_MBLK
                hid = jax.nn.gelu(a_ref[m0:m0 + CMP_MBLK, :] + b_ref[m0 + 1:m0 + 1 + CMP_MBLK, :] + b1_ref[kv])
                out_ref[m0:m0 + CMP_MBLK, tile * LANES:(tile + 1) * LANES] = _dot(hid.astype(BF16), w2_ref[kv])


def _compress_prompt_kernel(x_ref, pe_ref, w1_ref, b1_ref, w2_ref, o_ref, a_ref, b_ref, *, n_chunk):
    _compress_rows(x_ref.at[0], n_chunk, pe_ref, w1_ref, b1_ref, w2_ref, o_ref.at[0], a_ref, b_ref)


def _compress_prompt(rows, pe, w1, b1, w2):
    b, t_len, _ = rows.shape
    n_chunk = t_len // CMP_STRIDE
    assert n_chunk % CMP_MBLK == 0
    full = lambda a: pl.BlockSpec(a.shape, lambda i: (0,) * a.ndim)
    return pl.pallas_call(
        functools.partial(_compress_prompt_kernel, n_chunk=n_chunk),
        grid=(b,),
        in_specs=[pl.BlockSpec((1, t_len * ROW_TILES, LANES), lambda i: (i, 0, 0)),
                  full(pe), full(w1), full(b1), full(w2)],
        out_specs=pl.BlockSpec((1, n_chunk, D_KV), lambda i: (i, 0, 0)),
        out_shape=jax.ShapeDtypeStruct((b, n_chunk, D_KV), F32),
        scratch_shapes=[pltpu.VMEM((n_chunk, PAIR_HID), F32), pltpu.VMEM((n_chunk + SUBLANES, PAIR_HID), F32)],
        compiler_params=_cparams(1),
        name="compress_prompt",
    )(rows.reshape(b, t_len * ROW_TILES, LANES), pe, w1, b1, w2)


def _masked_softmax_rows(s, mask):
    sm = jnp.where(mask, s, NEG)
    m = jnp.max(sm, axis=-1, keepdims=True)
    e = jnp.where(mask, jnp.exp(sm - m), 0.0)
    tot = jnp.sum(e, axis=-1, keepdims=True)
    return e * jnp.where(tot > 0.0, 1.0 / tot, 0.0)


WIN_TILES = WINDOW // Q_BLOCK


KEY_TILE = 512
KEY_SUB = KEY_TILE // Q_BLOCK


def _attend_tiles_t(qts, k_ref, vt_ref, t_lo, t_hi, i_blk, bias_ref, bias_index, mask_fn):
    cols = qts[0].shape[1]

    def body(t, carry):
        k0 = pl.multiple_of(t * KEY_TILE, KEY_TILE)
        bases = [bias_index(i_blk - (t * KEY_SUB + u)) for u in range(KEY_SUB)]
        new = []
        for g, qt in enumerate(qts):
            m, l, acc = carry[g]
            kt = k_ref[0, g, pl.ds(k0, KEY_TILE), :]
            vt = vt_ref[0, g * HEAD_DIM:(g + 1) * HEAD_DIM, pl.ds(k0, KEY_TILE)]
            heads = [g * HEADS_PER_GROUP + hp for hp in range(HEADS_PER_GROUP)]
            bias = jnp.concatenate(
                [jnp.concatenate([bias_ref[base + h] for h in heads], axis=1) for base in bases], axis=0)
            s = _dot(kt, qt) + bias
            extra = mask_fn(t, g)
            if extra is not None:
                s = s + jnp.concatenate([extra] * HEADS_PER_GROUP, axis=1)
            m_new = jnp.maximum(m, jnp.max(s, axis=0, keepdims=True))
            alpha = jnp.exp(m - m_new)
            p = jnp.exp(s - m_new)
            new.append((m_new, alpha * l + jnp.sum(p, axis=0, keepdims=True), alpha * acc + _dot(vt, p.astype(BF16))))
        return tuple(new)

    init = tuple((jnp.full((1, cols), NEG, F32), jnp.zeros((1, cols), F32), jnp.zeros((HEAD_DIM, cols), F32))
                 for _ in qts)
    fin = lax.fori_loop(t_lo, t_hi, body, init)
    return [acc / l for (_, l, acc) in fin]


def _nsa_prompt_kernel(tab_ref, q_ref, gl_ref, ck_ref, cvt_ref, sk_ref, svt_ref, wk_ref, wvt_ref, o_ref,
                       bias_ref, cbias_ref, sel_ref, *, n_tiles):
    b = pl.program_id(0)
    i = pl.program_id(1)
    kk = lax.broadcasted_iota(I32, (Q_BLOCK, Q_BLOCK), 0)
    qq = lax.broadcasted_iota(I32, (Q_BLOCK, Q_BLOCK), 1)
    n_cmp = ck_ref.shape[2]
    n_sel_blocks = 2 * n_tiles
    edge_base = n_tiles * N_HEADS
    none_base = (n_tiles + 1) * N_HEADS

    @pl.when((b == 0) & (i == 0))
    def _():
        def build(delta, carry):
            dist = delta * Q_BLOCK + qq - kk
            biases = _bias_from_bucket(_rel_bucket(dist), tab_ref, range(N_HEADS))
            for h in range(N_HEADS):
                bias_ref[delta * N_HEADS + h] = jnp.where(dist >= 0, biases[h], NEG)
            return carry
        lax.fori_loop(0, n_tiles, build, 0)
        dist = WIN_TILES * Q_BLOCK + qq - kk
        biases = _bias_from_bucket(_rel_bucket(dist), tab_ref, range(N_HEADS))
        for h in range(N_HEADS):
            bias_ref[edge_base + h] = jnp.where(dist < WINDOW, biases[h], NEG)
            bias_ref[none_base + h] = jnp.full((Q_BLOCK, Q_BLOCK), NEG, F32)

        def build_c(t, carry):
            dist_c = t * Q_BLOCK + qq - (kk * CMP_STRIDE + (CMP_LEN - 1))
            ok = (dist_c >= 0) & (kk < n_cmp - 1)
            biases_c = _bias_from_bucket(_rel_bucket(dist_c), tab_ref, range(N_HEADS))
            for h in range(N_HEADS):
                cbias_ref[t * N_HEADS + h] = jnp.where(ok, biases_c[h], NEG)
            return carry
        lax.fori_loop(0, n_tiles, build_c, 0)

    qs = i * Q_BLOCK
    q_all = q_ref[0] * SCALE
    gates = jax.nn.sigmoid(gl_ref[0])

    blk = lax.broadcasted_iota(I32, (n_sel_blocks, Q_BLOCK), 0)
    col = lax.broadcasted_iota(I32, (n_sel_blocks, Q_BLOCK), 1)
    c_start = col * CMP_STRIDE
    s_start = blk * SEL_LEN
    cover_t = jnp.where((c_start < s_start + SEL_LEN) & (c_start + CMP_LEN > s_start) & (col < n_cmp - 1),
                        1.0, 0.0).astype(BF16)
    q_pos = qs + col
    cur = q_pos // SEL_LEN
    forced = (blk == 0) | (blk == cur) | (blk == cur - 1)
    valid = blk * SEL_LEN <= q_pos
    upper = kk < SEL_LEN

    qts, o_cs, splits = [], [], []
    for g in range(N_KV_GROUPS):
        heads = [g * HEADS_PER_GROUP + hp for hp in range(HEADS_PER_GROUP)]
        qt = jnp.concatenate([q_all[h * HEAD_DIM:(h + 1) * HEAD_DIM, :] for h in heads], axis=1)
        qts.append(qt)

        s_c = _dot(ck_ref[0, g], qt) + jnp.concatenate([cbias_ref[i * N_HEADS + h] for h in heads], axis=1)
        m_c = jnp.max(s_c, axis=0, keepdims=True)
        e_c = jnp.exp(s_c - m_c)
        tot = jnp.sum(e_c, axis=0, keepdims=True)
        p_c = e_c * jnp.where(m_c > 0.5 * NEG, 1.0 / tot, 0.0)
        o_cs.append(_dot(cvt_ref[0, g * HEAD_DIM:(g + 1) * HEAD_DIM, :], p_c.astype(BF16)))
        p_sum = p_c[:, 0:Q_BLOCK]
        for hp in range(1, HEADS_PER_GROUP):
            p_sum = p_sum + p_c[:, hp * Q_BLOCK:(hp + 1) * Q_BLOCK]
        splits.append(p_sum.astype(BF16))

    imp_all = _dot(cover_t, jnp.concatenate(splits, axis=1))
    for g in range(N_KV_GROUPS):
        imp = imp_all[:, g * Q_BLOCK:(g + 1) * Q_BLOCK]
        score = jnp.where(forced, FORCE, imp)
        score = jnp.where(valid, score, NEG)
        cnt = jnp.zeros(score.shape, F32)
        for s in range(n_sel_blocks):
            row = score[s:s + 1, :]
            beats = (row > score) | ((row == score) & (blk > s))
            cnt = cnt + jnp.where(beats, 1.0, 0.0)
        sel = (cnt < float(N_SEL)) & (score > 0.5 * NEG)
        sel_ref[g] = jnp.where(sel, 0.0, NEG)

    def mask_sel(t, g):
        rows = sel_ref[g, pl.ds(pl.multiple_of(2 * KEY_SUB * t, 2 * KEY_SUB), 2 * KEY_SUB), :]
        return jnp.concatenate([jnp.where(upper, rows[2 * u:2 * u + 1], rows[2 * u + 1:2 * u + 2])
                                for u in range(KEY_SUB)], axis=0)
    last = i // KEY_SUB + 1
    o_ss = _attend_tiles_t(qts, sk_ref, svt_ref, 0, last, i, bias_ref,
                           lambda d: jnp.where(d < 0, none_base, d * N_HEADS), mask_sel)

    def win_index(d):
        return jnp.where((d < 0) | (d > WIN_TILES), none_base, jnp.where(d == WIN_TILES, edge_base, d * N_HEADS))
    o_ws = _attend_tiles_t(qts, wk_ref, wvt_ref, jnp.maximum(i - WIN_TILES, 0) // KEY_SUB, last, i, bias_ref,
                           win_index, lambda t, g: None)

    for h in range(N_HEADS):
        g, hp = divmod(h, HEADS_PER_GROUP)
        c0 = hp * Q_BLOCK
        o_h = (gates[3 * h:3 * h + 1, :] * o_cs[g][:, c0:c0 + Q_BLOCK]
               + gates[3 * h + 1:3 * h + 2, :] * o_ss[g][:, c0:c0 + Q_BLOCK]
               + gates[3 * h + 2:3 * h + 3, :] * o_ws[g][:, c0:c0 + Q_BLOCK])
        o_ref[0, h * HEAD_DIM:(h + 1) * HEAD_DIM, :] = o_h.astype(o_ref.dtype)


def _nsa_prompt(tab, q_t, gl_t, comp_k, comp_vt, slc_k, slc_vt, win_k, win_vt):
    b, dq, t_len = q_t.shape
    n_tiles = t_len // Q_BLOCK
    n_cmp = comp_k.shape[2]
    assert n_cmp == Q_BLOCK and t_len % KEY_TILE == 0 and n_tiles > WIN_TILES
    dv = N_KV_GROUPS * HEAD_DIM
    return pl.pallas_call(
        functools.partial(_nsa_prompt_kernel, n_tiles=n_tiles),
        grid=(b, n_tiles),
        in_specs=[pl.BlockSpec(memory_space=pltpu.SMEM),
                  pl.BlockSpec((1, dq, Q_BLOCK), lambda i, j: (i, 0, j)),
                  pl.BlockSpec((1, LANES, Q_BLOCK), lambda i, j: (i, 0, j)),
                  pl.BlockSpec((1, N_KV_GROUPS, n_cmp, HEAD_DIM), lambda i, j: (i, 0, 0, 0)),
                  pl.BlockSpec((1, dv, n_cmp), lambda i, j: (i, 0, 0)),
                  pl.BlockSpec((1, N_KV_GROUPS, t_len, HEAD_DIM), lambda i, j: (i, 0, 0, 0)),
                  pl.BlockSpec((1, dv, t_len), lambda i, j: (i, 0, 0)),
                  pl.BlockSpec((1, N_KV_GROUPS, t_len, HEAD_DIM), lambda i, j: (i, 0, 0, 0)),
                  pl.BlockSpec((1, dv, t_len), lambda i, j: (i, 0, 0))],
        out_specs=pl.BlockSpec((1, dq, Q_BLOCK), lambda i, j: (i, 0, j)),
        out_shape=jax.ShapeDtypeStruct((b, dq, t_len), BF16),
        scratch_shapes=[pltpu.VMEM(((n_tiles + 2) * N_HEADS, Q_BLOCK, Q_BLOCK), F32),
                        pltpu.VMEM((n_tiles * N_HEADS, Q_BLOCK, Q_BLOCK), F32),
                        pltpu.VMEM((N_KV_GROUPS, 2 * n_tiles, Q_BLOCK), F32)],
        compiler_params=_cparams(2),
        name="nsa_prompt",
    )(tab, q_t, gl_t, comp_k, comp_vt, slc_k, slc_vt, win_k, win_vt)


def _page_copy(pt_ref, pool_ref, buf_ref, sem, b, j, page):
    return pltpu.make_async_copy(pool_ref.at[pt_ref[b, j]], buf_ref.at[pl.ds(j * page, page)], sem)


def _gather_pages_start(pt_ref, pool_ref, buf_ref, sem, b, n_pages, page):
    def go(j, carry):
        _page_copy(pt_ref, pool_ref, buf_ref, sem, b, j, page).start()
        return carry
    lax.fori_loop(0, n_pages, go, 0)


def _gather_pages_wait(pt_ref, pool_ref, buf_ref, sem, b, n_pages, page):
    def go(j, carry):
        _page_copy(pt_ref, pool_ref, buf_ref, sem, b, j, page).wait()
        return carry
    lax.fori_loop(0, n_pages, go, 0)


def _compress_pages_kernel(pt_ref, pool_ref, pe_ref, w1_ref, b1_ref, w2_ref, o_ref, x_ref, a_ref, b_ref, sem,
                           *, n_pages, page, n_chunk):
    b = pl.program_id(0)
    _gather_pages_start(pt_ref, pool_ref, x_ref, sem, b, n_pages, page)
    _gather_pages_wait(pt_ref, pool_ref, x_ref, sem, b, n_pages, page)
    _compress_rows(x_ref, n_chunk, pe_ref, w1_ref, b1_ref, w2_ref, o_ref.at[0], a_ref, b_ref)


def _compress_pages(page_table, pool, pe, w1, b1, w2):
    b, n_pages = page_table.shape
    n_pool, page, _ = pool.shape
    past = n_pages * page
    n_chunk = past // CMP_STRIDE
    assert n_chunk % CMP_MBLK == 0
    full = lambda a: pl.BlockSpec(a.shape, lambda i: (0,) * a.ndim)
    return pl.pallas_call(
        functools.partial(_compress_pages_kernel, n_pages=n_pages, page=page * ROW_TILES, n_chunk=n_chunk),
        grid=(b,),
        in_specs=[pl.BlockSpec(memory_space=pltpu.SMEM), pl.BlockSpec(memory_space=pl.ANY),
                  full(pe), full(w1), full(b1), full(w2)],
        out_specs=pl.BlockSpec((1, n_chunk, D_KV), lambda i: (i, 0, 0)),
        out_shape=jax.ShapeDtypeStruct((b, n_chunk, D_KV), F32),
        scratch_shapes=[pltpu.VMEM((past * ROW_TILES, LANES), F32),
                        pltpu.VMEM((n_chunk, PAIR_HID), F32), pltpu.VMEM((n_chunk + SUBLANES, PAIR_HID), F32),
                        pltpu.SemaphoreType.DMA(())],
        compiler_params=_cparams(1),
        name="compress_pages",
    )(page_table, pool.reshape(n_pool, page * ROW_TILES, LANES), pe, w1, b1, w2)


DEC_TILE = 512
DEC_ROWS = 16


def _row_bias(dist_row, tab_ref, bias_ref, col0):
    length = dist_row.shape[1]
    biases = _bias_from_bucket(_rel_bucket(dist_row), tab_ref, range(N_HEADS))
    zero = jnp.zeros((DEC_ROWS - HEADS_PER_GROUP, length), F32)
    for g in range(N_KV_GROUPS):
        rows = biases[g * HEADS_PER_GROUP:(g + 1) * HEADS_PER_GROUP] + [zero]
        bias_ref[g * DEC_ROWS:(g + 1) * DEC_ROWS, pl.ds(col0, length)] = jnp.concatenate(rows, axis=0)


def _nsa_step_kernel(tab_ref, pt_ref, q_ref, gl_ref, comp_ref, pool_ref, slc_new_ref, win_ref, win_new_ref, o_ref,
                     slc_buf, win_buf, bias_c_ref, bias_s_ref, bias_w_ref, sem,
                     *, n_pages, page, past, n_cmp, w_buf):
    b = pl.program_id(0)
    n_tiles = past // DEC_TILE + 1
    n_blocks = past // SEL_LEN + 1
    blk_lanes = -(-n_blocks // LANES) * LANES
    _gather_pages_start(pt_ref, pool_ref, slc_buf, sem, b, n_pages, page)

    lane_t = lax.broadcasted_iota(I32, (1, DEC_TILE), 1)

    @pl.when(b == 0)
    def _():
        lane_c = lax.broadcasted_iota(I32, (1, n_cmp), 1)
        _row_bias(past - (lane_c * CMP_STRIDE + (CMP_LEN - 1)), tab_ref, bias_c_ref, 0)

        def tile_bias(t, carry):
            k0 = pl.multiple_of(t * DEC_TILE, DEC_TILE)
            _row_bias(past - (k0 + lane_t), tab_ref, bias_s_ref, k0)
            return carry
        lax.fori_loop(0, n_tiles, tile_bias, 0)
        lane_w = lax.broadcasted_iota(I32, (1, win_buf.shape[0]), 1)
        _row_bias(w_buf - lane_w, tab_ref, bias_w_ref, 0)

    q_row = q_ref[0]
    gates = jax.nn.sigmoid(gl_ref[0])
    pad_rows = jnp.zeros((DEC_ROWS - HEADS_PER_GROUP, HEAD_DIM), BF16)
    qgs = []
    for g in range(N_KV_GROUPS):
        rows = [q_row[:, (g * HEADS_PER_GROUP + hp) * HEAD_DIM:(g * HEADS_PER_GROUP + hp + 1) * HEAD_DIM]
                for hp in range(HEADS_PER_GROUP)]
        qgs.append(jnp.concatenate(rows + [pad_rows], axis=0))

    comp = comp_ref[0].astype(BF16)
    lane_c8 = lax.broadcasted_iota(I32, (DEC_ROWS, n_cmp), 1)
    mask_c = lane_c8 < n_cmp - 1
    o_cs, p_sums = [], []
    for g in range(N_KV_GROUPS):
        c0 = g * HEAD_DIM
        s_c = _dot_nt(qgs[g], comp[:, c0:c0 + HEAD_DIM]) * SCALE + bias_c_ref[g * DEC_ROWS:(g + 1) * DEC_ROWS, :]
        p_c = _masked_softmax_rows(s_c, mask_c)
        o_cs.append(_dot(p_c.astype(BF16), comp[:, D_KV // 2 + c0:D_KV // 2 + c0 + HEAD_DIM]))
        p_sums.append(jnp.sum(p_c[0:HEADS_PER_GROUP], axis=0, keepdims=True))
    p_sum = jnp.concatenate(p_sums + [jnp.zeros((DEC_ROWS - N_KV_GROUPS, n_cmp), F32)], axis=0)

    c_start = lax.broadcasted_iota(I32, (n_cmp, blk_lanes), 0) * CMP_STRIDE
    s_start = lax.broadcasted_iota(I32, (n_cmp, blk_lanes), 1) * SEL_LEN
    c_idx = lax.broadcasted_iota(I32, (n_cmp, blk_lanes), 0)
    cover = jnp.where((c_start < s_start + SEL_LEN) & (c_start + CMP_LEN > s_start) & (c_idx < n_cmp - 1),
                      1.0, 0.0).astype(BF16)
    imp = _dot(p_sum.astype(BF16), cover)
    blk = lax.broadcasted_iota(I32, (DEC_ROWS, blk_lanes), 1)
    cur = past // SEL_LEN
    forced = (blk == 0) | (blk == cur) | (blk == cur - 1)
    score = jnp.where(forced, FORCE, imp)
    score = jnp.where(blk < n_blocks, score, NEG)
    sel = jnp.zeros(score.shape, F32)
    big = jnp.int32(4 * blk_lanes)
    for _ in range(N_SEL):
        mx = jnp.max(score, axis=-1, keepdims=True)
        idx = jnp.min(jnp.where(score == mx, blk, big), axis=-1, keepdims=True)
        hit = blk == idx
        sel = jnp.where(hit & (mx > 0.5 * NEG), 1.0, sel)
        score = jnp.where(hit, -jnp.inf, score)
    sel_bf = sel.astype(BF16)

    _gather_pages_wait(pt_ref, pool_ref, slc_buf, sem, b, n_pages, page)
    slc_buf[past:past + DEC_TILE, :] = jnp.zeros((DEC_TILE, D_KV), F32)
    slc_buf[past:past + 1, :] = slc_new_ref[0]
    e_row = lax.broadcasted_iota(I32, (blk_lanes, DEC_TILE), 0)
    e_col = lax.broadcasted_iota(I32, (blk_lanes, DEC_TILE), 1) // SEL_LEN

    def sel_tile(t, carry):
        k0 = pl.multiple_of(t * DEC_TILE, DEC_TILE)
        expand = jnp.where(e_row == e_col + t * (DEC_TILE // SEL_LEN), 1.0, 0.0).astype(BF16)
        key_sel = _dot(sel_bf, expand)
        causal = jnp.where(k0 + lane_t <= past, 0.0, NEG)
        new = []
        for g in range(N_KV_GROUPS):
            m, l, acc = carry[g]
            c0 = g * HEAD_DIM
            kt = slc_buf[pl.ds(k0, DEC_TILE), c0:c0 + HEAD_DIM].astype(BF16)
            vt = slc_buf[pl.ds(k0, DEC_TILE), D_KV // 2 + c0:D_KV // 2 + c0 + HEAD_DIM].astype(BF16)
            madd = jnp.where(key_sel[g:g + 1] > 0.5, 0.0, NEG) + causal
            s = (_dot_nt(qgs[g], kt) * SCALE + bias_s_ref[g * DEC_ROWS:(g + 1) * DEC_ROWS, pl.ds(k0, DEC_TILE)]
                 + madd)
            m_new = jnp.maximum(m, jnp.max(s, axis=-1, keepdims=True))
            alpha = jnp.exp(m - m_new)
            p = jnp.exp(s - m_new)
            new.append((m_new, alpha * l + jnp.sum(p, axis=-1, keepdims=True), alpha * acc + _dot(p.astype(BF16), vt)))
        return tuple(new)

    init = tuple((jnp.full((DEC_ROWS, 1), NEG, F32), jnp.zeros((DEC_ROWS, 1), F32),
                  jnp.zeros((DEC_ROWS, HEAD_DIM), F32)) for _ in range(N_KV_GROUPS))
    fin = lax.fori_loop(0, n_tiles, sel_tile, init)
    o_ss = [acc / l for (_, l, acc) in fin]

    n_win = win_buf.shape[0]
    win_buf[0:w_buf, :] = win_ref[0]
    win_buf[w_buf:n_win, :] = jnp.zeros((n_win - w_buf, D_KV), F32)
    win_buf[w_buf:w_buf + 1, :] = win_new_ref[0]
    kw = win_buf[...].astype(BF16)
    dist_w = w_buf - lax.broadcasted_iota(I32, (DEC_ROWS, n_win), 1)
    mask_w = (dist_w >= 0) & (dist_w < WINDOW)
    o_ws = []
    for g in range(N_KV_GROUPS):
        c0 = g * HEAD_DIM
        s_w = _dot_nt(qgs[g], kw[:, c0:c0 + HEAD_DIM]) * SCALE + bias_w_ref[g * DEC_ROWS:(g + 1) * DEC_ROWS, :]
        p_w = _masked_softmax_rows(s_w, mask_w)
        o_ws.append(_dot(p_w.astype(BF16), kw[:, D_KV // 2 + c0:D_KV // 2 + c0 + HEAD_DIM]))

    outs = []
    for g in range(N_KV_GROUPS):
        for hp in range(HEADS_PER_GROUP):
            h = g * HEADS_PER_GROUP + hp
            outs.append(gates[:, 3 * h:3 * h + 1] * o_cs[g][hp:hp + 1]
                        + gates[:, 3 * h + 1:3 * h + 2] * o_ss[g][hp:hp + 1]
                        + gates[:, 3 * h + 2:3 * h + 3] * o_ws[g][hp:hp + 1])
    o_ref[0] = jnp.concatenate(outs, axis=1).astype(o_ref.dtype)


def _nsa_step(tab, page_table, q, gl, comp, pool, slc_new, win_cache, win_new):
    b, n_pages = page_table.shape
    page = pool.shape[1]
    past = n_pages * page
    n_cmp = comp.shape[1]
    w_buf = win_cache.shape[1]
    dq = q.shape[2]
    assert past % DEC_TILE == 0 and past % SEL_LEN == 0 and w_buf == WINDOW and n_cmp == past // CMP_STRIDE
    n_keys = past + DEC_TILE
    n_win = w_buf + LANES
    row3 = lambda w: pl.BlockSpec((1, 1, w), lambda i: (i, 0, 0))
    return pl.pallas_call(
        functools.partial(_nsa_step_kernel, n_pages=n_pages, page=page, past=past, n_cmp=n_cmp, w_buf=w_buf),
        grid=(b,),
        in_specs=[pl.BlockSpec(memory_space=pltpu.SMEM), pl.BlockSpec(memory_space=pltpu.SMEM),
                  row3(dq), row3(LANES),
                  pl.BlockSpec((1, n_cmp, D_KV), lambda i: (i, 0, 0)),
                  pl.BlockSpec(memory_space=pl.ANY),
                  row3(D_KV),
                  pl.BlockSpec((1, w_buf, D_KV), lambda i: (i, 0, 0)),
                  row3(D_KV)],
        out_specs=row3(dq),
        out_shape=jax.ShapeDtypeStruct((b, 1, dq), BF16),
        scratch_shapes=[pltpu.VMEM((n_keys, D_KV), F32),
                        pltpu.VMEM((n_win, D_KV), F32),
                        pltpu.VMEM((N_KV_GROUPS * DEC_ROWS, n_cmp), F32),
                        pltpu.VMEM((N_KV_GROUPS * DEC_ROWS, n_keys), F32),
                        pltpu.VMEM((N_KV_GROUPS * DEC_ROWS, n_win), F32),
                        pltpu.SemaphoreType.DMA(())],
        compiler_params=_cparams(1),
        name="nsa_step",
    )(tab, page_table, q, gl, comp, pool, slc_new, win_cache, win_new)


def _block_diag_pairs(w):
    n, blk, _ = w.shape
    w = w.reshape(n // 2, 2, blk, blk)
    z = jnp.zeros((n // 2, blk, blk), w.dtype)
    top = jnp.concatenate([w[:, 0], z], axis=2)
    bot = jnp.concatenate([z, w[:, 1]], axis=2)
    return jnp.concatenate([top, bot], axis=1).astype(BF16)


def _prep_weights(w_in_a, conv_w, conv_b, w_rg, b_rg, w_ig, b_ig, lru_lambda, w_in_b, w_kv_shared, cmp_pe, cmp_w1,
                  cmp_b1, cmp_w2, w_mem_kv, w_out, ln1_g, ln1_b, ln2_g, ln2_b, w_router_g, b_router_g,
                  w_router_e, b_router_e, w_exp_gate, w_exp_up, w_exp_down):
    depth, d_model, _ = w_out.shape
    n_a = w_in_a.shape[0]
    nq = N_HEADS * HEAD_DIM
    n_gl = 3 * N_HEADS
    hid = EXPERTS_PER_GROUP * D_EXPERT
    p = {}
    p["w_in_a"] = w_in_a.astype(BF16)
    p["w_in_b"] = jnp.concatenate(
        [w_in_b[:, :, :nq], w_in_b[:, :, nq + n_gl:], w_in_b[:, :, nq:nq + n_gl],
         jnp.zeros((w_in_b.shape[0], d_model, LANES - n_gl), w_in_b.dtype)], axis=2).astype(BF16)
    p["w_kv"] = w_kv_shared.astype(BF16)
    p["w_mem"] = w_mem_kv.transpose(1, 0, 2).reshape(d_model, depth * 2 * D_MEM).astype(BF16)
    p["wo_a"] = w_out[:, :D_RNN].astype(BF16)
    p["wo_m"] = w_out[:, D_RNN:].astype(BF16)
    p["conv_w"] = conv_w
    p["conv_b"] = conv_b[:, None]
    p["w_rg"] = jnp.stack([_block_diag_pairs(w_rg[l]) for l in range(n_a)])
    p["w_ig"] = jnp.stack([_block_diag_pairs(w_ig[l]) for l in range(n_a)])
    p["b_rg"] = b_rg[:, None]
    p["b_ig"] = b_ig[:, None]
    p["lam"] = lru_lambda[:, None]
    p["pe"], p["cmp_w1"], p["cmp_b1"], p["cmp_w2"] = _compress_weights(cmp_pe, cmp_w1, cmp_b1, cmp_w2)
    p["ln1_g"], p["ln1_b"], p["ln2_g"], p["ln2_b"] = ln1_g[:, None], ln1_b[:, None], ln2_g[:, None], ln2_b[:, None]
    n_r = N_GROUPS + N_EXPERTS
    wr = jnp.concatenate([w_router_g, w_router_e, jnp.zeros((depth, d_model, LANES - n_r), F32)], axis=2)
    p["wr"] = wr.astype(BF16)
    p["rb"] = jnp.concatenate([b_router_g, b_router_e, jnp.zeros((depth, LANES - n_r), F32)], axis=1)[:, None]

    def by_group(w):
        w = w.astype(BF16).reshape(depth, N_GROUPS, EXPERTS_PER_GROUP, d_model, D_EXPERT)
        return w.transpose(0, 1, 3, 2, 4).reshape(depth, N_GROUPS, d_model, hid)
    p["w_gu"] = jnp.concatenate([by_group(w_exp_gate), by_group(w_exp_up)], axis=3)
    p["w_dn"] = w_exp_down.astype(BF16).reshape(depth, N_GROUPS, hid, d_model)
    return p


def _kv_layouts(rows):
    b, t_len, _ = rows.shape
    k = rows[:, :, :D_KV // 2].reshape(b, t_len, N_KV_GROUPS, HEAD_DIM).transpose(0, 2, 1, 3)
    return k, rows[:, :, D_KV // 2:].transpose(0, 2, 1)


def _trunk_tail(x, mix_a, mix_m, p, l, alpha, tm):
    x = _out_ln(x, mix_a, mix_m, p["wo_a"][l], p["wo_m"][l], p["ln1_g"][l], p["ln1_b"][l], alpha, tm)
    return _moe_ln(x, p["wr"][l], p["rb"][l], p["w_gu"][l], p["w_dn"][l],
                   p["ln2_g"][l], p["ln2_b"][l], alpha, tm)


def kernel(x_prompt, x_sample, mem_prompt, cache_cmp_kv, cache_slc_kv, cache_win_kv, cache_mem_kv, state_lru_h, state_conv, page_table, w_in_a, conv_w, conv_b, w_rg, b_rg, w_ig, b_ig, lru_lambda, w_in_b, w_kv_shared, cmp_pe, cmp_w1, cmp_b1, cmp_w2, rel_bias, w_mem_kv, w_out, ln1_g, ln1_b, ln2_g, ln2_b, w_router_g, b_router_g, w_router_e, b_router_e, w_exp_gate, w_exp_up, w_exp_down):
    p = _prep_weights(w_in_a, conv_w, conv_b, w_rg, b_rg, w_ig, b_ig, lru_lambda, w_in_b, w_kv_shared, cmp_pe,
                      cmp_w1, cmp_b1, cmp_w2, w_mem_kv, w_out, ln1_g, ln1_b, ln2_g, ln2_b, w_router_g,
                      b_router_g, w_router_e, b_router_e, w_exp_gate, w_exp_up, w_exp_down)
    depth, d_model, _ = w_out.shape
    n_a = w_in_a.shape[0]
    alpha = (2 * depth) ** 0.25
    kv_shape = (2, N_KV_GROUPS, HEAD_DIM)
    a_outs = [(0, D_RNN, F32), (D_RNN, D_RNN, F32), (2 * D_RNN, D_MEM, BF16)]
    b_outs = [(0, D_RNN, BF16), (D_RNN, D_MEM, BF16), (D_RNN + D_MEM, LANES, F32)]
    kv_outs = [(0, D_KV, F32), (D_KV, D_KV, F32), (2 * D_KV, D_KV, F32)]

    bp, t_len, _ = x_prompt.shape
    n_mem = mem_prompt.shape[1]
    tm = 512
    mem_rows = _proj(mem_prompt.reshape(bp * n_mem, d_model), p["w_mem"],
                     [(l * 2 * D_MEM, 2 * D_MEM, F32) for l in range(depth)], tm, "mem_kv_proj")
    p_mem = jnp.stack(mem_rows).reshape(depth, bp, n_mem, 2 * D_MEM)
    x = x_prompt.reshape(bp * t_len, d_model)
    lru_p, conv_p = [], []
    for l in range(depth):
        if l < n_a:
            gate, xr, mq = _proj(x, p["w_in_a"][l], a_outs, tm, "in_proj_a")
            mix_a, h_t, buf = _rglru_prompt(gate.reshape(bp, t_len, D_RNN), xr.reshape(bp, t_len, D_RNN),
                                            p["conv_w"][l], p["conv_b"][l], p["w_rg"][l], p["w_ig"][l],
                                            p["b_rg"][l], p["b_ig"][l], p["lam"][l])
            lru_p.append(h_t[:, 0])
            conv_p.append(buf)
        else:
            if l == n_a:
                cmp_rows, slc_rows, win_rows, slc_bf, win_bf = _proj(
                    x, p["w_kv"], kv_outs + [(D_KV, D_KV, BF16), (2 * D_KV, D_KV, BF16)], tm, "kv_proj")
                comp = _compress_prompt(cmp_rows.reshape(bp, t_len, D_KV), p["pe"], p["cmp_w1"], p["cmp_b1"],
                                        p["cmp_w2"])
                comp_k, comp_vt = _kv_layouts(comp.astype(BF16))
                slc_k, slc_vt = _kv_layouts(slc_bf.reshape(bp, t_len, D_KV))
                win_k, win_vt = _kv_layouts(win_bf.reshape(bp, t_len, D_KV))
            q, mq, gl = _proj(x, p["w_in_b"][l - n_a], b_outs, tm, "in_proj_b")
            mix_t = _nsa_prompt(rel_bias, q.reshape(bp, t_len, D_RNN).transpose(0, 2, 1),
                                gl.reshape(bp, t_len, LANES).transpose(0, 2, 1),
                                comp_k, comp_vt, slc_k, slc_vt, win_k, win_vt)
            mix_a = mix_t.transpose(0, 2, 1)
        mix_m = _mem_attn(mq.reshape(bp, t_len, D_MEM), p_mem, l)
        x = _trunk_tail(x, mix_a.reshape(bp * t_len, D_RNN), mix_m.reshape(bp * t_len, D_MEM), p, l, alpha, tm)
    y_prompt = x.reshape(bp, t_len, d_model)
    p_cmp_kv = cmp_rows.reshape((bp, t_len) + kv_shape)
    p_slc_kv = slc_rows.reshape((bp, t_len) + kv_shape)
    w_keep = min(WINDOW, t_len)
    p_win_kv = win_rows.reshape((bp, t_len) + kv_shape)[:, t_len - w_keep:]
    p_mem_kv = p_mem.reshape(depth, bp, n_mem, 2, N_MEM_HEADS, HEAD_DIM)
    p_lru_h = jnp.stack(lru_p)
    p_conv = jnp.stack(conv_p)

    bd, s_len, _ = x_sample.shape
    assert s_len == 1
    n_pool, page = cache_cmp_kv.shape[:2]
    w_buf = cache_win_kv.shape[1]
    mem_cache = cache_mem_kv.reshape(depth, bd, cache_mem_kv.shape[2], 2 * D_MEM)
    x = x_sample.reshape(bd, d_model)
    lru_s, conv_s = [], []
    for l in range(depth):
        if l < n_a:
            gate, xr, mq = _proj(x, p["w_in_a"][l], a_outs, bd, "in_proj_a_step")
            mix_a, h_t, buf = _rglru_step(gate, xr, state_conv[l].transpose(1, 0, 2), state_lru_h[l],
                                          p["conv_w"][l], p["conv_b"][l], p["w_rg"][l], p["w_ig"][l],
                                          p["b_rg"][l], p["b_ig"][l], p["lam"][l])
            lru_s.append(h_t)
            conv_s.append(buf.transpose(1, 0, 2))
        else:
            if l == n_a:
                s_cmp, s_slc, s_win = _proj(x, p["w_kv"], kv_outs, bd, "kv_proj_step")
                comp_s = _compress_pages(page_table, cache_cmp_kv.reshape(n_pool, page, D_KV), p["pe"],
                                         p["cmp_w1"], p["cmp_b1"], p["cmp_w2"])
            q, mq, gl = _proj(x, p["w_in_b"][l - n_a], b_outs, bd, "in_proj_b_step")
            mix_a = _nsa_step(rel_bias, page_table, q.reshape(bd, 1, D_RNN), gl.reshape(bd, 1, LANES), comp_s,
                              cache_slc_kv.reshape(n_pool, page, D_KV), s_slc.reshape(bd, 1, D_KV),
                              cache_win_kv.reshape(bd, w_buf, D_KV), s_win.reshape(bd, 1, D_KV))
            mix_a = mix_a.reshape(bd, D_RNN)
        mix_m = _mem_attn(mq.reshape(bd, 1, D_MEM), mem_cache, l).reshape(bd, D_MEM)
        x = _trunk_tail(x, mix_a, mix_m, p, l, alpha, bd)
    y_sample = x.reshape(bd, 1, d_model)
    s_cmp_kv = s_cmp.reshape((bd, 1) + kv_shape)
    s_slc_kv = s_slc.reshape((bd, 1) + kv_shape)
    s_win_kv = jnp.concatenate([cache_win_kv, s_win.reshape((bd, 1) + kv_shape)], axis=1)[:, 1:]
    s_lru_h = jnp.stack(lru_s)
    s_conv = jnp.stack(conv_s)
    return (y_prompt, y_sample, p_cmp_kv, p_slc_kv, p_win_kv, p_mem_kv, p_lru_h, p_conv,
            s_cmp_kv, s_slc_kv, s_win_kv, s_lru_h, s_conv)
```

```python
import functools
import math

import jax
import jax.numpy as jnp
from jax import lax
from jax.experimental import pallas as pl
from jax.experimental.pallas import tpu as pltpu

F32 = jnp.float32
BF16 = jnp.bfloat16
I32 = jnp.int32

HEAD_DIM = 64
N_MEM_HEADS = 4
D_MEM = N_MEM_HEADS * HEAD_DIM
N_HEADS = 12
D_RNN = N_HEADS * HEAD_DIM
N_KV_GROUPS = 4
HEADS_PER_GROUP = N_HEADS // N_KV_GROUPS
D_KV = 2 * N_KV_GROUPS * HEAD_DIM
CONV_W = 4
LRU_C = 8.0
CMP_LEN = 32
CMP_STRIDE = 16
CMP_HID = 2 * HEAD_DIM
SEL_LEN = 64
N_SEL = 16
WINDOW = 512
N_BUCKETS = 32
MAX_DISTANCE = 1024
N_GROUPS = 4
EXPERTS_PER_GROUP = 8
N_EXPERTS = N_GROUPS * EXPERTS_PER_GROUP
D_EXPERT = 128
Q_BLOCK = 128
LN_EPS = 1e-5
NEG = -1e30
FORCE = 1e9
SCALE = HEAD_DIM ** -0.5

LANES = 128
SUBLANES = 8
VMEM_LIMIT_BYTES = 56 * 1024 * 1024


def _cparams(n_axes):
    return pltpu.CompilerParams(dimension_semantics=("arbitrary",) * n_axes,
                                vmem_limit_bytes=VMEM_LIMIT_BYTES)


def _dot(a, b):
    return jnp.dot(a, b, preferred_element_type=F32)


def _dot_nt(a, b):
    return lax.dot_general(a, b, (((1,), (1,)), ((), ())), preferred_element_type=F32)


def _layer_norm(v, g, b):
    mu = jnp.mean(v, axis=-1, keepdims=True)
    d = v - mu
    var = jnp.mean(d * d, axis=-1, keepdims=True)
    return d * lax.rsqrt(var + LN_EPS) * g + b


def _rel_bucket(dist):
    n = jnp.maximum(dist, 0)
    max_exact = N_BUCKETS // 2
    nf = jnp.maximum(n, 1).astype(F32)
    large = max_exact + (jnp.log(nf / max_exact) / math.log(MAX_DISTANCE / max_exact)
                         * (N_BUCKETS - max_exact)).astype(I32)
    large = jnp.minimum(large, N_BUCKETS - 1)
    return jnp.where(n < max_exact, n, large)


def _bias_from_bucket(bucket, tab_ref, heads):
    masks = [bucket == k for k in range(1, N_BUCKETS)]
    out = []
    for h in heads:
        acc = jnp.full(bucket.shape, tab_ref[0, h], F32)
        for k in range(1, N_BUCKETS):
            acc = jnp.where(masks[k - 1], tab_ref[k, h], acc)
        out.append(acc)
    return out


def _proj_kernel(x_ref, w_ref, *o_refs, cols):
    y = _dot(x_ref[...].astype(BF16), w_ref[...])
    for o_ref, (off, n) in zip(o_refs, cols):
        o_ref[...] = y[:, off:off + n].astype(o_ref.dtype)


def _proj(x, w, outs, tm, name):
    m, k = x.shape
    n = w.shape[1]
    tm = min(tm, m)
    assert m % tm == 0 and all(off + wd <= n for off, wd, _ in outs)
    return pl.pallas_call(
        functools.partial(_proj_kernel, cols=tuple((off, wd) for off, wd, _ in outs)),
        grid=(m // tm,),
        in_specs=[pl.BlockSpec((tm, k), lambda i: (i, 0)),
                  pl.BlockSpec((k, n), lambda i: (0, 0))],
        out_specs=[pl.BlockSpec((tm, wd), lambda i: (i, 0)) for _, wd, _ in outs],
        out_shape=[jax.ShapeDtypeStruct((m, wd), dt) for _, wd, dt in outs],
        compiler_params=_cparams(1),
        name=name,
    )(x, w)


def _softplus(z):
    return jnp.maximum(z, 0.0) + jnp.log1p(jnp.exp(-jnp.abs(z)))


def _lru_gates(xc, wr_ref, wi_ref, br, bi, lam):
    xcb = xc.astype(BF16)
    nb = D_RNN // LANES
    r_l = jnp.concatenate([_dot(xcb[:, j * LANES:(j + 1) * LANES], wr_ref[j]) for j in range(nb)], axis=1)
    i_l = jnp.concatenate([_dot(xcb[:, j * LANES:(j + 1) * LANES], wi_ref[j]) for j in range(nb)], axis=1)
    r = jax.nn.sigmoid(r_l + br)
    i = jax.nn.sigmoid(i_l + bi)
    log_a = -LRU_C * r * _softplus(-lam)
    a = jnp.exp(log_a)
    u = jnp.sqrt(1.0 - a * a) * i * xc
    return a, u


def _rglru_prompt_kernel(gate_ref, xr_ref, cw_ref, cb_ref, wr_ref, wi_ref, br_ref, bi_ref, lam_ref,
                         y_ref, ht_ref, buf_ref, xp_ref, *, t_len, chunk):
    d = D_RNN
    pad = SUBLANES
    xp_ref[0:pad, :] = jnp.zeros((pad, d), F32)
    xp_ref[pad:pad + t_len, :] = xr_ref[0]
    buf_ref[0] = xr_ref[0, t_len - (CONV_W - 1):t_len, :]
    cw = cw_ref[...]
    cb = cb_ref[...]
    br = br_ref[...]
    bi = bi_ref[...]
    lam = lam_ref[...]
    row_in_tile = lax.broadcasted_iota(I32, (chunk, d), 0) & (SUBLANES - 1)
    h = jnp.zeros((1, d), F32)
    for c in range(t_len // chunk):
        base = c * chunk
        xc = cb
        for k in range(CONV_W):
            lo = pad - (CONV_W - 1) + k + base
            xc = xc + xp_ref[lo:lo + chunk, :] * cw[k:k + 1, :]
        a, u = _lru_gates(xc, wr_ref, wi_ref, br, bi, lam)
        for k in (1, 2, 4):
            a_s = pltpu.roll(a, k, axis=0)
            u_s = pltpu.roll(u, k, axis=0)
            m = row_in_tile >= k
            u = jnp.where(m, a * u_s + u, u)
            a = jnp.where(m, a * a_s, a)
        hs = []
        for j in range(chunk // SUBLANES):
            h_t = a[j * SUBLANES:(j + 1) * SUBLANES] * h + u[j * SUBLANES:(j + 1) * SUBLANES]
            h = h_t[SUBLANES - 1:SUBLANES]
            hs.append(h_t)
        hfull = jnp.concatenate(hs, axis=0)
        g = jax.nn.gelu(gate_ref[0, base:base + chunk, :])
        y_ref[0, base:base + chunk, :] = (g * hfull).astype(y_ref.dtype)
    ht_ref[0] = h


def _rglru_prompt(gate, xr, cw, cb, wr_bd, wi_bd, br, bi, lam):
    b, t_len, d = xr.shape
    chunk = min(256, t_len)
    full = lambda shape: pl.BlockSpec(shape, lambda i: (0,) * len(shape))
    return pl.pallas_call(
        functools.partial(_rglru_prompt_kernel, t_len=t_len, chunk=chunk),
        grid=(b,),
        in_specs=[pl.BlockSpec((1, t_len, d), lambda i: (i, 0, 0)),
                  pl.BlockSpec((1, t_len, d), lambda i: (i, 0, 0)),
                  full((CONV_W, d)), full((1, d)), full(wr_bd.shape), full(wi_bd.shape),
                  full((1, d)), full((1, d)), full((1, d))],
        out_specs=[pl.BlockSpec((1, t_len, d), lambda i: (i, 0, 0)),
                   pl.BlockSpec((1, 1, d), lambda i: (i, 0, 0)),
                   pl.BlockSpec((1, CONV_W - 1, d), lambda i: (i, 0, 0))],
        out_shape=[jax.ShapeDtypeStruct((b, t_len, d), BF16),
                   jax.ShapeDtypeStruct((b, 1, d), F32),
                   jax.ShapeDtypeStruct((b, CONV_W - 1, d), F32)],
        scratch_shapes=[pltpu.VMEM((t_len + SUBLANES, d), F32)],
        compiler_params=_cparams(1),
        name="rglru_prompt",
    )(gate, xr, cw, cb, wr_bd, wi_bd, br, bi, lam)


def _rglru_step_kernel(gate_ref, xr_ref, conv_ref, h0_ref, cw_ref, cb_ref, wr_ref, wi_ref, br_ref, bi_ref,
                       lam_ref, y_ref, ht_ref, buf_ref):
    xr = xr_ref[...]
    cw = cw_ref[...]
    xc = cb_ref[...] + xr * cw[CONV_W - 1:CONV_W, :]
    for k in range(CONV_W - 1):
        xc = xc + conv_ref[k] * cw[k:k + 1, :]
    a, u = _lru_gates(xc, wr_ref, wi_ref, br_ref[...], bi_ref[...], lam_ref[...])
    h = a * h0_ref[...] + u
    ht_ref[...] = h
    y_ref[...] = (jax.nn.gelu(gate_ref[...]) * h).astype(y_ref.dtype)
    for k in range(CONV_W - 2):
        buf_ref[k] = conv_ref[k + 1]
    buf_ref[CONV_W - 2] = xr


def _rglru_step(gate, xr, conv_t, h0, cw, cb, wr_bd, wi_bd, br, bi, lam):
    b, d = xr.shape
    return pl.pallas_call(
        _rglru_step_kernel,
        out_shape=[jax.ShapeDtypeStruct((b, d), BF16),
                   jax.ShapeDtypeStruct((b, d), F32),
                   jax.ShapeDtypeStruct((CONV_W - 1, b, d), F32)],
        compiler_params=pltpu.CompilerParams(vmem_limit_bytes=VMEM_LIMIT_BYTES),
        name="rglru_step",
    )(gate, xr, conv_t, h0, cw, cb, wr_bd, wi_bd, br, bi, lam)


def _mem_attn_kernel(q_ref, kv_ref, o_ref, *, rows):
    q = q_ref[0]
    if rows < SUBLANES:
        q = jnp.broadcast_to(q, (SUBLANES, D_MEM))
    kv = kv_ref[0, 0].astype(BF16)
    outs = []
    for h in range(N_MEM_HEADS):
        qh = q[:, h * HEAD_DIM:(h + 1) * HEAD_DIM]
        kh = kv[:, h * HEAD_DIM:(h + 1) * HEAD_DIM]
        vh = kv[:, D_MEM + h * HEAD_DIM:D_MEM + (h + 1) * HEAD_DIM]
        s = _dot_nt(qh, kh) * SCALE
        m = jnp.max(s, axis=-1, keepdims=True)
        e = jnp.exp(s - m)
        p = e / jnp.sum(e, axis=-1, keepdims=True)
        outs.append(_dot(p.astype(BF16), vh))
    o = jnp.concatenate(outs, axis=1)
    o_ref[0] = o[:rows].astype(o_ref.dtype)


def _mem_attn(q, mem_kv, layer):
    b, t_len, _ = q.shape
    n_mem = mem_kv.shape[2]
    tq = min(t_len, 512)
    return pl.pallas_call(
        functools.partial(_mem_attn_kernel, rows=tq),
        grid=(b, t_len // tq),
        in_specs=[pl.BlockSpec((1, tq, D_MEM), lambda i, j: (i, j, 0)),
                  pl.BlockSpec((1, 1, n_mem, 2 * D_MEM), lambda i, j: (layer, i, 0, 0))],
        out_specs=pl.BlockSpec((1, tq, D_MEM), lambda i, j: (i, j, 0)),
        out_shape=jax.ShapeDtypeStruct((b, t_len, D_MEM), BF16),
        compiler_params=_cparams(2),
        name="mem_attn",
    )(q, mem_kv)


def _mem_attn_step_kernel(q_ref, kv_ref, o_ref):
    q_row = q_ref[0] * SCALE
    rows = []
    for h in range(N_MEM_HEADS):
        parts = []
        if h > 0:
            parts.append(jnp.zeros((1, h * HEAD_DIM), BF16))
        parts.append(q_row[:, h * HEAD_DIM:(h + 1) * HEAD_DIM])
        if h < N_MEM_HEADS - 1:
            parts.append(jnp.zeros((1, (N_MEM_HEADS - 1 - h) * HEAD_DIM), BF16))
        rows.append(jnp.concatenate(parts, axis=1))
    n_rows = 2 * SUBLANES
    q_blk = jnp.concatenate(rows + [jnp.zeros((n_rows - N_MEM_HEADS, D_MEM), BF16)], axis=0)
    s = _dot(q_blk, kv_ref[0, 0, 0].astype(BF16))
    e = jnp.exp(s - jnp.max(s, axis=-1, keepdims=True))
    p = e / jnp.sum(e, axis=-1, keepdims=True)
    o_all = _dot_nt(p.astype(BF16), kv_ref[0, 0, 1].astype(BF16))
    o = jnp.concatenate([o_all[h:h + 1, h * HEAD_DIM:(h + 1) * HEAD_DIM] for h in range(N_MEM_HEADS)], axis=1)
    o_ref[0] = o.astype(o_ref.dtype)


def _mem_attn_step(q, mem_t, layer):
    b = q.shape[0]
    n_mem = mem_t.shape[4]
    return pl.pallas_call(
        _mem_attn_step_kernel,
        grid=(b,),
        in_specs=[pl.BlockSpec((1, 1, D_MEM), lambda i: (i, 0, 0)),
                  pl.BlockSpec((1, 1, 2, D_MEM, n_mem), lambda i: (layer, i, 0, 0, 0))],
        out_specs=pl.BlockSpec((1, 1, D_MEM), lambda i: (i, 0, 0)),
        out_shape=jax.ShapeDtypeStruct((b, 1, D_MEM), BF16),
        compiler_params=_cparams(1),
        name="mem_attn_step",
    )(q, mem_t)


def _out_ln_kernel(x_ref, ma_ref, mm_ref, wa_ref, wm_ref, g_ref, b_ref, o_ref, *, alpha):
    sub = _dot(ma_ref[...], wa_ref[...]) + _dot(mm_ref[...], wm_ref[...])
    o_ref[...] = _layer_norm(alpha * x_ref[...] + sub, g_ref[...], b_ref[...])


def _out_ln(x, mix_a, mix_m, wo_a, wo_m, g, b, alpha, tm):
    m, d = x.shape
    da = mix_a.shape[1]
    dm = mix_m.shape[1]
    return pl.pallas_call(
        functools.partial(_out_ln_kernel, alpha=alpha),
        grid=(m // tm,),
        in_specs=[pl.BlockSpec((tm, d), lambda i: (i, 0)),
                  pl.BlockSpec((tm, da), lambda i: (i, 0)),
                  pl.BlockSpec((tm, dm), lambda i: (i, 0)),
                  pl.BlockSpec((da, d), lambda i: (0, 0)),
                  pl.BlockSpec((dm, d), lambda i: (0, 0)),
                  pl.BlockSpec((1, d), lambda i: (0, 0)),
                  pl.BlockSpec((1, d), lambda i: (0, 0))],
        out_specs=pl.BlockSpec((tm, d), lambda i: (i, 0)),
        out_shape=jax.ShapeDtypeStruct((m, d), F32),
        compiler_params=_cparams(1),
        name="out_ln",
    )(x, mix_a, mix_m, wo_a, wo_m, g, b)


def _route(logits):
    lane = lax.broadcasted_iota(I32, logits.shape, 1)
    big = jnp.int32(4 * LANES)
    is_g = lane < N_GROUPS
    lg = jnp.where(is_g, logits, -jnp.inf)
    gmax = jnp.max(lg, axis=-1, keepdims=True)
    grp = jnp.min(jnp.where(lg == gmax, lane, big), axis=-1, keepdims=True)
    gsum = jnp.sum(jnp.where(is_g, jnp.exp(logits - gmax), 0.0), axis=-1, keepdims=True)
    g_w = 1.0 / gsum
    e_idx = lane - N_GROUPS
    in_grp = (lane >= N_GROUPS) & (lane < N_GROUPS + N_EXPERTS) & ((e_idx >> 3) == grp)
    v = jnp.where(in_grp, logits, -jnp.inf)
    v1 = jnp.max(v, axis=-1, keepdims=True)
    i1 = jnp.min(jnp.where(v == v1, lane, big), axis=-1, keepdims=True)
    vr = jnp.where(lane == i1, -jnp.inf, v)
    v2 = jnp.max(vr, axis=-1, keepdims=True)
    i2 = jnp.min(jnp.where(vr == v2, lane, big), axis=-1, keepdims=True)
    e2 = jnp.exp(v2 - v1)
    w1 = g_w / (1.0 + e2)
    w2 = g_w * e2 / (1.0 + e2)
    return jnp.where(lane == i1, w1, 0.0) + jnp.where(lane == i2, w2, 0.0)


def _moe_kernel(x_ref, wr_ref, rb_ref, wgu_ref, wd_ref, g_ref, b_ref, o_ref,
                xb_ref, comb_ref, acc_ref, *, alpha):
    grp = pl.program_id(1)
    hid = EXPERTS_PER_GROUP * D_EXPERT

    @pl.when(grp == 0)
    def _():
        xh = x_ref[...].astype(BF16)
        logits = _dot(xh, wr_ref[...]) + rb_ref[...]
        xb_ref[...] = xh
        comb_ref[...] = _route(logits)
        acc_ref[...] = jnp.zeros(acc_ref.shape, F32)

    xb = xb_ref[...]
    gu = _dot(xb, wgu_ref[0])
    gate = gu[:, :hid]
    up = gu[:, hid:]
    hdn = gate * jax.nn.sigmoid(gate) * up
    comb = comb_ref[...]
    lane = lax.broadcasted_iota(I32, comb.shape, 1)
    scale = []
    for e in range(EXPERTS_PER_GROUP):
        col = jnp.sum(jnp.where(lane == N_GROUPS + grp * EXPERTS_PER_GROUP + e, comb, 0.0),
                      axis=-1, keepdims=True)
        scale.append(jnp.broadcast_to(col, (comb.shape[0], D_EXPERT)))
    hdn = hdn * jnp.concatenate(scale, axis=1)
    acc_ref[...] += _dot(hdn.astype(BF16), wd_ref[0])

    @pl.when(grp == N_GROUPS - 1)
    def _():
        o_ref[...] = _layer_norm(alpha * x_ref[...] + acc_ref[...], g_ref[...], b_ref[...])


def _moe_ln(x, wr, rb, wgu, wd, g, b, alpha, tm):
    m, d = x.shape
    hid = EXPERTS_PER_GROUP * D_EXPERT
    return pl.pallas_call(
        functools.partial(_moe_kernel, alpha=alpha),
        grid=(m // tm, N_GROUPS),
        in_specs=[pl.BlockSpec((tm, d), lambda i, j: (i, 0)),
                  pl.BlockSpec((d, LANES), lambda i, j: (0, 0)),
                  pl.BlockSpec((1, LANES), lambda i, j: (0, 0)),
                  pl.BlockSpec((1, d, 2 * hid), lambda i, j: (j, 0, 0)),
                  pl.BlockSpec((1, hid, d), lambda i, j: (j, 0, 0)),
                  pl.BlockSpec((1, d), lambda i, j: (0, 0)),
                  pl.BlockSpec((1, d), lambda i, j: (0, 0))],
        out_specs=pl.BlockSpec((tm, d), lambda i, j: (i, 0)),
        out_shape=jax.ShapeDtypeStruct((m, d), F32),
        scratch_shapes=[pltpu.VMEM((tm, d), BF16), pltpu.VMEM((tm, LANES), F32), pltpu.VMEM((tm, d), F32)],
        compiler_params=_cparams(2),
        name="moe_ln",
    )(x, wr, rb, wgu, wd, g, b)


CMP_MBLK = 128
ROW_TILES = D_KV // LANES
PAIR_HID = 2 * CMP_HID


def _compress_weights(pe, w1, b1, w2):
    w1 = w1.reshape(2, CMP_LEN, HEAD_DIM, CMP_HID)
    z1 = jnp.zeros_like(w1)
    w1p = jnp.concatenate([jnp.concatenate([w1, z1], axis=3), jnp.concatenate([z1, w1], axis=3)], axis=2)
    z2 = jnp.zeros_like(w2)
    w2p = jnp.concatenate([jnp.concatenate([w2, z2], axis=2), jnp.concatenate([z2, w2], axis=2)], axis=1)
    w1p = w1p.astype(BF16).reshape(2, 2, CMP_STRIDE * LANES, PAIR_HID)
    return (jnp.concatenate([pe, pe], axis=2), w1p,
            jnp.concatenate([b1, b1], axis=1)[:, None], w2p.astype(BF16))


def _compress_rows(x_ref, n_chunk, pe_ref, w1_ref, b1_ref, w2_ref, out_ref, a_ref, b_ref):
    n_blk = n_chunk // CMP_MBLK
    step = CMP_STRIDE * ROW_TILES
    for kv in range(2):
        for pair in range(N_KV_GROUPS // 2):
            tile = kv * (N_KV_GROUPS // 2) + pair

            def fill(mb, carry, kv=kv, tile=tile):
                row0 = pl.multiple_of(mb * (CMP_MBLK * step), CMP_MBLK * step)
                top, bot = [], []
                for r in range(CMP_STRIDE):
                    xg = x_ref[pl.ds(row0 + r * ROW_TILES + tile, CMP_MBLK, stride=step), :]
                    top.append((xg + pe_ref[kv, r:r + 1, :]).astype(BF16))
                    bot.append((xg + pe_ref[kv, CMP_STRIDE + r:CMP_STRIDE + r + 1, :]).astype(BF16))
                m0 = pl.multiple_of(mb * CMP_MBLK, CMP_MBLK)
                a_ref[pl.ds(m0, CMP_MBLK), :] = _dot(jnp.concatenate(top, axis=1), w1_ref[kv, 0])
                b_ref[pl.ds(m0, CMP_MBLK), :] = _dot(jnp.concatenate(bot, axis=1), w1_ref[kv, 1])
                return carry

            lax.fori_loop(0, n_blk, fill, 0)
            b_ref[n_chunk:n_chunk + SUBLANES, :] = jnp.zeros((SUBLANES, PAIR_HID), F32)
            for mb in range(n_blk):
                m0 = mb * CMP_MBLK
                hid = jax.nn.gelu(a_ref[m0:m0 + CMP_MBLK, :] + b_ref[m0 + 1:m0 + 1 + CMP_MBLK, :] + b1_ref[kv])
                out_ref[m0:m0 + CMP_MBLK, tile * LANES:(tile + 1) * LANES] = _dot(hid.astype(BF16), w2_ref[kv])


def _compress_prompt_kernel(x_ref, pe_ref, w1_ref, b1_ref, w2_ref, o_ref, a_ref, b_ref, *, n_chunk):
    _compress_rows(x_ref.at[0], n_chunk, pe_ref, w1_ref, b1_ref, w2_ref, o_ref.at[0], a_ref, b_ref)


def _compress_prompt(rows, pe, w1, b1, w2):
    b, t_len, _ = rows.shape
    n_chunk = t_len // CMP_STRIDE
    assert n_chunk % CMP_MBLK == 0
    full = lambda a: pl.BlockSpec(a.shape, lambda i: (0,) * a.ndim)
    return pl.pallas_call(
        functools.partial(_compress_prompt_kernel, n_chunk=n_chunk),
        grid=(b,),
        in_specs=[pl.BlockSpec((1, t_len * ROW_TILES, LANES), lambda i: (i, 0, 0)),
                  full(pe), full(w1), full(b1), full(w2)],
        out_specs=pl.BlockSpec((1, n_chunk, D_KV), lambda i: (i, 0, 0)),
        out_shape=jax.ShapeDtypeStruct((b, n_chunk, D_KV), F32),
        scratch_shapes=[pltpu.VMEM((n_chunk, PAIR_HID), F32), pltpu.VMEM((n_chunk + SUBLANES, PAIR_HID), F32)],
        compiler_params=_cparams(1),
        name="compress_prompt",
    )(rows.reshape(b, t_len * ROW_TILES, LANES), pe, w1, b1, w2)


def _masked_softmax_rows(s, mask):
    sm = jnp.where(mask, s, NEG)
    m = jnp.max(sm, axis=-1, keepdims=True)
    e = jnp.where(mask, jnp.exp(sm - m), 0.0)
    tot = jnp.sum(e, axis=-1, keepdims=True)
    return e * jnp.where(tot > 0.0, 1.0 / tot, 0.0)


WIN_TILES = WINDOW // Q_BLOCK


KEY_TILE = 512
KEY_SUB = KEY_TILE // Q_BLOCK


def _attend_tiles_t(qts, k_ref, vt_ref, t_lo, t_hi, i_blk, bias_ref, bias_index, mask_fn):
    cols = qts[0].shape[1]

    def body(t, carry):
        k0 = pl.multiple_of(t * KEY_TILE, KEY_TILE)
        bases = [bias_index(i_blk - (t * KEY_SUB + u)) for u in range(KEY_SUB)]
        new = []
        for g, qt in enumerate(qts):
            m, l, acc = carry[g]
            kt = k_ref[0, g, pl.ds(k0, KEY_TILE), :]
            vt = vt_ref[0, g * HEAD_DIM:(g + 1) * HEAD_DIM, pl.ds(k0, KEY_TILE)]
            heads = [g * HEADS_PER_GROUP + hp for hp in range(HEADS_PER_GROUP)]
            bias = jnp.concatenate(
                [jnp.concatenate([bias_ref[base + h] for h in heads], axis=1) for base in bases], axis=0)
            s = _dot(kt, qt) + bias
            extra = mask_fn(t, g)
            if extra is not None:
                s = s + jnp.concatenate([extra] * HEADS_PER_GROUP, axis=1)
            m_new = jnp.maximum(m, jnp.max(s, axis=0, keepdims=True))
            alpha = jnp.exp(m - m_new)
            p = jnp.exp(s - m_new)
            new.append((m_new, alpha * l + jnp.sum(p, axis=0, keepdims=True), alpha * acc + _dot(vt, p.astype(BF16))))
        return tuple(new)

    init = tuple((jnp.full((1, cols), NEG, F32), jnp.zeros((1, cols), F32), jnp.zeros((HEAD_DIM, cols), F32))
                 for _ in qts)
    fin = lax.fori_loop(t_lo, t_hi, body, init)
    return [acc / l for (_, l, acc) in fin]


def _nsa_prompt_kernel(tab_ref, q_ref, gl_ref, ck_ref, cvt_ref, sk_ref, svt_ref, wk_ref, wvt_ref, o_ref,
                       bias_ref, cbias_ref, sel_ref, *, n_tiles):
    b = pl.program_id(0)
    i = pl.program_id(1)
    kk = lax.broadcasted_iota(I32, (Q_BLOCK, Q_BLOCK), 0)
    qq = lax.broadcasted_iota(I32, (Q_BLOCK, Q_BLOCK), 1)
    n_cmp = ck_ref.shape[2]
    n_sel_blocks = 2 * n_tiles
    edge_base = n_tiles * N_HEADS
    none_base = (n_tiles + 1) * N_HEADS

    @pl.when((b == 0) & (i == 0))
    def _():
        def build(delta, carry):
            dist = delta * Q_BLOCK + qq - kk
            biases = _bias_from_bucket(_rel_bucket(dist), tab_ref, range(N_HEADS))
            for h in range(N_HEADS):
                bias_ref[delta * N_HEADS + h] = jnp.where(dist >= 0, biases[h], NEG)
            return carry
        lax.fori_loop(0, n_tiles, build, 0)
        dist = WIN_TILES * Q_BLOCK + qq - kk
        biases = _bias_from_bucket(_rel_bucket(dist), tab_ref, range(N_HEADS))
        for h in range(N_HEADS):
            bias_ref[edge_base + h] = jnp.where(dist < WINDOW, biases[h], NEG)
            bias_ref[none_base + h] = jnp.full((Q_BLOCK, Q_BLOCK), NEG, F32)

        def build_c(t, carry):
            dist_c = t * Q_BLOCK + qq - (kk * CMP_STRIDE + (CMP_LEN - 1))
            ok = (dist_c >= 0) & (kk < n_cmp - 1)
            biases_c = _bias_from_bucket(_rel_bucket(dist_c), tab_ref, range(N_HEADS))
            for h in range(N_HEADS):
                cbias_ref[t * N_HEADS + h] = jnp.where(ok, biases_c[h], NEG)
            return carry
        lax.fori_loop(0, n_tiles, build_c, 0)

    qs = i * Q_BLOCK
    q_all = q_ref[0] * SCALE
    gates = jax.nn.sigmoid(gl_ref[0])

    blk = lax.broadcasted_iota(I32, (n_sel_blocks, Q_BLOCK), 0)
    col = lax.broadcasted_iota(I32, (n_sel_blocks, Q_BLOCK), 1)
    c_start = col * CMP_STRIDE
    s_start = blk * SEL_LEN
    cover_t = jnp.where((c_start < s_start + SEL_LEN) & (c_start + CMP_LEN > s_start) & (col < n_cmp - 1),
                        1.0, 0.0).astype(BF16)
    q_pos = qs + col
    cur = q_pos // SEL_LEN
    forced = (blk == 0) | (blk == cur) | (blk == cur - 1)
    valid = blk * SEL_LEN <= q_pos
    upper = kk < SEL_LEN

    qts, o_cs, splits = [], [], []
    for g in range(N_KV_GROUPS):
        heads = [g * HEADS_PER_GROUP + hp for hp in range(HEADS_PER_GROUP)]
        qt = jnp.concatenate([q_all[h * HEAD_DIM:(h + 1) * HEAD_DIM, :] for h in heads], axis=1)
        qts.append(qt)

        s_c = _dot(ck_ref[0, g], qt) + jnp.concatenate([cbias_ref[i * N_HEADS + h] for h in heads], axis=1)
        m_c = jnp.max(s_c, axis=0, keepdims=True)
        e_c = jnp.exp(s_c - m_c)
        tot = jnp.sum(e_c, axis=0, keepdims=True)
        p_c = e_c * jnp.where(m_c > 0.5 * NEG, 1.0 / tot, 0.0)
        o_cs.append(_dot(cvt_ref[0, g * HEAD_DIM:(g + 1) * HEAD_DIM, :], p_c.astype(BF16)))
        p_sum = p_c[:, 0:Q_BLOCK]
        for hp in range(1, HEADS_PER_GROUP):
            p_sum = p_sum + p_c[:, hp * Q_BLOCK:(hp + 1) * Q_BLOCK]
        splits.append(p_sum.astype(BF16))

    imp_all = _dot(cover_t, jnp.concatenate(splits, axis=1))
    for g in range(N_KV_GROUPS):
        imp = imp_all[:, g * Q_BLOCK:(g + 1) * Q_BLOCK]
        score = jnp.where(forced, FORCE, imp)
        score = jnp.where(valid, score, NEG)
        cnt = jnp.zeros(score.shape, F32)
        for s in range(n_sel_blocks):
            row = score[s:s + 1, :]
            beats = (row > score) | ((row == score) & (blk > s))
            cnt = cnt + jnp.where(beats, 1.0, 0.0)
        sel = (cnt < float(N_SEL)) & (score > 0.5 * NEG)
        sel_ref[g] = jnp.where(sel, 0.0, NEG)

    def mask_sel(t, g):
        rows = sel_ref[g, pl.ds(pl.multiple_of(2 * KEY_SUB * t, 2 * KEY_SUB), 2 * KEY_SUB), :]
        return jnp.concatenate([jnp.where(upper, rows[2 * u:2 * u + 1], rows[2 * u + 1:2 * u + 2])
                                for u in range(KEY_SUB)], axis=0)
    last = i // KEY_SUB + 1
    o_ss = _attend_tiles_t(qts, sk_ref, svt_ref, 0, last, i, bias_ref,
                           lambda d: jnp.where(d < 0, none_base, d * N_HEADS), mask_sel)

    def win_index(d):
        return jnp.where((d < 0) | (d > WIN_TILES), none_base, jnp.where(d == WIN_TILES, edge_base, d * N_HEADS))
    o_ws = _attend_tiles_t(qts, wk_ref, wvt_ref, jnp.maximum(i - WIN_TILES, 0) // KEY_SUB, last, i, bias_ref,
                           win_index, lambda t, g: None)

    for h in range(N_HEADS):
        g, hp = divmod(h, HEADS_PER_GROUP)
        c0 = hp * Q_BLOCK
        o_h = (gates[3 * h:3 * h + 1, :] * o_cs[g][:, c0:c0 + Q_BLOCK]
               + gates[3 * h + 1:3 * h + 2, :] * o_ss[g][:, c0:c0 + Q_BLOCK]
               + gates[3 * h + 2:3 * h + 3, :] * o_ws[g][:, c0:c0 + Q_BLOCK])
        o_ref[0, h * HEAD_DIM:(h + 1) * HEAD_DIM, :] = o_h.astype(o_ref.dtype)


def _nsa_prompt(tab, q_t, gl_t, comp_k, comp_vt, slc_k, slc_vt, win_k, win_vt):
    b, dq, t_len = q_t.shape
    n_tiles = t_len // Q_BLOCK
    n_cmp = comp_k.shape[2]
    assert n_cmp == Q_BLOCK and t_len % KEY_TILE == 0 and n_tiles > WIN_TILES
    dv = N_KV_GROUPS * HEAD_DIM
    return pl.pallas_call(
        functools.partial(_nsa_prompt_kernel, n_tiles=n_tiles),
        grid=(b, n_tiles),
        in_specs=[pl.BlockSpec(memory_space=pltpu.SMEM),
                  pl.BlockSpec((1, dq, Q_BLOCK), lambda i, j: (i, 0, j)),
                  pl.BlockSpec((1, LANES, Q_BLOCK), lambda i, j: (i, 0, j)),
                  pl.BlockSpec((1, N_KV_GROUPS, n_cmp, HEAD_DIM), lambda i, j: (i, 0, 0, 0)),
                  pl.BlockSpec((1, dv, n_cmp), lambda i, j: (i, 0, 0)),
                  pl.BlockSpec((1, N_KV_GROUPS, t_len, HEAD_DIM), lambda i, j: (i, 0, 0, 0)),
                  pl.BlockSpec((1, dv, t_len), lambda i, j: (i, 0, 0)),
                  pl.BlockSpec((1, N_KV_GROUPS, t_len, HEAD_DIM), lambda i, j: (i, 0, 0, 0)),
                  pl.BlockSpec((1, dv, t_len), lambda i, j: (i, 0, 0))],
        out_specs=pl.BlockSpec((1, dq, Q_BLOCK), lambda i, j: (i, 0, j)),
        out_shape=jax.ShapeDtypeStruct((b, dq, t_len), BF16),
        scratch_shapes=[pltpu.VMEM(((n_tiles + 2) * N_HEADS, Q_BLOCK, Q_BLOCK), F32),
                        pltpu.VMEM((n_tiles * N_HEADS, Q_BLOCK, Q_BLOCK), F32),
                        pltpu.VMEM((N_KV_GROUPS, 2 * n_tiles, Q_BLOCK), F32)],
        compiler_params=_cparams(2),
        name="nsa_prompt",
    )(tab, q_t, gl_t, comp_k, comp_vt, slc_k, slc_vt, win_k, win_vt)


def _page_copy(pt_ref, pool_ref, buf_ref, sem, b, j, page):
    start = pl.multiple_of(j * page, page)
    if len(buf_ref.shape) == 2:
        dst = buf_ref.at[pl.ds(start, page)]
    else:
        dst = buf_ref.at[:, :, pl.ds(start, page)]
    return pltpu.make_async_copy(pool_ref.at[pt_ref[b, j]], dst, sem)


def _gather_pages_start(pt_ref, pool_ref, buf_ref, sem, b, n_pages, page):
    def go(j, carry):
        _page_copy(pt_ref, pool_ref, buf_ref, sem, b, j, page).start()
        return carry
    lax.fori_loop(0, n_pages, go, 0)


def _gather_pages_wait(pt_ref, pool_ref, buf_ref, sem, b, n_pages, page):
    def go(j, carry):
        _page_copy(pt_ref, pool_ref, buf_ref, sem, b, j, page).wait()
        return carry
    lax.fori_loop(0, n_pages, go, 0)


def _compress_pages_kernel(pt_ref, pool_ref, pe_ref, w1_ref, b1_ref, w2_ref, o_ref, x_ref, a_ref, b_ref, sem,
                           *, n_pages, page, n_chunk):
    b = pl.program_id(0)
    _gather_pages_start(pt_ref, pool_ref, x_ref, sem, b, n_pages, page)
    _gather_pages_wait(pt_ref, pool_ref, x_ref, sem, b, n_pages, page)
    _compress_rows(x_ref, n_chunk, pe_ref, w1_ref, b1_ref, w2_ref, o_ref.at[0], a_ref, b_ref)


def _compress_pages(page_table, pool, pe, w1, b1, w2):
    b, n_pages = page_table.shape
    n_pool, page, _ = pool.shape
    past = n_pages * page
    n_chunk = past // CMP_STRIDE
    assert n_chunk % CMP_MBLK == 0
    full = lambda a: pl.BlockSpec(a.shape, lambda i: (0,) * a.ndim)
    return pl.pallas_call(
        functools.partial(_compress_pages_kernel, n_pages=n_pages, page=page * ROW_TILES, n_chunk=n_chunk),
        grid=(b,),
        in_specs=[pl.BlockSpec(memory_space=pltpu.SMEM), pl.BlockSpec(memory_space=pl.ANY),
                  full(pe), full(w1), full(b1), full(w2)],
        out_specs=pl.BlockSpec((1, n_chunk, D_KV), lambda i: (i, 0, 0)),
        out_shape=jax.ShapeDtypeStruct((b, n_chunk, D_KV), F32),
        scratch_shapes=[pltpu.VMEM((past * ROW_TILES, LANES), F32),
                        pltpu.VMEM((n_chunk, PAIR_HID), F32), pltpu.VMEM((n_chunk + SUBLANES, PAIR_HID), F32),
                        pltpu.SemaphoreType.DMA(())],
        compiler_params=_cparams(1),
        name="compress_pages",
    )(page_table, pool.reshape(n_pool, page * ROW_TILES, LANES), pe, w1, b1, w2)


DEC_TILE = 512
DEC_ROWS = 16
D_K = D_KV // 2


def _row_bias(dist_row, tab_ref, bias_ref, col0):
    length = dist_row.shape[1]
    biases = _bias_from_bucket(_rel_bucket(dist_row), tab_ref, range(N_HEADS))
    rows = biases + [jnp.zeros((DEC_ROWS - N_HEADS, length), F32)]
    bias_ref[:, pl.ds(col0, length)] = jnp.concatenate(rows, axis=0)


def _head_block(o_all, h):
    g = h // HEADS_PER_GROUP
    return o_all[h:h + 1, g * HEAD_DIM:(g + 1) * HEAD_DIM]


def _new_token_scores(q_blk, new_row, bias0):
    k_new = new_row[:, :D_K].astype(BF16).astype(F32)
    v_new = new_row[:, D_K:].astype(BF16).astype(F32)
    return jnp.sum(q_blk.astype(F32) * k_new, axis=-1, keepdims=True) + bias0, v_new


def _nsa_step_kernel(tab_ref, pt_ref, q_ref, gl_ref, comp_ref, pool_ref, slc_new_ref, win_ref, win_new_ref, o_ref,
                     kv_buf, bias_c_ref, bias_s_ref, bias_w_ref, sems,
                     *, n_seq, n_pages, page, past, n_cmp, w_buf):
    b = pl.program_id(0)
    slot = b % 2
    n_tiles = past // DEC_TILE
    n_blocks = past // SEL_LEN + 1
    blk_lanes = -(-n_blocks // LANES) * LANES

    @pl.when(b == 0)
    def _():
        _gather_pages_start(pt_ref, pool_ref, kv_buf.at[0], sems.at[0], b, n_pages, page)

    @pl.when(b + 1 < n_seq)
    def _():
        _gather_pages_start(pt_ref, pool_ref, kv_buf.at[1 - slot], sems.at[1 - slot], b + 1, n_pages, page)

    lane_t = lax.broadcasted_iota(I32, (1, DEC_TILE), 1)

    @pl.when(b == 0)
    def _():
        lane_c = lax.broadcasted_iota(I32, (1, n_cmp), 1)
        _row_bias(past - (lane_c * CMP_STRIDE + (CMP_LEN - 1)), tab_ref, bias_c_ref, 0)

        def tile_bias(t, carry):
            k0 = pl.multiple_of(t * DEC_TILE, DEC_TILE)
            _row_bias(past - (k0 + lane_t), tab_ref, bias_s_ref, k0)
            return carry
        lax.fori_loop(0, n_tiles, tile_bias, 0)
        lane_w = lax.broadcasted_iota(I32, (1, w_buf), 1)
        _row_bias(w_buf - lane_w, tab_ref, bias_w_ref, 0)

    bias0 = jnp.concatenate([jnp.full((1, 1), tab_ref[0, h], F32) for h in range(N_HEADS)]
                            + [jnp.zeros((DEC_ROWS - N_HEADS, 1), F32)], axis=0)

    q_row = q_ref[0] * SCALE
    gates = jax.nn.sigmoid(gl_ref[0])
    rows = []
    for h in range(N_HEADS):
        g = h // HEADS_PER_GROUP
        parts = []
        if g > 0:
            parts.append(jnp.zeros((1, g * HEAD_DIM), BF16))
        parts.append(q_row[:, h * HEAD_DIM:(h + 1) * HEAD_DIM])
        if g < N_KV_GROUPS - 1:
            parts.append(jnp.zeros((1, (N_KV_GROUPS - 1 - g) * HEAD_DIM), BF16))
        rows.append(jnp.concatenate(parts, axis=1))
    q_blk = jnp.concatenate(rows + [jnp.zeros((DEC_ROWS - N_HEADS, D_K), BF16)], axis=0)

    comp = comp_ref[0].astype(BF16)
    mask_c = lax.broadcasted_iota(I32, (DEC_ROWS, n_cmp), 1) < n_cmp - 1
    p_c = _masked_softmax_rows(_dot_nt(q_blk, comp[:, :D_K]) + bias_c_ref[...], mask_c)
    o_c = _dot(p_c.astype(BF16), comp[:, D_K:])
    p_sums = [jnp.sum(p_c[g * HEADS_PER_GROUP:(g + 1) * HEADS_PER_GROUP], axis=0, keepdims=True)
              for g in range(N_KV_GROUPS)]
    p_sum = jnp.concatenate(p_sums + [jnp.zeros((DEC_ROWS - N_KV_GROUPS, n_cmp), F32)], axis=0)

    c_start = lax.broadcasted_iota(I32, (n_cmp, blk_lanes), 0) * CMP_STRIDE
    s_start = lax.broadcasted_iota(I32, (n_cmp, blk_lanes), 1) * SEL_LEN
    c_idx = lax.broadcasted_iota(I32, (n_cmp, blk_lanes), 0)
    cover = jnp.where((c_start < s_start + SEL_LEN) & (c_start + CMP_LEN > s_start) & (c_idx < n_cmp - 1),
                      1.0, 0.0).astype(BF16)
    imp = _dot(p_sum.astype(BF16), cover)
    blk = lax.broadcasted_iota(I32, (DEC_ROWS, blk_lanes), 1)
    cur = past // SEL_LEN
    forced = (blk == 0) | (blk == cur) | (blk == cur - 1)
    score = jnp.where(forced, FORCE, imp)
    score = jnp.where(blk < n_blocks, score, NEG)
    sel = jnp.zeros(score.shape, F32)
    big = jnp.int32(4 * blk_lanes)
    for _ in range(N_SEL):
        mx = jnp.max(score, axis=-1, keepdims=True)
        idx = jnp.min(jnp.where(score == mx, blk, big), axis=-1, keepdims=True)
        hit = blk == idx
        sel = jnp.where(hit & (mx > 0.5 * NEG), 1.0, sel)
        score = jnp.where(hit, -jnp.inf, score)
    sel_rows = [sel[h // HEADS_PER_GROUP:h // HEADS_PER_GROUP + 1] for h in range(N_HEADS)]
    sel_bf = jnp.concatenate(sel_rows + [jnp.zeros((DEC_ROWS - N_HEADS, blk_lanes), F32)], axis=0).astype(BF16)

    _gather_pages_wait(pt_ref, pool_ref, kv_buf.at[slot], sems.at[slot], b, n_pages, page)
    e_row = lax.broadcasted_iota(I32, (blk_lanes, DEC_TILE), 0)
    e_col = lax.broadcasted_iota(I32, (blk_lanes, DEC_TILE), 1) // SEL_LEN

    def sel_tile(t, carry):
        m, l, acc = carry
        k0 = pl.multiple_of(t * DEC_TILE, DEC_TILE)
        expand = jnp.where(e_row == e_col + t * (DEC_TILE // SEL_LEN), 1.0, 0.0).astype(BF16)
        key_sel = _dot(sel_bf, expand)
        kt = kv_buf[slot, 0, :, pl.ds(k0, DEC_TILE)].astype(BF16)
        vt = kv_buf[slot, 1, :, pl.ds(k0, DEC_TILE)].astype(BF16)
        s = _dot(q_blk, kt) + bias_s_ref[:, pl.ds(k0, DEC_TILE)] + jnp.where(key_sel > 0.5, 0.0, NEG)
        m_new = jnp.maximum(m, jnp.max(s, axis=-1, keepdims=True))
        alpha = jnp.exp(m - m_new)
        p = jnp.exp(s - m_new)
        return m_new, alpha * l + jnp.sum(p, axis=-1, keepdims=True), alpha * acc + _dot_nt(p.astype(BF16), vt)

    init = (jnp.full((DEC_ROWS, 1), NEG, F32), jnp.zeros((DEC_ROWS, 1), F32), jnp.zeros((DEC_ROWS, D_K), F32))
    m_s, l_s, acc_s = lax.fori_loop(0, n_tiles, sel_tile, init)
    s_new, v_new = _new_token_scores(q_blk, slc_new_ref[0], bias0)
    cur_sel = sel_bf[:, cur:cur + 1].astype(F32)
    s_new = s_new + jnp.where(cur_sel > 0.5, 0.0, NEG)
    m_fin = jnp.maximum(m_s, s_new)
    alpha = jnp.exp(m_s - m_fin)
    p_new = jnp.exp(s_new - m_fin)
    o_s = (alpha * acc_s + p_new.astype(BF16).astype(F32) * v_new) / (alpha * l_s + p_new)

    kw = win_ref[0, 0].astype(BF16)
    vw = win_ref[0, 1].astype(BF16)
    idx_w = lax.broadcasted_iota(I32, (DEC_ROWS, w_buf), 1)
    s_w = jnp.where(w_buf - idx_w < WINDOW, _dot(q_blk, kw) + bias_w_ref[...], NEG)
    s_wn, v_wn = _new_token_scores(q_blk, win_new_ref[0], bias0)
    m_w = jnp.maximum(jnp.max(s_w, axis=-1, keepdims=True), s_wn)
    e_w = jnp.exp(s_w - m_w)
    e_wn = jnp.exp(s_wn - m_w)
    o_w = ((_dot_nt(e_w.astype(BF16), vw) + e_wn.astype(BF16).astype(F32) * v_wn)
           / (jnp.sum(e_w, axis=-1, keepdims=True) + e_wn))

    outs = []
    for h in range(N_HEADS):
        outs.append(gates[:, 3 * h:3 * h + 1] * _head_block(o_c, h)
                    + gates[:, 3 * h + 1:3 * h + 2] * _head_block(o_s, h)
                    + gates[:, 3 * h + 2:3 * h + 3] * _head_block(o_w, h))
    o_ref[0] = jnp.concatenate(outs, axis=1).astype(o_ref.dtype)


def _feature_major(cache):
    nd = cache.ndim
    perm = tuple(range(nd - 4)) + (nd - 3, nd - 2, nd - 1, nd - 4)
    t = cache.transpose(perm)
    return t.reshape(t.shape[:nd - 3] + (t.shape[nd - 3] * t.shape[nd - 2], t.shape[nd - 1]))


def _nsa_step(tab, page_table, q, gl, comp, pool_t, slc_new, win_t, win_new):
    b, n_pages = page_table.shape
    page = pool_t.shape[3]
    past = n_pages * page
    n_cmp = comp.shape[1]
    w_buf = win_t.shape[3]
    dq = q.shape[2]
    assert past % DEC_TILE == 0 and past % SEL_LEN == 0 and w_buf == WINDOW and n_cmp == past // CMP_STRIDE
    row3 = lambda w: pl.BlockSpec((1, 1, w), lambda i: (i, 0, 0))
    return pl.pallas_call(
        functools.partial(_nsa_step_kernel, n_seq=b, n_pages=n_pages, page=page, past=past, n_cmp=n_cmp,
                          w_buf=w_buf),
        grid=(b,),
        in_specs=[pl.BlockSpec(memory_space=pltpu.SMEM), pl.BlockSpec(memory_space=pltpu.SMEM),
                  row3(dq), row3(LANES),
                  pl.BlockSpec((1, n_cmp, D_KV), lambda i: (i, 0, 0)),
                  pl.BlockSpec(memory_space=pl.ANY),
                  row3(D_KV),
                  pl.BlockSpec((1, 2, D_K, w_buf), lambda i: (i, 0, 0, 0)),
                  row3(D_KV)],
        out_specs=row3(dq),
        out_shape=jax.ShapeDtypeStruct((b, 1, dq), BF16),
        scratch_shapes=[pltpu.VMEM((2, 2, D_K, past), F32),
                        pltpu.VMEM((DEC_ROWS, n_cmp), F32),
                        pltpu.VMEM((DEC_ROWS, past), F32),
                        pltpu.VMEM((DEC_ROWS, w_buf), F32),
                        pltpu.SemaphoreType.DMA((2,))],
        compiler_params=_cparams(1),
        name="nsa_step",
    )(tab, page_table, q, gl, comp, pool_t, slc_new, win_t, win_new)


def _block_diag_pairs(w):
    n, blk, _ = w.shape
    w = w.reshape(n // 2, 2, blk, blk)
    z = jnp.zeros((n // 2, blk, blk), w.dtype)
    top = jnp.concatenate([w[:, 0], z], axis=2)
    bot = jnp.concatenate([z, w[:, 1]], axis=2)
    return jnp.concatenate([top, bot], axis=1).astype(BF16)


def _prep_weights(w_in_a, conv_w, conv_b, w_rg, b_rg, w_ig, b_ig, lru_lambda, w_in_b, w_kv_shared, cmp_pe, cmp_w1,
                  cmp_b1, cmp_w2, w_mem_kv, w_out, ln1_g, ln1_b, ln2_g, ln2_b, w_router_g, b_router_g,
                  w_router_e, b_router_e, w_exp_gate, w_exp_up, w_exp_down):
    depth, d_model, _ = w_out.shape
    n_a = w_in_a.shape[0]
    nq = N_HEADS * HEAD_DIM
    n_gl = 3 * N_HEADS
    hid = EXPERTS_PER_GROUP * D_EXPERT
    p = {}
    p["w_in_a"] = w_in_a.astype(BF16)
    p["w_in_b"] = jnp.concatenate(
        [w_in_b[:, :, :nq], w_in_b[:, :, nq + n_gl:], w_in_b[:, :, nq:nq + n_gl],
         jnp.zeros((w_in_b.shape[0], d_model, LANES - n_gl), w_in_b.dtype)], axis=2).astype(BF16)
    p["w_kv"] = w_kv_shared.astype(BF16)
    p["w_mem"] = w_mem_kv.transpose(1, 0, 2).reshape(d_model, depth * 2 * D_MEM).astype(BF16)
    p["wo_a"] = w_out[:, :D_RNN].astype(BF16)
    p["wo_m"] = w_out[:, D_RNN:].astype(BF16)
    p["conv_w"] = conv_w
    p["conv_b"] = conv_b[:, None]
    p["w_rg"] = jnp.stack([_block_diag_pairs(w_rg[l]) for l in range(n_a)])
    p["w_ig"] = jnp.stack([_block_diag_pairs(w_ig[l]) for l in range(n_a)])
    p["b_rg"] = b_rg[:, None]
    p["b_ig"] = b_ig[:, None]
    p["lam"] = lru_lambda[:, None]
    p["pe"], p["cmp_w1"], p["cmp_b1"], p["cmp_w2"] = _compress_weights(cmp_pe, cmp_w1, cmp_b1, cmp_w2)
    p["ln1_g"], p["ln1_b"], p["ln2_g"], p["ln2_b"] = ln1_g[:, None], ln1_b[:, None], ln2_g[:, None], ln2_b[:, None]
    n_r = N_GROUPS + N_EXPERTS
    wr = jnp.concatenate([w_router_g, w_router_e, jnp.zeros((depth, d_model, LANES - n_r), F32)], axis=2)
    p["wr"] = wr.astype(BF16)
    p["rb"] = jnp.concatenate([b_router_g, b_router_e, jnp.zeros((depth, LANES - n_r), F32)], axis=1)[:, None]

    def by_group(w):
        w = w.astype(BF16).reshape(depth, N_GROUPS, EXPERTS_PER_GROUP, d_model, D_EXPERT)
        return w.transpose(0, 1, 3, 2, 4).reshape(depth, N_GROUPS, d_model, hid)
    p["w_gu"] = jnp.concatenate([by_group(w_exp_gate), by_group(w_exp_up)], axis=3)
    p["w_dn"] = w_exp_down.astype(BF16).reshape(depth, N_GROUPS, hid, d_model)
    return p


def _kv_layouts(rows):
    b, t_len, _ = rows.shape
    k = rows[:, :, :D_KV // 2].reshape(b, t_len, N_KV_GROUPS, HEAD_DIM).transpose(0, 2, 1, 3)
    return k, rows[:, :, D_KV // 2:].transpose(0, 2, 1)


def _trunk_tail(x, mix_a, mix_m, p, l, alpha, tm):
    x = _out_ln(x, mix_a, mix_m, p["wo_a"][l], p["wo_m"][l], p["ln1_g"][l], p["ln1_b"][l], alpha, tm)
    return _moe_ln(x, p["wr"][l], p["rb"][l], p["w_gu"][l], p["w_dn"][l],
                   p["ln2_g"][l], p["ln2_b"][l], alpha, tm)


def kernel(x_prompt, x_sample, mem_prompt, cache_cmp_kv, cache_slc_kv, cache_win_kv, cache_mem_kv, state_lru_h, state_conv, page_table, w_in_a, conv_w, conv_b, w_rg, b_rg, w_ig, b_ig, lru_lambda, w_in_b, w_kv_shared, cmp_pe, cmp_w1, cmp_b1, cmp_w2, rel_bias, w_mem_kv, w_out, ln1_g, ln1_b, ln2_g, ln2_b, w_router_g, b_router_g, w_router_e, b_router_e, w_exp_gate, w_exp_up, w_exp_down):
    p = _prep_weights(w_in_a, conv_w, conv_b, w_rg, b_rg, w_ig, b_ig, lru_lambda, w_in_b, w_kv_shared, cmp_pe,
                      cmp_w1, cmp_b1, cmp_w2, w_mem_kv, w_out, ln1_g, ln1_b, ln2_g, ln2_b, w_router_g,
                      b_router_g, w_router_e, b_router_e, w_exp_gate, w_exp_up, w_exp_down)
    depth, d_model, _ = w_out.shape
    n_a = w_in_a.shape[0]
    alpha = (2 * depth) ** 0.25
    kv_shape = (2, N_KV_GROUPS, HEAD_DIM)
    a_outs = [(0, D_RNN, F32), (D_RNN, D_RNN, F32), (2 * D_RNN, D_MEM, BF16)]
    b_outs = [(0, D_RNN, BF16), (D_RNN, D_MEM, BF16), (D_RNN + D_MEM, LANES, F32)]
    kv_outs = [(0, D_KV, F32), (D_KV, D_KV, F32), (2 * D_KV, D_KV, F32)]

    bp, t_len, _ = x_prompt.shape
    n_mem = mem_prompt.shape[1]
    tm = 512
    mem_rows = _proj(mem_prompt.reshape(bp * n_mem, d_model), p["w_mem"],
                     [(l * 2 * D_MEM, 2 * D_MEM, F32) for l in range(depth)], tm, "mem_kv_proj")
    p_mem = jnp.stack(mem_rows).reshape(depth, bp, n_mem, 2 * D_MEM)
    x = x_prompt.reshape(bp * t_len, d_model)
    lru_p, conv_p = [], []
    for l in range(depth):
        if l < n_a:
            gate, xr, mq = _proj(x, p["w_in_a"][l], a_outs, tm, "in_proj_a")
            mix_a, h_t, buf = _rglru_prompt(gate.reshape(bp, t_len, D_RNN), xr.reshape(bp, t_len, D_RNN),
                                            p["conv_w"][l], p["conv_b"][l], p["w_rg"][l], p["w_ig"][l],
                                            p["b_rg"][l], p["b_ig"][l], p["lam"][l])
            lru_p.append(h_t[:, 0])
            conv_p.append(buf)
        else:
            if l == n_a:
                cmp_rows, slc_rows, win_rows, slc_bf, win_bf = _proj(
                    x, p["w_kv"], kv_outs + [(D_KV, D_KV, BF16), (2 * D_KV, D_KV, BF16)], tm, "kv_proj")
                comp = _compress_prompt(cmp_rows.reshape(bp, t_len, D_KV), p["pe"], p["cmp_w1"], p["cmp_b1"],
                                        p["cmp_w2"])
                comp_k, comp_vt = _kv_layouts(comp.astype(BF16))
                slc_k, slc_vt = _kv_layouts(slc_bf.reshape(bp, t_len, D_KV))
                win_k, win_vt = _kv_layouts(win_bf.reshape(bp, t_len, D_KV))
            q, mq, gl = _proj(x, p["w_in_b"][l - n_a], b_outs, tm, "in_proj_b")
            mix_t = _nsa_prompt(rel_bias, q.reshape(bp, t_len, D_RNN).transpose(0, 2, 1),
                                gl.reshape(bp, t_len, LANES).transpose(0, 2, 1),
                                comp_k, comp_vt, slc_k, slc_vt, win_k, win_vt)
            mix_a = mix_t.transpose(0, 2, 1)
        mix_m = _mem_attn(mq.reshape(bp, t_len, D_MEM), p_mem, l)
        x = _trunk_tail(x, mix_a.reshape(bp * t_len, D_RNN), mix_m.reshape(bp * t_len, D_MEM), p, l, alpha, tm)
    y_prompt = x.reshape(bp, t_len, d_model)
    p_cmp_kv = cmp_rows.reshape((bp, t_len) + kv_shape)
    p_slc_kv = slc_rows.reshape((bp, t_len) + kv_shape)
    w_keep = min(WINDOW, t_len)
    p_win_kv = win_rows.reshape((bp, t_len) + kv_shape)[:, t_len - w_keep:]
    p_mem_kv = p_mem.reshape(depth, bp, n_mem, 2, N_MEM_HEADS, HEAD_DIM)
    p_lru_h = jnp.stack(lru_p)
    p_conv = jnp.stack(conv_p)

    bd, s_len, _ = x_sample.shape
    assert s_len == 1
    n_pool, page = cache_cmp_kv.shape[:2]
    w_buf = cache_win_kv.shape[1]
    mem_t = _feature_major(cache_mem_kv)
    slc_pool_t = _feature_major(cache_slc_kv)
    win_t = _feature_major(cache_win_kv)
    x = x_sample.reshape(bd, d_model)
    lru_s, conv_s = [], []
    for l in range(depth):
        if l < n_a:
            gate, xr, mq = _proj(x, p["w_in_a"][l], a_outs, bd, "in_proj_a_step")
            mix_a, h_t, buf = _rglru_step(gate, xr, state_conv[l].transpose(1, 0, 2), state_lru_h[l],
                                          p["conv_w"][l], p["conv_b"][l], p["w_rg"][l], p["w_ig"][l],
                                          p["b_rg"][l], p["b_ig"][l], p["lam"][l])
            lru_s.append(h_t)
            conv_s.append(buf.transpose(1, 0, 2))
        else:
            if l == n_a:
                s_cmp, s_slc, s_win = _proj(x, p["w_kv"], kv_outs, bd, "kv_proj_step")
                comp_s = _compress_pages(page_table, cache_cmp_kv.reshape(n_pool, page, D_KV), p["pe"],
                                         p["cmp_w1"], p["cmp_b1"], p["cmp_w2"])
            q, mq, gl = _proj(x, p["w_in_b"][l - n_a], b_outs, bd, "in_proj_b_step")
            mix_a = _nsa_step(rel_bias, page_table, q.reshape(bd, 1, D_RNN), gl.reshape(bd, 1, LANES), comp_s,
                              slc_pool_t, s_slc.reshape(bd, 1, D_KV), win_t, s_win.reshape(bd, 1, D_KV))
            mix_a = mix_a.reshape(bd, D_RNN)
        mix_m = _mem_attn_step(mq.reshape(bd, 1, D_MEM), mem_t, l).reshape(bd, D_MEM)
        x = _trunk_tail(x, mix_a, mix_m, p, l, alpha, bd)
    y_sample = x.reshape(bd, 1, d_model)
    s_cmp_kv = s_cmp.reshape((bd, 1) + kv_shape)
    s_slc_kv = s_slc.reshape((bd, 1) + kv_shape)
    s_win_kv = jnp.concatenate([cache_win_kv, s_win.reshape((bd, 1) + kv_shape)], axis=1)[:, 1:]
    s_lru_h = jnp.stack(lru_s)
    s_conv = jnp.stack(conv_s)
    return (y_prompt, y_sample, p_cmp_kv, p_slc_kv, p_win_kv, p_mem_kv, p_lru_h, p_conv,
            s_cmp_kv, s_slc_kv, s_win_kv, s_lru_h, s_conv)
```

```python
import functools
import math

import jax
import jax.numpy as jnp
from jax import lax
from jax.experimental import pallas as pl
from jax.experimental.pallas import tpu as pltpu

F32 = jnp.float32
BF16 = jnp.bfloat16
I32 = jnp.int32

HEAD_DIM = 64
N_MEM_HEADS = 4
D_MEM = N_MEM_HEADS * HEAD_DIM
N_HEADS = 12
D_RNN = N_HEADS * HEAD_DIM
N_KV_GROUPS = 4
HEADS_PER_GROUP = N_HEADS // N_KV_GROUPS
D_KV = 2 * N_KV_GROUPS * HEAD_DIM
CONV_W = 4
LRU_C = 8.0
CMP_LEN = 32
CMP_STRIDE = 16
CMP_HID = 2 * HEAD_DIM
SEL_LEN = 64
N_SEL = 16
WINDOW = 512
N_BUCKETS = 32
MAX_DISTANCE = 1024
N_GROUPS = 4
EXPERTS_PER_GROUP = 8
N_EXPERTS = N_GROUPS * EXPERTS_PER_GROUP
D_EXPERT = 128
Q_BLOCK = 128
LN_EPS = 1e-5
NEG = -1e30
FORCE = 1e9
SCALE = HEAD_DIM ** -0.5

LANES = 128
SUBLANES = 8
VMEM_LIMIT_BYTES = 56 * 1024 * 1024


def _cparams(n_axes):
    return pltpu.CompilerParams(dimension_semantics=("arbitrary",) * n_axes,
                                vmem_limit_bytes=VMEM_LIMIT_BYTES)


def _dot(a, b):
    return jnp.dot(a, b, preferred_element_type=F32)


def _dot_nt(a, b):
    return lax.dot_general(a, b, (((1,), (1,)), ((), ())), preferred_element_type=F32)


def _layer_norm(v, g, b):
    mu = jnp.mean(v, axis=-1, keepdims=True)
    d = v - mu
    var = jnp.mean(d * d, axis=-1, keepdims=True)
    return d * lax.rsqrt(var + LN_EPS) * g + b


def _rel_bucket(dist):
    n = jnp.maximum(dist, 0)
    max_exact = N_BUCKETS // 2
    nf = jnp.maximum(n, 1).astype(F32)
    large = max_exact + (jnp.log(nf / max_exact) / math.log(MAX_DISTANCE / max_exact)
                         * (N_BUCKETS - max_exact)).astype(I32)
    large = jnp.minimum(large, N_BUCKETS - 1)
    return jnp.where(n < max_exact, n, large)


def _bias_from_bucket(bucket, tab_ref, heads):
    masks = [bucket == k for k in range(1, N_BUCKETS)]
    out = []
    for h in heads:
        acc = jnp.full(bucket.shape, tab_ref[0, h], F32)
        for k in range(1, N_BUCKETS):
            acc = jnp.where(masks[k - 1], tab_ref[k, h], acc)
        out.append(acc)
    return out


def _proj_kernel(x_ref, w_ref, *o_refs, cols):
    y = _dot(x_ref[...].astype(BF16), w_ref[...])
    for o_ref, (off, n) in zip(o_refs, cols):
        o_ref[...] = y[:, off:off + n].astype(o_ref.dtype)


def _proj(x, w, outs, tm, name):
    m, k = x.shape
    n = w.shape[1]
    tm = min(tm, m)
    assert m % tm == 0 and all(off + wd <= n for off, wd, _ in outs)
    return pl.pallas_call(
        functools.partial(_proj_kernel, cols=tuple((off, wd) for off, wd, _ in outs)),
        grid=(m // tm,),
        in_specs=[pl.BlockSpec((tm, k), lambda i: (i, 0)),
                  pl.BlockSpec((k, n), lambda i: (0, 0))],
        out_specs=[pl.BlockSpec((tm, wd), lambda i: (i, 0)) for _, wd, _ in outs],
        out_shape=[jax.ShapeDtypeStruct((m, wd), dt) for _, wd, dt in outs],
        compiler_params=_cparams(1),
        name=name,
    )(x, w)


def _softplus(z):
    return jnp.maximum(z, 0.0) + jnp.log1p(jnp.exp(-jnp.abs(z)))


def _lru_gates(xc, wr_ref, wi_ref, br, bi, lam):
    xcb = xc.astype(BF16)
    nb = D_RNN // LANES
    r_l = jnp.concatenate([_dot(xcb[:, j * LANES:(j + 1) * LANES], wr_ref[j]) for j in range(nb)], axis=1)
    i_l = jnp.concatenate([_dot(xcb[:, j * LANES:(j + 1) * LANES], wi_ref[j]) for j in range(nb)], axis=1)
    r = jax.nn.sigmoid(r_l + br)
    i = jax.nn.sigmoid(i_l + bi)
    log_a = -LRU_C * r * _softplus(-lam)
    a = jnp.exp(log_a)
    u = jnp.sqrt(1.0 - a * a) * i * xc
    return a, u


def _rglru_prompt_kernel(gate_ref, xr_ref, cw_ref, cb_ref, wr_ref, wi_ref, br_ref, bi_ref, lam_ref,
                         y_ref, ht_ref, buf_ref, xp_ref, *, t_len, chunk):
    d = D_RNN
    pad = SUBLANES
    xp_ref[0:pad, :] = jnp.zeros((pad, d), F32)
    xp_ref[pad:pad + t_len, :] = xr_ref[0]
    buf_ref[0] = xr_ref[0, t_len - (CONV_W - 1):t_len, :]
    cw = cw_ref[...]
    cb = cb_ref[...]
    br = br_ref[...]
    bi = bi_ref[...]
    lam = lam_ref[...]
    row_in_tile = lax.broadcasted_iota(I32, (chunk, d), 0) & (SUBLANES - 1)
    h = jnp.zeros((1, d), F32)
    for c in range(t_len // chunk):
        base = c * chunk
        xc = cb
        for k in range(CONV_W):
            lo = pad - (CONV_W - 1) + k + base
            xc = xc + xp_ref[lo:lo + chunk, :] * cw[k:k + 1, :]
        a, u = _lru_gates(xc, wr_ref, wi_ref, br, bi, lam)
        for k in (1, 2, 4):
            a_s = pltpu.roll(a, k, axis=0)
            u_s = pltpu.roll(u, k, axis=0)
            m = row_in_tile >= k
            u = jnp.where(m, a * u_s + u, u)
            a = jnp.where(m, a * a_s, a)
        hs = []
        for j in range(chunk // SUBLANES):
            h_t = a[j * SUBLANES:(j + 1) * SUBLANES] * h + u[j * SUBLANES:(j + 1) * SUBLANES]
            h = h_t[SUBLANES - 1:SUBLANES]
            hs.append(h_t)
        hfull = jnp.concatenate(hs, axis=0)
        g = jax.nn.gelu(gate_ref[0, base:base + chunk, :])
        y_ref[0, base:base + chunk, :] = (g * hfull).astype(y_ref.dtype)
    ht_ref[0] = h


def _rglru_prompt(gate, xr, cw, cb, wr_bd, wi_bd, br, bi, lam):
    b, t_len, d = xr.shape
    chunk = min(256, t_len)
    full = lambda shape: pl.BlockSpec(shape, lambda i: (0,) * len(shape))
    return pl.pallas_call(
        functools.partial(_rglru_prompt_kernel, t_len=t_len, chunk=chunk),
        grid=(b,),
        in_specs=[pl.BlockSpec((1, t_len, d), lambda i: (i, 0, 0)),
                  pl.BlockSpec((1, t_len, d), lambda i: (i, 0, 0)),
                  full((CONV_W, d)), full((1, d)), full(wr_bd.shape), full(wi_bd.shape),
                  full((1, d)), full((1, d)), full((1, d))],
        out_specs=[pl.BlockSpec((1, t_len, d), lambda i: (i, 0, 0)),
                   pl.BlockSpec((1, 1, d), lambda i: (i, 0, 0)),
                   pl.BlockSpec((1, CONV_W - 1, d), lambda i: (i, 0, 0))],
        out_shape=[jax.ShapeDtypeStruct((b, t_len, d), BF16),
                   jax.ShapeDtypeStruct((b, 1, d), F32),
                   jax.ShapeDtypeStruct((b, CONV_W - 1, d), F32)],
        scratch_shapes=[pltpu.VMEM((t_len + SUBLANES, d), F32)],
        compiler_params=_cparams(1),
        name="rglru_prompt",
    )(gate, xr, cw, cb, wr_bd, wi_bd, br, bi, lam)


def _rglru_step_kernel(gate_ref, xr_ref, conv_ref, h0_ref, cw_ref, cb_ref, wr_ref, wi_ref, br_ref, bi_ref,
                       lam_ref, y_ref, ht_ref, buf_ref):
    xr = xr_ref[...]
    cw = cw_ref[...]
    xc = cb_ref[...] + xr * cw[CONV_W - 1:CONV_W, :]
    for k in range(CONV_W - 1):
        xc = xc + conv_ref[k] * cw[k:k + 1, :]
    a, u = _lru_gates(xc, wr_ref, wi_ref, br_ref[...], bi_ref[...], lam_ref[...])
    h = a * h0_ref[...] + u
    ht_ref[...] = h
    y_ref[...] = (jax.nn.gelu(gate_ref[...]) * h).astype(y_ref.dtype)
    for k in range(CONV_W - 2):
        buf_ref[k] = conv_ref[k + 1]
    buf_ref[CONV_W - 2] = xr


def _rglru_step(gate, xr, conv_t, h0, cw, cb, wr_bd, wi_bd, br, bi, lam):
    b, d = xr.shape
    return pl.pallas_call(
        _rglru_step_kernel,
        out_shape=[jax.ShapeDtypeStruct((b, d), BF16),
                   jax.ShapeDtypeStruct((b, d), F32),
                   jax.ShapeDtypeStruct((CONV_W - 1, b, d), F32)],
        compiler_params=pltpu.CompilerParams(vmem_limit_bytes=VMEM_LIMIT_BYTES),
        name="rglru_step",
    )(gate, xr, conv_t, h0, cw, cb, wr_bd, wi_bd, br, bi, lam)


def _mem_attn_kernel(q_ref, kv_ref, o_ref, *, rows):
    q = q_ref[0]
    if rows < SUBLANES:
        q = jnp.broadcast_to(q, (SUBLANES, D_MEM))
    kv = kv_ref[0, 0].astype(BF16)
    outs = []
    for h in range(N_MEM_HEADS):
        qh = q[:, h * HEAD_DIM:(h + 1) * HEAD_DIM]
        kh = kv[:, h * HEAD_DIM:(h + 1) * HEAD_DIM]
        vh = kv[:, D_MEM + h * HEAD_DIM:D_MEM + (h + 1) * HEAD_DIM]
        s = _dot_nt(qh, kh) * SCALE
        m = jnp.max(s, axis=-1, keepdims=True)
        e = jnp.exp(s - m)
        p = e / jnp.sum(e, axis=-1, keepdims=True)
        outs.append(_dot(p.astype(BF16), vh))
    o = jnp.concatenate(outs, axis=1)
    o_ref[0] = o[:rows].astype(o_ref.dtype)


def _mem_attn(q, mem_kv, layer):
    b, t_len, _ = q.shape
    n_mem = mem_kv.shape[2]
    tq = min(t_len, 512)
    return pl.pallas_call(
        functools.partial(_mem_attn_kernel, rows=tq),
        grid=(b, t_len // tq),
        in_specs=[pl.BlockSpec((1, tq, D_MEM), lambda i, j: (i, j, 0)),
                  pl.BlockSpec((1, 1, n_mem, 2 * D_MEM), lambda i, j: (layer, i, 0, 0))],
        out_specs=pl.BlockSpec((1, tq, D_MEM), lambda i, j: (i, j, 0)),
        out_shape=jax.ShapeDtypeStruct((b, t_len, D_MEM), BF16),
        compiler_params=_cparams(2),
        name="mem_attn",
    )(q, mem_kv)


def _mem_attn_step_kernel(q_ref, kv_ref, o_ref):
    q_row = q_ref[0] * SCALE
    rows = []
    for h in range(N_MEM_HEADS):
        parts = []
        if h > 0:
            parts.append(jnp.zeros((1, h * HEAD_DIM), BF16))
        parts.append(q_row[:, h * HEAD_DIM:(h + 1) * HEAD_DIM])
        if h < N_MEM_HEADS - 1:
            parts.append(jnp.zeros((1, (N_MEM_HEADS - 1 - h) * HEAD_DIM), BF16))
        rows.append(jnp.concatenate(parts, axis=1))
    n_rows = 2 * SUBLANES
    q_blk = jnp.concatenate(rows + [jnp.zeros((n_rows - N_MEM_HEADS, D_MEM), BF16)], axis=0)
    s = _dot(q_blk, kv_ref[0, 0, 0].astype(BF16))
    e = jnp.exp(s - jnp.max(s, axis=-1, keepdims=True))
    p = e / jnp.sum(e, axis=-1, keepdims=True)
    o_all = _dot_nt(p.astype(BF16), kv_ref[0, 0, 1].astype(BF16))
    o = jnp.concatenate([o_all[h:h + 1, h * HEAD_DIM:(h + 1) * HEAD_DIM] for h in range(N_MEM_HEADS)], axis=1)
    o_ref[0] = o.astype(o_ref.dtype)


def _mem_attn_step(q, mem_t, layer):
    b = q.shape[0]
    n_mem = mem_t.shape[4]
    return pl.pallas_call(
        _mem_attn_step_kernel,
        grid=(b,),
        in_specs=[pl.BlockSpec((1, 1, D_MEM), lambda i: (i, 0, 0)),
                  pl.BlockSpec((1, 1, 2, D_MEM, n_mem), lambda i: (layer, i, 0, 0, 0))],
        out_specs=pl.BlockSpec((1, 1, D_MEM), lambda i: (i, 0, 0)),
        out_shape=jax.ShapeDtypeStruct((b, 1, D_MEM), BF16),
        compiler_params=_cparams(1),
        name="mem_attn_step",
    )(q, mem_t)


def _out_ln_kernel(x_ref, ma_ref, mm_ref, wa_ref, wm_ref, g_ref, b_ref, o_ref, *, alpha):
    sub = _dot(ma_ref[...], wa_ref[...]) + _dot(mm_ref[...], wm_ref[...])
    o_ref[...] = _layer_norm(alpha * x_ref[...] + sub, g_ref[...], b_ref[...])


def _out_ln(x, mix_a, mix_m, wo_a, wo_m, g, b, alpha, tm):
    m, d = x.shape
    da = mix_a.shape[1]
    dm = mix_m.shape[1]
    return pl.pallas_call(
        functools.partial(_out_ln_kernel, alpha=alpha),
        grid=(m // tm,),
        in_specs=[pl.BlockSpec((tm, d), lambda i: (i, 0)),
                  pl.BlockSpec((tm, da), lambda i: (i, 0)),
                  pl.BlockSpec((tm, dm), lambda i: (i, 0)),
                  pl.BlockSpec((da, d), lambda i: (0, 0)),
                  pl.BlockSpec((dm, d), lambda i: (0, 0)),
                  pl.BlockSpec((1, d), lambda i: (0, 0)),
                  pl.BlockSpec((1, d), lambda i: (0, 0))],
        out_specs=pl.BlockSpec((tm, d), lambda i: (i, 0)),
        out_shape=jax.ShapeDtypeStruct((m, d), F32),
        compiler_params=_cparams(1),
        name="out_ln",
    )(x, mix_a, mix_m, wo_a, wo_m, g, b)


def _route(logits):
    lane = lax.broadcasted_iota(I32, logits.shape, 1)
    big = jnp.int32(4 * LANES)
    is_g = lane < N_GROUPS
    lg = jnp.where(is_g, logits, -jnp.inf)
    gmax = jnp.max(lg, axis=-1, keepdims=True)
    grp = jnp.min(jnp.where(lg == gmax, lane, big), axis=-1, keepdims=True)
    gsum = jnp.sum(jnp.where(is_g, jnp.exp(logits - gmax), 0.0), axis=-1, keepdims=True)
    g_w = 1.0 / gsum
    e_idx = lane - N_GROUPS
    in_grp = (lane >= N_GROUPS) & (lane < N_GROUPS + N_EXPERTS) & ((e_idx >> 3) == grp)
    v = jnp.where(in_grp, logits, -jnp.inf)
    v1 = jnp.max(v, axis=-1, keepdims=True)
    i1 = jnp.min(jnp.where(v == v1, lane, big), axis=-1, keepdims=True)
    vr = jnp.where(lane == i1, -jnp.inf, v)
    v2 = jnp.max(vr, axis=-1, keepdims=True)
    i2 = jnp.min(jnp.where(vr == v2, lane, big), axis=-1, keepdims=True)
    e2 = jnp.exp(v2 - v1)
    w1 = g_w / (1.0 + e2)
    w2 = g_w * e2 / (1.0 + e2)
    return jnp.where(lane == i1, w1, 0.0) + jnp.where(lane == i2, w2, 0.0)


def _moe_kernel(x_ref, wr_ref, rb_ref, wgu_ref, wd_ref, g_ref, b_ref, o_ref,
                xb_ref, comb_ref, acc_ref, *, alpha):
    grp = pl.program_id(1)
    hid = EXPERTS_PER_GROUP * D_EXPERT

    @pl.when(grp == 0)
    def _():
        xh = x_ref[...].astype(BF16)
        logits = _dot(xh, wr_ref[...]) + rb_ref[...]
        xb_ref[...] = xh
        comb_ref[...] = _route(logits)
        acc_ref[...] = jnp.zeros(acc_ref.shape, F32)

    xb = xb_ref[...]
    gu = _dot(xb, wgu_ref[0])
    gate = gu[:, :hid]
    up = gu[:, hid:]
    hdn = gate * jax.nn.sigmoid(gate) * up
    comb = comb_ref[...]
    lane = lax.broadcasted_iota(I32, comb.shape, 1)
    scale = []
    for e in range(EXPERTS_PER_GROUP):
        col = jnp.sum(jnp.where(lane == N_GROUPS + grp * EXPERTS_PER_GROUP + e, comb, 0.0),
                      axis=-1, keepdims=True)
        scale.append(jnp.broadcast_to(col, (comb.shape[0], D_EXPERT)))
    hdn = hdn * jnp.concatenate(scale, axis=1)
    acc_ref[...] += _dot(hdn.astype(BF16), wd_ref[0])

    @pl.when(grp == N_GROUPS - 1)
    def _():
        o_ref[...] = _layer_norm(alpha * x_ref[...] + acc_ref[...], g_ref[...], b_ref[...])


def _moe_ln(x, wr, rb, wgu, wd, g, b, alpha, tm):
    m, d = x.shape
    hid = EXPERTS_PER_GROUP * D_EXPERT
    return pl.pallas_call(
        functools.partial(_moe_kernel, alpha=alpha),
        grid=(m // tm, N_GROUPS),
        in_specs=[pl.BlockSpec((tm, d), lambda i, j: (i, 0)),
                  pl.BlockSpec((d, LANES), lambda i, j: (0, 0)),
                  pl.BlockSpec((1, LANES), lambda i, j: (0, 0)),
                  pl.BlockSpec((1, d, 2 * hid), lambda i, j: (j, 0, 0)),
                  pl.BlockSpec((1, hid, d), lambda i, j: (j, 0, 0)),
                  pl.BlockSpec((1, d), lambda i, j: (0, 0)),
                  pl.BlockSpec((1, d), lambda i, j: (0, 0))],
        out_specs=pl.BlockSpec((tm, d), lambda i, j: (i, 0)),
        out_shape=jax.ShapeDtypeStruct((m, d), F32),
        scratch_shapes=[pltpu.VMEM((tm, d), BF16), pltpu.VMEM((tm, LANES), F32), pltpu.VMEM((tm, d), F32)],
        compiler_params=_cparams(2),
        name="moe_ln",
    )(x, wr, rb, wgu, wd, g, b)


CMP_MBLK = 128
ROW_TILES = D_KV // LANES
PAIR_HID = 2 * CMP_HID


def _compress_weights(pe, w1, b1, w2):
    w1 = w1.reshape(2, CMP_LEN, HEAD_DIM, CMP_HID)
    z1 = jnp.zeros_like(w1)
    w1p = jnp.concatenate([jnp.concatenate([w1, z1], axis=3), jnp.concatenate([z1, w1], axis=3)], axis=2)
    z2 = jnp.zeros_like(w2)
    w2p = jnp.concatenate([jnp.concatenate([w2, z2], axis=2), jnp.concatenate([z2, w2], axis=2)], axis=1)
    w1p = w1p.astype(BF16).reshape(2, 2, CMP_STRIDE * LANES, PAIR_HID)
    return (jnp.concatenate([pe, pe], axis=2), w1p,
            jnp.concatenate([b1, b1], axis=1)[:, None], w2p.astype(BF16))


def _load_interleaved(x_ref):
    step = CMP_STRIDE * ROW_TILES

    def load(tile, r, mb):
        row0 = pl.multiple_of(mb * (CMP_MBLK * step), CMP_MBLK * step)
        return x_ref[pl.ds(row0 + r * ROW_TILES + tile, CMP_MBLK, stride=step), :]
    return load


def _load_tiled(x_ref):
    def load(tile, r, mb):
        row0 = pl.multiple_of(mb * (CMP_MBLK * CMP_STRIDE), CMP_MBLK * CMP_STRIDE)
        return x_ref[tile, pl.ds(row0 + r, CMP_MBLK, stride=CMP_STRIDE), :]
    return load


def _compress_rows(load, n_chunk, pe_ref, w1_ref, b1_ref, w2_ref, out_ref, a_ref, b_ref):
    n_blk = n_chunk // CMP_MBLK
    for kv in range(2):
        for pair in range(N_KV_GROUPS // 2):
            tile = kv * (N_KV_GROUPS // 2) + pair

            def fill(mb, carry, kv=kv, tile=tile):
                top, bot = [], []
                for r in range(CMP_STRIDE):
                    xg = load(tile, r, mb)
                    top.append((xg + pe_ref[kv, r:r + 1, :]).astype(BF16))
                    bot.append((xg + pe_ref[kv, CMP_STRIDE + r:CMP_STRIDE + r + 1, :]).astype(BF16))
                m0 = pl.multiple_of(mb * CMP_MBLK, CMP_MBLK)
                a_ref[pl.ds(m0, CMP_MBLK), :] = _dot(jnp.concatenate(top, axis=1), w1_ref[kv, 0])
                b_ref[pl.ds(m0, CMP_MBLK), :] = _dot(jnp.concatenate(bot, axis=1), w1_ref[kv, 1])
                return carry

            lax.fori_loop(0, n_blk, fill, 0)
            b_ref[n_chunk:n_chunk + SUBLANES, :] = jnp.zeros((SUBLANES, PAIR_HID), F32)
            for mb in range(n_blk):
                m0 = mb * CMP_MBLK
                hid = jax.nn.gelu(a_ref[m0:m0 + CMP_MBLK, :] + b_ref[m0 + 1:m0 + 1 + CMP_MBLK, :] + b1_ref[kv])
                out_ref[m0:m0 + CMP_MBLK, tile * LANES:(tile + 1) * LANES] = _dot(hid.astype(BF16), w2_ref[kv])


def _compress_prompt_kernel(x_ref, pe_ref, w1_ref, b1_ref, w2_ref, o_ref, a_ref, b_ref, *, n_chunk):
    _compress_rows(_load_interleaved(x_ref.at[0]), n_chunk, pe_ref, w1_ref, b1_ref, w2_ref, o_ref.at[0],
                   a_ref, b_ref)


def _compress_prompt(rows, pe, w1, b1, w2):
    b, t_len, _ = rows.shape
    n_chunk = t_len // CMP_STRIDE
    assert n_chunk % CMP_MBLK == 0
    full = lambda a: pl.BlockSpec(a.shape, lambda i: (0,) * a.ndim)
    return pl.pallas_call(
        functools.partial(_compress_prompt_kernel, n_chunk=n_chunk),
        grid=(b,),
        in_specs=[pl.BlockSpec((1, t_len * ROW_TILES, LANES), lambda i: (i, 0, 0)),
                  full(pe), full(w1), full(b1), full(w2)],
        out_specs=pl.BlockSpec((1, n_chunk, D_KV), lambda i: (i, 0, 0)),
        out_shape=jax.ShapeDtypeStruct((b, n_chunk, D_KV), F32),
        scratch_shapes=[pltpu.VMEM((n_chunk, PAIR_HID), F32), pltpu.VMEM((n_chunk + SUBLANES, PAIR_HID), F32)],
        compiler_params=_cparams(1),
        name="compress_prompt",
    )(rows.reshape(b, t_len * ROW_TILES, LANES), pe, w1, b1, w2)


def _masked_softmax_rows(s, mask):
    sm = jnp.where(mask, s, NEG)
    m = jnp.max(sm, axis=-1, keepdims=True)
    e = jnp.where(mask, jnp.exp(sm - m), 0.0)
    tot = jnp.sum(e, axis=-1, keepdims=True)
    return e * jnp.where(tot > 0.0, 1.0 / tot, 0.0)


WIN_TILES = WINDOW // Q_BLOCK


KEY_TILE = 512
KEY_SUB = KEY_TILE // Q_BLOCK


def _attend_tiles_t(qts, k_ref, vt_ref, t_lo, t_hi, i_blk, bias_ref, bias_index, mask_fn):
    cols = qts[0].shape[1]

    def body(t, carry):
        k0 = pl.multiple_of(t * KEY_TILE, KEY_TILE)
        bases = [bias_index(i_blk - (t * KEY_SUB + u)) for u in range(KEY_SUB)]
        new = []
        for g, qt in enumerate(qts):
            m, l, acc = carry[g]
            kt = k_ref[0, g, pl.ds(k0, KEY_TILE), :]
            vt = vt_ref[0, g * HEAD_DIM:(g + 1) * HEAD_DIM, pl.ds(k0, KEY_TILE)]
            heads = [g * HEADS_PER_GROUP + hp for hp in range(HEADS_PER_GROUP)]
            bias = jnp.concatenate(
                [jnp.concatenate([bias_ref[base + h] for h in heads], axis=1) for base in bases], axis=0)
            s = _dot(kt, qt) + bias
            extra = mask_fn(t, g)
            if extra is not None:
                s = s + jnp.concatenate([extra] * HEADS_PER_GROUP, axis=1)
            m_new = jnp.maximum(m, jnp.max(s, axis=0, keepdims=True))
            alpha = jnp.exp(m - m_new)
            p = jnp.exp(s - m_new)
            new.append((m_new, alpha * l + jnp.sum(p, axis=0, keepdims=True), alpha * acc + _dot(vt, p.astype(BF16))))
        return tuple(new)

    init = tuple((jnp.full((1, cols), NEG, F32), jnp.zeros((1, cols), F32), jnp.zeros((HEAD_DIM, cols), F32))
                 for _ in qts)
    fin = lax.fori_loop(t_lo, t_hi, body, init)
    return [acc / l for (_, l, acc) in fin]


def _nsa_prompt_kernel(tab_ref, q_ref, gl_ref, ck_ref, cvt_ref, sk_ref, svt_ref, wk_ref, wvt_ref, o_ref,
                       bias_ref, cbias_ref, sel_ref, *, n_tiles):
    b = pl.program_id(0)
    i = pl.program_id(1)
    kk = lax.broadcasted_iota(I32, (Q_BLOCK, Q_BLOCK), 0)
    qq = lax.broadcasted_iota(I32, (Q_BLOCK, Q_BLOCK), 1)
    n_cmp = ck_ref.shape[2]
    n_sel_blocks = 2 * n_tiles
    edge_base = n_tiles * N_HEADS
    none_base = (n_tiles + 1) * N_HEADS

    @pl.when((b == 0) & (i == 0))
    def _():
        def build(delta, carry):
            dist = delta * Q_BLOCK + qq - kk
            biases = _bias_from_bucket(_rel_bucket(dist), tab_ref, range(N_HEADS))
            for h in range(N_HEADS):
                bias_ref[delta * N_HEADS + h] = jnp.where(dist >= 0, biases[h], NEG)
            return carry
        lax.fori_loop(0, n_tiles, build, 0)
        dist = WIN_TILES * Q_BLOCK + qq - kk
        biases = _bias_from_bucket(_rel_bucket(dist), tab_ref, range(N_HEADS))
        for h in range(N_HEADS):
            bias_ref[edge_base + h] = jnp.where(dist < WINDOW, biases[h], NEG)
            bias_ref[none_base + h] = jnp.full((Q_BLOCK, Q_BLOCK), NEG, F32)

        def build_c(t, carry):
            dist_c = t * Q_BLOCK + qq - (kk * CMP_STRIDE + (CMP_LEN - 1))
            ok = (dist_c >= 0) & (kk < n_cmp - 1)
            biases_c = _bias_from_bucket(_rel_bucket(dist_c), tab_ref, range(N_HEADS))
            for h in range(N_HEADS):
                cbias_ref[t * N_HEADS + h] = jnp.where(ok, biases_c[h], NEG)
            return carry
        lax.fori_loop(0, n_tiles, build_c, 0)

    qs = i * Q_BLOCK
    q_all = q_ref[0] * SCALE
    gates = jax.nn.sigmoid(gl_ref[0])

    blk = lax.broadcasted_iota(I32, (n_sel_blocks, Q_BLOCK), 0)
    col = lax.broadcasted_iota(I32, (n_sel_blocks, Q_BLOCK), 1)
    c_start = col * CMP_STRIDE
    s_start = blk * SEL_LEN
    cover_t = jnp.where((c_start < s_start + SEL_LEN) & (c_start + CMP_LEN > s_start) & (col < n_cmp - 1),
                        1.0, 0.0).astype(BF16)
    q_pos = qs + col
    cur = q_pos // SEL_LEN
    forced = (blk == 0) | (blk == cur) | (blk == cur - 1)
    valid = blk * SEL_LEN <= q_pos
    upper = kk < SEL_LEN

    qts, o_cs, splits = [], [], []
    for g in range(N_KV_GROUPS):
        heads = [g * HEADS_PER_GROUP + hp for hp in range(HEADS_PER_GROUP)]
        qt = jnp.concatenate([q_all[h * HEAD_DIM:(h + 1) * HEAD_DIM, :] for h in heads], axis=1)
        qts.append(qt)

        s_c = _dot(ck_ref[0, g], qt) + jnp.concatenate([cbias_ref[i * N_HEADS + h] for h in heads], axis=1)
        m_c = jnp.max(s_c, axis=0, keepdims=True)
        e_c = jnp.exp(s_c - m_c)
        tot = jnp.sum(e_c, axis=0, keepdims=True)
        p_c = e_c * jnp.where(m_c > 0.5 * NEG, 1.0 / tot, 0.0)
        o_cs.append(_dot(cvt_ref[0, g * HEAD_DIM:(g + 1) * HEAD_DIM, :], p_c.astype(BF16)))
        p_sum = p_c[:, 0:Q_BLOCK]
        for hp in range(1, HEADS_PER_GROUP):
            p_sum = p_sum + p_c[:, hp * Q_BLOCK:(hp + 1) * Q_BLOCK]
        splits.append(p_sum.astype(BF16))

    imp_all = _dot(cover_t, jnp.concatenate(splits, axis=1))
    for g in range(N_KV_GROUPS):
        imp = imp_all[:, g * Q_BLOCK:(g + 1) * Q_BLOCK]
        score = jnp.where(forced, FORCE, imp)
        score = jnp.where(valid, score, NEG)
        cnt = jnp.zeros(score.shape, F32)
        for s in range(n_sel_blocks):
            row = score[s:s + 1, :]
            beats = (row > score) | ((row == score) & (blk > s))
            cnt = cnt + jnp.where(beats, 1.0, 0.0)
        sel = (cnt < float(N_SEL)) & (score > 0.5 * NEG)
        sel_ref[g] = jnp.where(sel, 0.0, NEG)

    def mask_sel(t, g):
        rows = sel_ref[g, pl.ds(pl.multiple_of(2 * KEY_SUB * t, 2 * KEY_SUB), 2 * KEY_SUB), :]
        return jnp.concatenate([jnp.where(upper, rows[2 * u:2 * u + 1], rows[2 * u + 1:2 * u + 2])
                                for u in range(KEY_SUB)], axis=0)
    last = i // KEY_SUB + 1
    o_ss = _attend_tiles_t(qts, sk_ref, svt_ref, 0, last, i, bias_ref,
                           lambda d: jnp.where(d < 0, none_base, d * N_HEADS), mask_sel)

    def win_index(d):
        return jnp.where((d < 0) | (d > WIN_TILES), none_base, jnp.where(d == WIN_TILES, edge_base, d * N_HEADS))
    o_ws = _attend_tiles_t(qts, wk_ref, wvt_ref, jnp.maximum(i - WIN_TILES, 0) // KEY_SUB, last, i, bias_ref,
                           win_index, lambda t, g: None)

    for h in range(N_HEADS):
        g, hp = divmod(h, HEADS_PER_GROUP)
        c0 = hp * Q_BLOCK
        o_h = (gates[3 * h:3 * h + 1, :] * o_cs[g][:, c0:c0 + Q_BLOCK]
               + gates[3 * h + 1:3 * h + 2, :] * o_ss[g][:, c0:c0 + Q_BLOCK]
               + gates[3 * h + 2:3 * h + 3, :] * o_ws[g][:, c0:c0 + Q_BLOCK])
        o_ref[0, h * HEAD_DIM:(h + 1) * HEAD_DIM, :] = o_h.astype(o_ref.dtype)


def _nsa_prompt(tab, q_t, gl_t, comp_k, comp_vt, slc_k, slc_vt, win_k, win_vt):
    b, dq, t_len = q_t.shape
    n_tiles = t_len // Q_BLOCK
    n_cmp = comp_k.shape[2]
    assert n_cmp == Q_BLOCK and t_len % KEY_TILE == 0 and n_tiles > WIN_TILES
    dv = N_KV_GROUPS * HEAD_DIM
    return pl.pallas_call(
        functools.partial(_nsa_prompt_kernel, n_tiles=n_tiles),
        grid=(b, n_tiles),
        in_specs=[pl.BlockSpec(memory_space=pltpu.SMEM),
                  pl.BlockSpec((1, dq, Q_BLOCK), lambda i, j: (i, 0, j)),
                  pl.BlockSpec((1, LANES, Q_BLOCK), lambda i, j: (i, 0, j)),
                  pl.BlockSpec((1, N_KV_GROUPS, n_cmp, HEAD_DIM), lambda i, j: (i, 0, 0, 0)),
                  pl.BlockSpec((1, dv, n_cmp), lambda i, j: (i, 0, 0)),
                  pl.BlockSpec((1, N_KV_GROUPS, t_len, HEAD_DIM), lambda i, j: (i, 0, 0, 0)),
                  pl.BlockSpec((1, dv, t_len), lambda i, j: (i, 0, 0)),
                  pl.BlockSpec((1, N_KV_GROUPS, t_len, HEAD_DIM), lambda i, j: (i, 0, 0, 0)),
                  pl.BlockSpec((1, dv, t_len), lambda i, j: (i, 0, 0))],
        out_specs=pl.BlockSpec((1, dq, Q_BLOCK), lambda i, j: (i, 0, j)),
        out_shape=jax.ShapeDtypeStruct((b, dq, t_len), BF16),
        scratch_shapes=[pltpu.VMEM(((n_tiles + 2) * N_HEADS, Q_BLOCK, Q_BLOCK), F32),
                        pltpu.VMEM((n_tiles * N_HEADS, Q_BLOCK, Q_BLOCK), F32),
                        pltpu.VMEM((N_KV_GROUPS, 2 * n_tiles, Q_BLOCK), F32)],
        compiler_params=_cparams(2),
        name="nsa_prompt",
    )(tab, q_t, gl_t, comp_k, comp_vt, slc_k, slc_vt, win_k, win_vt)


def _page_copy(pt_ref, pool_ref, buf_ref, sem, b, j, page):
    start = pl.multiple_of(j * page, page)
    if len(buf_ref.shape) == 2:
        dst = buf_ref.at[pl.ds(start, page)]
    else:
        dst = buf_ref.at[:, :, pl.ds(start, page)]
    return pltpu.make_async_copy(pool_ref.at[pt_ref[b, j]], dst, sem)


def _gather_pages_start(pt_ref, pool_ref, buf_ref, sem, b, n_pages, page):
    def go(j, carry):
        _page_copy(pt_ref, pool_ref, buf_ref, sem, b, j, page).start()
        return carry
    lax.fori_loop(0, n_pages, go, 0)


def _gather_pages_wait(pt_ref, pool_ref, buf_ref, sem, b, n_pages, page):
    def go(j, carry):
        _page_copy(pt_ref, pool_ref, buf_ref, sem, b, j, page).wait()
        return carry
    lax.fori_loop(0, n_pages, go, 0)


def _compress_pages_kernel(pt_ref, pool_ref, pe_ref, w1_ref, b1_ref, w2_ref, o_ref, raw_ref, x_ref, a_ref, b_ref,
                           sem, *, n_seq, n_pages, page, n_chunk):
    b = pl.program_id(0)

    @pl.when(b == 0)
    def _():
        _gather_pages_start(pt_ref, pool_ref, raw_ref, sem, b, n_pages, page)

    _gather_pages_wait(pt_ref, pool_ref, raw_ref, sem, b, n_pages, page)

    def relayout(j, carry):
        r0 = pl.multiple_of(j * page, page)
        for tile in range(ROW_TILES):
            kv, pair = divmod(tile, N_KV_GROUPS // 2)
            x_ref[tile, pl.ds(r0, page), :] = raw_ref[kv, pair * LANES:(pair + 1) * LANES, pl.ds(r0, page)].T
        return carry
    lax.fori_loop(0, n_pages, relayout, 0)

    @pl.when(b + 1 < n_seq)
    def _():
        _gather_pages_start(pt_ref, pool_ref, raw_ref, sem, b + 1, n_pages, page)

    _compress_rows(_load_tiled(x_ref), n_chunk, pe_ref, w1_ref, b1_ref, w2_ref, o_ref.at[0], a_ref, b_ref)


def _compress_pages(page_table, pool_t, pe, w1, b1, w2):
    b, n_pages = page_table.shape
    page = pool_t.shape[3]
    past = n_pages * page
    n_chunk = past // CMP_STRIDE
    assert n_chunk % CMP_MBLK == 0 and page == LANES
    full = lambda a: pl.BlockSpec(a.shape, lambda i: (0,) * a.ndim)
    return pl.pallas_call(
        functools.partial(_compress_pages_kernel, n_seq=b, n_pages=n_pages, page=page, n_chunk=n_chunk),
        grid=(b,),
        in_specs=[pl.BlockSpec(memory_space=pltpu.SMEM), pl.BlockSpec(memory_space=pl.ANY),
                  full(pe), full(w1), full(b1), full(w2)],
        out_specs=pl.BlockSpec((1, n_chunk, D_KV), lambda i: (i, 0, 0)),
        out_shape=jax.ShapeDtypeStruct((b, n_chunk, D_KV), F32),
        scratch_shapes=[pltpu.VMEM((2, D_K, past), F32),
                        pltpu.VMEM((ROW_TILES, past, LANES), F32),
                        pltpu.VMEM((n_chunk, PAIR_HID), F32), pltpu.VMEM((n_chunk + SUBLANES, PAIR_HID), F32),
                        pltpu.SemaphoreType.DMA(())],
        compiler_params=_cparams(1),
        name="compress_pages",
    )(page_table, pool_t, pe, w1, b1, w2)


DEC_TILE = 512
DEC_ROWS = 16
D_K = D_KV // 2


def _row_bias(dist_row, tab_ref, bias_ref, col0):
    length = dist_row.shape[1]
    biases = _bias_from_bucket(_rel_bucket(dist_row), tab_ref, range(N_HEADS))
    rows = biases + [jnp.zeros((DEC_ROWS - N_HEADS, length), F32)]
    bias_ref[:, pl.ds(col0, length)] = jnp.concatenate(rows, axis=0)


def _head_block(o_all, h):
    g = h // HEADS_PER_GROUP
    return o_all[h:h + 1, g * HEAD_DIM:(g + 1) * HEAD_DIM]


def _new_token_scores(q_blk, new_row, bias0):
    k_new = new_row[:, :D_K].astype(BF16).astype(F32)
    v_new = new_row[:, D_K:].astype(BF16).astype(F32)
    return jnp.sum(q_blk.astype(F32) * k_new, axis=-1, keepdims=True) + bias0, v_new


def _nsa_step_kernel(tab_ref, pt_ref, q_ref, gl_ref, comp_ref, pool_ref, slc_new_ref, win_ref, win_new_ref, o_ref,
                     kv_buf, bias_c_ref, bias_s_ref, bias_w_ref, sems,
                     *, n_seq, n_pages, page, past, n_cmp, w_buf):
    b = pl.program_id(0)
    slot = b % 2
    n_tiles = past // DEC_TILE
    n_blocks = past // SEL_LEN + 1
    blk_lanes = -(-n_blocks // LANES) * LANES

    @pl.when(b == 0)
    def _():
        _gather_pages_start(pt_ref, pool_ref, kv_buf.at[0], sems.at[0], b, n_pages, page)

    @pl.when(b + 1 < n_seq)
    def _():
        _gather_pages_start(pt_ref, pool_ref, kv_buf.at[1 - slot], sems.at[1 - slot], b + 1, n_pages, page)

    lane_t = lax.broadcasted_iota(I32, (1, DEC_TILE), 1)

    @pl.when(b == 0)
    def _():
        lane_c = lax.broadcasted_iota(I32, (1, n_cmp), 1)
        _row_bias(past - (lane_c * CMP_STRIDE + (CMP_LEN - 1)), tab_ref, bias_c_ref, 0)

        def tile_bias(t, carry):
            k0 = pl.multiple_of(t * DEC_TILE, DEC_TILE)
            _row_bias(past - (k0 + lane_t), tab_ref, bias_s_ref, k0)
            return carry
        lax.fori_loop(0, n_tiles, tile_bias, 0)
        lane_w = lax.broadcasted_iota(I32, (1, w_buf), 1)
        _row_bias(w_buf - lane_w, tab_ref, bias_w_ref, 0)

    bias0 = jnp.concatenate([jnp.full((1, 1), tab_ref[0, h], F32) for h in range(N_HEADS)]
                            + [jnp.zeros((DEC_ROWS - N_HEADS, 1), F32)], axis=0)

    q_row = q_ref[0] * SCALE
    gates = jax.nn.sigmoid(gl_ref[0])
    rows = []
    for h in range(N_HEADS):
        g = h // HEADS_PER_GROUP
        parts = []
        if g > 0:
            parts.append(jnp.zeros((1, g * HEAD_DIM), BF16))
        parts.append(q_row[:, h * HEAD_DIM:(h + 1) * HEAD_DIM])
        if g < N_KV_GROUPS - 1:
            parts.append(jnp.zeros((1, (N_KV_GROUPS - 1 - g) * HEAD_DIM), BF16))
        rows.append(jnp.concatenate(parts, axis=1))
    q_blk = jnp.concatenate(rows + [jnp.zeros((DEC_ROWS - N_HEADS, D_K), BF16)], axis=0)

    comp = comp_ref[0].astype(BF16)
    mask_c = lax.broadcasted_iota(I32, (DEC_ROWS, n_cmp), 1) < n_cmp - 1
    p_c = _masked_softmax_rows(_dot_nt(q_blk, comp[:, :D_K]) + bias_c_ref[...], mask_c)
    o_c = _dot(p_c.astype(BF16), comp[:, D_K:])
    p_sums = [jnp.sum(p_c[g * HEADS_PER_GROUP:(g + 1) * HEADS_PER_GROUP], axis=0, keepdims=True)
              for g in range(N_KV_GROUPS)]
    p_sum = jnp.concatenate(p_sums + [jnp.zeros((DEC_ROWS - N_KV_GROUPS, n_cmp), F32)], axis=0)

    c_start = lax.broadcasted_iota(I32, (n_cmp, blk_lanes), 0) * CMP_STRIDE
    s_start = lax.broadcasted_iota(I32, (n_cmp, blk_lanes), 1) * SEL_LEN
    c_idx = lax.broadcasted_iota(I32, (n_cmp, blk_lanes), 0)
    cover = jnp.where((c_start < s_start + SEL_LEN) & (c_start + CMP_LEN > s_start) & (c_idx < n_cmp - 1),
                      1.0, 0.0).astype(BF16)
    imp = _dot(p_sum.astype(BF16), cover)
    blk = lax.broadcasted_iota(I32, (DEC_ROWS, blk_lanes), 1)
    cur = past // SEL_LEN
    forced = (blk == 0) | (blk == cur) | (blk == cur - 1)
    score = jnp.where(forced, FORCE, imp)
    score = jnp.where(blk < n_blocks, score, NEG)
    score_t = score.T
    j_idx = lax.broadcasted_iota(I32, (blk_lanes, blk_lanes), 0)
    i_idx = lax.broadcasted_iota(I32, (blk_lanes, blk_lanes), 1)
    sel_groups = []
    for g in range(N_KV_GROUPS):
        col = score_t[:, g:g + 1]
        row = score[g:g + 1, :]
        beats = (col > row) | ((col == row) & (j_idx < i_idx))
        rank = jnp.sum(jnp.where(beats, 1.0, 0.0), axis=0, keepdims=True)
        sel_groups.append(jnp.where((rank < float(N_SEL)) & (row > 0.5 * NEG), 1.0, 0.0))
    sel_rows = [sel_groups[h // HEADS_PER_GROUP] for h in range(N_HEADS)]
    sel_bf = jnp.concatenate(sel_rows + [jnp.zeros((DEC_ROWS - N_HEADS, blk_lanes), F32)], axis=0).astype(BF16)

    _gather_pages_wait(pt_ref, pool_ref, kv_buf.at[slot], sems.at[slot], b, n_pages, page)
    e_row = lax.broadcasted_iota(I32, (blk_lanes, DEC_TILE), 0)
    e_col = lax.broadcasted_iota(I32, (blk_lanes, DEC_TILE), 1) // SEL_LEN

    def sel_tile(t, carry):
        m, l, acc = carry
        k0 = pl.multiple_of(t * DEC_TILE, DEC_TILE)
        expand = jnp.where(e_row == e_col + t * (DEC_TILE // SEL_LEN), 1.0, 0.0).astype(BF16)
        key_sel = _dot(sel_bf, expand)
        kt = kv_buf[slot, 0, :, pl.ds(k0, DEC_TILE)].astype(BF16)
        vt = kv_buf[slot, 1, :, pl.ds(k0, DEC_TILE)].astype(BF16)
        s = _dot(q_blk, kt) + bias_s_ref[:, pl.ds(k0, DEC_TILE)] + jnp.where(key_sel > 0.5, 0.0, NEG)
        m_new = jnp.maximum(m, jnp.max(s, axis=-1, keepdims=True))
        alpha = jnp.exp(m - m_new)
        p = jnp.exp(s - m_new)
        return m_new, alpha * l + jnp.sum(p, axis=-1, keepdims=True), alpha * acc + _dot_nt(p.astype(BF16), vt)

    init = (jnp.full((DEC_ROWS, 1), NEG, F32), jnp.zeros((DEC_ROWS, 1), F32), jnp.zeros((DEC_ROWS, D_K), F32))
    m_s, l_s, acc_s = lax.fori_loop(0, n_tiles, sel_tile, init)
    s_new, v_new = _new_token_scores(q_blk, slc_new_ref[0], bias0)
    cur_sel = sel_bf[:, cur:cur + 1].astype(F32)
    s_new = s_new + jnp.where(cur_sel > 0.5, 0.0, NEG)
    m_fin = jnp.maximum(m_s, s_new)
    alpha = jnp.exp(m_s - m_fin)
    p_new = jnp.exp(s_new - m_fin)
    o_s = (alpha * acc_s + p_new.astype(BF16).astype(F32) * v_new) / (alpha * l_s + p_new)

    kw = win_ref[0, 0].astype(BF16)
    vw = win_ref[0, 1].astype(BF16)
    idx_w = lax.broadcasted_iota(I32, (DEC_ROWS, w_buf), 1)
    s_w = jnp.where(w_buf - idx_w < WINDOW, _dot(q_blk, kw) + bias_w_ref[...], NEG)
    s_wn, v_wn = _new_token_scores(q_blk, win_new_ref[0], bias0)
    m_w = jnp.maximum(jnp.max(s_w, axis=-1, keepdims=True), s_wn)
    e_w = jnp.exp(s_w - m_w)
    e_wn = jnp.exp(s_wn - m_w)
    o_w = ((_dot_nt(e_w.astype(BF16), vw) + e_wn.astype(BF16).astype(F32) * v_wn)
           / (jnp.sum(e_w, axis=-1, keepdims=True) + e_wn))

    outs = []
    for h in range(N_HEADS):
        outs.append(gates[:, 3 * h:3 * h + 1] * _head_block(o_c, h)
                    + gates[:, 3 * h + 1:3 * h + 2] * _head_block(o_s, h)
                    + gates[:, 3 * h + 2:3 * h + 3] * _head_block(o_w, h))
    o_ref[0] = jnp.concatenate(outs, axis=1).astype(o_ref.dtype)


def _feature_major(cache):
    nd = cache.ndim
    perm = tuple(range(nd - 4)) + (nd - 3, nd - 2, nd - 1, nd - 4)
    t = cache.transpose(perm)
    return t.reshape(t.shape[:nd - 3] + (t.shape[nd - 3] * t.shape[nd - 2], t.shape[nd - 1]))


def _nsa_step(tab, page_table, q, gl, comp, pool_t, slc_new, win_t, win_new):
    b, n_pages = page_table.shape
    page = pool_t.shape[3]
    past = n_pages * page
    n_cmp = comp.shape[1]
    w_buf = win_t.shape[3]
    dq = q.shape[2]
    assert past % DEC_TILE == 0 and past % SEL_LEN == 0 and w_buf == WINDOW and n_cmp == past // CMP_STRIDE
    row3 = lambda w: pl.BlockSpec((1, 1, w), lambda i: (i, 0, 0))
    return pl.pallas_call(
        functools.partial(_nsa_step_kernel, n_seq=b, n_pages=n_pages, page=page, past=past, n_cmp=n_cmp,
                          w_buf=w_buf),
        grid=(b,),
        in_specs=[pl.BlockSpec(memory_space=pltpu.SMEM), pl.BlockSpec(memory_space=pltpu.SMEM),
                  row3(dq), row3(LANES),
                  pl.BlockSpec((1, n_cmp, D_KV), lambda i: (i, 0, 0)),
                  pl.BlockSpec(memory_space=pl.ANY),
                  row3(D_KV),
                  pl.BlockSpec((1, 2, D_K, w_buf), lambda i: (i, 0, 0, 0)),
                  row3(D_KV)],
        out_specs=row3(dq),
        out_shape=jax.ShapeDtypeStruct((b, 1, dq), BF16),
        scratch_shapes=[pltpu.VMEM((2, 2, D_K, past), F32),
                        pltpu.VMEM((DEC_ROWS, n_cmp), F32),
                        pltpu.VMEM((DEC_ROWS, past), F32),
                        pltpu.VMEM((DEC_ROWS, w_buf), F32),
                        pltpu.SemaphoreType.DMA((2,))],
        compiler_params=_cparams(1),
        name="nsa_step",
    )(tab, page_table, q, gl, comp, pool_t, slc_new, win_t, win_new)


def _block_diag_pairs(w):
    n, blk, _ = w.shape
    w = w.reshape(n // 2, 2, blk, blk)
    z = jnp.zeros((n // 2, blk, blk), w.dtype)
    top = jnp.concatenate([w[:, 0], z], axis=2)
    bot = jnp.concatenate([z, w[:, 1]], axis=2)
    return jnp.concatenate([top, bot], axis=1).astype(BF16)


def _prep_weights(w_in_a, conv_w, conv_b, w_rg, b_rg, w_ig, b_ig, lru_lambda, w_in_b, w_kv_shared, cmp_pe, cmp_w1,
                  cmp_b1, cmp_w2, w_mem_kv, w_out, ln1_g, ln1_b, ln2_g, ln2_b, w_router_g, b_router_g,
                  w_router_e, b_router_e, w_exp_gate, w_exp_up, w_exp_down):
    depth, d_model, _ = w_out.shape
    n_a = w_in_a.shape[0]
    nq = N_HEADS * HEAD_DIM
    n_gl = 3 * N_HEADS
    hid = EXPERTS_PER_GROUP * D_EXPERT
    p = {}
    p["w_in_a"] = w_in_a.astype(BF16)
    p["w_in_b"] = jnp.concatenate(
        [w_in_b[:, :, :nq], w_in_b[:, :, nq + n_gl:], w_in_b[:, :, nq:nq + n_gl],
         jnp.zeros((w_in_b.shape[0], d_model, LANES - n_gl), w_in_b.dtype)], axis=2).astype(BF16)
    p["w_kv"] = w_kv_shared.astype(BF16)
    p["w_mem"] = w_mem_kv.transpose(1, 0, 2).reshape(d_model, depth * 2 * D_MEM).astype(BF16)
    p["wo_a"] = w_out[:, :D_RNN].astype(BF16)
    p["wo_m"] = w_out[:, D_RNN:].astype(BF16)
    p["conv_w"] = conv_w
    p["conv_b"] = conv_b[:, None]
    p["w_rg"] = jnp.stack([_block_diag_pairs(w_rg[l]) for l in range(n_a)])
    p["w_ig"] = jnp.stack([_block_diag_pairs(w_ig[l]) for l in range(n_a)])
    p["b_rg"] = b_rg[:, None]
    p["b_ig"] = b_ig[:, None]
    p["lam"] = lru_lambda[:, None]
    p["pe"], p["cmp_w1"], p["cmp_b1"], p["cmp_w2"] = _compress_weights(cmp_pe, cmp_w1, cmp_b1, cmp_w2)
    p["ln1_g"], p["ln1_b"], p["ln2_g"], p["ln2_b"] = ln1_g[:, None], ln1_b[:, None], ln2_g[:, None], ln2_b[:, None]
    n_r = N_GROUPS + N_EXPERTS
    wr = jnp.concatenate([w_router_g, w_router_e, jnp.zeros((depth, d_model, LANES - n_r), F32)], axis=2)
    p["wr"] = wr.astype(BF16)
    p["rb"] = jnp.concatenate([b_router_g, b_router_e, jnp.zeros((depth, LANES - n_r), F32)], axis=1)[:, None]

    def by_group(w):
        w = w.astype(BF16).reshape(depth, N_GROUPS, EXPERTS_PER_GROUP, d_model, D_EXPERT)
        return w.transpose(0, 1, 3, 2, 4).reshape(depth, N_GROUPS, d_model, hid)
    p["w_gu"] = jnp.concatenate([by_group(w_exp_gate), by_group(w_exp_up)], axis=3)
    p["w_dn"] = w_exp_down.astype(BF16).reshape(depth, N_GROUPS, hid, d_model)
    return p


def _kv_layouts(rows):
    b, t_len, _ = rows.shape
    k = rows[:, :, :D_KV // 2].reshape(b, t_len, N_KV_GROUPS, HEAD_DIM).transpose(0, 2, 1, 3)
    return k, rows[:, :, D_KV // 2:].transpose(0, 2, 1)


def _trunk_tail(x, mix_a, mix_m, p, l, alpha, tm):
    x = _out_ln(x, mix_a, mix_m, p["wo_a"][l], p["wo_m"][l], p["ln1_g"][l], p["ln1_b"][l], alpha, tm)
    return _moe_ln(x, p["wr"][l], p["rb"][l], p["w_gu"][l], p["w_dn"][l],
                   p["ln2_g"][l], p["ln2_b"][l], alpha, tm)


def kernel(x_prompt, x_sample, mem_prompt, cache_cmp_kv, cache_slc_kv, cache_win_kv, cache_mem_kv, state_lru_h, state_conv, page_table, w_in_a, conv_w, conv_b, w_rg, b_rg, w_ig, b_ig, lru_lambda, w_in_b, w_kv_shared, cmp_pe, cmp_w1, cmp_b1, cmp_w2, rel_bias, w_mem_kv, w_out, ln1_g, ln1_b, ln2_g, ln2_b, w_router_g, b_router_g, w_router_e, b_router_e, w_exp_gate, w_exp_up, w_exp_down):
    p = _prep_weights(w_in_a, conv_w, conv_b, w_rg, b_rg, w_ig, b_ig, lru_lambda, w_in_b, w_kv_shared, cmp_pe,
                      cmp_w1, cmp_b1, cmp_w2, w_mem_kv, w_out, ln1_g, ln1_b, ln2_g, ln2_b, w_router_g,
                      b_router_g, w_router_e, b_router_e, w_exp_gate, w_exp_up, w_exp_down)
    depth, d_model, _ = w_out.shape
    n_a = w_in_a.shape[0]
    alpha = (2 * depth) ** 0.25
    kv_shape = (2, N_KV_GROUPS, HEAD_DIM)
    a_outs = [(0, D_RNN, F32), (D_RNN, D_RNN, F32), (2 * D_RNN, D_MEM, BF16)]
    b_outs = [(0, D_RNN, BF16), (D_RNN, D_MEM, BF16), (D_RNN + D_MEM, LANES, F32)]
    kv_outs = [(0, D_KV, F32), (D_KV, D_KV, F32), (2 * D_KV, D_KV, F32)]

    bp, t_len, _ = x_prompt.shape
    n_mem = mem_prompt.shape[1]
    tm = 512
    mem_rows = _proj(mem_prompt.reshape(bp * n_mem, d_model), p["w_mem"],
                     [(l * 2 * D_MEM, 2 * D_MEM, F32) for l in range(depth)], tm, "mem_kv_proj")
    p_mem = jnp.stack(mem_rows).reshape(depth, bp, n_mem, 2 * D_MEM)
    x = x_prompt.reshape(bp * t_len, d_model)
    lru_p, conv_p = [], []
    for l in range(depth):
        if l < n_a:
            gate, xr, mq = _proj(x, p["w_in_a"][l], a_outs, tm, "in_proj_a")
            mix_a, h_t, buf = _rglru_prompt(gate.reshape(bp, t_len, D_RNN), xr.reshape(bp, t_len, D_RNN),
                                            p["conv_w"][l], p["conv_b"][l], p["w_rg"][l], p["w_ig"][l],
                                            p["b_rg"][l], p["b_ig"][l], p["lam"][l])
            lru_p.append(h_t[:, 0])
            conv_p.append(buf)
        else:
            if l == n_a:
                cmp_rows, slc_rows, win_rows, slc_bf, win_bf = _proj(
                    x, p["w_kv"], kv_outs + [(D_KV, D_KV, BF16), (2 * D_KV, D_KV, BF16)], tm, "kv_proj")
                comp = _compress_prompt(cmp_rows.reshape(bp, t_len, D_KV), p["pe"], p["cmp_w1"], p["cmp_b1"],
                                        p["cmp_w2"])
                comp_k, comp_vt = _kv_layouts(comp.astype(BF16))
                slc_k, slc_vt = _kv_layouts(slc_bf.reshape(bp, t_len, D_KV))
                win_k, win_vt = _kv_layouts(win_bf.reshape(bp, t_len, D_KV))
            q, mq, gl = _proj(x, p["w_in_b"][l - n_a], b_outs, tm, "in_proj_b")
            mix_t = _nsa_prompt(rel_bias, q.reshape(bp, t_len, D_RNN).transpose(0, 2, 1),
                                gl.reshape(bp, t_len, LANES).transpose(0, 2, 1),
                                comp_k, comp_vt, slc_k, slc_vt, win_k, win_vt)
            mix_a = mix_t.transpose(0, 2, 1)
        mix_m = _mem_attn(mq.reshape(bp, t_len, D_MEM), p_mem, l)
        x = _trunk_tail(x, mix_a.reshape(bp * t_len, D_RNN), mix_m.reshape(bp * t_len, D_MEM), p, l, alpha, tm)
    y_prompt = x.reshape(bp, t_len, d_model)
    p_cmp_kv = cmp_rows.reshape((bp, t_len) + kv_shape)
    p_slc_kv = slc_rows.reshape((bp, t_len) + kv_shape)
    w_keep = min(WINDOW, t_len)
    p_win_kv = win_rows.reshape((bp, t_len) + kv_shape)[:, t_len - w_keep:]
    p_mem_kv = p_mem.reshape(depth, bp, n_mem, 2, N_MEM_HEADS, HEAD_DIM)
    p_lru_h = jnp.stack(lru_p)
    p_conv = jnp.stack(conv_p)

    bd, s_len, _ = x_sample.shape
    assert s_len == 1
    n_pool, page = cache_cmp_kv.shape[:2]
    w_buf = cache_win_kv.shape[1]
    mem_t = _feature_major(cache_mem_kv)
    slc_pool_t = _feature_major(cache_slc_kv)
    win_t = _feature_major(cache_win_kv)
    x = x_sample.reshape(bd, d_model)
    lru_s, conv_s = [], []
    for l in range(depth):
        if l < n_a:
            gate, xr, mq = _proj(x, p["w_in_a"][l], a_outs, bd, "in_proj_a_step")
            mix_a, h_t, buf = _rglru_step(gate, xr, state_conv[l].transpose(1, 0, 2), state_lru_h[l],
                                          p["conv_w"][l], p["conv_b"][l], p["w_rg"][l], p["w_ig"][l],
                                          p["b_rg"][l], p["b_ig"][l], p["lam"][l])
            lru_s.append(h_t)
            conv_s.append(buf.transpose(1, 0, 2))
        else:
            if l == n_a:
                s_cmp, s_slc, s_win = _proj(x, p["w_kv"], kv_outs, bd, "kv_proj_step")
                comp_s = _compress_pages(page_table, _feature_major(cache_cmp_kv), p["pe"],
                                         p["cmp_w1"], p["cmp_b1"], p["cmp_w2"])
            q, mq, gl = _proj(x, p["w_in_b"][l - n_a], b_outs, bd, "in_proj_b_step")
            mix_a = _nsa_step(rel_bias, page_table, q.reshape(bd, 1, D_RNN), gl.reshape(bd, 1, LANES), comp_s,
                              slc_pool_t, s_slc.reshape(bd, 1, D_KV), win_t, s_win.reshape(bd, 1, D_KV))
            mix_a = mix_a.reshape(bd, D_RNN)
        mix_m = _mem_attn_step(mq.reshape(bd, 1, D_MEM), mem_t, l).reshape(bd, D_MEM)
        x = _trunk_tail(x, mix_a, mix_m, p, l, alpha, bd)
    y_sample = x.reshape(bd, 1, d_model)
    s_cmp_kv = s_cmp.reshape((bd, 1) + kv_shape)
    s_slc_kv = s_slc.reshape((bd, 1) + kv_shape)
    s_win_kv = jnp.concatenate([cache_win_kv, s_win.reshape((bd, 1) + kv_shape)], axis=1)[:, 1:]
    s_lru_h = jnp.stack(lru_s)
    s_conv = jnp.stack(conv_s)
    return (y_prompt, y_sample, p_cmp_kv, p_slc_kv, p_win_kv, p_mem_kv, p_lru_h, p_conv,
            s_cmp_kv, s_slc_kv, s_win_kv, s_lru_h, s_conv)
```

```python
import functools
import math

import jax
import jax.numpy as jnp
from jax import lax
from jax.experimental import pallas as pl
from jax.experimental.pallas import tpu as pltpu

F32 = jnp.float32
BF16 = jnp.bfloat16
I32 = jnp.int32

HEAD_DIM = 64
N_MEM_HEADS = 4
D_MEM = N_MEM_HEADS * HEAD_DIM
N_HEADS = 12
D_RNN = N_HEADS * HEAD_DIM
N_KV_GROUPS = 4
HEADS_PER_GROUP = N_HEADS // N_KV_GROUPS
D_KV = 2 * N_KV_GROUPS * HEAD_DIM
CONV_W = 4
LRU_C = 8.0
CMP_LEN = 32
CMP_STRIDE = 16
CMP_HID = 2 * HEAD_DIM
SEL_LEN = 64
N_SEL = 16
WINDOW = 512
N_BUCKETS = 32
MAX_DISTANCE = 1024
N_GROUPS = 4
EXPERTS_PER_GROUP = 8
N_EXPERTS = N_GROUPS * EXPERTS_PER_GROUP
D_EXPERT = 128
Q_BLOCK = 128
LN_EPS = 1e-5
NEG = -1e30
FORCE = 1e9
SCALE = HEAD_DIM ** -0.5

LANES = 128
SUBLANES = 8
VMEM_LIMIT_BYTES = 56 * 1024 * 1024


def _cparams(n_axes):
    return pltpu.CompilerParams(dimension_semantics=("arbitrary",) * n_axes,
                                vmem_limit_bytes=VMEM_LIMIT_BYTES)


def _dot(a, b):
    return jnp.dot(a, b, preferred_element_type=F32)


def _dot_nt(a, b):
    return lax.dot_general(a, b, (((1,), (1,)), ((), ())), preferred_element_type=F32)


def _layer_norm(v, g, b):
    mu = jnp.mean(v, axis=-1, keepdims=True)
    d = v - mu
    var = jnp.mean(d * d, axis=-1, keepdims=True)
    return d * lax.rsqrt(var + LN_EPS) * g + b


def _rel_bucket(dist):
    n = jnp.maximum(dist, 0)
    max_exact = N_BUCKETS // 2
    nf = jnp.maximum(n, 1).astype(F32)
    large = max_exact + (jnp.log(nf / max_exact) / math.log(MAX_DISTANCE / max_exact)
                         * (N_BUCKETS - max_exact)).astype(I32)
    large = jnp.minimum(large, N_BUCKETS - 1)
    return jnp.where(n < max_exact, n, large)


def _bias_from_bucket(bucket, tab_ref, heads):
    masks = [bucket == k for k in range(1, N_BUCKETS)]
    out = []
    for h in heads:
        acc = jnp.full(bucket.shape, tab_ref[0, h], F32)
        for k in range(1, N_BUCKETS):
            acc = jnp.where(masks[k - 1], tab_ref[k, h], acc)
        out.append(acc)
    return out


def _proj_kernel(x_ref, w_ref, *o_refs, cols):
    y = _dot(x_ref[...].astype(BF16), w_ref[...])
    for o_ref, (off, n) in zip(o_refs, cols):
        o_ref[...] = y[:, off:off + n].astype(o_ref.dtype)


def _proj(x, w, outs, tm, name):
    m, k = x.shape
    n = w.shape[1]
    tm = min(tm, m)
    assert m % tm == 0 and all(off + wd <= n for off, wd, _ in outs)
    return pl.pallas_call(
        functools.partial(_proj_kernel, cols=tuple((off, wd) for off, wd, _ in outs)),
        grid=(m // tm,),
        in_specs=[pl.BlockSpec((tm, k), lambda i: (i, 0)),
                  pl.BlockSpec((k, n), lambda i: (0, 0))],
        out_specs=[pl.BlockSpec((tm, wd), lambda i: (i, 0)) for _, wd, _ in outs],
        out_shape=[jax.ShapeDtypeStruct((m, wd), dt) for _, wd, dt in outs],
        compiler_params=_cparams(1),
        name=name,
    )(x, w)


def _softplus(z):
    return jnp.maximum(z, 0.0) + jnp.log1p(jnp.exp(-jnp.abs(z)))


def _lru_gates(xc, wr_ref, wi_ref, br, bi, lam):
    xcb = xc.astype(BF16)
    nb = D_RNN // LANES
    r_l = jnp.concatenate([_dot(xcb[:, j * LANES:(j + 1) * LANES], wr_ref[j]) for j in range(nb)], axis=1)
    i_l = jnp.concatenate([_dot(xcb[:, j * LANES:(j + 1) * LANES], wi_ref[j]) for j in range(nb)], axis=1)
    r = jax.nn.sigmoid(r_l + br)
    i = jax.nn.sigmoid(i_l + bi)
    log_a = -LRU_C * r * _softplus(-lam)
    a = jnp.exp(log_a)
    u = jnp.sqrt(1.0 - a * a) * i * xc
    return a, u


def _rglru_prompt_kernel(gate_ref, xr_ref, cw_ref, cb_ref, wr_ref, wi_ref, br_ref, bi_ref, lam_ref,
                         y_ref, ht_ref, buf_ref, xp_ref, *, t_len, chunk):
    d = D_RNN
    pad = SUBLANES
    xp_ref[0:pad, :] = jnp.zeros((pad, d), F32)
    xp_ref[pad:pad + t_len, :] = xr_ref[0]
    buf_ref[0] = xr_ref[0, t_len - (CONV_W - 1):t_len, :]
    cw = cw_ref[...]
    cb = cb_ref[...]
    br = br_ref[...]
    bi = bi_ref[...]
    lam = lam_ref[...]
    row_in_tile = lax.broadcasted_iota(I32, (chunk, d), 0) & (SUBLANES - 1)
    h = jnp.zeros((1, d), F32)
    for c in range(t_len // chunk):
        base = c * chunk
        xc = cb
        for k in range(CONV_W):
            lo = pad - (CONV_W - 1) + k + base
            xc = xc + xp_ref[lo:lo + chunk, :] * cw[k:k + 1, :]
        a, u = _lru_gates(xc, wr_ref, wi_ref, br, bi, lam)
        for k in (1, 2, 4):
            a_s = pltpu.roll(a, k, axis=0)
            u_s = pltpu.roll(u, k, axis=0)
            m = row_in_tile >= k
            u = jnp.where(m, a * u_s + u, u)
            a = jnp.where(m, a * a_s, a)
        hs = []
        for j in range(chunk // SUBLANES):
            h_t = a[j * SUBLANES:(j + 1) * SUBLANES] * h + u[j * SUBLANES:(j + 1) * SUBLANES]
            h = h_t[SUBLANES - 1:SUBLANES]
            hs.append(h_t)
        hfull = jnp.concatenate(hs, axis=0)
        g = jax.nn.gelu(gate_ref[0, base:base + chunk, :])
        y_ref[0, base:base + chunk, :] = (g * hfull).astype(y_ref.dtype)
    ht_ref[0] = h


def _rglru_prompt(gate, xr, cw, cb, wr_bd, wi_bd, br, bi, lam):
    b, t_len, d = xr.shape
    chunk = min(256, t_len)
    full = lambda shape: pl.BlockSpec(shape, lambda i: (0,) * len(shape))
    return pl.pallas_call(
        functools.partial(_rglru_prompt_kernel, t_len=t_len, chunk=chunk),
        grid=(b,),
        in_specs=[pl.BlockSpec((1, t_len, d), lambda i: (i, 0, 0)),
                  pl.BlockSpec((1, t_len, d), lambda i: (i, 0, 0)),
                  full((CONV_W, d)), full((1, d)), full(wr_bd.shape), full(wi_bd.shape),
                  full((1, d)), full((1, d)), full((1, d))],
        out_specs=[pl.BlockSpec((1, t_len, d), lambda i: (i, 0, 0)),
                   pl.BlockSpec((1, 1, d), lambda i: (i, 0, 0)),
                   pl.BlockSpec((1, CONV_W - 1, d), lambda i: (i, 0, 0))],
        out_shape=[jax.ShapeDtypeStruct((b, t_len, d), BF16),
                   jax.ShapeDtypeStruct((b, 1, d), F32),
                   jax.ShapeDtypeStruct((b, CONV_W - 1, d), F32)],
        scratch_shapes=[pltpu.VMEM((t_len + SUBLANES, d), F32)],
        compiler_params=_cparams(1),
        name="rglru_prompt",
    )(gate, xr, cw, cb, wr_bd, wi_bd, br, bi, lam)


def _rglru_step_kernel(gate_ref, xr_ref, conv_ref, h0_ref, cw_ref, cb_ref, wr_ref, wi_ref, br_ref, bi_ref,
                       lam_ref, y_ref, ht_ref, buf_ref):
    xr = xr_ref[...]
    cw = cw_ref[...]
    xc = cb_ref[...] + xr * cw[CONV_W - 1:CONV_W, :]
    for k in range(CONV_W - 1):
        xc = xc + conv_ref[k] * cw[k:k + 1, :]
    a, u = _lru_gates(xc, wr_ref, wi_ref, br_ref[...], bi_ref[...], lam_ref[...])
    h = a * h0_ref[...] + u
    ht_ref[...] = h
    y_ref[...] = (jax.nn.gelu(gate_ref[...]) * h).astype(y_ref.dtype)
    for k in range(CONV_W - 2):
        buf_ref[k] = conv_ref[k + 1]
    buf_ref[CONV_W - 2] = xr


def _rglru_step(gate, xr, conv_t, h0, cw, cb, wr_bd, wi_bd, br, bi, lam):
    b, d = xr.shape
    return pl.pallas_call(
        _rglru_step_kernel,
        out_shape=[jax.ShapeDtypeStruct((b, d), BF16),
                   jax.ShapeDtypeStruct((b, d), F32),
                   jax.ShapeDtypeStruct((CONV_W - 1, b, d), F32)],
        compiler_params=pltpu.CompilerParams(vmem_limit_bytes=VMEM_LIMIT_BYTES),
        name="rglru_step",
    )(gate, xr, conv_t, h0, cw, cb, wr_bd, wi_bd, br, bi, lam)


def _mem_attn_kernel(q_ref, kv_ref, o_ref, *, rows):
    q = q_ref[0]
    if rows < SUBLANES:
        q = jnp.broadcast_to(q, (SUBLANES, D_MEM))
    kv = kv_ref[0, 0].astype(BF16)
    outs = []
    for h in range(N_MEM_HEADS):
        qh = q[:, h * HEAD_DIM:(h + 1) * HEAD_DIM]
        kh = kv[:, h * HEAD_DIM:(h + 1) * HEAD_DIM]
        vh = kv[:, D_MEM + h * HEAD_DIM:D_MEM + (h + 1) * HEAD_DIM]
        s = _dot_nt(qh, kh) * SCALE
        m = jnp.max(s, axis=-1, keepdims=True)
        e = jnp.exp(s - m)
        p = e / jnp.sum(e, axis=-1, keepdims=True)
        outs.append(_dot(p.astype(BF16), vh))
    o = jnp.concatenate(outs, axis=1)
    o_ref[0] = o[:rows].astype(o_ref.dtype)


def _mem_attn(q, mem_kv, layer):
    b, t_len, _ = q.shape
    n_mem = mem_kv.shape[2]
    tq = min(t_len, 512)
    return pl.pallas_call(
        functools.partial(_mem_attn_kernel, rows=tq),
        grid=(b, t_len // tq),
        in_specs=[pl.BlockSpec((1, tq, D_MEM), lambda i, j: (i, j, 0)),
                  pl.BlockSpec((1, 1, n_mem, 2 * D_MEM), lambda i, j: (layer, i, 0, 0))],
        out_specs=pl.BlockSpec((1, tq, D_MEM), lambda i, j: (i, j, 0)),
        out_shape=jax.ShapeDtypeStruct((b, t_len, D_MEM), BF16),
        compiler_params=_cparams(2),
        name="mem_attn",
    )(q, mem_kv)


def _mem_attn_step_kernel(q_ref, kv_ref, o_ref):
    q_row = q_ref[0] * SCALE
    rows = []
    for h in range(N_MEM_HEADS):
        parts = []
        if h > 0:
            parts.append(jnp.zeros((1, h * HEAD_DIM), BF16))
        parts.append(q_row[:, h * HEAD_DIM:(h + 1) * HEAD_DIM])
        if h < N_MEM_HEADS - 1:
            parts.append(jnp.zeros((1, (N_MEM_HEADS - 1 - h) * HEAD_DIM), BF16))
        rows.append(jnp.concatenate(parts, axis=1))
    n_rows = 2 * SUBLANES
    q_blk = jnp.concatenate(rows + [jnp.zeros((n_rows - N_MEM_HEADS, D_MEM), BF16)], axis=0)
    s = _dot(q_blk, kv_ref[0, 0, 0].astype(BF16))
    e = jnp.exp(s - jnp.max(s, axis=-1, keepdims=True))
    p = e / jnp.sum(e, axis=-1, keepdims=True)
    o_all = _dot_nt(p.astype(BF16), kv_ref[0, 0, 1].astype(BF16))
    o = jnp.concatenate([o_all[h:h + 1, h * HEAD_DIM:(h + 1) * HEAD_DIM] for h in range(N_MEM_HEADS)], axis=1)
    o_ref[0] = o.astype(o_ref.dtype)


def _mem_attn_step(q, mem_t, layer):
    b = q.shape[0]
    n_mem = mem_t.shape[4]
    return pl.pallas_call(
        _mem_attn_step_kernel,
        grid=(b,),
        in_specs=[pl.BlockSpec((1, 1, D_MEM), lambda i: (i, 0, 0)),
                  pl.BlockSpec((1, 1, 2, D_MEM, n_mem), lambda i: (layer, i, 0, 0, 0))],
        out_specs=pl.BlockSpec((1, 1, D_MEM), lambda i: (i, 0, 0)),
        out_shape=jax.ShapeDtypeStruct((b, 1, D_MEM), BF16),
        compiler_params=_cparams(1),
        name="mem_attn_step",
    )(q, mem_t)


def _out_ln_kernel(x_ref, ma_ref, mm_ref, wa_ref, wm_ref, g_ref, b_ref, o_ref, *, alpha):
    sub = _dot(ma_ref[...], wa_ref[...]) + _dot(mm_ref[...], wm_ref[...])
    o_ref[...] = _layer_norm(alpha * x_ref[...] + sub, g_ref[...], b_ref[...])


def _out_ln(x, mix_a, mix_m, wo_a, wo_m, g, b, alpha, tm):
    m, d = x.shape
    da = mix_a.shape[1]
    dm = mix_m.shape[1]
    return pl.pallas_call(
        functools.partial(_out_ln_kernel, alpha=alpha),
        grid=(m // tm,),
        in_specs=[pl.BlockSpec((tm, d), lambda i: (i, 0)),
                  pl.BlockSpec((tm, da), lambda i: (i, 0)),
                  pl.BlockSpec((tm, dm), lambda i: (i, 0)),
                  pl.BlockSpec((da, d), lambda i: (0, 0)),
                  pl.BlockSpec((dm, d), lambda i: (0, 0)),
                  pl.BlockSpec((1, d), lambda i: (0, 0)),
                  pl.BlockSpec((1, d), lambda i: (0, 0))],
        out_specs=pl.BlockSpec((tm, d), lambda i: (i, 0)),
        out_shape=jax.ShapeDtypeStruct((m, d), F32),
        compiler_params=_cparams(1),
        name="out_ln",
    )(x, mix_a, mix_m, wo_a, wo_m, g, b)


def _route_group(logits):
    lane = lax.broadcasted_iota(I32, logits.shape, 1)
    lg = jnp.where(lane < N_GROUPS, logits, -jnp.inf)
    gmax = jnp.max(lg, axis=-1, keepdims=True)
    return jnp.min(jnp.where(lg == gmax, lane, jnp.int32(4 * LANES)), axis=-1, keepdims=True), gmax


def _route(logits):
    lane = lax.broadcasted_iota(I32, logits.shape, 1)
    big = jnp.int32(4 * LANES)
    is_g = lane < N_GROUPS
    grp, gmax = _route_group(logits)
    gsum = jnp.sum(jnp.where(is_g, jnp.exp(logits - gmax), 0.0), axis=-1, keepdims=True)
    g_w = 1.0 / gsum
    e_idx = lane - N_GROUPS
    in_grp = (lane >= N_GROUPS) & (lane < N_GROUPS + N_EXPERTS) & ((e_idx >> 3) == grp)
    v = jnp.where(in_grp, logits, -jnp.inf)
    v1 = jnp.max(v, axis=-1, keepdims=True)
    i1 = jnp.min(jnp.where(v == v1, lane, big), axis=-1, keepdims=True)
    vr = jnp.where(lane == i1, -jnp.inf, v)
    v2 = jnp.max(vr, axis=-1, keepdims=True)
    i2 = jnp.min(jnp.where(vr == v2, lane, big), axis=-1, keepdims=True)
    e2 = jnp.exp(v2 - v1)
    w1 = g_w / (1.0 + e2)
    w2 = g_w * e2 / (1.0 + e2)
    return jnp.where(lane == i1, w1, 0.0) + jnp.where(lane == i2, w2, 0.0)


def _moe_kernel(x_ref, wr_ref, rb_ref, wgu_ref, wd_ref, g_ref, b_ref, o_ref,
                xb_ref, comb_ref, acc_ref, *, alpha):
    grp = pl.program_id(1)
    hid = EXPERTS_PER_GROUP * D_EXPERT

    @pl.when(grp == 0)
    def _():
        xh = x_ref[...].astype(BF16)
        logits = _dot(xh, wr_ref[...]) + rb_ref[...]
        xb_ref[...] = xh
        comb_ref[...] = _route(logits)
        acc_ref[...] = jnp.zeros(acc_ref.shape, F32)

    xb = xb_ref[...]
    gu = _dot(xb, wgu_ref[0])
    gate = gu[:, :hid]
    up = gu[:, hid:]
    hdn = gate * jax.nn.sigmoid(gate) * up
    comb = comb_ref[...]
    lane = lax.broadcasted_iota(I32, comb.shape, 1)
    scale = []
    for e in range(EXPERTS_PER_GROUP):
        col = jnp.sum(jnp.where(lane == N_GROUPS + grp * EXPERTS_PER_GROUP + e, comb, 0.0),
                      axis=-1, keepdims=True)
        scale.append(jnp.broadcast_to(col, (comb.shape[0], D_EXPERT)))
    hdn = hdn * jnp.concatenate(scale, axis=1)
    acc_ref[...] += _dot(hdn.astype(BF16), wd_ref[0])

    @pl.when(grp == N_GROUPS - 1)
    def _():
        o_ref[...] = _layer_norm(alpha * x_ref[...] + acc_ref[...], g_ref[...], b_ref[...])


def _moe_ln(x, wr, rb, wgu, wd, g, b, alpha, tm):
    m, d = x.shape
    hid = EXPERTS_PER_GROUP * D_EXPERT
    return pl.pallas_call(
        functools.partial(_moe_kernel, alpha=alpha),
        grid=(m // tm, N_GROUPS),
        in_specs=[pl.BlockSpec((tm, d), lambda i, j: (i, 0)),
                  pl.BlockSpec((d, LANES), lambda i, j: (0, 0)),
                  pl.BlockSpec((1, LANES), lambda i, j: (0, 0)),
                  pl.BlockSpec((1, d, 2 * hid), lambda i, j: (j, 0, 0)),
                  pl.BlockSpec((1, hid, d), lambda i, j: (j, 0, 0)),
                  pl.BlockSpec((1, d), lambda i, j: (0, 0)),
                  pl.BlockSpec((1, d), lambda i, j: (0, 0))],
        out_specs=pl.BlockSpec((tm, d), lambda i, j: (i, 0)),
        out_shape=jax.ShapeDtypeStruct((m, d), F32),
        scratch_shapes=[pltpu.VMEM((tm, d), BF16), pltpu.VMEM((tm, LANES), F32), pltpu.VMEM((tm, d), F32)],
        compiler_params=_cparams(2),
        name="moe_ln",
    )(x, wr, rb, wgu, wd, g, b)


MOE_TILE = 256


def _group_ids_kernel(x_ref, wr_ref, rb_ref, o_ref):
    grp, _ = _route_group(_dot(x_ref[...].astype(BF16), wr_ref[...]) + rb_ref[...])
    o_ref[...] = jnp.broadcast_to(grp, o_ref.shape)


def _group_ids(x, wr, rb, tm):
    m, d = x.shape
    return pl.pallas_call(
        _group_ids_kernel,
        grid=(m // tm,),
        in_specs=[pl.BlockSpec((tm, d), lambda i: (i, 0)),
                  pl.BlockSpec((d, LANES), lambda i: (0, 0)),
                  pl.BlockSpec((1, LANES), lambda i: (0, 0))],
        out_specs=pl.BlockSpec((tm, LANES), lambda i: (i, 0)),
        out_shape=jax.ShapeDtypeStruct((m, LANES), I32),
        compiler_params=_cparams(1),
        name="moe_group_ids",
    )(x, wr, rb)


def _moe_dispatch(grp):
    n = grp.shape[0]
    onehot = (grp[:, None] == jnp.arange(N_GROUPS, dtype=I32)[None, :]).astype(I32)
    before = jnp.cumsum(onehot, axis=0) - onehot
    counts = jnp.sum(onehot, axis=0)
    padded = (counts + MOE_TILE - 1) // MOE_TILE * MOE_TILE
    ends = jnp.cumsum(padded)
    dest = jnp.sum(onehot * (before + (ends - padded)[None, :]), axis=1)
    n_slots = n + N_GROUPS * MOE_TILE
    slot_tok = jnp.full((n_slots,), -1, I32).at[dest].set(jnp.arange(n, dtype=I32))
    is_pad = slot_tok < 0
    pad_rank = jnp.cumsum(is_pad.astype(I32)) - 1
    gather_row = jnp.where(is_pad, 0, slot_tok)
    scatter_row = jnp.where(is_pad, n + pad_rank, slot_tok)
    tile_start = jnp.arange(n_slots // MOE_TILE, dtype=I32) * MOE_TILE
    tile_grp = jnp.minimum(jnp.sum((tile_start[:, None] >= ends[None, :]).astype(I32), axis=1), N_GROUPS - 1)
    return gather_row, scatter_row, tile_grp


def _moe_sorted_kernel(gsrc_ref, sdst_ref, tgrp_ref, x_hbm, wr_ref, rb_ref, wgu_ref, wd_ref, y_hbm,
                       xbuf, ybuf, gsem, ssem, *, n_tiles):
    i = pl.program_id(0)
    slot = i % 2
    other = 1 - slot
    hid = EXPERTS_PER_GROUP * D_EXPERT

    def gather_copy(tile, buf, r):
        return pltpu.make_async_copy(x_hbm.at[pl.ds(gsrc_ref[tile * MOE_TILE + r], 1)],
                                     xbuf.at[buf, pl.ds(r, 1)], gsem.at[buf])

    def scatter_copy(tile, buf, r):
        return pltpu.make_async_copy(ybuf.at[buf, pl.ds(r, 1)],
                                     y_hbm.at[pl.ds(sdst_ref[tile * MOE_TILE + r], 1)], ssem.at[buf])

    def start_all(copy, tile, buf):
        for r in range(MOE_TILE):
            copy(tile, buf, r).start()

    def wait_all(copy, tile, buf):
        if copy is gather_copy:
            pltpu.make_async_copy(x_hbm.at[pl.ds(0, MOE_TILE)], xbuf.at[buf], gsem.at[buf]).wait()
        else:
            pltpu.make_async_copy(ybuf.at[buf], y_hbm.at[pl.ds(0, MOE_TILE)], ssem.at[buf]).wait()

    @pl.when(i == 0)
    def _():
        start_all(gather_copy, i, slot)

    @pl.when(i + 1 < n_tiles)
    def _():
        start_all(gather_copy, i + 1, other)

    @pl.when(i >= 1)
    def _():
        start_all(scatter_copy, i - 1, other)

    wait_all(gather_copy, i, slot)
    xh = xbuf[slot].astype(BF16)
    comb = _route(_dot(xh, wr_ref[...]) + rb_ref[...])
    gu = _dot(xh, wgu_ref[0])
    gate = gu[:, :hid]
    hdn = gate * jax.nn.sigmoid(gate) * gu[:, hid:]
    grp = tgrp_ref[i]
    lane = lax.broadcasted_iota(I32, comb.shape, 1)
    scale = []
    for e in range(EXPERTS_PER_GROUP):
        col = jnp.sum(jnp.where(lane == N_GROUPS + grp * EXPERTS_PER_GROUP + e, comb, 0.0),
                      axis=-1, keepdims=True)
        scale.append(jnp.broadcast_to(col, (comb.shape[0], D_EXPERT)))
    y = _dot((hdn * jnp.concatenate(scale, axis=1)).astype(BF16), wd_ref[0])

    @pl.when(i >= 2)
    def _():
        wait_all(scatter_copy, i - 2, slot)

    ybuf[slot] = y

    @pl.when(i == n_tiles - 1)
    def _():
        start_all(scatter_copy, i, slot)

        @pl.when(i >= 1)
        def _():
            wait_all(scatter_copy, i - 1, other)
        wait_all(scatter_copy, i, slot)


def _moe_sorted(x, gather_row, scatter_row, tile_grp, wr, rb, wgu, wd):
    n, d = x.shape
    hid = EXPERTS_PER_GROUP * D_EXPERT
    n_slots = gather_row.shape[0]
    n_tiles = n_slots // MOE_TILE
    grid_spec = pltpu.PrefetchScalarGridSpec(
        num_scalar_prefetch=3,
        grid=(n_tiles,),
        in_specs=[pl.BlockSpec(memory_space=pl.ANY),
                  pl.BlockSpec((d, LANES), lambda i, *_: (0, 0)),
                  pl.BlockSpec((1, LANES), lambda i, *_: (0, 0)),
                  pl.BlockSpec((1, d, 2 * hid), lambda i, gs, sd, tg: (tg[i], 0, 0)),
                  pl.BlockSpec((1, hid, d), lambda i, gs, sd, tg: (tg[i], 0, 0))],
        out_specs=pl.BlockSpec(memory_space=pl.ANY),
        scratch_shapes=[pltpu.VMEM((2, MOE_TILE, d), F32), pltpu.VMEM((2, MOE_TILE, d), F32),
                        pltpu.SemaphoreType.DMA((2,)), pltpu.SemaphoreType.DMA((2,))])
    return pl.pallas_call(
        functools.partial(_moe_sorted_kernel, n_tiles=n_tiles),
        grid_spec=grid_spec,
        out_shape=jax.ShapeDtypeStruct((n_slots, d), F32),
        compiler_params=_cparams(1),
        name="moe_sorted",
    )(gather_row, scatter_row, tile_grp, x, wr, rb, wgu, wd)


def _add_ln_kernel(x_ref, y_ref, g_ref, b_ref, o_ref, *, alpha):
    o_ref[...] = _layer_norm(alpha * x_ref[...] + y_ref[...], g_ref[...], b_ref[...])


def _add_ln(x, y, g, b, alpha, tm):
    m, d = x.shape
    return pl.pallas_call(
        functools.partial(_add_ln_kernel, alpha=alpha),
        grid=(m // tm,),
        in_specs=[pl.BlockSpec((tm, d), lambda i: (i, 0)),
                  pl.BlockSpec((tm, d), lambda i: (i, 0)),
                  pl.BlockSpec((1, d), lambda i: (0, 0)),
                  pl.BlockSpec((1, d), lambda i: (0, 0))],
        out_specs=pl.BlockSpec((tm, d), lambda i: (i, 0)),
        out_shape=jax.ShapeDtypeStruct((m, d), F32),
        compiler_params=_cparams(1),
        name="add_ln",
    )(x, y, g, b)


CMP_MBLK = 128
ROW_TILES = D_KV // LANES
PAIR_HID = 2 * CMP_HID


def _compress_weights(pe, w1, b1, w2):
    w1 = w1.reshape(2, CMP_LEN, HEAD_DIM, CMP_HID)
    z1 = jnp.zeros_like(w1)
    w1p = jnp.concatenate([jnp.concatenate([w1, z1], axis=3), jnp.concatenate([z1, w1], axis=3)], axis=2)
    z2 = jnp.zeros_like(w2)
    w2p = jnp.concatenate([jnp.concatenate([w2, z2], axis=2), jnp.concatenate([z2, w2], axis=2)], axis=1)
    w1p = w1p.astype(BF16).reshape(2, 2, CMP_STRIDE * LANES, PAIR_HID)
    return (jnp.concatenate([pe, pe], axis=2), w1p,
            jnp.concatenate([b1, b1], axis=1)[:, None], w2p.astype(BF16))


def _load_interleaved(x_ref):
    step = CMP_STRIDE * ROW_TILES

    def load(tile, r, mb):
        row0 = pl.multiple_of(mb * (CMP_MBLK * step), CMP_MBLK * step)
        return x_ref[pl.ds(row0 + r * ROW_TILES + tile, CMP_MBLK, stride=step), :]
    return load


def _load_tiled(x_ref):
    def load(tile, r, mb):
        row0 = pl.multiple_of(mb * (CMP_MBLK * CMP_STRIDE), CMP_MBLK * CMP_STRIDE)
        return x_ref[tile, pl.ds(row0 + r, CMP_MBLK, stride=CMP_STRIDE), :]
    return load


def _compress_rows(load, n_chunk, pe_ref, w1_ref, b1_ref, w2_ref, out_ref, a_ref, b_ref):
    n_blk = n_chunk // CMP_MBLK
    for kv in range(2):
        for pair in range(N_KV_GROUPS // 2):
            tile = kv * (N_KV_GROUPS // 2) + pair

            def fill(mb, carry, kv=kv, tile=tile):
                top, bot = [], []
                for r in range(CMP_STRIDE):
                    xg = load(tile, r, mb)
                    top.append((xg + pe_ref[kv, r:r + 1, :]).astype(BF16))
                    bot.append((xg + pe_ref[kv, CMP_STRIDE + r:CMP_STRIDE + r + 1, :]).astype(BF16))
                m0 = pl.multiple_of(mb * CMP_MBLK, CMP_MBLK)
                a_ref[pl.ds(m0, CMP_MBLK), :] = _dot(jnp.concatenate(top, axis=1), w1_ref[kv, 0])
                b_ref[pl.ds(m0, CMP_MBLK), :] = _dot(jnp.concatenate(bot, axis=1), w1_ref[kv, 1])
                return carry

            lax.fori_loop(0, n_blk, fill, 0)
            b_ref[n_chunk:n_chunk + SUBLANES, :] = jnp.zeros((SUBLANES, PAIR_HID), F32)
            for mb in range(n_blk):
                m0 = mb * CMP_MBLK
                hid = jax.nn.gelu(a_ref[m0:m0 + CMP_MBLK, :] + b_ref[m0 + 1:m0 + 1 + CMP_MBLK, :] + b1_ref[kv])
                out_ref[m0:m0 + CMP_MBLK, tile * LANES:(tile + 1) * LANES] = _dot(hid.astype(BF16), w2_ref[kv])


def _compress_prompt_kernel(x_ref, pe_ref, w1_ref, b1_ref, w2_ref, o_ref, a_ref, b_ref, *, n_chunk):
    _compress_rows(_load_interleaved(x_ref.at[0]), n_chunk, pe_ref, w1_ref, b1_ref, w2_ref, o_ref.at[0],
                   a_ref, b_ref)


def _compress_prompt(rows, pe, w1, b1, w2):
    b, t_len, _ = rows.shape
    n_chunk = t_len // CMP_STRIDE
    assert n_chunk % CMP_MBLK == 0
    full = lambda a: pl.BlockSpec(a.shape, lambda i: (0,) * a.ndim)
    return pl.pallas_call(
        functools.partial(_compress_prompt_kernel, n_chunk=n_chunk),
        grid=(b,),
        in_specs=[pl.BlockSpec((1, t_len * ROW_TILES, LANES), lambda i: (i, 0, 0)),
                  full(pe), full(w1), full(b1), full(w2)],
        out_specs=pl.BlockSpec((1, n_chunk, D_KV), lambda i: (i, 0, 0)),
        out_shape=jax.ShapeDtypeStruct((b, n_chunk, D_KV), F32),
        scratch_shapes=[pltpu.VMEM((n_chunk, PAIR_HID), F32), pltpu.VMEM((n_chunk + SUBLANES, PAIR_HID), F32)],
        compiler_params=_cparams(1),
        name="compress_prompt",
    )(rows.reshape(b, t_len * ROW_TILES, LANES), pe, w1, b1, w2)


def _masked_softmax_rows(s, mask):
    sm = jnp.where(mask, s, NEG)
    m = jnp.max(sm, axis=-1, keepdims=True)
    e = jnp.where(mask, jnp.exp(sm - m), 0.0)
    tot = jnp.sum(e, axis=-1, keepdims=True)
    return e * jnp.where(tot > 0.0, 1.0 / tot, 0.0)


WIN_TILES = WINDOW // Q_BLOCK


KEY_TILE = 512
KEY_SUB = KEY_TILE // Q_BLOCK


def _attend_tiles_t(qts, k_ref, vt_ref, t_lo, t_hi, i_blk, bias_ref, bias_index, mask_fn):
    cols = qts[0].shape[1]

    def body(t, carry):
        k0 = pl.multiple_of(t * KEY_TILE, KEY_TILE)
        bases = [bias_index(i_blk - (t * KEY_SUB + u)) for u in range(KEY_SUB)]
        scores = [_dot(k_ref[0, g, pl.ds(k0, KEY_TILE), :], qt) for g, qt in enumerate(qts)]
        new = []
        for g in range(len(qts)):
            m, l, acc = carry[g]
            vt = vt_ref[0, g * HEAD_DIM:(g + 1) * HEAD_DIM, pl.ds(k0, KEY_TILE)]
            heads = [g * HEADS_PER_GROUP + hp for hp in range(HEADS_PER_GROUP)]
            bias = jnp.concatenate(
                [jnp.concatenate([bias_ref[base + h] for h in heads], axis=1) for base in bases], axis=0)
            s = scores[g] + bias
            extra = mask_fn(t, g)
            if extra is not None:
                s = s + jnp.concatenate([extra] * HEADS_PER_GROUP, axis=1)
            m_new = jnp.maximum(m, jnp.max(s, axis=0, keepdims=True))
            alpha = jnp.exp(m - m_new)
            p = jnp.exp(s - m_new)
            new.append((m_new, alpha * l + jnp.sum(p, axis=0, keepdims=True), alpha * acc + _dot(vt, p.astype(BF16))))
        return tuple(new)

    init = tuple((jnp.full((1, cols), NEG, F32), jnp.zeros((1, cols), F32), jnp.zeros((HEAD_DIM, cols), F32))
                 for _ in qts)
    fin = lax.fori_loop(t_lo, t_hi, body, init)
    return [acc / l for (_, l, acc) in fin]


def _nsa_prompt_kernel(tab_ref, q_ref, gl_ref, ck_ref, cvt_ref, sk_ref, svt_ref, wk_ref, wvt_ref, o_ref,
                       bias_ref, cbias_ref, sel_ref, *, n_tiles):
    b = pl.program_id(0)
    i = pl.program_id(1)
    kk = lax.broadcasted_iota(I32, (Q_BLOCK, Q_BLOCK), 0)
    qq = lax.broadcasted_iota(I32, (Q_BLOCK, Q_BLOCK), 1)
    n_cmp = ck_ref.shape[2]
    n_sel_blocks = 2 * n_tiles
    edge_base = n_tiles * N_HEADS
    none_base = (n_tiles + 1) * N_HEADS

    @pl.when((b == 0) & (i == 0))
    def _():
        def build(delta, carry):
            dist = delta * Q_BLOCK + qq - kk
            biases = _bias_from_bucket(_rel_bucket(dist), tab_ref, range(N_HEADS))
            for h in range(N_HEADS):
                bias_ref[delta * N_HEADS + h] = jnp.where(dist >= 0, biases[h], NEG)
            return carry
        lax.fori_loop(0, n_tiles, build, 0)
        dist = WIN_TILES * Q_BLOCK + qq - kk
        biases = _bias_from_bucket(_rel_bucket(dist), tab_ref, range(N_HEADS))
        for h in range(N_HEADS):
            bias_ref[edge_base + h] = jnp.where(dist < WINDOW, biases[h], NEG)
            bias_ref[none_base + h] = jnp.full((Q_BLOCK, Q_BLOCK), NEG, F32)

        def build_c(t, carry):
            dist_c = t * Q_BLOCK + qq - (kk * CMP_STRIDE + (CMP_LEN - 1))
            ok = (dist_c >= 0) & (kk < n_cmp - 1)
            biases_c = _bias_from_bucket(_rel_bucket(dist_c), tab_ref, range(N_HEADS))
            for h in range(N_HEADS):
                cbias_ref[t * N_HEADS + h] = jnp.where(ok, biases_c[h], NEG)
            return carry
        lax.fori_loop(0, n_tiles, build_c, 0)

    qs = i * Q_BLOCK
    q_all = q_ref[0] * SCALE
    gates = jax.nn.sigmoid(gl_ref[0])

    blk = lax.broadcasted_iota(I32, (n_sel_blocks, Q_BLOCK), 0)
    col = lax.broadcasted_iota(I32, (n_sel_blocks, Q_BLOCK), 1)
    c_start = col * CMP_STRIDE
    s_start = blk * SEL_LEN
    cover_t = jnp.where((c_start < s_start + SEL_LEN) & (c_start + CMP_LEN > s_start) & (col < n_cmp - 1),
                        1.0, 0.0).astype(BF16)
    q_pos = qs + col
    cur = q_pos // SEL_LEN
    forced = (blk == 0) | (blk == cur) | (blk == cur - 1)
    valid = blk * SEL_LEN <= q_pos
    upper = kk < SEL_LEN

    qts, o_cs, splits = [], [], []
    for g in range(N_KV_GROUPS):
        heads = [g * HEADS_PER_GROUP + hp for hp in range(HEADS_PER_GROUP)]
        qt = jnp.concatenate([q_all[h * HEAD_DIM:(h + 1) * HEAD_DIM, :] for h in heads], axis=1)
        qts.append(qt)

        s_c = _dot(ck_ref[0, g], qt) + jnp.concatenate([cbias_ref[i * N_HEADS + h] for h in heads], axis=1)
        m_c = jnp.max(s_c, axis=0, keepdims=True)
        e_c = jnp.exp(s_c - m_c)
        tot = jnp.sum(e_c, axis=0, keepdims=True)
        p_c = e_c * jnp.where(m_c > 0.5 * NEG, 1.0 / tot, 0.0)
        o_cs.append(_dot(cvt_ref[0, g * HEAD_DIM:(g + 1) * HEAD_DIM, :], p_c.astype(BF16)))
        p_sum = p_c[:, 0:Q_BLOCK]
        for hp in range(1, HEADS_PER_GROUP):
            p_sum = p_sum + p_c[:, hp * Q_BLOCK:(hp + 1) * Q_BLOCK]
        splits.append(p_sum.astype(BF16))

    imp_all = _dot(cover_t, jnp.concatenate(splits, axis=1))
    for g in range(N_KV_GROUPS):
        imp = imp_all[:, g * Q_BLOCK:(g + 1) * Q_BLOCK]
        score = jnp.where(forced, FORCE, imp)
        score = jnp.where(valid, score, NEG)
        cnt = jnp.zeros(score.shape, F32)
        for s in range(n_sel_blocks):
            row = score[s:s + 1, :]
            beats = (row > score) | ((row == score) & (blk > s))
            cnt = cnt + jnp.where(beats, 1.0, 0.0)
        sel = (cnt < float(N_SEL)) & (score > 0.5 * NEG)
        sel_ref[g] = jnp.where(sel, 0.0, NEG)

    def mask_sel(t, g):
        rows = sel_ref[g, pl.ds(pl.multiple_of(2 * KEY_SUB * t, 2 * KEY_SUB), 2 * KEY_SUB), :]
        return jnp.concatenate([jnp.where(upper, rows[2 * u:2 * u + 1], rows[2 * u + 1:2 * u + 2])
                                for u in range(KEY_SUB)], axis=0)
    last = i // KEY_SUB + 1
    o_ss = _attend_tiles_t(qts, sk_ref, svt_ref, 0, last, i, bias_ref,
                           lambda d: jnp.where(d < 0, none_base, d * N_HEADS), mask_sel)

    def win_index(d):
        return jnp.where((d < 0) | (d > WIN_TILES), none_base, jnp.where(d == WIN_TILES, edge_base, d * N_HEADS))
    o_ws = _attend_tiles_t(qts, wk_ref, wvt_ref, jnp.maximum(i - WIN_TILES, 0) // KEY_SUB, last, i, bias_ref,
                           win_index, lambda t, g: None)

    for h in range(N_HEADS):
        g, hp = divmod(h, HEADS_PER_GROUP)
        c0 = hp * Q_BLOCK
        o_h = (gates[3 * h:3 * h + 1, :] * o_cs[g][:, c0:c0 + Q_BLOCK]
               + gates[3 * h + 1:3 * h + 2, :] * o_ss[g][:, c0:c0 + Q_BLOCK]
               + gates[3 * h + 2:3 * h + 3, :] * o_ws[g][:, c0:c0 + Q_BLOCK])
        o_ref[0, h * HEAD_DIM:(h + 1) * HEAD_DIM, :] = o_h.astype(o_ref.dtype)


def _nsa_prompt(tab, q_t, gl_t, comp_k, comp_vt, slc_k, slc_vt, win_k, win_vt):
    b, dq, t_len = q_t.shape
    n_tiles = t_len // Q_BLOCK
    n_cmp = comp_k.shape[2]
    assert n_cmp == Q_BLOCK and t_len % KEY_TILE == 0 and n_tiles > WIN_TILES
    dv = N_KV_GROUPS * HEAD_DIM
    return pl.pallas_call(
        functools.partial(_nsa_prompt_kernel, n_tiles=n_tiles),
        grid=(b, n_tiles),
        in_specs=[pl.BlockSpec(memory_space=pltpu.SMEM),
                  pl.BlockSpec((1, dq, Q_BLOCK), lambda i, j: (i, 0, j)),
                  pl.BlockSpec((1, LANES, Q_BLOCK), lambda i, j: (i, 0, j)),
                  pl.BlockSpec((1, N_KV_GROUPS, n_cmp, HEAD_DIM), lambda i, j: (i, 0, 0, 0)),
                  pl.BlockSpec((1, dv, n_cmp), lambda i, j: (i, 0, 0)),
                  pl.BlockSpec((1, N_KV_GROUPS, t_len, HEAD_DIM), lambda i, j: (i, 0, 0, 0)),
                  pl.BlockSpec((1, dv, t_len), lambda i, j: (i, 0, 0)),
                  pl.BlockSpec((1, N_KV_GROUPS, t_len, HEAD_DIM), lambda i, j: (i, 0, 0, 0)),
                  pl.BlockSpec((1, dv, t_len), lambda i, j: (i, 0, 0))],
        out_specs=pl.BlockSpec((1, dq, Q_BLOCK), lambda i, j: (i, 0, j)),
        out_shape=jax.ShapeDtypeStruct((b, dq, t_len), BF16),
        scratch_shapes=[pltpu.VMEM(((n_tiles + 2) * N_HEADS, Q_BLOCK, Q_BLOCK), F32),
                        pltpu.VMEM((n_tiles * N_HEADS, Q_BLOCK, Q_BLOCK), F32),
                        pltpu.VMEM((N_KV_GROUPS, 2 * n_tiles, Q_BLOCK), F32)],
        compiler_params=_cparams(2),
        name="nsa_prompt",
    )(tab, q_t, gl_t, comp_k, comp_vt, slc_k, slc_vt, win_k, win_vt)


def _page_copy(pt_ref, pool_ref, buf_ref, sem, b, j, page):
    start = pl.multiple_of(j * page, page)
    if len(buf_ref.shape) == 2:
        dst = buf_ref.at[pl.ds(start, page)]
    else:
        dst = buf_ref.at[:, :, pl.ds(start, page)]
    return pltpu.make_async_copy(pool_ref.at[pt_ref[b, j]], dst, sem)


def _gather_pages_start(pt_ref, pool_ref, buf_ref, sem, b, n_pages, page):
    def go(j, carry):
        _page_copy(pt_ref, pool_ref, buf_ref, sem, b, j, page).start()
        return carry
    lax.fori_loop(0, n_pages, go, 0)


def _gather_pages_wait(pt_ref, pool_ref, buf_ref, sem, b, n_pages, page):
    def go(j, carry):
        _page_copy(pt_ref, pool_ref, buf_ref, sem, b, j, page).wait()
        return carry
    lax.fori_loop(0, n_pages, go, 0)


def _compress_pages_kernel(pt_ref, pool_ref, pe_ref, w1_ref, b1_ref, w2_ref, o_ref, raw_ref, x_ref, a_ref, b_ref,
                           sem, *, n_seq, n_pages, page, n_chunk):
    b = pl.program_id(0)

    @pl.when(b == 0)
    def _():
        _gather_pages_start(pt_ref, pool_ref, raw_ref, sem, b, n_pages, page)

    _gather_pages_wait(pt_ref, pool_ref, raw_ref, sem, b, n_pages, page)

    def relayout(j, carry):
        r0 = pl.multiple_of(j * page, page)
        for tile in range(ROW_TILES):
            kv, pair = divmod(tile, N_KV_GROUPS // 2)
            x_ref[tile, pl.ds(r0, page), :] = raw_ref[kv, pair * LANES:(pair + 1) * LANES, pl.ds(r0, page)].T
        return carry
    lax.fori_loop(0, n_pages, relayout, 0)

    @pl.when(b + 1 < n_seq)
    def _():
        _gather_pages_start(pt_ref, pool_ref, raw_ref, sem, b + 1, n_pages, page)

    _compress_rows(_load_tiled(x_ref), n_chunk, pe_ref, w1_ref, b1_ref, w2_ref, o_ref.at[0], a_ref, b_ref)


def _compress_pages(page_table, pool_t, pe, w1, b1, w2):
    b, n_pages = page_table.shape
    page = pool_t.shape[3]
    past = n_pages * page
    n_chunk = past // CMP_STRIDE
    assert n_chunk % CMP_MBLK == 0 and page == LANES
    full = lambda a: pl.BlockSpec(a.shape, lambda i: (0,) * a.ndim)
    return pl.pallas_call(
        functools.partial(_compress_pages_kernel, n_seq=b, n_pages=n_pages, page=page, n_chunk=n_chunk),
        grid=(b,),
        in_specs=[pl.BlockSpec(memory_space=pltpu.SMEM), pl.BlockSpec(memory_space=pl.ANY),
                  full(pe), full(w1), full(b1), full(w2)],
        out_specs=pl.BlockSpec((1, n_chunk, D_KV), lambda i: (i, 0, 0)),
        out_shape=jax.ShapeDtypeStruct((b, n_chunk, D_KV), F32),
        scratch_shapes=[pltpu.VMEM((2, D_K, past), F32),
                        pltpu.VMEM((ROW_TILES, past, LANES), F32),
                        pltpu.VMEM((n_chunk, PAIR_HID), F32), pltpu.VMEM((n_chunk + SUBLANES, PAIR_HID), F32),
                        pltpu.SemaphoreType.DMA(())],
        compiler_params=_cparams(1),
        name="compress_pages",
    )(page_table, pool_t, pe, w1, b1, w2)


DEC_TILE = 512
DEC_ROWS = 16
D_K = D_KV // 2


def _row_bias(dist_row, tab_ref, bias_ref, col0):
    length = dist_row.shape[1]
    biases = _bias_from_bucket(_rel_bucket(dist_row), tab_ref, range(N_HEADS))
    rows = biases + [jnp.zeros((DEC_ROWS - N_HEADS, length), F32)]
    bias_ref[:, pl.ds(col0, length)] = jnp.concatenate(rows, axis=0)


def _head_block(o_all, h):
    g = h // HEADS_PER_GROUP
    return o_all[h:h + 1, g * HEAD_DIM:(g + 1) * HEAD_DIM]


def _new_token_scores(q_blk, new_row, bias0):
    k_new = new_row[:, :D_K].astype(BF16).astype(F32)
    v_new = new_row[:, D_K:].astype(BF16).astype(F32)
    return jnp.sum(q_blk.astype(F32) * k_new, axis=-1, keepdims=True) + bias0, v_new


def _nsa_step_kernel(tab_ref, pt_ref, q_ref, gl_ref, comp_ref, pool_ref, slc_new_ref, win_ref, win_new_ref, o_ref,
                     kv_buf, bias_c_ref, bias_s_ref, bias_w_ref, sems,
                     *, n_seq, n_pages, page, past, n_cmp, w_buf):
    b = pl.program_id(0)
    slot = b % 2
    n_tiles = past // DEC_TILE
    n_blocks = past // SEL_LEN + 1
    blk_lanes = -(-n_blocks // LANES) * LANES

    @pl.when(b == 0)
    def _():
        _gather_pages_start(pt_ref, pool_ref, kv_buf.at[0], sems.at[0], b, n_pages, page)

    @pl.when(b + 1 < n_seq)
    def _():
        _gather_pages_start(pt_ref, pool_ref, kv_buf.at[1 - slot], sems.at[1 - slot], b + 1, n_pages, page)

    lane_t = lax.broadcasted_iota(I32, (1, DEC_TILE), 1)

    @pl.when(b == 0)
    def _():
        lane_c = lax.broadcasted_iota(I32, (1, n_cmp), 1)
        _row_bias(past - (lane_c * CMP_STRIDE + (CMP_LEN - 1)), tab_ref, bias_c_ref, 0)

        def tile_bias(t, carry):
            k0 = pl.multiple_of(t * DEC_TILE, DEC_TILE)
            _row_bias(past - (k0 + lane_t), tab_ref, bias_s_ref, k0)
            return carry
        lax.fori_loop(0, n_tiles, tile_bias, 0)
        lane_w = lax.broadcasted_iota(I32, (1, w_buf), 1)
        _row_bias(w_buf - lane_w, tab_ref, bias_w_ref, 0)

    bias0 = jnp.concatenate([jnp.full((1, 1), tab_ref[0, h], F32) for h in range(N_HEADS)]
                            + [jnp.zeros((DEC_ROWS - N_HEADS, 1), F32)], axis=0)

    q_row = q_ref[0] * SCALE
    gates = jax.nn.sigmoid(gl_ref[0])
    rows = []
    for h in range(N_HEADS):
        g = h // HEADS_PER_GROUP
        parts = []
        if g > 0:
            parts.append(jnp.zeros((1, g * HEAD_DIM), BF16))
        parts.append(q_row[:, h * HEAD_DIM:(h + 1) * HEAD_DIM])
        if g < N_KV_GROUPS - 1:
            parts.append(jnp.zeros((1, (N_KV_GROUPS - 1 - g) * HEAD_DIM), BF16))
        rows.append(jnp.concatenate(parts, axis=1))
    q_blk = jnp.concatenate(rows + [jnp.zeros((DEC_ROWS - N_HEADS, D_K), BF16)], axis=0)

    comp = comp_ref[0].astype(BF16)
    mask_c = lax.broadcasted_iota(I32, (DEC_ROWS, n_cmp), 1) < n_cmp - 1
    p_c = _masked_softmax_rows(_dot_nt(q_blk, comp[:, :D_K]) + bias_c_ref[...], mask_c)
    o_c = _dot(p_c.astype(BF16), comp[:, D_K:])
    p_sums = [jnp.sum(p_c[g * HEADS_PER_GROUP:(g + 1) * HEADS_PER_GROUP], axis=0, keepdims=True)
              for g in range(N_KV_GROUPS)]
    p_sum = jnp.concatenate(p_sums + [jnp.zeros((DEC_ROWS - N_KV_GROUPS, n_cmp), F32)], axis=0)

    c_start = lax.broadcasted_iota(I32, (n_cmp, blk_lanes), 0) * CMP_STRIDE
    s_start = lax.broadcasted_iota(I32, (n_cmp, blk_lanes), 1) * SEL_LEN
    c_idx = lax.broadcasted_iota(I32, (n_cmp, blk_lanes), 0)
    cover = jnp.where((c_start < s_start + SEL_LEN) & (c_start + CMP_LEN > s_start) & (c_idx < n_cmp - 1),
                      1.0, 0.0).astype(BF16)
    imp = _dot(p_sum.astype(BF16), cover)
    blk = lax.broadcasted_iota(I32, (DEC_ROWS, blk_lanes), 1)
    cur = past // SEL_LEN
    forced = (blk == 0) | (blk == cur) | (blk == cur - 1)
    score = jnp.where(forced, FORCE, imp)
    score = jnp.where(blk < n_blocks, score, NEG)
    score_t = score.T
    j_idx = lax.broadcasted_iota(I32, (blk_lanes, blk_lanes), 0)
    i_idx = lax.broadcasted_iota(I32, (blk_lanes, blk_lanes), 1)
    sel_groups = []
    for g in range(N_KV_GROUPS):
        col = score_t[:, g:g + 1]
        row = score[g:g + 1, :]
        beats = (col > row) | ((col == row) & (j_idx < i_idx))
        rank = jnp.sum(jnp.where(beats, 1.0, 0.0), axis=0, keepdims=True)
        sel_groups.append(jnp.where((rank < float(N_SEL)) & (row > 0.5 * NEG), 1.0, 0.0))
    sel_rows = [sel_groups[h // HEADS_PER_GROUP] for h in range(N_HEADS)]
    sel_bf = jnp.concatenate(sel_rows + [jnp.zeros((DEC_ROWS - N_HEADS, blk_lanes), F32)], axis=0).astype(BF16)

    _gather_pages_wait(pt_ref, pool_ref, kv_buf.at[slot], sems.at[slot], b, n_pages, page)
    e_row = lax.broadcasted_iota(I32, (blk_lanes, DEC_TILE), 0)
    e_col = lax.broadcasted_iota(I32, (blk_lanes, DEC_TILE), 1) // SEL_LEN

    def sel_tile(t, carry):
        m, l, acc = carry
        k0 = pl.multiple_of(t * DEC_TILE, DEC_TILE)
        expand = jnp.where(e_row == e_col + t * (DEC_TILE // SEL_LEN), 1.0, 0.0).astype(BF16)
        key_sel = _dot(sel_bf, expand)
        kt = kv_buf[slot, 0, :, pl.ds(k0, DEC_TILE)].astype(BF16)
        vt = kv_buf[slot, 1, :, pl.ds(k0, DEC_TILE)].astype(BF16)
        s = _dot(q_blk, kt) + bias_s_ref[:, pl.ds(k0, DEC_TILE)] + jnp.where(key_sel > 0.5, 0.0, NEG)
        m_new = jnp.maximum(m, jnp.max(s, axis=-1, keepdims=True))
        alpha = jnp.exp(m - m_new)
        p = jnp.exp(s - m_new)
        return m_new, alpha * l + jnp.sum(p, axis=-1, keepdims=True), alpha * acc + _dot_nt(p.astype(BF16), vt)

    init = (jnp.full((DEC_ROWS, 1), NEG, F32), jnp.zeros((DEC_ROWS, 1), F32), jnp.zeros((DEC_ROWS, D_K), F32))
    m_s, l_s, acc_s = lax.fori_loop(0, n_tiles, sel_tile, init)
    s_new, v_new = _new_token_scores(q_blk, slc_new_ref[0], bias0)
    cur_sel = sel_bf[:, cur:cur + 1].astype(F32)
    s_new = s_new + jnp.where(cur_sel > 0.5, 0.0, NEG)
    m_fin = jnp.maximum(m_s, s_new)
    alpha = jnp.exp(m_s - m_fin)
    p_new = jnp.exp(s_new - m_fin)
    o_s = (alpha * acc_s + p_new.astype(BF16).astype(F32) * v_new) / (alpha * l_s + p_new)

    kw = win_ref[0, 0].astype(BF16)
    vw = win_ref[0, 1].astype(BF16)
    idx_w = lax.broadcasted_iota(I32, (DEC_ROWS, w_buf), 1)
    s_w = jnp.where(w_buf - idx_w < WINDOW, _dot(q_blk, kw) + bias_w_ref[...], NEG)
    s_wn, v_wn = _new_token_scores(q_blk, win_new_ref[0], bias0)
    m_w = jnp.maximum(jnp.max(s_w, axis=-1, keepdims=True), s_wn)
    e_w = jnp.exp(s_w - m_w)
    e_wn = jnp.exp(s_wn - m_w)
    o_w = ((_dot_nt(e_w.astype(BF16), vw) + e_wn.astype(BF16).astype(F32) * v_wn)
           / (jnp.sum(e_w, axis=-1, keepdims=True) + e_wn))

    outs = []
    for h in range(N_HEADS):
        outs.append(gates[:, 3 * h:3 * h + 1] * _head_block(o_c, h)
                    + gates[:, 3 * h + 1:3 * h + 2] * _head_block(o_s, h)
                    + gates[:, 3 * h + 2:3 * h + 3] * _head_block(o_w, h))
    o_ref[0] = jnp.concatenate(outs, axis=1).astype(o_ref.dtype)


def _feature_major(cache):
    nd = cache.ndim
    perm = tuple(range(nd - 4)) + (nd - 3, nd - 2, nd - 1, nd - 4)
    t = cache.transpose(perm)
    return t.reshape(t.shape[:nd - 3] + (t.shape[nd - 3] * t.shape[nd - 2], t.shape[nd - 1]))


def _nsa_step(tab, page_table, q, gl, comp, pool_t, slc_new, win_t, win_new):
    b, n_pages = page_table.shape
    page = pool_t.shape[3]
    past = n_pages * page
    n_cmp = comp.shape[1]
    w_buf = win_t.shape[3]
    dq = q.shape[2]
    assert past % DEC_TILE == 0 and past % SEL_LEN == 0 and w_buf == WINDOW and n_cmp == past // CMP_STRIDE
    row3 = lambda w: pl.BlockSpec((1, 1, w), lambda i: (i, 0, 0))
    return pl.pallas_call(
        functools.partial(_nsa_step_kernel, n_seq=b, n_pages=n_pages, page=page, past=past, n_cmp=n_cmp,
                          w_buf=w_buf),
        grid=(b,),
        in_specs=[pl.BlockSpec(memory_space=pltpu.SMEM), pl.BlockSpec(memory_space=pltpu.SMEM),
                  row3(dq), row3(LANES),
                  pl.BlockSpec((1, n_cmp, D_KV), lambda i: (i, 0, 0)),
                  pl.BlockSpec(memory_space=pl.ANY),
                  row3(D_KV),
                  pl.BlockSpec((1, 2, D_K, w_buf), lambda i: (i, 0, 0, 0)),
                  row3(D_KV)],
        out_specs=row3(dq),
        out_shape=jax.ShapeDtypeStruct((b, 1, dq), BF16),
        scratch_shapes=[pltpu.VMEM((2, 2, D_K, past), F32),
                        pltpu.VMEM((DEC_ROWS, n_cmp), F32),
                        pltpu.VMEM((DEC_ROWS, past), F32),
                        pltpu.VMEM((DEC_ROWS, w_buf), F32),
                        pltpu.SemaphoreType.DMA((2,))],
        compiler_params=_cparams(1),
        name="nsa_step",
    )(tab, page_table, q, gl, comp, pool_t, slc_new, win_t, win_new)


def _block_diag_pairs(w):
    n, blk, _ = w.shape
    w = w.reshape(n // 2, 2, blk, blk)
    z = jnp.zeros((n // 2, blk, blk), w.dtype)
    top = jnp.concatenate([w[:, 0], z], axis=2)
    bot = jnp.concatenate([z, w[:, 1]], axis=2)
    return jnp.concatenate([top, bot], axis=1).astype(BF16)


def _prep_weights(w_in_a, conv_w, conv_b, w_rg, b_rg, w_ig, b_ig, lru_lambda, w_in_b, w_kv_shared, cmp_pe, cmp_w1,
                  cmp_b1, cmp_w2, w_mem_kv, w_out, ln1_g, ln1_b, ln2_g, ln2_b, w_router_g, b_router_g,
                  w_router_e, b_router_e, w_exp_gate, w_exp_up, w_exp_down):
    depth, d_model, _ = w_out.shape
    n_a = w_in_a.shape[0]
    nq = N_HEADS * HEAD_DIM
    n_gl = 3 * N_HEADS
    hid = EXPERTS_PER_GROUP * D_EXPERT
    p = {}
    p["w_in_a"] = w_in_a.astype(BF16)
    p["w_in_b"] = jnp.concatenate(
        [w_in_b[:, :, :nq], w_in_b[:, :, nq + n_gl:], w_in_b[:, :, nq:nq + n_gl],
         jnp.zeros((w_in_b.shape[0], d_model, LANES - n_gl), w_in_b.dtype)], axis=2).astype(BF16)
    p["w_kv"] = w_kv_shared.astype(BF16)
    p["w_mem"] = w_mem_kv.transpose(1, 0, 2).reshape(d_model, depth * 2 * D_MEM).astype(BF16)
    p["wo_a"] = w_out[:, :D_RNN].astype(BF16)
    p["wo_m"] = w_out[:, D_RNN:].astype(BF16)
    p["conv_w"] = conv_w
    p["conv_b"] = conv_b[:, None]
    p["w_rg"] = jnp.stack([_block_diag_pairs(w_rg[l]) for l in range(n_a)])
    p["w_ig"] = jnp.stack([_block_diag_pairs(w_ig[l]) for l in range(n_a)])
    p["b_rg"] = b_rg[:, None]
    p["b_ig"] = b_ig[:, None]
    p["lam"] = lru_lambda[:, None]
    p["pe"], p["cmp_w1"], p["cmp_b1"], p["cmp_w2"] = _compress_weights(cmp_pe, cmp_w1, cmp_b1, cmp_w2)
    p["ln1_g"], p["ln1_b"], p["ln2_g"], p["ln2_b"] = ln1_g[:, None], ln1_b[:, None], ln2_g[:, None], ln2_b[:, None]
    n_r = N_GROUPS + N_EXPERTS
    wr = jnp.concatenate([w_router_g, w_router_e, jnp.zeros((depth, d_model, LANES - n_r), F32)], axis=2)
    p["wr"] = wr.astype(BF16)
    p["rb"] = jnp.concatenate([b_router_g, b_router_e, jnp.zeros((depth, LANES - n_r), F32)], axis=1)[:, None]

    def by_group(w):
        w = w.astype(BF16).reshape(depth, N_GROUPS, EXPERTS_PER_GROUP, d_model, D_EXPERT)
        return w.transpose(0, 1, 3, 2, 4).reshape(depth, N_GROUPS, d_model, hid)
    p["w_gu"] = jnp.concatenate([by_group(w_exp_gate), by_group(w_exp_up)], axis=3)
    p["w_dn"] = w_exp_down.astype(BF16).reshape(depth, N_GROUPS, hid, d_model)
    return p


def _kv_layouts(rows):
    b, t_len, _ = rows.shape
    k = rows[:, :, :D_KV // 2].reshape(b, t_len, N_KV_GROUPS, HEAD_DIM).transpose(0, 2, 1, 3)
    return k, rows[:, :, D_KV // 2:].transpose(0, 2, 1)


def _trunk_tail(x, mix_a, mix_m, p, l, alpha, tm):
    x = _out_ln(x, mix_a, mix_m, p["wo_a"][l], p["wo_m"][l], p["ln1_g"][l], p["ln1_b"][l], alpha, tm)
    if x.shape[0] < 8 * MOE_TILE:
        return _moe_ln(x, p["wr"][l], p["rb"][l], p["w_gu"][l], p["w_dn"][l],
                       p["ln2_g"][l], p["ln2_b"][l], alpha, tm)
    grp = _group_ids(x, p["wr"][l], p["rb"][l], tm)[:, 0]
    gather_row, scatter_row, tile_grp = _moe_dispatch(grp)
    y = _moe_sorted(x, gather_row, scatter_row, tile_grp, p["wr"][l], p["rb"][l], p["w_gu"][l], p["w_dn"][l])
    return _add_ln(x, y, p["ln2_g"][l], p["ln2_b"][l], alpha, tm)


def kernel(x_prompt, x_sample, mem_prompt, cache_cmp_kv, cache_slc_kv, cache_win_kv, cache_mem_kv, state_lru_h, state_conv, page_table, w_in_a, conv_w, conv_b, w_rg, b_rg, w_ig, b_ig, lru_lambda, w_in_b, w_kv_shared, cmp_pe, cmp_w1, cmp_b1, cmp_w2, rel_bias, w_mem_kv, w_out, ln1_g, ln1_b, ln2_g, ln2_b, w_router_g, b_router_g, w_router_e, b_router_e, w_exp_gate, w_exp_up, w_exp_down):
    p = _prep_weights(w_in_a, conv_w, conv_b, w_rg, b_rg, w_ig, b_ig, lru_lambda, w_in_b, w_kv_shared, cmp_pe,
                      cmp_w1, cmp_b1, cmp_w2, w_mem_kv, w_out, ln1_g, ln1_b, ln2_g, ln2_b, w_router_g,
                      b_router_g, w_router_e, b_router_e, w_exp_gate, w_exp_up, w_exp_down)
    depth, d_model, _ = w_out.shape
    n_a = w_in_a.shape[0]
    alpha = (2 * depth) ** 0.25
    kv_shape = (2, N_KV_GROUPS, HEAD_DIM)
    a_outs = [(0, D_RNN, F32), (D_RNN, D_RNN, F32), (2 * D_RNN, D_MEM, BF16)]
    b_outs = [(0, D_RNN, BF16), (D_RNN, D_MEM, BF16), (D_RNN + D_MEM, LANES, F32)]
    kv_outs = [(0, D_KV, F32), (D_KV, D_KV, F32), (2 * D_KV, D_KV, F32)]

    bp, t_len, _ = x_prompt.shape
    n_mem = mem_prompt.shape[1]
    tm = 512
    mem_rows = _proj(mem_prompt.reshape(bp * n_mem, d_model), p["w_mem"],
                     [(l * 2 * D_MEM, 2 * D_MEM, F32) for l in range(depth)], tm, "mem_kv_proj")
    p_mem = jnp.stack(mem_rows).reshape(depth, bp, n_mem, 2 * D_MEM)
    x = x_prompt.reshape(bp * t_len, d_model)
    lru_p, conv_p = [], []
    for l in range(depth):
        if l < n_a:
            gate, xr, mq = _proj(x, p["w_in_a"][l], a_outs, tm, "in_proj_a")
            mix_a, h_t, buf = _rglru_prompt(gate.reshape(bp, t_len, D_RNN), xr.reshape(bp, t_len, D_RNN),
                                            p["conv_w"][l], p["conv_b"][l], p["w_rg"][l], p["w_ig"][l],
                                            p["b_rg"][l], p["b_ig"][l], p["lam"][l])
            lru_p.append(h_t[:, 0])
            conv_p.append(buf)
        else:
            if l == n_a:
                cmp_rows, slc_rows, win_rows, slc_bf, win_bf = _proj(
                    x, p["w_kv"], kv_outs + [(D_KV, D_KV, BF16), (2 * D_KV, D_KV, BF16)], tm, "kv_proj")
                comp = _compress_prompt(cmp_rows.reshape(bp, t_len, D_KV), p["pe"], p["cmp_w1"], p["cmp_b1"],
                                        p["cmp_w2"])
                comp_k, comp_vt = _kv_layouts(comp.astype(BF16))
                slc_k, slc_vt = _kv_layouts(slc_bf.reshape(bp, t_len, D_KV))
                win_k, win_vt = _kv_layouts(win_bf.reshape(bp, t_len, D_KV))
            q, mq, gl = _proj(x, p["w_in_b"][l - n_a], b_outs, tm, "in_proj_b")
            mix_t = _nsa_prompt(rel_bias, q.reshape(bp, t_len, D_RNN).transpose(0, 2, 1),
                                gl.reshape(bp, t_len, LANES).transpose(0, 2, 1),
                                comp_k, comp_vt, slc_k, slc_vt, win_k, win_vt)
            mix_a = mix_t.transpose(0, 2, 1)
        mix_m = _mem_attn(mq.reshape(bp, t_len, D_MEM), p_mem, l)
        x = _trunk_tail(x, mix_a.reshape(bp * t_len, D_RNN), mix_m.reshape(bp * t_len, D_MEM), p, l, alpha, tm)
    y_prompt = x.reshape(bp, t_len, d_model)
    p_cmp_kv = cmp_rows.reshape((bp, t_len) + kv_shape)
    p_slc_kv = slc_rows.reshape((bp, t_len) + kv_shape)
    w_keep = min(WINDOW, t_len)
    p_win_kv = win_rows.reshape((bp, t_len) + kv_shape)[:, t_len - w_keep:]
    p_mem_kv = p_mem.reshape(depth, bp, n_mem, 2, N_MEM_HEADS, HEAD_DIM)
    p_lru_h = jnp.stack(lru_p)
    p_conv = jnp.stack(conv_p)

    bd, s_len, _ = x_sample.shape
    assert s_len == 1
    n_pool, page = cache_cmp_kv.shape[:2]
    w_buf = cache_win_kv.shape[1]
    mem_t = _feature_major(cache_mem_kv)
    slc_pool_t = _feature_major(cache_slc_kv)
    win_t = _feature_major(cache_win_kv)
    x = x_sample.reshape(bd, d_model)
    lru_s, conv_s = [], []
    for l in range(depth):
        if l < n_a:
            gate, xr, mq = _proj(x, p["w_in_a"][l], a_outs, bd, "in_proj_a_step")
            mix_a, h_t, buf = _rglru_step(gate, xr, state_conv[l].transpose(1, 0, 2), state_lru_h[l],
                                          p["conv_w"][l], p["conv_b"][l], p["w_rg"][l], p["w_ig"][l],
                                          p["b_rg"][l], p["b_ig"][l], p["lam"][l])
            lru_s.append(h_t)
            conv_s.append(buf.transpose(1, 0, 2))
        else:
            if l == n_a:
                s_cmp, s_slc, s_win = _proj(x, p["w_kv"], kv_outs, bd, "kv_proj_step")
                comp_s = _compress_pages(page_table, _feature_major(cache_cmp_kv), p["pe"],
                                         p["cmp_w1"], p["cmp_b1"], p["cmp_w2"])
            q, mq, gl = _proj(x, p["w_in_b"][l - n_a], b_outs, bd, "in_proj_b_step")
            mix_a = _nsa_step(rel_bias, page_table, q.reshape(bd, 1, D_RNN), gl.reshape(bd, 1, LANES), comp_s,
                              slc_pool_t, s_slc.reshape(bd, 1, D_KV), win_t, s_win.reshape(bd, 1, D_KV))
            mix_a = mix_a.reshape(bd, D_RNN)
        mix_m = _mem_attn_step(mq.reshape(bd, 1, D_MEM), mem_t, l).reshape(bd, D_MEM)
        x = _trunk_tail(x, mix_a, mix_m, p, l, alpha, bd)
    y_sample = x.reshape(bd, 1, d_model)
    s_cmp_kv = s_cmp.reshape((bd, 1) + kv_shape)
    s_slc_kv = s_slc.reshape((bd, 1) + kv_shape)
    s_win_kv = jnp.concatenate([cache_win_kv, s_win.reshape((bd, 1) + kv_shape)], axis=1)[:, 1:]
    s_lru_h = jnp.stack(lru_s)
    s_conv = jnp.stack(conv_s)
    return (y_prompt, y_sample, p_cmp_kv, p_slc_kv, p_win_kv, p_mem_kv, p_lru_h, p_conv,
            s_cmp_kv, s_slc_kv, s_win_kv, s_lru_h, s_conv)
```

```python
import functools
import math

import jax
import jax.numpy as jnp
from jax import lax
from jax.experimental import pallas as pl
from jax.experimental.pallas import tpu as pltpu

F32 = jnp.float32
BF16 = jnp.bfloat16
I32 = jnp.int32

HEAD_DIM = 64
N_MEM_HEADS = 4
D_MEM = N_MEM_HEADS * HEAD_DIM
N_HEADS = 12
D_RNN = N_HEADS * HEAD_DIM
N_KV_GROUPS = 4
HEADS_PER_GROUP = N_HEADS // N_KV_GROUPS
D_KV = 2 * N_KV_GROUPS * HEAD_DIM
CONV_W = 4
LRU_C = 8.0
CMP_LEN = 32
CMP_STRIDE = 16
CMP_HID = 2 * HEAD_DIM
SEL_LEN = 64
N_SEL = 16
WINDOW = 512
N_BUCKETS = 32
MAX_DISTANCE = 1024
N_GROUPS = 4
EXPERTS_PER_GROUP = 8
N_EXPERTS = N_GROUPS * EXPERTS_PER_GROUP
D_EXPERT = 128
Q_BLOCK = 128
LN_EPS = 1e-5
NEG = -1e30
FORCE = 1e9
SCALE = HEAD_DIM ** -0.5

LANES = 128
SUBLANES = 8
VMEM_LIMIT_BYTES = 56 * 1024 * 1024


def _cparams(n_axes):
    return pltpu.CompilerParams(dimension_semantics=("arbitrary",) * n_axes,
                                vmem_limit_bytes=VMEM_LIMIT_BYTES)


def _dot(a, b):
    return jnp.dot(a, b, preferred_element_type=F32)


def _dot_nt(a, b):
    return lax.dot_general(a, b, (((1,), (1,)), ((), ())), preferred_element_type=F32)


def _layer_norm(v, g, b):
    mu = jnp.mean(v, axis=-1, keepdims=True)
    d = v - mu
    var = jnp.mean(d * d, axis=-1, keepdims=True)
    return d * lax.rsqrt(var + LN_EPS) * g + b


def _rel_bucket(dist):
    n = jnp.maximum(dist, 0)
    max_exact = N_BUCKETS // 2
    nf = jnp.maximum(n, 1).astype(F32)
    large = max_exact + (jnp.log(nf / max_exact) / math.log(MAX_DISTANCE / max_exact)
                         * (N_BUCKETS - max_exact)).astype(I32)
    large = jnp.minimum(large, N_BUCKETS - 1)
    return jnp.where(n < max_exact, n, large)


def _bias_from_bucket(bucket, tab_ref, heads):
    masks = [bucket == k for k in range(1, N_BUCKETS)]
    out = []
    for h in heads:
        acc = jnp.full(bucket.shape, tab_ref[0, h], F32)
        for k in range(1, N_BUCKETS):
            acc = jnp.where(masks[k - 1], tab_ref[k, h], acc)
        out.append(acc)
    return out


def _proj_kernel(x_ref, w_ref, *o_refs, cols):
    y = _dot(x_ref[...].astype(BF16), w_ref[...])
    for o_ref, (off, n) in zip(o_refs, cols):
        o_ref[...] = y[:, off:off + n].astype(o_ref.dtype)


def _proj(x, w, outs, tm, name):
    m, k = x.shape
    n = w.shape[1]
    tm = min(tm, m)
    assert m % tm == 0 and all(off + wd <= n for off, wd, _ in outs)
    return pl.pallas_call(
        functools.partial(_proj_kernel, cols=tuple((off, wd) for off, wd, _ in outs)),
        grid=(m // tm,),
        in_specs=[pl.BlockSpec((tm, k), lambda i: (i, 0)),
                  pl.BlockSpec((k, n), lambda i: (0, 0))],
        out_specs=[pl.BlockSpec((tm, wd), lambda i: (i, 0)) for _, wd, _ in outs],
        out_shape=[jax.ShapeDtypeStruct((m, wd), dt) for _, wd, dt in outs],
        compiler_params=_cparams(1),
        name=name,
    )(x, w)


def _softplus(z):
    return jnp.maximum(z, 0.0) + jnp.log1p(jnp.exp(-jnp.abs(z)))


def _lru_gates(xc, wr_ref, wi_ref, br, bi, lam):
    xcb = xc.astype(BF16)
    nb = D_RNN // LANES
    r_l = jnp.concatenate([_dot(xcb[:, j * LANES:(j + 1) * LANES], wr_ref[j]) for j in range(nb)], axis=1)
    i_l = jnp.concatenate([_dot(xcb[:, j * LANES:(j + 1) * LANES], wi_ref[j]) for j in range(nb)], axis=1)
    r = jax.nn.sigmoid(r_l + br)
    i = jax.nn.sigmoid(i_l + bi)
    log_a = -LRU_C * r * _softplus(-lam)
    a = jnp.exp(log_a)
    u = jnp.sqrt(1.0 - a * a) * i * xc
    return a, u


def _rglru_prompt_kernel(gate_ref, xr_ref, cw_ref, cb_ref, wr_ref, wi_ref, br_ref, bi_ref, lam_ref,
                         y_ref, ht_ref, buf_ref, xp_ref, *, t_len, chunk):
    d = D_RNN
    pad = SUBLANES
    xp_ref[0:pad, :] = jnp.zeros((pad, d), F32)
    xp_ref[pad:pad + t_len, :] = xr_ref[0]
    buf_ref[0] = xr_ref[0, t_len - (CONV_W - 1):t_len, :]
    cw = cw_ref[...]
    cb = cb_ref[...]
    br = br_ref[...]
    bi = bi_ref[...]
    lam = lam_ref[...]
    row_in_tile = lax.broadcasted_iota(I32, (chunk, d), 0) & (SUBLANES - 1)
    h = jnp.zeros((1, d), F32)
    for c in range(t_len // chunk):
        base = c * chunk
        xc = cb
        for k in range(CONV_W):
            lo = pad - (CONV_W - 1) + k + base
            xc = xc + xp_ref[lo:lo + chunk, :] * cw[k:k + 1, :]
        a, u = _lru_gates(xc, wr_ref, wi_ref, br, bi, lam)
        for k in (1, 2, 4):
            a_s = pltpu.roll(a, k, axis=0)
            u_s = pltpu.roll(u, k, axis=0)
            m = row_in_tile >= k
            u = jnp.where(m, a * u_s + u, u)
            a = jnp.where(m, a * a_s, a)
        hs = []
        for j in range(chunk // SUBLANES):
            h_t = a[j * SUBLANES:(j + 1) * SUBLANES] * h + u[j * SUBLANES:(j + 1) * SUBLANES]
            h = h_t[SUBLANES - 1:SUBLANES]
            hs.append(h_t)
        hfull = jnp.concatenate(hs, axis=0)
        g = jax.nn.gelu(gate_ref[0, base:base + chunk, :])
        y_ref[0, base:base + chunk, :] = (g * hfull).astype(y_ref.dtype)
    ht_ref[0] = h


def _rglru_prompt(gate, xr, cw, cb, wr_bd, wi_bd, br, bi, lam):
    b, t_len, d = xr.shape
    chunk = min(256, t_len)
    full = lambda shape: pl.BlockSpec(shape, lambda i: (0,) * len(shape))
    return pl.pallas_call(
        functools.partial(_rglru_prompt_kernel, t_len=t_len, chunk=chunk),
        grid=(b,),
        in_specs=[pl.BlockSpec((1, t_len, d), lambda i: (i, 0, 0)),
                  pl.BlockSpec((1, t_len, d), lambda i: (i, 0, 0)),
                  full((CONV_W, d)), full((1, d)), full(wr_bd.shape), full(wi_bd.shape),
                  full((1, d)), full((1, d)), full((1, d))],
        out_specs=[pl.BlockSpec((1, t_len, d), lambda i: (i, 0, 0)),
                   pl.BlockSpec((1, 1, d), lambda i: (i, 0, 0)),
                   pl.BlockSpec((1, CONV_W - 1, d), lambda i: (i, 0, 0))],
        out_shape=[jax.ShapeDtypeStruct((b, t_len, d), BF16),
                   jax.ShapeDtypeStruct((b, 1, d), F32),
                   jax.ShapeDtypeStruct((b, CONV_W - 1, d), F32)],
        scratch_shapes=[pltpu.VMEM((t_len + SUBLANES, d), F32)],
        compiler_params=_cparams(1),
        name="rglru_prompt",
    )(gate, xr, cw, cb, wr_bd, wi_bd, br, bi, lam)


def _rglru_step_kernel(gate_ref, xr_ref, conv_ref, h0_ref, cw_ref, cb_ref, wr_ref, wi_ref, br_ref, bi_ref,
                       lam_ref, y_ref, ht_ref, buf_ref):
    xr = xr_ref[...]
    cw = cw_ref[...]
    xc = cb_ref[...] + xr * cw[CONV_W - 1:CONV_W, :]
    for k in range(CONV_W - 1):
        xc = xc + conv_ref[k] * cw[k:k + 1, :]
    a, u = _lru_gates(xc, wr_ref, wi_ref, br_ref[...], bi_ref[...], lam_ref[...])
    h = a * h0_ref[...] + u
    ht_ref[...] = h
    y_ref[...] = (jax.nn.gelu(gate_ref[...]) * h).astype(y_ref.dtype)
    for k in range(CONV_W - 2):
        buf_ref[k] = conv_ref[k + 1]
    buf_ref[CONV_W - 2] = xr


def _rglru_step(gate, xr, conv_t, h0, cw, cb, wr_bd, wi_bd, br, bi, lam):
    b, d = xr.shape
    return pl.pallas_call(
        _rglru_step_kernel,
        out_shape=[jax.ShapeDtypeStruct((b, d), BF16),
                   jax.ShapeDtypeStruct((b, d), F32),
                   jax.ShapeDtypeStruct((CONV_W - 1, b, d), F32)],
        compiler_params=pltpu.CompilerParams(vmem_limit_bytes=VMEM_LIMIT_BYTES),
        name="rglru_step",
    )(gate, xr, conv_t, h0, cw, cb, wr_bd, wi_bd, br, bi, lam)


def _mem_attn_kernel(q_ref, kv_ref, o_ref, *, rows):
    q = q_ref[0]
    if rows < SUBLANES:
        q = jnp.broadcast_to(q, (SUBLANES, D_MEM))
    kv = kv_ref[0, 0].astype(BF16)
    outs = []
    for h in range(N_MEM_HEADS):
        qh = q[:, h * HEAD_DIM:(h + 1) * HEAD_DIM]
        kh = kv[:, h * HEAD_DIM:(h + 1) * HEAD_DIM]
        vh = kv[:, D_MEM + h * HEAD_DIM:D_MEM + (h + 1) * HEAD_DIM]
        s = _dot_nt(qh, kh) * SCALE
        m = jnp.max(s, axis=-1, keepdims=True)
        e = jnp.exp(s - m)
        p = e / jnp.sum(e, axis=-1, keepdims=True)
        outs.append(_dot(p.astype(BF16), vh))
    o = jnp.concatenate(outs, axis=1)
    o_ref[0] = o[:rows].astype(o_ref.dtype)


def _mem_attn(q, mem_kv, layer):
    b, t_len, _ = q.shape
    n_mem = mem_kv.shape[2]
    tq = min(t_len, 512)
    return pl.pallas_call(
        functools.partial(_mem_attn_kernel, rows=tq),
        grid=(b, t_len // tq),
        in_specs=[pl.BlockSpec((1, tq, D_MEM), lambda i, j: (i, j, 0)),
                  pl.BlockSpec((1, 1, n_mem, 2 * D_MEM), lambda i, j: (layer, i, 0, 0))],
        out_specs=pl.BlockSpec((1, tq, D_MEM), lambda i, j: (i, j, 0)),
        out_shape=jax.ShapeDtypeStruct((b, t_len, D_MEM), BF16),
        compiler_params=_cparams(2),
        name="mem_attn",
    )(q, mem_kv)


def _mem_attn_step_kernel(q_ref, kv_ref, o_ref):
    q_row = q_ref[0] * SCALE
    rows = []
    for h in range(N_MEM_HEADS):
        parts = []
        if h > 0:
            parts.append(jnp.zeros((1, h * HEAD_DIM), BF16))
        parts.append(q_row[:, h * HEAD_DIM:(h + 1) * HEAD_DIM])
        if h < N_MEM_HEADS - 1:
            parts.append(jnp.zeros((1, (N_MEM_HEADS - 1 - h) * HEAD_DIM), BF16))
        rows.append(jnp.concatenate(parts, axis=1))
    n_rows = 2 * SUBLANES
    q_blk = jnp.concatenate(rows + [jnp.zeros((n_rows - N_MEM_HEADS, D_MEM), BF16)], axis=0)
    s = _dot(q_blk, kv_ref[0, 0, 0].astype(BF16))
    e = jnp.exp(s - jnp.max(s, axis=-1, keepdims=True))
    p = e / jnp.sum(e, axis=-1, keepdims=True)
    o_all = _dot_nt(p.astype(BF16), kv_ref[0, 0, 1].astype(BF16))
    o = jnp.concatenate([o_all[h:h + 1, h * HEAD_DIM:(h + 1) * HEAD_DIM] for h in range(N_MEM_HEADS)], axis=1)
    o_ref[0] = o.astype(o_ref.dtype)


def _mem_attn_step(q, mem_t, layer):
    b = q.shape[0]
    n_mem = mem_t.shape[4]
    return pl.pallas_call(
        _mem_attn_step_kernel,
        grid=(b,),
        in_specs=[pl.BlockSpec((1, 1, D_MEM), lambda i: (i, 0, 0)),
                  pl.BlockSpec((1, 1, 2, D_MEM, n_mem), lambda i: (layer, i, 0, 0, 0))],
        out_specs=pl.BlockSpec((1, 1, D_MEM), lambda i: (i, 0, 0)),
        out_shape=jax.ShapeDtypeStruct((b, 1, D_MEM), BF16),
        compiler_params=_cparams(1),
        name="mem_attn_step",
    )(q, mem_t)


def _out_ln_kernel(x_ref, ma_ref, mm_ref, wa_ref, wm_ref, g_ref, b_ref, o_ref, *, alpha):
    sub = _dot(ma_ref[...], wa_ref[...]) + _dot(mm_ref[...], wm_ref[...])
    o_ref[...] = _layer_norm(alpha * x_ref[...] + sub, g_ref[...], b_ref[...])


def _out_ln(x, mix_a, mix_m, wo_a, wo_m, g, b, alpha, tm):
    m, d = x.shape
    da = mix_a.shape[1]
    dm = mix_m.shape[1]
    return pl.pallas_call(
        functools.partial(_out_ln_kernel, alpha=alpha),
        grid=(m // tm,),
        in_specs=[pl.BlockSpec((tm, d), lambda i: (i, 0)),
                  pl.BlockSpec((tm, da), lambda i: (i, 0)),
                  pl.BlockSpec((tm, dm), lambda i: (i, 0)),
                  pl.BlockSpec((da, d), lambda i: (0, 0)),
                  pl.BlockSpec((dm, d), lambda i: (0, 0)),
                  pl.BlockSpec((1, d), lambda i: (0, 0)),
                  pl.BlockSpec((1, d), lambda i: (0, 0))],
        out_specs=pl.BlockSpec((tm, d), lambda i: (i, 0)),
        out_shape=jax.ShapeDtypeStruct((m, d), F32),
        compiler_params=_cparams(1),
        name="out_ln",
    )(x, mix_a, mix_m, wo_a, wo_m, g, b)


def _route_group(logits):
    lane = lax.broadcasted_iota(I32, logits.shape, 1)
    lg = jnp.where(lane < N_GROUPS, logits, -jnp.inf)
    gmax = jnp.max(lg, axis=-1, keepdims=True)
    return jnp.min(jnp.where(lg == gmax, lane, jnp.int32(4 * LANES)), axis=-1, keepdims=True), gmax


def _route(logits):
    lane = lax.broadcasted_iota(I32, logits.shape, 1)
    big = jnp.int32(4 * LANES)
    is_g = lane < N_GROUPS
    grp, gmax = _route_group(logits)
    gsum = jnp.sum(jnp.where(is_g, jnp.exp(logits - gmax), 0.0), axis=-1, keepdims=True)
    g_w = 1.0 / gsum
    e_idx = lane - N_GROUPS
    in_grp = (lane >= N_GROUPS) & (lane < N_GROUPS + N_EXPERTS) & ((e_idx >> 3) == grp)
    v = jnp.where(in_grp, logits, -jnp.inf)
    v1 = jnp.max(v, axis=-1, keepdims=True)
    i1 = jnp.min(jnp.where(v == v1, lane, big), axis=-1, keepdims=True)
    vr = jnp.where(lane == i1, -jnp.inf, v)
    v2 = jnp.max(vr, axis=-1, keepdims=True)
    i2 = jnp.min(jnp.where(vr == v2, lane, big), axis=-1, keepdims=True)
    e2 = jnp.exp(v2 - v1)
    w1 = g_w / (1.0 + e2)
    w2 = g_w * e2 / (1.0 + e2)
    return jnp.where(lane == i1, w1, 0.0) + jnp.where(lane == i2, w2, 0.0)


def _moe_kernel(x_ref, wr_ref, rb_ref, wgu_ref, wd_ref, g_ref, b_ref, o_ref,
                xb_ref, comb_ref, acc_ref, *, alpha):
    grp = pl.program_id(1)
    hid = EXPERTS_PER_GROUP * D_EXPERT

    @pl.when(grp == 0)
    def _():
        xh = x_ref[...].astype(BF16)
        logits = _dot(xh, wr_ref[...]) + rb_ref[...]
        xb_ref[...] = xh
        comb_ref[...] = _route(logits)
        acc_ref[...] = jnp.zeros(acc_ref.shape, F32)

    xb = xb_ref[...]
    gu = _dot(xb, wgu_ref[0])
    gate = gu[:, :hid]
    up = gu[:, hid:]
    hdn = gate * jax.nn.sigmoid(gate) * up
    comb = comb_ref[...]
    lane = lax.broadcasted_iota(I32, comb.shape, 1)
    scale = []
    for e in range(EXPERTS_PER_GROUP):
        col = jnp.sum(jnp.where(lane == N_GROUPS + grp * EXPERTS_PER_GROUP + e, comb, 0.0),
                      axis=-1, keepdims=True)
        scale.append(jnp.broadcast_to(col, (comb.shape[0], D_EXPERT)))
    hdn = hdn * jnp.concatenate(scale, axis=1)
    acc_ref[...] += _dot(hdn.astype(BF16), wd_ref[0])

    @pl.when(grp == N_GROUPS - 1)
    def _():
        o_ref[...] = _layer_norm(alpha * x_ref[...] + acc_ref[...], g_ref[...], b_ref[...])


def _moe_ln(x, wr, rb, wgu, wd, g, b, alpha, tm):
    m, d = x.shape
    hid = EXPERTS_PER_GROUP * D_EXPERT
    return pl.pallas_call(
        functools.partial(_moe_kernel, alpha=alpha),
        grid=(m // tm, N_GROUPS),
        in_specs=[pl.BlockSpec((tm, d), lambda i, j: (i, 0)),
                  pl.BlockSpec((d, LANES), lambda i, j: (0, 0)),
                  pl.BlockSpec((1, LANES), lambda i, j: (0, 0)),
                  pl.BlockSpec((1, d, 2 * hid), lambda i, j: (j, 0, 0)),
                  pl.BlockSpec((1, hid, d), lambda i, j: (j, 0, 0)),
                  pl.BlockSpec((1, d), lambda i, j: (0, 0)),
                  pl.BlockSpec((1, d), lambda i, j: (0, 0))],
        out_specs=pl.BlockSpec((tm, d), lambda i, j: (i, 0)),
        out_shape=jax.ShapeDtypeStruct((m, d), F32),
        scratch_shapes=[pltpu.VMEM((tm, d), BF16), pltpu.VMEM((tm, LANES), F32), pltpu.VMEM((tm, d), F32)],
        compiler_params=_cparams(2),
        name="moe_ln",
    )(x, wr, rb, wgu, wd, g, b)


MOE_TILE = 256


def _group_ids_kernel(x_ref, wr_ref, rb_ref, o_ref):
    grp, _ = _route_group(_dot(x_ref[...].astype(BF16), wr_ref[...]) + rb_ref[...])
    o_ref[...] = jnp.broadcast_to(grp, o_ref.shape)


def _group_ids(x, wr, rb, tm):
    m, d = x.shape
    return pl.pallas_call(
        _group_ids_kernel,
        grid=(m // tm,),
        in_specs=[pl.BlockSpec((tm, d), lambda i: (i, 0)),
                  pl.BlockSpec((d, LANES), lambda i: (0, 0)),
                  pl.BlockSpec((1, LANES), lambda i: (0, 0))],
        out_specs=pl.BlockSpec((tm, LANES), lambda i: (i, 0)),
        out_shape=jax.ShapeDtypeStruct((m, LANES), I32),
        compiler_params=_cparams(1),
        name="moe_group_ids",
    )(x, wr, rb)


def _moe_dispatch(grp):
    n = grp.shape[0]
    onehot = (grp[:, None] == jnp.arange(N_GROUPS, dtype=I32)[None, :]).astype(I32)
    before = jnp.cumsum(onehot, axis=0) - onehot
    counts = jnp.sum(onehot, axis=0)
    padded = (counts + MOE_TILE - 1) // MOE_TILE * MOE_TILE
    ends = jnp.cumsum(padded)
    dest = jnp.sum(onehot * (before + (ends - padded)[None, :]), axis=1).astype(I32)
    n_slots = n + N_GROUPS * MOE_TILE
    tile_start = jnp.arange(n_slots // MOE_TILE, dtype=I32) * MOE_TILE
    tile_grp = jnp.minimum(jnp.sum((tile_start[:, None] >= ends[None, :]).astype(I32), axis=1), N_GROUPS - 1)
    return dest, tile_grp


def _moe_place_kernel(dest_ref, x_hbm, zero_hbm, o_hbm, sem, *, n_tiles):
    del zero_hbm
    i = pl.program_id(0)
    for r in range(MOE_TILE):
        row = i * MOE_TILE + r
        pltpu.make_async_copy(x_hbm.at[pl.ds(row, 1)], o_hbm.at[pl.ds(dest_ref[row], 1)], sem.at[i % 2]).start()

    def wait_tile(buf):
        pltpu.make_async_copy(x_hbm.at[pl.ds(0, MOE_TILE)], o_hbm.at[pl.ds(0, MOE_TILE)], sem.at[buf]).wait()

    @pl.when(i >= 1)
    def _():
        wait_tile(1 - i % 2)

    @pl.when(i == n_tiles - 1)
    def _():
        wait_tile(i % 2)


def _moe_place(x, dest, n_slots):
    n, d = x.shape
    assert n % MOE_TILE == 0
    n_tiles = n // MOE_TILE
    grid_spec = pltpu.PrefetchScalarGridSpec(
        num_scalar_prefetch=1,
        grid=(n_tiles,),
        in_specs=[pl.BlockSpec(memory_space=pl.ANY), pl.BlockSpec(memory_space=pl.ANY)],
        out_specs=pl.BlockSpec(memory_space=pl.ANY),
        scratch_shapes=[pltpu.SemaphoreType.DMA((2,))])
    return pl.pallas_call(
        functools.partial(_moe_place_kernel, n_tiles=n_tiles),
        grid_spec=grid_spec,
        out_shape=jax.ShapeDtypeStruct((n_slots, d), F32),
        input_output_aliases={2: 0},
        compiler_params=_cparams(1),
        name="moe_place",
    )(dest, x, jnp.zeros((n_slots, d), F32))


def _moe_grouped_kernel(tgrp_ref, x_ref, wr_ref, rb_ref, wgu_ref, wd_ref, y_ref):
    i = pl.program_id(0)
    hid = EXPERTS_PER_GROUP * D_EXPERT
    xh = x_ref[...].astype(BF16)
    comb = _route(_dot(xh, wr_ref[...]) + rb_ref[...])
    gu = _dot(xh, wgu_ref[0])
    gate = gu[:, :hid]
    hdn = gate * jax.nn.sigmoid(gate) * gu[:, hid:]
    grp = tgrp_ref[i]
    lane = lax.broadcasted_iota(I32, comb.shape, 1)
    scale = []
    for e in range(EXPERTS_PER_GROUP):
        col = jnp.sum(jnp.where(lane == N_GROUPS + grp * EXPERTS_PER_GROUP + e, comb, 0.0),
                      axis=-1, keepdims=True)
        scale.append(jnp.broadcast_to(col, (comb.shape[0], D_EXPERT)))
    y_ref[...] = _dot((hdn * jnp.concatenate(scale, axis=1)).astype(BF16), wd_ref[0])


def _moe_grouped(xs, tile_grp, wr, rb, wgu, wd):
    n_slots, d = xs.shape
    hid = EXPERTS_PER_GROUP * D_EXPERT
    grid_spec = pltpu.PrefetchScalarGridSpec(
        num_scalar_prefetch=1,
        grid=(n_slots // MOE_TILE,),
        in_specs=[pl.BlockSpec((MOE_TILE, d), lambda i, tg: (i, 0)),
                  pl.BlockSpec((d, LANES), lambda i, tg: (0, 0)),
                  pl.BlockSpec((1, LANES), lambda i, tg: (0, 0)),
                  pl.BlockSpec((1, d, 2 * hid), lambda i, tg: (tg[i], 0, 0)),
                  pl.BlockSpec((1, hid, d), lambda i, tg: (tg[i], 0, 0))],
        out_specs=pl.BlockSpec((MOE_TILE, d), lambda i, tg: (i, 0)))
    return pl.pallas_call(
        _moe_grouped_kernel,
        grid_spec=grid_spec,
        out_shape=jax.ShapeDtypeStruct((n_slots, d), F32),
        compiler_params=_cparams(1),
        name="moe_grouped",
    )(tile_grp, xs, wr, rb, wgu, wd)


def _moe_return_ln_kernel(dest_ref, x_ref, y_hbm, g_ref, b_ref, o_ref, ybuf, sem, *, alpha, n_tiles):
    i = pl.program_id(0)
    slot = i % 2

    def start_tile(tile, buf):
        for r in range(MOE_TILE):
            pltpu.make_async_copy(y_hbm.at[pl.ds(dest_ref[tile * MOE_TILE + r], 1)],
                                  ybuf.at[buf, pl.ds(r, 1)], sem.at[buf]).start()

    @pl.when(i == 0)
    def _():
        start_tile(i, slot)

    @pl.when(i + 1 < n_tiles)
    def _():
        start_tile(i + 1, 1 - slot)

    pltpu.make_async_copy(y_hbm.at[pl.ds(0, MOE_TILE)], ybuf.at[slot], sem.at[slot]).wait()
    o_ref[...] = _layer_norm(alpha * x_ref[...] + ybuf[slot], g_ref[...], b_ref[...])


def _moe_return_ln(x, ys, dest, g, b, alpha):
    n, d = x.shape
    n_tiles = n // MOE_TILE
    grid_spec = pltpu.PrefetchScalarGridSpec(
        num_scalar_prefetch=1,
        grid=(n_tiles,),
        in_specs=[pl.BlockSpec((MOE_TILE, d), lambda i, ds: (i, 0)),
                  pl.BlockSpec(memory_space=pl.ANY),
                  pl.BlockSpec((1, d), lambda i, ds: (0, 0)),
                  pl.BlockSpec((1, d), lambda i, ds: (0, 0))],
        out_specs=pl.BlockSpec((MOE_TILE, d), lambda i, ds: (i, 0)),
        scratch_shapes=[pltpu.VMEM((2, MOE_TILE, d), F32), pltpu.SemaphoreType.DMA((2,))])
    return pl.pallas_call(
        functools.partial(_moe_return_ln_kernel, alpha=alpha, n_tiles=n_tiles),
        grid_spec=grid_spec,
        out_shape=jax.ShapeDtypeStruct((n, d), F32),
        compiler_params=_cparams(1),
        name="moe_return_ln",
    )(dest, x, ys, g, b)


CMP_MBLK = 128
ROW_TILES = D_KV // LANES
PAIR_HID = 2 * CMP_HID


def _compress_weights(pe, w1, b1, w2):
    w1 = w1.reshape(2, CMP_LEN, HEAD_DIM, CMP_HID)
    z1 = jnp.zeros_like(w1)
    w1p = jnp.concatenate([jnp.concatenate([w1, z1], axis=3), jnp.concatenate([z1, w1], axis=3)], axis=2)
    z2 = jnp.zeros_like(w2)
    w2p = jnp.concatenate([jnp.concatenate([w2, z2], axis=2), jnp.concatenate([z2, w2], axis=2)], axis=1)
    w1p = w1p.astype(BF16).reshape(2, 2, CMP_STRIDE * LANES, PAIR_HID)
    return (jnp.concatenate([pe, pe], axis=2), w1p,
            jnp.concatenate([b1, b1], axis=1)[:, None], w2p.astype(BF16))


def _load_interleaved(x_ref):
    step = CMP_STRIDE * ROW_TILES

    def load(tile, r, mb):
        row0 = pl.multiple_of(mb * (CMP_MBLK * step), CMP_MBLK * step)
        return x_ref[pl.ds(row0 + r * ROW_TILES + tile, CMP_MBLK, stride=step), :]
    return load


def _load_tiled(x_ref):
    def load(tile, r, mb):
        row0 = pl.multiple_of(mb * (CMP_MBLK * CMP_STRIDE), CMP_MBLK * CMP_STRIDE)
        return x_ref[tile, pl.ds(row0 + r, CMP_MBLK, stride=CMP_STRIDE), :]
    return load


def _compress_rows(load, n_chunk, pe_ref, w1_ref, b1_ref, w2_ref, out_ref, a_ref, b_ref):
    n_blk = n_chunk // CMP_MBLK
    for kv in range(2):
        for pair in range(N_KV_GROUPS // 2):
            tile = kv * (N_KV_GROUPS // 2) + pair

            def fill(mb, carry, kv=kv, tile=tile):
                top, bot = [], []
                for r in range(CMP_STRIDE):
                    xg = load(tile, r, mb)
                    top.append((xg + pe_ref[kv, r:r + 1, :]).astype(BF16))
                    bot.append((xg + pe_ref[kv, CMP_STRIDE + r:CMP_STRIDE + r + 1, :]).astype(BF16))
                m0 = pl.multiple_of(mb * CMP_MBLK, CMP_MBLK)
                a_ref[pl.ds(m0, CMP_MBLK), :] = _dot(jnp.concatenate(top, axis=1), w1_ref[kv, 0])
                b_ref[pl.ds(m0, CMP_MBLK), :] = _dot(jnp.concatenate(bot, axis=1), w1_ref[kv, 1])
                return carry

            lax.fori_loop(0, n_blk, fill, 0)
            b_ref[n_chunk:n_chunk + SUBLANES, :] = jnp.zeros((SUBLANES, PAIR_HID), F32)
            for mb in range(n_blk):
                m0 = mb * CMP_MBLK
                hid = jax.nn.gelu(a_ref[m0:m0 + CMP_MBLK, :] + b_ref[m0 + 1:m0 + 1 + CMP_MBLK, :] + b1_ref[kv])
                out_ref[m0:m0 + CMP_MBLK, tile * LANES:(tile + 1) * LANES] = _dot(hid.astype(BF16), w2_ref[kv])


def _compress_prompt_kernel(x_ref, pe_ref, w1_ref, b1_ref, w2_ref, o_ref, a_ref, b_ref, *, n_chunk):
    _compress_rows(_load_interleaved(x_ref.at[0]), n_chunk, pe_ref, w1_ref, b1_ref, w2_ref, o_ref.at[0],
                   a_ref, b_ref)


def _compress_prompt(rows, pe, w1, b1, w2):
    b, t_len, _ = rows.shape
    n_chunk = t_len // CMP_STRIDE
    assert n_chunk % CMP_MBLK == 0
    full = lambda a: pl.BlockSpec(a.shape, lambda i: (0,) * a.ndim)
    return pl.pallas_call(
        functools.partial(_compress_prompt_kernel, n_chunk=n_chunk),
        grid=(b,),
        in_specs=[pl.BlockSpec((1, t_len * ROW_TILES, LANES), lambda i: (i, 0, 0)),
                  full(pe), full(w1), full(b1), full(w2)],
        out_specs=pl.BlockSpec((1, n_chunk, D_KV), lambda i: (i, 0, 0)),
        out_shape=jax.ShapeDtypeStruct((b, n_chunk, D_KV), F32),
        scratch_shapes=[pltpu.VMEM((n_chunk, PAIR_HID), F32), pltpu.VMEM((n_chunk + SUBLANES, PAIR_HID), F32)],
        compiler_params=_cparams(1),
        name="compress_prompt",
    )(rows.reshape(b, t_len * ROW_TILES, LANES), pe, w1, b1, w2)


def _masked_softmax_rows(s, mask):
    sm = jnp.where(mask, s, NEG)
    m = jnp.max(sm, axis=-1, keepdims=True)
    e = jnp.where(mask, jnp.exp(sm - m), 0.0)
    tot = jnp.sum(e, axis=-1, keepdims=True)
    return e * jnp.where(tot > 0.0, 1.0 / tot, 0.0)


WIN_TILES = WINDOW // Q_BLOCK


KEY_TILE = 512
KEY_SUB = KEY_TILE // Q_BLOCK


def _attend_tiles_t(qts, k_ref, vt_ref, t_lo, t_hi, i_blk, bias_ref, bias_index, mask_fn):
    cols = qts[0].shape[1]

    def body(t, carry):
        k0 = pl.multiple_of(t * KEY_TILE, KEY_TILE)
        bases = [bias_index(i_blk - (t * KEY_SUB + u)) for u in range(KEY_SUB)]
        scores = [_dot(k_ref[0, g, pl.ds(k0, KEY_TILE), :], qt) for g, qt in enumerate(qts)]
        new = []
        for g in range(len(qts)):
            m, l, acc = carry[g]
            vt = vt_ref[0, g * HEAD_DIM:(g + 1) * HEAD_DIM, pl.ds(k0, KEY_TILE)]
            heads = [g * HEADS_PER_GROUP + hp for hp in range(HEADS_PER_GROUP)]
            bias = jnp.concatenate(
                [jnp.concatenate([bias_ref[base + h] for h in heads], axis=1) for base in bases], axis=0)
            s = scores[g] + bias
            extra = mask_fn(t, g)
            if extra is not None:
                s = s + jnp.concatenate([extra] * HEADS_PER_GROUP, axis=1)
            m_new = jnp.maximum(m, jnp.max(s, axis=0, keepdims=True))
            alpha = jnp.exp(m - m_new)
            p = jnp.exp(s - m_new)
            new.append((m_new, alpha * l + jnp.sum(p, axis=0, keepdims=True), alpha * acc + _dot(vt, p.astype(BF16))))
        return tuple(new)

    init = tuple((jnp.full((1, cols), NEG, F32), jnp.zeros((1, cols), F32), jnp.zeros((HEAD_DIM, cols), F32))
                 for _ in qts)
    fin = lax.fori_loop(t_lo, t_hi, body, init)
    return [acc / l for (_, l, acc) in fin]


def _nsa_prompt_kernel(tab_ref, q_ref, gl_ref, ck_ref, cvt_ref, sk_ref, svt_ref, wk_ref, wvt_ref, o_ref,
                       bias_ref, cbias_ref, sel_ref, *, n_tiles):
    b = pl.program_id(0)
    i = pl.program_id(1)
    kk = lax.broadcasted_iota(I32, (Q_BLOCK, Q_BLOCK), 0)
    qq = lax.broadcasted_iota(I32, (Q_BLOCK, Q_BLOCK), 1)
    n_cmp = ck_ref.shape[2]
    n_sel_blocks = 2 * n_tiles
    edge_base = n_tiles * N_HEADS
    none_base = (n_tiles + 1) * N_HEADS

    @pl.when((b == 0) & (i == 0))
    def _():
        def build(delta, carry):
            dist = delta * Q_BLOCK + qq - kk
            biases = _bias_from_bucket(_rel_bucket(dist), tab_ref, range(N_HEADS))
            for h in range(N_HEADS):
                bias_ref[delta * N_HEADS + h] = jnp.where(dist >= 0, biases[h], NEG)
            return carry
        lax.fori_loop(0, n_tiles, build, 0)
        dist = WIN_TILES * Q_BLOCK + qq - kk
        biases = _bias_from_bucket(_rel_bucket(dist), tab_ref, range(N_HEADS))
        for h in range(N_HEADS):
            bias_ref[edge_base + h] = jnp.where(dist < WINDOW, biases[h], NEG)
            bias_ref[none_base + h] = jnp.full((Q_BLOCK, Q_BLOCK), NEG, F32)

        def build_c(t, carry):
            dist_c = t * Q_BLOCK + qq - (kk * CMP_STRIDE + (CMP_LEN - 1))
            ok = (dist_c >= 0) & (kk < n_cmp - 1)
            biases_c = _bias_from_bucket(_rel_bucket(dist_c), tab_ref, range(N_HEADS))
            for h in range(N_HEADS):
                cbias_ref[t * N_HEADS + h] = jnp.where(ok, biases_c[h], NEG)
            return carry
        lax.fori_loop(0, n_tiles, build_c, 0)

    qs = i * Q_BLOCK
    q_all = q_ref[0] * SCALE
    gates = jax.nn.sigmoid(gl_ref[0])

    blk = lax.broadcasted_iota(I32, (n_sel_blocks, Q_BLOCK), 0)
    col = lax.broadcasted_iota(I32, (n_sel_blocks, Q_BLOCK), 1)
    c_start = col * CMP_STRIDE
    s_start = blk * SEL_LEN
    cover_t = jnp.where((c_start < s_start + SEL_LEN) & (c_start + CMP_LEN > s_start) & (col < n_cmp - 1),
                        1.0, 0.0).astype(BF16)
    q_pos = qs + col
    cur = q_pos // SEL_LEN
    forced = (blk == 0) | (blk == cur) | (blk == cur - 1)
    valid = blk * SEL_LEN <= q_pos
    upper = kk < SEL_LEN

    qts, o_cs, splits = [], [], []
    for g in range(N_KV_GROUPS):
        heads = [g * HEADS_PER_GROUP + hp for hp in range(HEADS_PER_GROUP)]
        qt = jnp.concatenate([q_all[h * HEAD_DIM:(h + 1) * HEAD_DIM, :] for h in heads], axis=1)
        qts.append(qt)

        s_c = _dot(ck_ref[0, g], qt) + jnp.concatenate([cbias_ref[i * N_HEADS + h] for h in heads], axis=1)
        m_c = jnp.max(s_c, axis=0, keepdims=True)
        e_c = jnp.exp(s_c - m_c)
        tot = jnp.sum(e_c, axis=0, keepdims=True)
        p_c = e_c * jnp.where(m_c > 0.5 * NEG, 1.0 / tot, 0.0)
        o_cs.append(_dot(cvt_ref[0, g * HEAD_DIM:(g + 1) * HEAD_DIM, :], p_c.astype(BF16)))
        p_sum = p_c[:, 0:Q_BLOCK]
        for hp in range(1, HEADS_PER_GROUP):
            p_sum = p_sum + p_c[:, hp * Q_BLOCK:(hp + 1) * Q_BLOCK]
        splits.append(p_sum.astype(BF16))

    imp_all = _dot(cover_t, jnp.concatenate(splits, axis=1))
    for g in range(N_KV_GROUPS):
        imp = imp_all[:, g * Q_BLOCK:(g + 1) * Q_BLOCK]
        score = jnp.where(forced, FORCE, imp)
        score = jnp.where(valid, score, NEG)
        cnt = jnp.zeros(score.shape, F32)
        for s in range(n_sel_blocks):
            row = score[s:s + 1, :]
            beats = (row > score) | ((row == score) & (blk > s))
            cnt = cnt + jnp.where(beats, 1.0, 0.0)
        sel = (cnt < float(N_SEL)) & (score > 0.5 * NEG)
        sel_ref[g] = jnp.where(sel, 0.0, NEG)

    def mask_sel(t, g):
        rows = sel_ref[g, pl.ds(pl.multiple_of(2 * KEY_SUB * t, 2 * KEY_SUB), 2 * KEY_SUB), :]
        return jnp.concatenate([jnp.where(upper, rows[2 * u:2 * u + 1], rows[2 * u + 1:2 * u + 2])
                                for u in range(KEY_SUB)], axis=0)
    last = i // KEY_SUB + 1
    o_ss = _attend_tiles_t(qts, sk_ref, svt_ref, 0, last, i, bias_ref,
                           lambda d: jnp.where(d < 0, none_base, d * N_HEADS), mask_sel)

    def win_index(d):
        return jnp.where((d < 0) | (d > WIN_TILES), none_base, jnp.where(d == WIN_TILES, edge_base, d * N_HEADS))
    o_ws = _attend_tiles_t(qts, wk_ref, wvt_ref, jnp.maximum(i - WIN_TILES, 0) // KEY_SUB, last, i, bias_ref,
                           win_index, lambda t, g: None)

    for h in range(N_HEADS):
        g, hp = divmod(h, HEADS_PER_GROUP)
        c0 = hp * Q_BLOCK
        o_h = (gates[3 * h:3 * h + 1, :] * o_cs[g][:, c0:c0 + Q_BLOCK]
               + gates[3 * h + 1:3 * h + 2, :] * o_ss[g][:, c0:c0 + Q_BLOCK]
               + gates[3 * h + 2:3 * h + 3, :] * o_ws[g][:, c0:c0 + Q_BLOCK])
        o_ref[0, h * HEAD_DIM:(h + 1) * HEAD_DIM, :] = o_h.astype(o_ref.dtype)


def _nsa_prompt(tab, q_t, gl_t, comp_k, comp_vt, slc_k, slc_vt, win_k, win_vt):
    b, dq, t_len = q_t.shape
    n_tiles = t_len // Q_BLOCK
    n_cmp = comp_k.shape[2]
    assert n_cmp == Q_BLOCK and t_len % KEY_TILE == 0 and n_tiles > WIN_TILES
    dv = N_KV_GROUPS * HEAD_DIM
    return pl.pallas_call(
        functools.partial(_nsa_prompt_kernel, n_tiles=n_tiles),
        grid=(b, n_tiles),
        in_specs=[pl.BlockSpec(memory_space=pltpu.SMEM),
                  pl.BlockSpec((1, dq, Q_BLOCK), lambda i, j: (i, 0, j)),
                  pl.BlockSpec((1, LANES, Q_BLOCK), lambda i, j: (i, 0, j)),
                  pl.BlockSpec((1, N_KV_GROUPS, n_cmp, HEAD_DIM), lambda i, j: (i, 0, 0, 0)),
                  pl.BlockSpec((1, dv, n_cmp), lambda i, j: (i, 0, 0)),
                  pl.BlockSpec((1, N_KV_GROUPS, t_len, HEAD_DIM), lambda i, j: (i, 0, 0, 0)),
                  pl.BlockSpec((1, dv, t_len), lambda i, j: (i, 0, 0)),
                  pl.BlockSpec((1, N_KV_GROUPS, t_len, HEAD_DIM), lambda i, j: (i, 0, 0, 0)),
                  pl.BlockSpec((1, dv, t_len), lambda i, j: (i, 0, 0))],
        out_specs=pl.BlockSpec((1, dq, Q_BLOCK), lambda i, j: (i, 0, j)),
        out_shape=jax.ShapeDtypeStruct((b, dq, t_len), BF16),
        scratch_shapes=[pltpu.VMEM(((n_tiles + 2) * N_HEADS, Q_BLOCK, Q_BLOCK), F32),
                        pltpu.VMEM((n_tiles * N_HEADS, Q_BLOCK, Q_BLOCK), F32),
                        pltpu.VMEM((N_KV_GROUPS, 2 * n_tiles, Q_BLOCK), F32)],
        compiler_params=_cparams(2),
        name="nsa_prompt",
    )(tab, q_t, gl_t, comp_k, comp_vt, slc_k, slc_vt, win_k, win_vt)


def _page_copy(pt_ref, pool_ref, buf_ref, sem, b, j, page):
    start = pl.multiple_of(j * page, page)
    if len(buf_ref.shape) == 2:
        dst = buf_ref.at[pl.ds(start, page)]
    else:
        dst = buf_ref.at[:, :, pl.ds(start, page)]
    return pltpu.make_async_copy(pool_ref.at[pt_ref[b, j]], dst, sem)


def _gather_pages_start(pt_ref, pool_ref, buf_ref, sem, b, n_pages, page):
    def go(j, carry):
        _page_copy(pt_ref, pool_ref, buf_ref, sem, b, j, page).start()
        return carry
    lax.fori_loop(0, n_pages, go, 0)


def _gather_pages_wait(pt_ref, pool_ref, buf_ref, sem, b, n_pages, page):
    def go(j, carry):
        _page_copy(pt_ref, pool_ref, buf_ref, sem, b, j, page).wait()
        return carry
    lax.fori_loop(0, n_pages, go, 0)


def _compress_pages_kernel(pt_ref, pool_ref, pe_ref, w1_ref, b1_ref, w2_ref, o_ref, raw_ref, x_ref, a_ref, b_ref,
                           sem, *, n_seq, n_pages, page, n_chunk):
    b = pl.program_id(0)

    @pl.when(b == 0)
    def _():
        _gather_pages_start(pt_ref, pool_ref, raw_ref, sem, b, n_pages, page)

    _gather_pages_wait(pt_ref, pool_ref, raw_ref, sem, b, n_pages, page)

    def relayout(j, carry):
        r0 = pl.multiple_of(j * page, page)
        for tile in range(ROW_TILES):
            kv, pair = divmod(tile, N_KV_GROUPS // 2)
            x_ref[tile, pl.ds(r0, page), :] = raw_ref[kv, pair * LANES:(pair + 1) * LANES, pl.ds(r0, page)].T
        return carry
    lax.fori_loop(0, n_pages, relayout, 0)

    @pl.when(b + 1 < n_seq)
    def _():
        _gather_pages_start(pt_ref, pool_ref, raw_ref, sem, b + 1, n_pages, page)

    _compress_rows(_load_tiled(x_ref), n_chunk, pe_ref, w1_ref, b1_ref, w2_ref, o_ref.at[0], a_ref, b_ref)


def _compress_pages(page_table, pool_t, pe, w1, b1, w2):
    b, n_pages = page_table.shape
    page = pool_t.shape[3]
    past = n_pages * page
    n_chunk = past // CMP_STRIDE
    assert n_chunk % CMP_MBLK == 0 and page == LANES
    full = lambda a: pl.BlockSpec(a.shape, lambda i: (0,) * a.ndim)
    return pl.pallas_call(
        functools.partial(_compress_pages_kernel, n_seq=b, n_pages=n_pages, page=page, n_chunk=n_chunk),
        grid=(b,),
        in_specs=[pl.BlockSpec(memory_space=pltpu.SMEM), pl.BlockSpec(memory_space=pl.ANY),
                  full(pe), full(w1), full(b1), full(w2)],
        out_specs=pl.BlockSpec((1, n_chunk, D_KV), lambda i: (i, 0, 0)),
        out_shape=jax.ShapeDtypeStruct((b, n_chunk, D_KV), F32),
        scratch_shapes=[pltpu.VMEM((2, D_K, past), F32),
                        pltpu.VMEM((ROW_TILES, past, LANES), F32),
                        pltpu.VMEM((n_chunk, PAIR_HID), F32), pltpu.VMEM((n_chunk + SUBLANES, PAIR_HID), F32),
                        pltpu.SemaphoreType.DMA(())],
        compiler_params=_cparams(1),
        name="compress_pages",
    )(page_table, pool_t, pe, w1, b1, w2)


DEC_TILE = 512
DEC_ROWS = 16
D_K = D_KV // 2


def _row_bias(dist_row, tab_ref, bias_ref, col0):
    length = dist_row.shape[1]
    biases = _bias_from_bucket(_rel_bucket(dist_row), tab_ref, range(N_HEADS))
    rows = biases + [jnp.zeros((DEC_ROWS - N_HEADS, length), F32)]
    bias_ref[:, pl.ds(col0, length)] = jnp.concatenate(rows, axis=0)


def _head_block(o_all, h):
    g = h // HEADS_PER_GROUP
    return o_all[h:h + 1, g * HEAD_DIM:(g + 1) * HEAD_DIM]


def _new_token_scores(q_blk, new_row, bias0):
    k_new = new_row[:, :D_K].astype(BF16).astype(F32)
    v_new = new_row[:, D_K:].astype(BF16).astype(F32)
    return jnp.sum(q_blk.astype(F32) * k_new, axis=-1, keepdims=True) + bias0, v_new


def _nsa_step_kernel(tab_ref, pt_ref, q_ref, gl_ref, comp_ref, pool_ref, slc_new_ref, win_ref, win_new_ref, o_ref,
                     kv_buf, bias_c_ref, bias_s_ref, bias_w_ref, sems,
                     *, n_seq, n_pages, page, past, n_cmp, w_buf):
    b = pl.program_id(0)
    slot = b % 2
    n_tiles = past // DEC_TILE
    n_blocks = past // SEL_LEN + 1
    blk_lanes = -(-n_blocks // LANES) * LANES

    @pl.when(b == 0)
    def _():
        _gather_pages_start(pt_ref, pool_ref, kv_buf.at[0], sems.at[0], b, n_pages, page)

    @pl.when(b + 1 < n_seq)
    def _():
        _gather_pages_start(pt_ref, pool_ref, kv_buf.at[1 - slot], sems.at[1 - slot], b + 1, n_pages, page)

    lane_t = lax.broadcasted_iota(I32, (1, DEC_TILE), 1)

    @pl.when(b == 0)
    def _():
        lane_c = lax.broadcasted_iota(I32, (1, n_cmp), 1)
        _row_bias(past - (lane_c * CMP_STRIDE + (CMP_LEN - 1)), tab_ref, bias_c_ref, 0)

        def tile_bias(t, carry):
            k0 = pl.multiple_of(t * DEC_TILE, DEC_TILE)
            _row_bias(past - (k0 + lane_t), tab_ref, bias_s_ref, k0)
            return carry
        lax.fori_loop(0, n_tiles, tile_bias, 0)
        lane_w = lax.broadcasted_iota(I32, (1, w_buf), 1)
        _row_bias(w_buf - lane_w, tab_ref, bias_w_ref, 0)

    bias0 = jnp.concatenate([jnp.full((1, 1), tab_ref[0, h], F32) for h in range(N_HEADS)]
                            + [jnp.zeros((DEC_ROWS - N_HEADS, 1), F32)], axis=0)

    q_row = q_ref[0] * SCALE
    gates = jax.nn.sigmoid(gl_ref[0])
    rows = []
    for h in range(N_HEADS):
        g = h // HEADS_PER_GROUP
        parts = []
        if g > 0:
            parts.append(jnp.zeros((1, g * HEAD_DIM), BF16))
        parts.append(q_row[:, h * HEAD_DIM:(h + 1) * HEAD_DIM])
        if g < N_KV_GROUPS - 1:
            parts.append(jnp.zeros((1, (N_KV_GROUPS - 1 - g) * HEAD_DIM), BF16))
        rows.append(jnp.concatenate(parts, axis=1))
    q_blk = jnp.concatenate(rows + [jnp.zeros((DEC_ROWS - N_HEADS, D_K), BF16)], axis=0)

    comp = comp_ref[0].astype(BF16)
    mask_c = lax.broadcasted_iota(I32, (DEC_ROWS, n_cmp), 1) < n_cmp - 1
    p_c = _masked_softmax_rows(_dot_nt(q_blk, comp[:, :D_K]) + bias_c_ref[...], mask_c)
    o_c = _dot(p_c.astype(BF16), comp[:, D_K:])
    p_sums = [jnp.sum(p_c[g * HEADS_PER_GROUP:(g + 1) * HEADS_PER_GROUP], axis=0, keepdims=True)
              for g in range(N_KV_GROUPS)]
    p_sum = jnp.concatenate(p_sums + [jnp.zeros((DEC_ROWS - N_KV_GROUPS, n_cmp), F32)], axis=0)

    c_start = lax.broadcasted_iota(I32, (n_cmp, blk_lanes), 0) * CMP_STRIDE
    s_start = lax.broadcasted_iota(I32, (n_cmp, blk_lanes), 1) * SEL_LEN
    c_idx = lax.broadcasted_iota(I32, (n_cmp, blk_lanes), 0)
    cover = jnp.where((c_start < s_start + SEL_LEN) & (c_start + CMP_LEN > s_start) & (c_idx < n_cmp - 1),
                      1.0, 0.0).astype(BF16)
    imp = _dot(p_sum.astype(BF16), cover)
    blk = lax.broadcasted_iota(I32, (DEC_ROWS, blk_lanes), 1)
    cur = past // SEL_LEN
    forced = (blk == 0) | (blk == cur) | (blk == cur - 1)
    score = jnp.where(forced, FORCE, imp)
    score = jnp.where(blk < n_blocks, score, NEG)
    score_t = score.T
    j_idx = lax.broadcasted_iota(I32, (blk_lanes, blk_lanes), 0)
    i_idx = lax.broadcasted_iota(I32, (blk_lanes, blk_lanes), 1)
    sel_groups = []
    for g in range(N_KV_GROUPS):
        col = score_t[:, g:g + 1]
        row = score[g:g + 1, :]
        beats = (col > row) | ((col == row) & (j_idx < i_idx))
        rank = jnp.sum(jnp.where(beats, 1.0, 0.0), axis=0, keepdims=True)
        sel_groups.append(jnp.where((rank < float(N_SEL)) & (row > 0.5 * NEG), 1.0, 0.0))
    sel_rows = [sel_groups[h // HEADS_PER_GROUP] for h in range(N_HEADS)]
    sel_bf = jnp.concatenate(sel_rows + [jnp.zeros((DEC_ROWS - N_HEADS, blk_lanes), F32)], axis=0).astype(BF16)

    _gather_pages_wait(pt_ref, pool_ref, kv_buf.at[slot], sems.at[slot], b, n_pages, page)
    e_row = lax.broadcasted_iota(I32, (blk_lanes, DEC_TILE), 0)
    e_col = lax.broadcasted_iota(I32, (blk_lanes, DEC_TILE), 1) // SEL_LEN

    def sel_tile(t, carry):
        m, l, acc = carry
        k0 = pl.multiple_of(t * DEC_TILE, DEC_TILE)
        expand = jnp.where(e_row == e_col + t * (DEC_TILE // SEL_LEN), 1.0, 0.0).astype(BF16)
        key_sel = _dot(sel_bf, expand)
        kt = kv_buf[slot, 0, :, pl.ds(k0, DEC_TILE)].astype(BF16)
        vt = kv_buf[slot, 1, :, pl.ds(k0, DEC_TILE)].astype(BF16)
        s = _dot(q_blk, kt) + bias_s_ref[:, pl.ds(k0, DEC_TILE)] + jnp.where(key_sel > 0.5, 0.0, NEG)
        m_new = jnp.maximum(m, jnp.max(s, axis=-1, keepdims=True))
        alpha = jnp.exp(m - m_new)
        p = jnp.exp(s - m_new)
        return m_new, alpha * l + jnp.sum(p, axis=-1, keepdims=True), alpha * acc + _dot_nt(p.astype(BF16), vt)

    init = (jnp.full((DEC_ROWS, 1), NEG, F32), jnp.zeros((DEC_ROWS, 1), F32), jnp.zeros((DEC_ROWS, D_K), F32))
    m_s, l_s, acc_s = lax.fori_loop(0, n_tiles, sel_tile, init)
    s_new, v_new = _new_token_scores(q_blk, slc_new_ref[0], bias0)
    cur_sel = sel_bf[:, cur:cur + 1].astype(F32)
    s_new = s_new + jnp.where(cur_sel > 0.5, 0.0, NEG)
    m_fin = jnp.maximum(m_s, s_new)
    alpha = jnp.exp(m_s - m_fin)
    p_new = jnp.exp(s_new - m_fin)
    o_s = (alpha * acc_s + p_new.astype(BF16).astype(F32) * v_new) / (alpha * l_s + p_new)

    kw = win_ref[0, 0].astype(BF16)
    vw = win_ref[0, 1].astype(BF16)
    idx_w = lax.broadcasted_iota(I32, (DEC_ROWS, w_buf), 1)
    s_w = jnp.where(w_buf - idx_w < WINDOW, _dot(q_blk, kw) + bias_w_ref[...], NEG)
    s_wn, v_wn = _new_token_scores(q_blk, win_new_ref[0], bias0)
    m_w = jnp.maximum(jnp.max(s_w, axis=-1, keepdims=True), s_wn)
    e_w = jnp.exp(s_w - m_w)
    e_wn = jnp.exp(s_wn - m_w)
    o_w = ((_dot_nt(e_w.astype(BF16), vw) + e_wn.astype(BF16).astype(F32) * v_wn)
           / (jnp.sum(e_w, axis=-1, keepdims=True) + e_wn))

    outs = []
    for h in range(N_HEADS):
        outs.append(gates[:, 3 * h:3 * h + 1] * _head_block(o_c, h)
                    + gates[:, 3 * h + 1:3 * h + 2] * _head_block(o_s, h)
                    + gates[:, 3 * h + 2:3 * h + 3] * _head_block(o_w, h))
    o_ref[0] = jnp.concatenate(outs, axis=1).astype(o_ref.dtype)


def _feature_major(cache):
    nd = cache.ndim
    perm = tuple(range(nd - 4)) + (nd - 3, nd - 2, nd - 1, nd - 4)
    t = cache.transpose(perm)
    return t.reshape(t.shape[:nd - 3] + (t.shape[nd - 3] * t.shape[nd - 2], t.shape[nd - 1]))


def _nsa_step(tab, page_table, q, gl, comp, pool_t, slc_new, win_t, win_new):
    b, n_pages = page_table.shape
    page = pool_t.shape[3]
    past = n_pages * page
    n_cmp = comp.shape[1]
    w_buf = win_t.shape[3]
    dq = q.shape[2]
    assert past % DEC_TILE == 0 and past % SEL_LEN == 0 and w_buf == WINDOW and n_cmp == past // CMP_STRIDE
    row3 = lambda w: pl.BlockSpec((1, 1, w), lambda i: (i, 0, 0))
    return pl.pallas_call(
        functools.partial(_nsa_step_kernel, n_seq=b, n_pages=n_pages, page=page, past=past, n_cmp=n_cmp,
                          w_buf=w_buf),
        grid=(b,),
        in_specs=[pl.BlockSpec(memory_space=pltpu.SMEM), pl.BlockSpec(memory_space=pltpu.SMEM),
                  row3(dq), row3(LANES),
                  pl.BlockSpec((1, n_cmp, D_KV), lambda i: (i, 0, 0)),
                  pl.BlockSpec(memory_space=pl.ANY),
                  row3(D_KV),
                  pl.BlockSpec((1, 2, D_K, w_buf), lambda i: (i, 0, 0, 0)),
                  row3(D_KV)],
        out_specs=row3(dq),
        out_shape=jax.ShapeDtypeStruct((b, 1, dq), BF16),
        scratch_shapes=[pltpu.VMEM((2, 2, D_K, past), F32),
                        pltpu.VMEM((DEC_ROWS, n_cmp), F32),
                        pltpu.VMEM((DEC_ROWS, past), F32),
                        pltpu.VMEM((DEC_ROWS, w_buf), F32),
                        pltpu.SemaphoreType.DMA((2,))],
        compiler_params=_cparams(1),
        name="nsa_step",
    )(tab, page_table, q, gl, comp, pool_t, slc_new, win_t, win_new)


def _block_diag_pairs(w):
    n, blk, _ = w.shape
    w = w.reshape(n // 2, 2, blk, blk)
    z = jnp.zeros((n // 2, blk, blk), w.dtype)
    top = jnp.concatenate([w[:, 0], z], axis=2)
    bot = jnp.concatenate([z, w[:, 1]], axis=2)
    return jnp.concatenate([top, bot], axis=1).astype(BF16)


def _prep_weights(w_in_a, conv_w, conv_b, w_rg, b_rg, w_ig, b_ig, lru_lambda, w_in_b, w_kv_shared, cmp_pe, cmp_w1,
                  cmp_b1, cmp_w2, w_mem_kv, w_out, ln1_g, ln1_b, ln2_g, ln2_b, w_router_g, b_router_g,
                  w_router_e, b_router_e, w_exp_gate, w_exp_up, w_exp_down):
    depth, d_model, _ = w_out.shape
    n_a = w_in_a.shape[0]
    nq = N_HEADS * HEAD_DIM
    n_gl = 3 * N_HEADS
    hid = EXPERTS_PER_GROUP * D_EXPERT
    p = {}
    p["w_in_a"] = w_in_a.astype(BF16)
    p["w_in_b"] = jnp.concatenate(
        [w_in_b[:, :, :nq], w_in_b[:, :, nq + n_gl:], w_in_b[:, :, nq:nq + n_gl],
         jnp.zeros((w_in_b.shape[0], d_model, LANES - n_gl), w_in_b.dtype)], axis=2).astype(BF16)
    p["w_kv"] = w_kv_shared.astype(BF16)
    p["w_mem"] = w_mem_kv.transpose(1, 0, 2).reshape(d_model, depth * 2 * D_MEM).astype(BF16)
    p["wo_a"] = w_out[:, :D_RNN].astype(BF16)
    p["wo_m"] = w_out[:, D_RNN:].astype(BF16)
    p["conv_w"] = conv_w
    p["conv_b"] = conv_b[:, None]
    p["w_rg"] = jnp.stack([_block_diag_pairs(w_rg[l]) for l in range(n_a)])
    p["w_ig"] = jnp.stack([_block_diag_pairs(w_ig[l]) for l in range(n_a)])
    p["b_rg"] = b_rg[:, None]
    p["b_ig"] = b_ig[:, None]
    p["lam"] = lru_lambda[:, None]
    p["pe"], p["cmp_w1"], p["cmp_b1"], p["cmp_w2"] = _compress_weights(cmp_pe, cmp_w1, cmp_b1, cmp_w2)
    p["ln1_g"], p["ln1_b"], p["ln2_g"], p["ln2_b"] = ln1_g[:, None], ln1_b[:, None], ln2_g[:, None], ln2_b[:, None]
    n_r = N_GROUPS + N_EXPERTS
    wr = jnp.concatenate([w_router_g, w_router_e, jnp.zeros((depth, d_model, LANES - n_r), F32)], axis=2)
    p["wr"] = wr.astype(BF16)
    p["rb"] = jnp.concatenate([b_router_g, b_router_e, jnp.zeros((depth, LANES - n_r), F32)], axis=1)[:, None]

    def by_group(w):
        w = w.astype(BF16).reshape(depth, N_GROUPS, EXPERTS_PER_GROUP, d_model, D_EXPERT)
        return w.transpose(0, 1, 3, 2, 4).reshape(depth, N_GROUPS, d_model, hid)
    p["w_gu"] = jnp.concatenate([by_group(w_exp_gate), by_group(w_exp_up)], axis=3)
    p["w_dn"] = w_exp_down.astype(BF16).reshape(depth, N_GROUPS, hid, d_model)
    return p


def _kv_layouts(rows):
    b, t_len, _ = rows.shape
    k = rows[:, :, :D_KV // 2].reshape(b, t_len, N_KV_GROUPS, HEAD_DIM).transpose(0, 2, 1, 3)
    return k, rows[:, :, D_KV // 2:].transpose(0, 2, 1)


def _trunk_tail(x, mix_a, mix_m, p, l, alpha, tm):
    x = _out_ln(x, mix_a, mix_m, p["wo_a"][l], p["wo_m"][l], p["ln1_g"][l], p["ln1_b"][l], alpha, tm)
    if x.shape[0] < 8 * MOE_TILE:
        return _moe_ln(x, p["wr"][l], p["rb"][l], p["w_gu"][l], p["w_dn"][l],
                       p["ln2_g"][l], p["ln2_b"][l], alpha, tm)
    grp = _group_ids(x, p["wr"][l], p["rb"][l], tm)[:, 0]
    dest, tile_grp = _moe_dispatch(grp)
    xs = _moe_place(x, dest, tile_grp.shape[0] * MOE_TILE)
    ys = _moe_grouped(xs, tile_grp, p["wr"][l], p["rb"][l], p["w_gu"][l], p["w_dn"][l])
    return _moe_return_ln(x, ys, dest, p["ln2_g"][l], p["ln2_b"][l], alpha)


def kernel(x_prompt, x_sample, mem_prompt, cache_cmp_kv, cache_slc_kv, cache_win_kv, cache_mem_kv, state_lru_h, state_conv, page_table, w_in_a, conv_w, conv_b, w_rg, b_rg, w_ig, b_ig, lru_lambda, w_in_b, w_kv_shared, cmp_pe, cmp_w1, cmp_b1, cmp_w2, rel_bias, w_mem_kv, w_out, ln1_g, ln1_b, ln2_g, ln2_b, w_router_g, b_router_g, w_router_e, b_router_e, w_exp_gate, w_exp_up, w_exp_down):
    p = _prep_weights(w_in_a, conv_w, conv_b, w_rg, b_rg, w_ig, b_ig, lru_lambda, w_in_b, w_kv_shared, cmp_pe,
                      cmp_w1, cmp_b1, cmp_w2, w_mem_kv, w_out, ln1_g, ln1_b, ln2_g, ln2_b, w_router_g,
                      b_router_g, w_router_e, b_router_e, w_exp_gate, w_exp_up, w_exp_down)
    depth, d_model, _ = w_out.shape
    n_a = w_in_a.shape[0]
    alpha = (2 * depth) ** 0.25
    kv_shape = (2, N_KV_GROUPS, HEAD_DIM)
    a_outs = [(0, D_RNN, F32), (D_RNN, D_RNN, F32), (2 * D_RNN, D_MEM, BF16)]
    b_outs = [(0, D_RNN, BF16), (D_RNN, D_MEM, BF16), (D_RNN + D_MEM, LANES, F32)]
    kv_outs = [(0, D_KV, F32), (D_KV, D_KV, F32), (2 * D_KV, D_KV, F32)]

    bp, t_len, _ = x_prompt.shape
    n_mem = mem_prompt.shape[1]
    tm = 512
    mem_rows = _proj(mem_prompt.reshape(bp * n_mem, d_model), p["w_mem"],
                     [(l * 2 * D_MEM, 2 * D_MEM, F32) for l in range(depth)], tm, "mem_kv_proj")
    p_mem = jnp.stack(mem_rows).reshape(depth, bp, n_mem, 2 * D_MEM)
    x = x_prompt.reshape(bp * t_len, d_model)
    lru_p, conv_p = [], []
    for l in range(depth):
        if l < n_a:
            gate, xr, mq = _proj(x, p["w_in_a"][l], a_outs, tm, "in_proj_a")
            mix_a, h_t, buf = _rglru_prompt(gate.reshape(bp, t_len, D_RNN), xr.reshape(bp, t_len, D_RNN),
                                            p["conv_w"][l], p["conv_b"][l], p["w_rg"][l], p["w_ig"][l],
                                            p["b_rg"][l], p["b_ig"][l], p["lam"][l])
            lru_p.append(h_t[:, 0])
            conv_p.append(buf)
        else:
            if l == n_a:
                cmp_rows, slc_rows, win_rows, slc_bf, win_bf = _proj(
                    x, p["w_kv"], kv_outs + [(D_KV, D_KV, BF16), (2 * D_KV, D_KV, BF16)], tm, "kv_proj")
                comp = _compress_prompt(cmp_rows.reshape(bp, t_len, D_KV), p["pe"], p["cmp_w1"], p["cmp_b1"],
                                        p["cmp_w2"])
                comp_k, comp_vt = _kv_layouts(comp.astype(BF16))
                slc_k, slc_vt = _kv_layouts(slc_bf.reshape(bp, t_len, D_KV))
                win_k, win_vt = _kv_layouts(win_bf.reshape(bp, t_len, D_KV))
            q, mq, gl = _proj(x, p["w_in_b"][l - n_a], b_outs, tm, "in_proj_b")
            mix_t = _nsa_prompt(rel_bias, q.reshape(bp, t_len, D_RNN).transpose(0, 2, 1),
                                gl.reshape(bp, t_len, LANES).transpose(0, 2, 1),
                                comp_k, comp_vt, slc_k, slc_vt, win_k, win_vt)
            mix_a = mix_t.transpose(0, 2, 1)
        mix_m = _mem_attn(mq.reshape(bp, t_len, D_MEM), p_mem, l)
        x = _trunk_tail(x, mix_a.reshape(bp * t_len, D_RNN), mix_m.reshape(bp * t_len, D_MEM), p, l, alpha, tm)
    y_prompt = x.reshape(bp, t_len, d_model)
    p_cmp_kv = cmp_rows.reshape((bp, t_len) + kv_shape)
    p_slc_kv = slc_rows.reshape((bp, t_len) + kv_shape)
    w_keep = min(WINDOW, t_len)
    p_win_kv = win_rows.reshape((bp, t_len) + kv_shape)[:, t_len - w_keep:]
    p_mem_kv = p_mem.reshape(depth, bp, n_mem, 2, N_MEM_HEADS, HEAD_DIM)
    p_lru_h = jnp.stack(lru_p)
    p_conv = jnp.stack(conv_p)

    bd, s_len, _ = x_sample.shape
    assert s_len == 1
    n_pool, page = cache_cmp_kv.shape[:2]
    w_buf = cache_win_kv.shape[1]
    mem_t = _feature_major(cache_mem_kv)
    slc_pool_t = _feature_major(cache_slc_kv)
    win_t = _feature_major(cache_win_kv)
    x = x_sample.reshape(bd, d_model)
    lru_s, conv_s = [], []
    for l in range(depth):
        if l < n_a:
            gate, xr, mq = _proj(x, p["w_in_a"][l], a_outs, bd, "in_proj_a_step")
            mix_a, h_t, buf = _rglru_step(gate, xr, state_conv[l].transpose(1, 0, 2), state_lru_h[l],
                                          p["conv_w"][l], p["conv_b"][l], p["w_rg"][l], p["w_ig"][l],
                                          p["b_rg"][l], p["b_ig"][l], p["lam"][l])
            lru_s.append(h_t)
            conv_s.append(buf.transpose(1, 0, 2))
        else:
            if l == n_a:
                s_cmp, s_slc, s_win = _proj(x, p["w_kv"], kv_outs, bd, "kv_proj_step")
                comp_s = _compress_pages(page_table, _feature_major(cache_cmp_kv), p["pe"],
                                         p["cmp_w1"], p["cmp_b1"], p["cmp_w2"])
            q, mq, gl = _proj(x, p["w_in_b"][l - n_a], b_outs, bd, "in_proj_b_step")
            mix_a = _nsa_step(rel_bias, page_table, q.reshape(bd, 1, D_RNN), gl.reshape(bd, 1, LANES), comp_s,
                              slc_pool_t, s_slc.reshape(bd, 1, D_KV), win_t, s_win.reshape(bd, 1, D_KV))
            mix_a = mix_a.reshape(bd, D_RNN)
        mix_m = _mem_attn_step(mq.reshape(bd, 1, D_MEM), mem_t, l).reshape(bd, D_MEM)
        x = _trunk_tail(x, mix_a, mix_m, p, l, alpha, bd)
    y_sample = x.reshape(bd, 1, d_model)
    s_cmp_kv = s_cmp.reshape((bd, 1) + kv_shape)
    s_slc_kv = s_slc.reshape((bd, 1) + kv_shape)
    s_win_kv = jnp.concatenate([cache_win_kv, s_win.reshape((bd, 1) + kv_shape)], axis=1)[:, 1:]
    s_lru_h = jnp.stack(lru_s)
    s_conv = jnp.stack(conv_s)
    return (y_prompt, y_sample, p_cmp_kv, p_slc_kv, p_win_kv, p_mem_kv, p_lru_h, p_conv,
            s_cmp_kv, s_slc_kv, s_win_kv, s_lru_h, s_conv)
```

```python
import functools
import math

import jax
import jax.numpy as jnp
from jax import lax
from jax.experimental import pallas as pl
from jax.experimental.pallas import tpu as pltpu

F32 = jnp.float32
BF16 = jnp.bfloat16
I32 = jnp.int32

HEAD_DIM = 64
N_MEM_HEADS = 4
D_MEM = N_MEM_HEADS * HEAD_DIM
N_HEADS = 12
D_RNN = N_HEADS * HEAD_DIM
N_KV_GROUPS = 4
HEADS_PER_GROUP = N_HEADS // N_KV_GROUPS
D_KV = 2 * N_KV_GROUPS * HEAD_DIM
CONV_W = 4
LRU_C = 8.0
CMP_LEN = 32
CMP_STRIDE = 16
CMP_HID = 2 * HEAD_DIM
SEL_LEN = 64
N_SEL = 16
WINDOW = 512
N_BUCKETS = 32
MAX_DISTANCE = 1024
N_GROUPS = 4
EXPERTS_PER_GROUP = 8
N_EXPERTS = N_GROUPS * EXPERTS_PER_GROUP
D_EXPERT = 128
Q_BLOCK = 128
LN_EPS = 1e-5
NEG = -1e30
FORCE = 1e9
SCALE = HEAD_DIM ** -0.5

LANES = 128
SUBLANES = 8
VMEM_LIMIT_BYTES = 56 * 1024 * 1024


def _cparams(n_axes):
    return pltpu.CompilerParams(dimension_semantics=("arbitrary",) * n_axes,
                                vmem_limit_bytes=VMEM_LIMIT_BYTES)


def _dot(a, b):
    return jnp.dot(a, b, preferred_element_type=F32)


def _dot_nt(a, b):
    return lax.dot_general(a, b, (((1,), (1,)), ((), ())), preferred_element_type=F32)


def _layer_norm(v, g, b):
    mu = jnp.mean(v, axis=-1, keepdims=True)
    d = v - mu
    var = jnp.mean(d * d, axis=-1, keepdims=True)
    return d * lax.rsqrt(var + LN_EPS) * g + b


def _rel_bucket(dist):
    n = jnp.maximum(dist, 0)
    max_exact = N_BUCKETS // 2
    nf = jnp.maximum(n, 1).astype(F32)
    large = max_exact + (jnp.log(nf / max_exact) / math.log(MAX_DISTANCE / max_exact)
                         * (N_BUCKETS - max_exact)).astype(I32)
    large = jnp.minimum(large, N_BUCKETS - 1)
    return jnp.where(n < max_exact, n, large)


def _bias_from_bucket(bucket, tab_ref, heads):
    masks = [bucket == k for k in range(1, N_BUCKETS)]
    out = []
    for h in heads:
        acc = jnp.full(bucket.shape, tab_ref[0, h], F32)
        for k in range(1, N_BUCKETS):
            acc = jnp.where(masks[k - 1], tab_ref[k, h], acc)
        out.append(acc)
    return out


def _proj_kernel(x_ref, w_ref, *o_refs, cols):
    y = _dot(x_ref[...].astype(BF16), w_ref[...])
    for o_ref, (off, n) in zip(o_refs, cols):
        o_ref[...] = y[:, off:off + n].astype(o_ref.dtype)


def _proj(x, w, outs, tm, name):
    m, k = x.shape
    n = w.shape[1]
    tm = min(tm, m)
    assert m % tm == 0 and all(off + wd <= n for off, wd, _ in outs)
    return pl.pallas_call(
        functools.partial(_proj_kernel, cols=tuple((off, wd) for off, wd, _ in outs)),
        grid=(m // tm,),
        in_specs=[pl.BlockSpec((tm, k), lambda i: (i, 0)),
                  pl.BlockSpec((k, n), lambda i: (0, 0))],
        out_specs=[pl.BlockSpec((tm, wd), lambda i: (i, 0)) for _, wd, _ in outs],
        out_shape=[jax.ShapeDtypeStruct((m, wd), dt) for _, wd, dt in outs],
        compiler_params=_cparams(1),
        name=name,
    )(x, w)


def _softplus(z):
    return jnp.maximum(z, 0.0) + jnp.log1p(jnp.exp(-jnp.abs(z)))


def _lru_gates(xc, wr_ref, wi_ref, br, bi, lam):
    xcb = xc.astype(BF16)
    nb = D_RNN // LANES
    r_l = jnp.concatenate([_dot(xcb[:, j * LANES:(j + 1) * LANES], wr_ref[j]) for j in range(nb)], axis=1)
    i_l = jnp.concatenate([_dot(xcb[:, j * LANES:(j + 1) * LANES], wi_ref[j]) for j in range(nb)], axis=1)
    r = jax.nn.sigmoid(r_l + br)
    i = jax.nn.sigmoid(i_l + bi)
    log_a = -LRU_C * r * _softplus(-lam)
    a = jnp.exp(log_a)
    u = jnp.sqrt(1.0 - a * a) * i * xc
    return a, u


def _rglru_prompt_kernel(gate_ref, xr_ref, cw_ref, cb_ref, wr_ref, wi_ref, br_ref, bi_ref, lam_ref,
                         y_ref, ht_ref, buf_ref, xp_ref, *, t_len, chunk):
    d = D_RNN
    pad = SUBLANES
    xp_ref[0:pad, :] = jnp.zeros((pad, d), F32)
    xp_ref[pad:pad + t_len, :] = xr_ref[0]
    buf_ref[0] = xr_ref[0, t_len - (CONV_W - 1):t_len, :]
    cw = cw_ref[...]
    cb = cb_ref[...]
    br = br_ref[...]
    bi = bi_ref[...]
    lam = lam_ref[...]
    row_in_tile = lax.broadcasted_iota(I32, (chunk, d), 0) & (SUBLANES - 1)
    h = jnp.zeros((1, d), F32)
    for c in range(t_len // chunk):
        base = c * chunk
        xc = cb
        for k in range(CONV_W):
            lo = pad - (CONV_W - 1) + k + base
            xc = xc + xp_ref[lo:lo + chunk, :] * cw[k:k + 1, :]
        a, u = _lru_gates(xc, wr_ref, wi_ref, br, bi, lam)
        for k in (1, 2, 4):
            a_s = pltpu.roll(a, k, axis=0)
            u_s = pltpu.roll(u, k, axis=0)
            m = row_in_tile >= k
            u = jnp.where(m, a * u_s + u, u)
            a = jnp.where(m, a * a_s, a)
        hs = []
        for j in range(chunk // SUBLANES):
            h_t = a[j * SUBLANES:(j + 1) * SUBLANES] * h + u[j * SUBLANES:(j + 1) * SUBLANES]
            h = h_t[SUBLANES - 1:SUBLANES]
            hs.append(h_t)
        hfull = jnp.concatenate(hs, axis=0)
        g = jax.nn.gelu(gate_ref[0, base:base + chunk, :])
        y_ref[0, base:base + chunk, :] = (g * hfull).astype(y_ref.dtype)
    ht_ref[0] = h


def _rglru_prompt(gate, xr, cw, cb, wr_bd, wi_bd, br, bi, lam):
    b, t_len, d = xr.shape
    chunk = min(256, t_len)
    full = lambda shape: pl.BlockSpec(shape, lambda i: (0,) * len(shape))
    return pl.pallas_call(
        functools.partial(_rglru_prompt_kernel, t_len=t_len, chunk=chunk),
        grid=(b,),
        in_specs=[pl.BlockSpec((1, t_len, d), lambda i: (i, 0, 0)),
                  pl.BlockSpec((1, t_len, d), lambda i: (i, 0, 0)),
                  full((CONV_W, d)), full((1, d)), full(wr_bd.shape), full(wi_bd.shape),
                  full((1, d)), full((1, d)), full((1, d))],
        out_specs=[pl.BlockSpec((1, t_len, d), lambda i: (i, 0, 0)),
                   pl.BlockSpec((1, 1, d), lambda i: (i, 0, 0)),
                   pl.BlockSpec((1, CONV_W - 1, d), lambda i: (i, 0, 0))],
        out_shape=[jax.ShapeDtypeStruct((b, t_len, d), BF16),
                   jax.ShapeDtypeStruct((b, 1, d), F32),
                   jax.ShapeDtypeStruct((b, CONV_W - 1, d), F32)],
        scratch_shapes=[pltpu.VMEM((t_len + SUBLANES, d), F32)],
        compiler_params=_cparams(1),
        name="rglru_prompt",
    )(gate, xr, cw, cb, wr_bd, wi_bd, br, bi, lam)


def _rglru_step_kernel(gate_ref, xr_ref, conv_ref, h0_ref, cw_ref, cb_ref, wr_ref, wi_ref, br_ref, bi_ref,
                       lam_ref, y_ref, ht_ref, buf_ref):
    xr = xr_ref[...]
    cw = cw_ref[...]
    xc = cb_ref[...] + xr * cw[CONV_W - 1:CONV_W, :]
    for k in range(CONV_W - 1):
        xc = xc + conv_ref[k] * cw[k:k + 1, :]
    a, u = _lru_gates(xc, wr_ref, wi_ref, br_ref[...], bi_ref[...], lam_ref[...])
    h = a * h0_ref[...] + u
    ht_ref[...] = h
    y_ref[...] = (jax.nn.gelu(gate_ref[...]) * h).astype(y_ref.dtype)
    for k in range(CONV_W - 2):
        buf_ref[k] = conv_ref[k + 1]
    buf_ref[CONV_W - 2] = xr


def _rglru_step(gate, xr, conv_t, h0, cw, cb, wr_bd, wi_bd, br, bi, lam):
    b, d = xr.shape
    return pl.pallas_call(
        _rglru_step_kernel,
        out_shape=[jax.ShapeDtypeStruct((b, d), BF16),
                   jax.ShapeDtypeStruct((b, d), F32),
                   jax.ShapeDtypeStruct((CONV_W - 1, b, d), F32)],
        compiler_params=pltpu.CompilerParams(vmem_limit_bytes=VMEM_LIMIT_BYTES),
        name="rglru_step",
    )(gate, xr, conv_t, h0, cw, cb, wr_bd, wi_bd, br, bi, lam)


def _mem_attn_kernel(q_ref, kv_ref, o_ref, *, rows):
    q = q_ref[0]
    if rows < SUBLANES:
        q = jnp.broadcast_to(q, (SUBLANES, D_MEM))
    kv = kv_ref[0, 0].astype(BF16)
    outs = []
    for h in range(N_MEM_HEADS):
        qh = q[:, h * HEAD_DIM:(h + 1) * HEAD_DIM]
        kh = kv[:, h * HEAD_DIM:(h + 1) * HEAD_DIM]
        vh = kv[:, D_MEM + h * HEAD_DIM:D_MEM + (h + 1) * HEAD_DIM]
        s = _dot_nt(qh, kh) * SCALE
        m = jnp.max(s, axis=-1, keepdims=True)
        e = jnp.exp(s - m)
        p = e / jnp.sum(e, axis=-1, keepdims=True)
        outs.append(_dot(p.astype(BF16), vh))
    o = jnp.concatenate(outs, axis=1)
    o_ref[0] = o[:rows].astype(o_ref.dtype)


def _mem_attn(q, mem_kv, layer):
    b, t_len, _ = q.shape
    n_mem = mem_kv.shape[2]
    tq = min(t_len, 512)
    return pl.pallas_call(
        functools.partial(_mem_attn_kernel, rows=tq),
        grid=(b, t_len // tq),
        in_specs=[pl.BlockSpec((1, tq, D_MEM), lambda i, j: (i, j, 0)),
                  pl.BlockSpec((1, 1, n_mem, 2 * D_MEM), lambda i, j: (layer, i, 0, 0))],
        out_specs=pl.BlockSpec((1, tq, D_MEM), lambda i, j: (i, j, 0)),
        out_shape=jax.ShapeDtypeStruct((b, t_len, D_MEM), BF16),
        compiler_params=_cparams(2),
        name="mem_attn",
    )(q, mem_kv)


def _mem_attn_step_kernel(q_ref, kv_ref, o_ref):
    q_row = q_ref[0] * SCALE
    rows = []
    for h in range(N_MEM_HEADS):
        parts = []
        if h > 0:
            parts.append(jnp.zeros((1, h * HEAD_DIM), BF16))
        parts.append(q_row[:, h * HEAD_DIM:(h + 1) * HEAD_DIM])
        if h < N_MEM_HEADS - 1:
            parts.append(jnp.zeros((1, (N_MEM_HEADS - 1 - h) * HEAD_DIM), BF16))
        rows.append(jnp.concatenate(parts, axis=1))
    n_rows = 2 * SUBLANES
    q_blk = jnp.concatenate(rows + [jnp.zeros((n_rows - N_MEM_HEADS, D_MEM), BF16)], axis=0)
    s = _dot(q_blk, kv_ref[0, 0, 0].astype(BF16))
    e = jnp.exp(s - jnp.max(s, axis=-1, keepdims=True))
    p = e / jnp.sum(e, axis=-1, keepdims=True)
    o_all = _dot_nt(p.astype(BF16), kv_ref[0, 0, 1].astype(BF16))
    o = jnp.concatenate([o_all[h:h + 1, h * HEAD_DIM:(h + 1) * HEAD_DIM] for h in range(N_MEM_HEADS)], axis=1)
    o_ref[0] = o.astype(o_ref.dtype)


def _mem_attn_step(q, mem_t, layer):
    b = q.shape[0]
    n_mem = mem_t.shape[4]
    return pl.pallas_call(
        _mem_attn_step_kernel,
        grid=(b,),
        in_specs=[pl.BlockSpec((1, 1, D_MEM), lambda i: (i, 0, 0)),
                  pl.BlockSpec((1, 1, 2, D_MEM, n_mem), lambda i: (layer, i, 0, 0, 0))],
        out_specs=pl.BlockSpec((1, 1, D_MEM), lambda i: (i, 0, 0)),
        out_shape=jax.ShapeDtypeStruct((b, 1, D_MEM), BF16),
        compiler_params=_cparams(1),
        name="mem_attn_step",
    )(q, mem_t)


def _out_ln_kernel(x_ref, ma_ref, mm_ref, wa_ref, wm_ref, g_ref, b_ref, wr_ref, rb_ref, o_ref, grp_ref, *, alpha):
    sub = _dot(ma_ref[...], wa_ref[...]) + _dot(mm_ref[...], wm_ref[...])
    y = _layer_norm(alpha * x_ref[...] + sub, g_ref[...], b_ref[...])
    o_ref[...] = y
    grp, _ = _route_group(_dot(y.astype(BF16), wr_ref[...]) + rb_ref[...])
    grp_ref[...] = jnp.broadcast_to(grp, grp_ref.shape)


def _out_ln(x, mix_a, mix_m, wo_a, wo_m, g, b, wr, rb, alpha, tm):
    m, d = x.shape
    da = mix_a.shape[1]
    dm = mix_m.shape[1]
    return pl.pallas_call(
        functools.partial(_out_ln_kernel, alpha=alpha),
        grid=(m // tm,),
        in_specs=[pl.BlockSpec((tm, d), lambda i: (i, 0)),
                  pl.BlockSpec((tm, da), lambda i: (i, 0)),
                  pl.BlockSpec((tm, dm), lambda i: (i, 0)),
                  pl.BlockSpec((da, d), lambda i: (0, 0)),
                  pl.BlockSpec((dm, d), lambda i: (0, 0)),
                  pl.BlockSpec((1, d), lambda i: (0, 0)),
                  pl.BlockSpec((1, d), lambda i: (0, 0)),
                  pl.BlockSpec((d, LANES), lambda i: (0, 0)),
                  pl.BlockSpec((1, LANES), lambda i: (0, 0))],
        out_specs=[pl.BlockSpec((tm, d), lambda i: (i, 0)), pl.BlockSpec((tm, LANES), lambda i: (i, 0))],
        out_shape=[jax.ShapeDtypeStruct((m, d), F32), jax.ShapeDtypeStruct((m, LANES), I32)],
        compiler_params=_cparams(1),
        name="out_ln",
    )(x, mix_a, mix_m, wo_a, wo_m, g, b, wr, rb)


def _route_group(logits):
    lane = lax.broadcasted_iota(I32, logits.shape, 1)
    lg = jnp.where(lane < N_GROUPS, logits, -jnp.inf)
    gmax = jnp.max(lg, axis=-1, keepdims=True)
    return jnp.min(jnp.where(lg == gmax, lane, jnp.int32(4 * LANES)), axis=-1, keepdims=True), gmax


def _route(logits):
    lane = lax.broadcasted_iota(I32, logits.shape, 1)
    big = jnp.int32(4 * LANES)
    is_g = lane < N_GROUPS
    grp, gmax = _route_group(logits)
    gsum = jnp.sum(jnp.where(is_g, jnp.exp(logits - gmax), 0.0), axis=-1, keepdims=True)
    g_w = 1.0 / gsum
    e_idx = lane - N_GROUPS
    in_grp = (lane >= N_GROUPS) & (lane < N_GROUPS + N_EXPERTS) & ((e_idx >> 3) == grp)
    v = jnp.where(in_grp, logits, -jnp.inf)
    v1 = jnp.max(v, axis=-1, keepdims=True)
    i1 = jnp.min(jnp.where(v == v1, lane, big), axis=-1, keepdims=True)
    vr = jnp.where(lane == i1, -jnp.inf, v)
    v2 = jnp.max(vr, axis=-1, keepdims=True)
    i2 = jnp.min(jnp.where(vr == v2, lane, big), axis=-1, keepdims=True)
    e2 = jnp.exp(v2 - v1)
    w1 = g_w / (1.0 + e2)
    w2 = g_w * e2 / (1.0 + e2)
    return jnp.where(lane == i1, w1, 0.0) + jnp.where(lane == i2, w2, 0.0)


def _moe_kernel(x_ref, wr_ref, rb_ref, wgu_ref, wd_ref, g_ref, b_ref, o_ref,
                xb_ref, comb_ref, acc_ref, *, alpha):
    grp = pl.program_id(1)
    hid = EXPERTS_PER_GROUP * D_EXPERT

    @pl.when(grp == 0)
    def _():
        xh = x_ref[...].astype(BF16)
        logits = _dot(xh, wr_ref[...]) + rb_ref[...]
        xb_ref[...] = xh
        comb_ref[...] = _route(logits)
        acc_ref[...] = jnp.zeros(acc_ref.shape, F32)

    xb = xb_ref[...]
    gu = _dot(xb, wgu_ref[0])
    gate = gu[:, :hid]
    up = gu[:, hid:]
    hdn = gate * jax.nn.sigmoid(gate) * up
    comb = comb_ref[...]
    lane = lax.broadcasted_iota(I32, comb.shape, 1)
    scale = []
    for e in range(EXPERTS_PER_GROUP):
        col = jnp.sum(jnp.where(lane == N_GROUPS + grp * EXPERTS_PER_GROUP + e, comb, 0.0),
                      axis=-1, keepdims=True)
        scale.append(jnp.broadcast_to(col, (comb.shape[0], D_EXPERT)))
    hdn = hdn * jnp.concatenate(scale, axis=1)
    acc_ref[...] += _dot(hdn.astype(BF16), wd_ref[0])

    @pl.when(grp == N_GROUPS - 1)
    def _():
        o_ref[...] = _layer_norm(alpha * x_ref[...] + acc_ref[...], g_ref[...], b_ref[...])


def _moe_ln(x, wr, rb, wgu, wd, g, b, alpha, tm):
    m, d = x.shape
    hid = EXPERTS_PER_GROUP * D_EXPERT
    return pl.pallas_call(
        functools.partial(_moe_kernel, alpha=alpha),
        grid=(m // tm, N_GROUPS),
        in_specs=[pl.BlockSpec((tm, d), lambda i, j: (i, 0)),
                  pl.BlockSpec((d, LANES), lambda i, j: (0, 0)),
                  pl.BlockSpec((1, LANES), lambda i, j: (0, 0)),
                  pl.BlockSpec((1, d, 2 * hid), lambda i, j: (j, 0, 0)),
                  pl.BlockSpec((1, hid, d), lambda i, j: (j, 0, 0)),
                  pl.BlockSpec((1, d), lambda i, j: (0, 0)),
                  pl.BlockSpec((1, d), lambda i, j: (0, 0))],
        out_specs=pl.BlockSpec((tm, d), lambda i, j: (i, 0)),
        out_shape=jax.ShapeDtypeStruct((m, d), F32),
        scratch_shapes=[pltpu.VMEM((tm, d), BF16), pltpu.VMEM((tm, LANES), F32), pltpu.VMEM((tm, d), F32)],
        compiler_params=_cparams(2),
        name="moe_ln",
    )(x, wr, rb, wgu, wd, g, b)


MOE_TILE = 256


def _moe_dispatch(grp):
    n = grp.shape[0]
    onehot = (grp[:, None] == jnp.arange(N_GROUPS, dtype=I32)[None, :]).astype(I32)
    before = jnp.cumsum(onehot, axis=0) - onehot
    counts = jnp.sum(onehot, axis=0)
    padded = (counts + MOE_TILE - 1) // MOE_TILE * MOE_TILE
    ends = jnp.cumsum(padded)
    dest = jnp.sum(onehot * (before + (ends - padded)[None, :]), axis=1).astype(I32)
    n_slots = n + N_GROUPS * MOE_TILE
    tile_start = jnp.arange(n_slots // MOE_TILE, dtype=I32) * MOE_TILE
    tile_grp = jnp.minimum(jnp.sum((tile_start[:, None] >= ends[None, :]).astype(I32), axis=1), N_GROUPS - 1)
    return dest, tile_grp


def _moe_place_kernel(dest_ref, x_ref, zero_hbm, o_hbm, stage, sem, *, n_tiles):
    del zero_hbm
    i = pl.program_id(0)
    slot = i % 2

    def wait_tile(buf):
        pltpu.make_async_copy(stage.at[buf], o_hbm.at[pl.ds(0, MOE_TILE)], sem.at[buf]).wait()

    @pl.when(i >= 2)
    def _():
        wait_tile(slot)

    stage[slot] = x_ref[...]
    for r in range(MOE_TILE):
        pltpu.make_async_copy(stage.at[slot, pl.ds(r, 1)], o_hbm.at[pl.ds(dest_ref[i * MOE_TILE + r], 1)],
                              sem.at[slot]).start()

    @pl.when(i == n_tiles - 1)
    def _():
        @pl.when(i >= 1)
        def _():
            wait_tile(1 - slot)
        wait_tile(slot)


def _moe_place(x, dest, n_slots):
    n, d = x.shape
    assert n % MOE_TILE == 0
    n_tiles = n // MOE_TILE
    grid_spec = pltpu.PrefetchScalarGridSpec(
        num_scalar_prefetch=1,
        grid=(n_tiles,),
        in_specs=[pl.BlockSpec((MOE_TILE, d), lambda i, ds: (i, 0)), pl.BlockSpec(memory_space=pl.ANY)],
        out_specs=pl.BlockSpec(memory_space=pl.ANY),
        scratch_shapes=[pltpu.VMEM((2, MOE_TILE, d), F32), pltpu.SemaphoreType.DMA((2,))])
    return pl.pallas_call(
        functools.partial(_moe_place_kernel, n_tiles=n_tiles),
        grid_spec=grid_spec,
        out_shape=jax.ShapeDtypeStruct((n_slots, d), F32),
        input_output_aliases={2: 0},
        compiler_params=_cparams(1),
        name="moe_place",
    )(dest, x, jnp.zeros((n_slots, d), F32))


def _moe_grouped_kernel(tgrp_ref, x_ref, wr_ref, rb_ref, wgu_ref, wd_ref, y_ref):
    i = pl.program_id(0)
    hid = EXPERTS_PER_GROUP * D_EXPERT
    xh = x_ref[...].astype(BF16)
    comb = _route(_dot(xh, wr_ref[...]) + rb_ref[...])
    gu = _dot(xh, wgu_ref[0])
    gate = gu[:, :hid]
    hdn = gate * jax.nn.sigmoid(gate) * gu[:, hid:]
    grp = tgrp_ref[i]
    lane = lax.broadcasted_iota(I32, comb.shape, 1)
    scale = []
    for e in range(EXPERTS_PER_GROUP):
        col = jnp.sum(jnp.where(lane == N_GROUPS + grp * EXPERTS_PER_GROUP + e, comb, 0.0),
                      axis=-1, keepdims=True)
        scale.append(jnp.broadcast_to(col, (comb.shape[0], D_EXPERT)))
    y_ref[...] = _dot((hdn * jnp.concatenate(scale, axis=1)).astype(BF16), wd_ref[0])


def _moe_grouped(xs, tile_grp, wr, rb, wgu, wd):
    n_slots, d = xs.shape
    hid = EXPERTS_PER_GROUP * D_EXPERT
    grid_spec = pltpu.PrefetchScalarGridSpec(
        num_scalar_prefetch=1,
        grid=(n_slots // MOE_TILE,),
        in_specs=[pl.BlockSpec((MOE_TILE, d), lambda i, tg: (i, 0)),
                  pl.BlockSpec((d, LANES), lambda i, tg: (0, 0)),
                  pl.BlockSpec((1, LANES), lambda i, tg: (0, 0)),
                  pl.BlockSpec((1, d, 2 * hid), lambda i, tg: (tg[i], 0, 0)),
                  pl.BlockSpec((1, hid, d), lambda i, tg: (tg[i], 0, 0))],
        out_specs=pl.BlockSpec((MOE_TILE, d), lambda i, tg: (i, 0)))
    return pl.pallas_call(
        _moe_grouped_kernel,
        grid_spec=grid_spec,
        out_shape=jax.ShapeDtypeStruct((n_slots, d), F32),
        compiler_params=_cparams(1),
        name="moe_grouped",
    )(tile_grp, xs, wr, rb, wgu, wd)


def _moe_return_ln_kernel(dest_ref, x_ref, y_hbm, g_ref, b_ref, o_ref, ybuf, sem, *, alpha, n_tiles):
    i = pl.program_id(0)
    slot = i % 2

    def start_tile(tile, buf):
        for r in range(MOE_TILE):
            pltpu.make_async_copy(y_hbm.at[pl.ds(dest_ref[tile * MOE_TILE + r], 1)],
                                  ybuf.at[buf, pl.ds(r, 1)], sem.at[buf]).start()

    @pl.when(i == 0)
    def _():
        start_tile(i, slot)

    @pl.when(i + 1 < n_tiles)
    def _():
        start_tile(i + 1, 1 - slot)

    pltpu.make_async_copy(y_hbm.at[pl.ds(0, MOE_TILE)], ybuf.at[slot], sem.at[slot]).wait()
    o_ref[...] = _layer_norm(alpha * x_ref[...] + ybuf[slot], g_ref[...], b_ref[...])


def _moe_return_ln(x, ys, dest, g, b, alpha):
    n, d = x.shape
    n_tiles = n // MOE_TILE
    grid_spec = pltpu.PrefetchScalarGridSpec(
        num_scalar_prefetch=1,
        grid=(n_tiles,),
        in_specs=[pl.BlockSpec((MOE_TILE, d), lambda i, ds: (i, 0)),
                  pl.BlockSpec(memory_space=pl.ANY),
                  pl.BlockSpec((1, d), lambda i, ds: (0, 0)),
                  pl.BlockSpec((1, d), lambda i, ds: (0, 0))],
        out_specs=pl.BlockSpec((MOE_TILE, d), lambda i, ds: (i, 0)),
        scratch_shapes=[pltpu.VMEM((2, MOE_TILE, d), F32), pltpu.SemaphoreType.DMA((2,))])
    return pl.pallas_call(
        functools.partial(_moe_return_ln_kernel, alpha=alpha, n_tiles=n_tiles),
        grid_spec=grid_spec,
        out_shape=jax.ShapeDtypeStruct((n, d), F32),
        compiler_params=_cparams(1),
        name="moe_return_ln",
    )(dest, x, ys, g, b)


CMP_MBLK = 128
ROW_TILES = D_KV // LANES
PAIR_HID = 2 * CMP_HID


def _compress_weights(pe, w1, b1, w2):
    w1 = w1.reshape(2, CMP_LEN, HEAD_DIM, CMP_HID)
    z1 = jnp.zeros_like(w1)
    w1p = jnp.concatenate([jnp.concatenate([w1, z1], axis=3), jnp.concatenate([z1, w1], axis=3)], axis=2)
    z2 = jnp.zeros_like(w2)
    w2p = jnp.concatenate([jnp.concatenate([w2, z2], axis=2), jnp.concatenate([z2, w2], axis=2)], axis=1)
    w1p = w1p.astype(BF16).reshape(2, 2, CMP_STRIDE * LANES, PAIR_HID)
    return (jnp.concatenate([pe, pe], axis=2), w1p,
            jnp.concatenate([b1, b1], axis=1)[:, None], w2p.astype(BF16))


def _load_interleaved(x_ref):
    step = CMP_STRIDE * ROW_TILES

    def load(tile, r, mb):
        row0 = pl.multiple_of(mb * (CMP_MBLK * step), CMP_MBLK * step)
        return x_ref[pl.ds(row0 + r * ROW_TILES + tile, CMP_MBLK, stride=step), :]
    return load


def _load_tiled(x_ref):
    def load(tile, r, mb):
        row0 = pl.multiple_of(mb * (CMP_MBLK * CMP_STRIDE), CMP_MBLK * CMP_STRIDE)
        return x_ref[tile, pl.ds(row0 + r, CMP_MBLK, stride=CMP_STRIDE), :]
    return load


def _compress_rows(load, n_chunk, pe_ref, w1_ref, b1_ref, w2_ref, out_ref, a_ref, b_ref):
    n_blk = n_chunk // CMP_MBLK
    for kv in range(2):
        for pair in range(N_KV_GROUPS // 2):
            tile = kv * (N_KV_GROUPS // 2) + pair

            def fill(mb, carry, kv=kv, tile=tile):
                top, bot = [], []
                for r in range(CMP_STRIDE):
                    xg = load(tile, r, mb)
                    top.append((xg + pe_ref[kv, r:r + 1, :]).astype(BF16))
                    bot.append((xg + pe_ref[kv, CMP_STRIDE + r:CMP_STRIDE + r + 1, :]).astype(BF16))
                m0 = pl.multiple_of(mb * CMP_MBLK, CMP_MBLK)
                a_ref[pl.ds(m0, CMP_MBLK), :] = _dot(jnp.concatenate(top, axis=1), w1_ref[kv, 0])
                b_ref[pl.ds(m0, CMP_MBLK), :] = _dot(jnp.concatenate(bot, axis=1), w1_ref[kv, 1])
                return carry

            lax.fori_loop(0, n_blk, fill, 0)
            b_ref[n_chunk:n_chunk + SUBLANES, :] = jnp.zeros((SUBLANES, PAIR_HID), F32)
            for mb in range(n_blk):
                m0 = mb * CMP_MBLK
                hid = jax.nn.gelu(a_ref[m0:m0 + CMP_MBLK, :] + b_ref[m0 + 1:m0 + 1 + CMP_MBLK, :] + b1_ref[kv])
                out_ref[m0:m0 + CMP_MBLK, tile * LANES:(tile + 1) * LANES] = _dot(hid.astype(BF16), w2_ref[kv])


def _compress_prompt_kernel(x_ref, pe_ref, w1_ref, b1_ref, w2_ref, o_ref, a_ref, b_ref, *, n_chunk):
    _compress_rows(_load_interleaved(x_ref.at[0]), n_chunk, pe_ref, w1_ref, b1_ref, w2_ref, o_ref.at[0],
                   a_ref, b_ref)


def _compress_prompt(rows, pe, w1, b1, w2):
    b, t_len, _ = rows.shape
    n_chunk = t_len // CMP_STRIDE
    assert n_chunk % CMP_MBLK == 0
    full = lambda a: pl.BlockSpec(a.shape, lambda i: (0,) * a.ndim)
    return pl.pallas_call(
        functools.partial(_compress_prompt_kernel, n_chunk=n_chunk),
        grid=(b,),
        in_specs=[pl.BlockSpec((1, t_len * ROW_TILES, LANES), lambda i: (i, 0, 0)),
                  full(pe), full(w1), full(b1), full(w2)],
        out_specs=pl.BlockSpec((1, n_chunk, D_KV), lambda i: (i, 0, 0)),
        out_shape=jax.ShapeDtypeStruct((b, n_chunk, D_KV), F32),
        scratch_shapes=[pltpu.VMEM((n_chunk, PAIR_HID), F32), pltpu.VMEM((n_chunk + SUBLANES, PAIR_HID), F32)],
        compiler_params=_cparams(1),
        name="compress_prompt",
    )(rows.reshape(b, t_len * ROW_TILES, LANES), pe, w1, b1, w2)


def _masked_softmax_rows(s, mask):
    sm = jnp.where(mask, s, NEG)
    m = jnp.max(sm, axis=-1, keepdims=True)
    e = jnp.where(mask, jnp.exp(sm - m), 0.0)
    tot = jnp.sum(e, axis=-1, keepdims=True)
    return e * jnp.where(tot > 0.0, 1.0 / tot, 0.0)


WIN_TILES = WINDOW // Q_BLOCK


KEY_TILE = 512
KEY_SUB = KEY_TILE // Q_BLOCK


def _attend_tiles_t(qts, k_ref, vt_ref, t_lo, t_hi, i_blk, bias_ref, bias_index, mask_fn):
    cols = qts[0].shape[1]

    def body(t, carry):
        k0 = pl.multiple_of(t * KEY_TILE, KEY_TILE)
        bases = [bias_index(i_blk - (t * KEY_SUB + u)) for u in range(KEY_SUB)]
        scores = [_dot(k_ref[0, g, pl.ds(k0, KEY_TILE), :], qt) for g, qt in enumerate(qts)]
        new = []
        for g in range(len(qts)):
            m, l, acc = carry[g]
            vt = vt_ref[0, g * HEAD_DIM:(g + 1) * HEAD_DIM, pl.ds(k0, KEY_TILE)]
            heads = [g * HEADS_PER_GROUP + hp for hp in range(HEADS_PER_GROUP)]
            bias = jnp.concatenate(
                [jnp.concatenate([bias_ref[base + h] for h in heads], axis=1) for base in bases], axis=0)
            s = scores[g] + bias
            extra = mask_fn(t, g)
            if extra is not None:
                s = s + jnp.concatenate([extra] * HEADS_PER_GROUP, axis=1)
            m_new = jnp.maximum(m, jnp.max(s, axis=0, keepdims=True))
            alpha = jnp.exp(m - m_new)
            p = jnp.exp(s - m_new)
            new.append((m_new, alpha * l + jnp.sum(p, axis=0, keepdims=True), alpha * acc + _dot(vt, p.astype(BF16))))
        return tuple(new)

    init = tuple((jnp.full((1, cols), NEG, F32), jnp.zeros((1, cols), F32), jnp.zeros((HEAD_DIM, cols), F32))
                 for _ in qts)
    fin = lax.fori_loop(t_lo, t_hi, body, init)
    return [acc / l for (_, l, acc) in fin]


def _nsa_prompt_kernel(tab_ref, q_ref, gl_ref, ck_ref, cvt_ref, sk_ref, svt_ref, wk_ref, wvt_ref, o_ref,
                       bias_ref, cbias_ref, sel_ref, *, n_tiles):
    b = pl.program_id(0)
    i = pl.program_id(1)
    kk = lax.broadcasted_iota(I32, (Q_BLOCK, Q_BLOCK), 0)
    qq = lax.broadcasted_iota(I32, (Q_BLOCK, Q_BLOCK), 1)
    n_cmp = ck_ref.shape[2]
    n_sel_blocks = 2 * n_tiles
    edge_base = n_tiles * N_HEADS
    none_base = (n_tiles + 1) * N_HEADS

    @pl.when((b == 0) & (i == 0))
    def _():
        def build(delta, carry):
            dist = delta * Q_BLOCK + qq - kk
            biases = _bias_from_bucket(_rel_bucket(dist), tab_ref, range(N_HEADS))
            for h in range(N_HEADS):
                bias_ref[delta * N_HEADS + h] = jnp.where(dist >= 0, biases[h], NEG)
            return carry
        lax.fori_loop(0, n_tiles, build, 0)
        dist = WIN_TILES * Q_BLOCK + qq - kk
        biases = _bias_from_bucket(_rel_bucket(dist), tab_ref, range(N_HEADS))
        for h in range(N_HEADS):
            bias_ref[edge_base + h] = jnp.where(dist < WINDOW, biases[h], NEG)
            bias_ref[none_base + h] = jnp.full((Q_BLOCK, Q_BLOCK), NEG, F32)

        def build_c(t, carry):
            dist_c = t * Q_BLOCK + qq - (kk * CMP_STRIDE + (CMP_LEN - 1))
            ok = (dist_c >= 0) & (kk < n_cmp - 1)
            biases_c = _bias_from_bucket(_rel_bucket(dist_c), tab_ref, range(N_HEADS))
            for h in range(N_HEADS):
                cbias_ref[t * N_HEADS + h] = jnp.where(ok, biases_c[h], NEG)
            return carry
        lax.fori_loop(0, n_tiles, build_c, 0)

    qs = i * Q_BLOCK
    q_all = q_ref[0] * SCALE
    gates = jax.nn.sigmoid(gl_ref[0])

    blk = lax.broadcasted_iota(I32, (n_sel_blocks, Q_BLOCK), 0)
    col = lax.broadcasted_iota(I32, (n_sel_blocks, Q_BLOCK), 1)
    c_start = col * CMP_STRIDE
    s_start = blk * SEL_LEN
    cover_t = jnp.where((c_start < s_start + SEL_LEN) & (c_start + CMP_LEN > s_start) & (col < n_cmp - 1),
                        1.0, 0.0).astype(BF16)
    q_pos = qs + col
    cur = q_pos // SEL_LEN
    forced = (blk == 0) | (blk == cur) | (blk == cur - 1)
    valid = blk * SEL_LEN <= q_pos
    upper = kk < SEL_LEN

    qts, o_cs, splits = [], [], []
    for g in range(N_KV_GROUPS):
        heads = [g * HEADS_PER_GROUP + hp for hp in range(HEADS_PER_GROUP)]
        qt = jnp.concatenate([q_all[h * HEAD_DIM:(h + 1) * HEAD_DIM, :] for h in heads], axis=1)
        qts.append(qt)

        s_c = _dot(ck_ref[0, g], qt) + jnp.concatenate([cbias_ref[i * N_HEADS + h] for h in heads], axis=1)
        m_c = jnp.max(s_c, axis=0, keepdims=True)
        e_c = jnp.exp(s_c - m_c)
        tot = jnp.sum(e_c, axis=0, keepdims=True)
        p_c = e_c * jnp.where(m_c > 0.5 * NEG, 1.0 / tot, 0.0)
        o_cs.append(_dot(cvt_ref[0, g * HEAD_DIM:(g + 1) * HEAD_DIM, :], p_c.astype(BF16)))
        p_sum = p_c[:, 0:Q_BLOCK]
        for hp in range(1, HEADS_PER_GROUP):
            p_sum = p_sum + p_c[:, hp * Q_BLOCK:(hp + 1) * Q_BLOCK]
        splits.append(p_sum.astype(BF16))

    imp_all = _dot(cover_t, jnp.concatenate(splits, axis=1))
    for g in range(N_KV_GROUPS):
        imp = imp_all[:, g * Q_BLOCK:(g + 1) * Q_BLOCK]
        score = jnp.where(forced, FORCE, imp)
        score = jnp.where(valid, score, NEG)
        cnt = jnp.zeros(score.shape, F32)
        for s in range(n_sel_blocks):
            row = score[s:s + 1, :]
            beats = (row > score) | ((row == score) & (blk > s))
            cnt = cnt + jnp.where(beats, 1.0, 0.0)
        sel = (cnt < float(N_SEL)) & (score > 0.5 * NEG)
        sel_ref[g] = jnp.where(sel, 0.0, NEG)

    def mask_sel(t, g):
        rows = sel_ref[g, pl.ds(pl.multiple_of(2 * KEY_SUB * t, 2 * KEY_SUB), 2 * KEY_SUB), :]
        return jnp.concatenate([jnp.where(upper, rows[2 * u:2 * u + 1], rows[2 * u + 1:2 * u + 2])
                                for u in range(KEY_SUB)], axis=0)
    last = i // KEY_SUB + 1
    o_ss = _attend_tiles_t(qts, sk_ref, svt_ref, 0, last, i, bias_ref,
                           lambda d: jnp.where(d < 0, none_base, d * N_HEADS), mask_sel)

    def win_index(d):
        return jnp.where((d < 0) | (d > WIN_TILES), none_base, jnp.where(d == WIN_TILES, edge_base, d * N_HEADS))
    o_ws = _attend_tiles_t(qts, wk_ref, wvt_ref, jnp.maximum(i - WIN_TILES, 0) // KEY_SUB, last, i, bias_ref,
                           win_index, lambda t, g: None)

    for h in range(N_HEADS):
        g, hp = divmod(h, HEADS_PER_GROUP)
        c0 = hp * Q_BLOCK
        o_h = (gates[3 * h:3 * h + 1, :] * o_cs[g][:, c0:c0 + Q_BLOCK]
               + gates[3 * h + 1:3 * h + 2, :] * o_ss[g][:, c0:c0 + Q_BLOCK]
               + gates[3 * h + 2:3 * h + 3, :] * o_ws[g][:, c0:c0 + Q_BLOCK])
        o_ref[0, h * HEAD_DIM:(h + 1) * HEAD_DIM, :] = o_h.astype(o_ref.dtype)


def _nsa_prompt(tab, q_t, gl_t, comp_k, comp_vt, slc_k, slc_vt, win_k, win_vt):
    b, dq, t_len = q_t.shape
    n_tiles = t_len // Q_BLOCK
    n_cmp = comp_k.shape[2]
    assert n_cmp == Q_BLOCK and t_len % KEY_TILE == 0 and n_tiles > WIN_TILES
    dv = N_KV_GROUPS * HEAD_DIM
    return pl.pallas_call(
        functools.partial(_nsa_prompt_kernel, n_tiles=n_tiles),
        grid=(b, n_tiles),
        in_specs=[pl.BlockSpec(memory_space=pltpu.SMEM),
                  pl.BlockSpec((1, dq, Q_BLOCK), lambda i, j: (i, 0, j)),
                  pl.BlockSpec((1, LANES, Q_BLOCK), lambda i, j: (i, 0, j)),
                  pl.BlockSpec((1, N_KV_GROUPS, n_cmp, HEAD_DIM), lambda i, j: (i, 0, 0, 0)),
                  pl.BlockSpec((1, dv, n_cmp), lambda i, j: (i, 0, 0)),
                  pl.BlockSpec((1, N_KV_GROUPS, t_len, HEAD_DIM), lambda i, j: (i, 0, 0, 0)),
                  pl.BlockSpec((1, dv, t_len), lambda i, j: (i, 0, 0)),
                  pl.BlockSpec((1, N_KV_GROUPS, t_len, HEAD_DIM), lambda i, j: (i, 0, 0, 0)),
                  pl.BlockSpec((1, dv, t_len), lambda i, j: (i, 0, 0))],
        out_specs=pl.BlockSpec((1, dq, Q_BLOCK), lambda i, j: (i, 0, j)),
        out_shape=jax.ShapeDtypeStruct((b, dq, t_len), BF16),
        scratch_shapes=[pltpu.VMEM(((n_tiles + 2) * N_HEADS, Q_BLOCK, Q_BLOCK), F32),
                        pltpu.VMEM((n_tiles * N_HEADS, Q_BLOCK, Q_BLOCK), F32),
                        pltpu.VMEM((N_KV_GROUPS, 2 * n_tiles, Q_BLOCK), F32)],
        compiler_params=_cparams(2),
        name="nsa_prompt",
    )(tab, q_t, gl_t, comp_k, comp_vt, slc_k, slc_vt, win_k, win_vt)


def _page_copy(pt_ref, pool_ref, buf_ref, sem, b, j, page):
    start = pl.multiple_of(j * page, page)
    if len(buf_ref.shape) == 2:
        dst = buf_ref.at[pl.ds(start, page)]
    else:
        dst = buf_ref.at[:, :, pl.ds(start, page)]
    return pltpu.make_async_copy(pool_ref.at[pt_ref[b, j]], dst, sem)


def _gather_pages_start(pt_ref, pool_ref, buf_ref, sem, b, n_pages, page):
    def go(j, carry):
        _page_copy(pt_ref, pool_ref, buf_ref, sem, b, j, page).start()
        return carry
    lax.fori_loop(0, n_pages, go, 0)


def _gather_pages_wait(pt_ref, pool_ref, buf_ref, sem, b, n_pages, page):
    def go(j, carry):
        _page_copy(pt_ref, pool_ref, buf_ref, sem, b, j, page).wait()
        return carry
    lax.fori_loop(0, n_pages, go, 0)


def _compress_pages_kernel(pt_ref, pool_ref, pe_ref, w1_ref, b1_ref, w2_ref, o_ref, raw_ref, x_ref, a_ref, b_ref,
                           sem, *, n_seq, n_pages, page, n_chunk):
    b = pl.program_id(0)

    @pl.when(b == 0)
    def _():
        _gather_pages_start(pt_ref, pool_ref, raw_ref, sem, b, n_pages, page)

    _gather_pages_wait(pt_ref, pool_ref, raw_ref, sem, b, n_pages, page)

    def relayout(j, carry):
        r0 = pl.multiple_of(j * page, page)
        for tile in range(ROW_TILES):
            kv, pair = divmod(tile, N_KV_GROUPS // 2)
            x_ref[tile, pl.ds(r0, page), :] = raw_ref[kv, pair * LANES:(pair + 1) * LANES, pl.ds(r0, page)].T
        return carry
    lax.fori_loop(0, n_pages, relayout, 0)

    @pl.when(b + 1 < n_seq)
    def _():
        _gather_pages_start(pt_ref, pool_ref, raw_ref, sem, b + 1, n_pages, page)

    _compress_rows(_load_tiled(x_ref), n_chunk, pe_ref, w1_ref, b1_ref, w2_ref, o_ref.at[0], a_ref, b_ref)


def _compress_pages(page_table, pool_t, pe, w1, b1, w2):
    b, n_pages = page_table.shape
    page = pool_t.shape[3]
    past = n_pages * page
    n_chunk = past // CMP_STRIDE
    assert n_chunk % CMP_MBLK == 0 and page == LANES
    full = lambda a: pl.BlockSpec(a.shape, lambda i: (0,) * a.ndim)
    return pl.pallas_call(
        functools.partial(_compress_pages_kernel, n_seq=b, n_pages=n_pages, page=page, n_chunk=n_chunk),
        grid=(b,),
        in_specs=[pl.BlockSpec(memory_space=pltpu.SMEM), pl.BlockSpec(memory_space=pl.ANY),
                  full(pe), full(w1), full(b1), full(w2)],
        out_specs=pl.BlockSpec((1, n_chunk, D_KV), lambda i: (i, 0, 0)),
        out_shape=jax.ShapeDtypeStruct((b, n_chunk, D_KV), F32),
        scratch_shapes=[pltpu.VMEM((2, D_K, past), F32),
                        pltpu.VMEM((ROW_TILES, past, LANES), F32),
                        pltpu.VMEM((n_chunk, PAIR_HID), F32), pltpu.VMEM((n_chunk + SUBLANES, PAIR_HID), F32),
                        pltpu.SemaphoreType.DMA(())],
        compiler_params=_cparams(1),
        name="compress_pages",
    )(page_table, pool_t, pe, w1, b1, w2)


DEC_TILE = 512
DEC_ROWS = 16
D_K = D_KV // 2


def _row_bias(dist_row, tab_ref, bias_ref, col0):
    length = dist_row.shape[1]
    biases = _bias_from_bucket(_rel_bucket(dist_row), tab_ref, range(N_HEADS))
    rows = biases + [jnp.zeros((DEC_ROWS - N_HEADS, length), F32)]
    bias_ref[:, pl.ds(col0, length)] = jnp.concatenate(rows, axis=0)


def _head_block(o_all, h):
    g = h // HEADS_PER_GROUP
    return o_all[h:h + 1, g * HEAD_DIM:(g + 1) * HEAD_DIM]


def _new_token_scores(q_blk, new_row, bias0):
    k_new = new_row[:, :D_K].astype(BF16).astype(F32)
    v_new = new_row[:, D_K:].astype(BF16).astype(F32)
    return jnp.sum(q_blk.astype(F32) * k_new, axis=-1, keepdims=True) + bias0, v_new


def _nsa_step_kernel(tab_ref, pt_ref, q_ref, gl_ref, comp_ref, pool_ref, slc_new_ref, win_ref, win_new_ref, o_ref,
                     kv_buf, bias_c_ref, bias_s_ref, bias_w_ref, sems,
                     *, n_seq, n_pages, page, past, n_cmp, w_buf):
    b = pl.program_id(0)
    slot = b % 2
    n_tiles = past // DEC_TILE
    n_blocks = past // SEL_LEN + 1
    blk_lanes = -(-n_blocks // LANES) * LANES

    @pl.when(b == 0)
    def _():
        _gather_pages_start(pt_ref, pool_ref, kv_buf.at[0], sems.at[0], b, n_pages, page)

    @pl.when(b + 1 < n_seq)
    def _():
        _gather_pages_start(pt_ref, pool_ref, kv_buf.at[1 - slot], sems.at[1 - slot], b + 1, n_pages, page)

    lane_t = lax.broadcasted_iota(I32, (1, DEC_TILE), 1)

    @pl.when(b == 0)
    def _():
        lane_c = lax.broadcasted_iota(I32, (1, n_cmp), 1)
        _row_bias(past - (lane_c * CMP_STRIDE + (CMP_LEN - 1)), tab_ref, bias_c_ref, 0)

        def tile_bias(t, carry):
            k0 = pl.multiple_of(t * DEC_TILE, DEC_TILE)
            _row_bias(past - (k0 + lane_t), tab_ref, bias_s_ref, k0)
            return carry
        lax.fori_loop(0, n_tiles, tile_bias, 0)
        lane_w = lax.broadcasted_iota(I32, (1, w_buf), 1)
        _row_bias(w_buf - lane_w, tab_ref, bias_w_ref, 0)

    bias0 = jnp.concatenate([jnp.full((1, 1), tab_ref[0, h], F32) for h in range(N_HEADS)]
                            + [jnp.zeros((DEC_ROWS - N_HEADS, 1), F32)], axis=0)

    q_row = q_ref[0] * SCALE
    gates = jax.nn.sigmoid(gl_ref[0])
    rows = []
    for h in range(N_HEADS):
        g = h // HEADS_PER_GROUP
        parts = []
        if g > 0:
            parts.append(jnp.zeros((1, g * HEAD_DIM), BF16))
        parts.append(q_row[:, h * HEAD_DIM:(h + 1) * HEAD_DIM])
        if g < N_KV_GROUPS - 1:
            parts.append(jnp.zeros((1, (N_KV_GROUPS - 1 - g) * HEAD_DIM), BF16))
        rows.append(jnp.concatenate(parts, axis=1))
    q_blk = jnp.concatenate(rows + [jnp.zeros((DEC_ROWS - N_HEADS, D_K), BF16)], axis=0)

    comp = comp_ref[0].astype(BF16)
    mask_c = lax.broadcasted_iota(I32, (DEC_ROWS, n_cmp), 1) < n_cmp - 1
    p_c = _masked_softmax_rows(_dot_nt(q_blk, comp[:, :D_K]) + bias_c_ref[...], mask_c)
    o_c = _dot(p_c.astype(BF16), comp[:, D_K:])
    p_sums = [jnp.sum(p_c[g * HEADS_PER_GROUP:(g + 1) * HEADS_PER_GROUP], axis=0, keepdims=True)
              for g in range(N_KV_GROUPS)]
    p_sum = jnp.concatenate(p_sums + [jnp.zeros((DEC_ROWS - N_KV_GROUPS, n_cmp), F32)], axis=0)

    c_start = lax.broadcasted_iota(I32, (n_cmp, blk_lanes), 0) * CMP_STRIDE
    s_start = lax.broadcasted_iota(I32, (n_cmp, blk_lanes), 1) * SEL_LEN
    c_idx = lax.broadcasted_iota(I32, (n_cmp, blk_lanes), 0)
    cover = jnp.where((c_start < s_start + SEL_LEN) & (c_start + CMP_LEN > s_start) & (c_idx < n_cmp - 1),
                      1.0, 0.0).astype(BF16)
    imp = _dot(p_sum.astype(BF16), cover)
    blk = lax.broadcasted_iota(I32, (DEC_ROWS, blk_lanes), 1)
    cur = past // SEL_LEN
    forced = (blk == 0) | (blk == cur) | (blk == cur - 1)
    score = jnp.where(forced, FORCE, imp)
    score = jnp.where(blk < n_blocks, score, NEG)
    score_t = score.T
    j_idx = lax.broadcasted_iota(I32, (blk_lanes, blk_lanes), 0)
    i_idx = lax.broadcasted_iota(I32, (blk_lanes, blk_lanes), 1)
    sel_groups = []
    for g in range(N_KV_GROUPS):
        col = score_t[:, g:g + 1]
        row = score[g:g + 1, :]
        beats = (col > row) | ((col == row) & (j_idx < i_idx))
        rank = jnp.sum(jnp.where(beats, 1.0, 0.0), axis=0, keepdims=True)
        sel_groups.append(jnp.where((rank < float(N_SEL)) & (row > 0.5 * NEG), 1.0, 0.0))
    sel_rows = [sel_groups[h // HEADS_PER_GROUP] for h in range(N_HEADS)]
    sel_bf = jnp.concatenate(sel_rows + [jnp.zeros((DEC_ROWS - N_HEADS, blk_lanes), F32)], axis=0).astype(BF16)

    _gather_pages_wait(pt_ref, pool_ref, kv_buf.at[slot], sems.at[slot], b, n_pages, page)
    e_row = lax.broadcasted_iota(I32, (blk_lanes, DEC_TILE), 0)
    e_col = lax.broadcasted_iota(I32, (blk_lanes, DEC_TILE), 1) // SEL_LEN

    def sel_tile(t, carry):
        m, l, acc = carry
        k0 = pl.multiple_of(t * DEC_TILE, DEC_TILE)
        expand = jnp.where(e_row == e_col + t * (DEC_TILE // SEL_LEN), 1.0, 0.0).astype(BF16)
        key_sel = _dot(sel_bf, expand)
        kt = kv_buf[slot, 0, :, pl.ds(k0, DEC_TILE)].astype(BF16)
        vt = kv_buf[slot, 1, :, pl.ds(k0, DEC_TILE)].astype(BF16)
        s = _dot(q_blk, kt) + bias_s_ref[:, pl.ds(k0, DEC_TILE)] + jnp.where(key_sel > 0.5, 0.0, NEG)
        m_new = jnp.maximum(m, jnp.max(s, axis=-1, keepdims=True))
        alpha = jnp.exp(m - m_new)
        p = jnp.exp(s - m_new)
        return m_new, alpha * l + jnp.sum(p, axis=-1, keepdims=True), alpha * acc + _dot_nt(p.astype(BF16), vt)

    init = (jnp.full((DEC_ROWS, 1), NEG, F32), jnp.zeros((DEC_ROWS, 1), F32), jnp.zeros((DEC_ROWS, D_K), F32))
    m_s, l_s, acc_s = lax.fori_loop(0, n_tiles, sel_tile, init)
    s_new, v_new = _new_token_scores(q_blk, slc_new_ref[0], bias0)
    cur_sel = sel_bf[:, cur:cur + 1].astype(F32)
    s_new = s_new + jnp.where(cur_sel > 0.5, 0.0, NEG)
    m_fin = jnp.maximum(m_s, s_new)
    alpha = jnp.exp(m_s - m_fin)
    p_new = jnp.exp(s_new - m_fin)
    o_s = (alpha * acc_s + p_new.astype(BF16).astype(F32) * v_new) / (alpha * l_s + p_new)

    kw = win_ref[0, 0].astype(BF16)
    vw = win_ref[0, 1].astype(BF16)
    idx_w = lax.broadcasted_iota(I32, (DEC_ROWS, w_buf), 1)
    s_w = jnp.where(w_buf - idx_w < WINDOW, _dot(q_blk, kw) + bias_w_ref[...], NEG)
    s_wn, v_wn = _new_token_scores(q_blk, win_new_ref[0], bias0)
    m_w = jnp.maximum(jnp.max(s_w, axis=-1, keepdims=True), s_wn)
    e_w = jnp.exp(s_w - m_w)
    e_wn = jnp.exp(s_wn - m_w)
    o_w = ((_dot_nt(e_w.astype(BF16), vw) + e_wn.astype(BF16).astype(F32) * v_wn)
           / (jnp.sum(e_w, axis=-1, keepdims=True) + e_wn))

    outs = []
    for h in range(N_HEADS):
        outs.append(gates[:, 3 * h:3 * h + 1] * _head_block(o_c, h)
                    + gates[:, 3 * h + 1:3 * h + 2] * _head_block(o_s, h)
                    + gates[:, 3 * h + 2:3 * h + 3] * _head_block(o_w, h))
    o_ref[0] = jnp.concatenate(outs, axis=1).astype(o_ref.dtype)


def _feature_major(cache):
    nd = cache.ndim
    perm = tuple(range(nd - 4)) + (nd - 3, nd - 2, nd - 1, nd - 4)
    t = cache.transpose(perm)
    return t.reshape(t.shape[:nd - 3] + (t.shape[nd - 3] * t.shape[nd - 2], t.shape[nd - 1]))


def _nsa_step(tab, page_table, q, gl, comp, pool_t, slc_new, win_t, win_new):
    b, n_pages = page_table.shape
    page = pool_t.shape[3]
    past = n_pages * page
    n_cmp = comp.shape[1]
    w_buf = win_t.shape[3]
    dq = q.shape[2]
    assert past % DEC_TILE == 0 and past % SEL_LEN == 0 and w_buf == WINDOW and n_cmp == past // CMP_STRIDE
    row3 = lambda w: pl.BlockSpec((1, 1, w), lambda i: (i, 0, 0))
    return pl.pallas_call(
        functools.partial(_nsa_step_kernel, n_seq=b, n_pages=n_pages, page=page, past=past, n_cmp=n_cmp,
                          w_buf=w_buf),
        grid=(b,),
        in_specs=[pl.BlockSpec(memory_space=pltpu.SMEM), pl.BlockSpec(memory_space=pltpu.SMEM),
                  row3(dq), row3(LANES),
                  pl.BlockSpec((1, n_cmp, D_KV), lambda i: (i, 0, 0)),
                  pl.BlockSpec(memory_space=pl.ANY),
                  row3(D_KV),
                  pl.BlockSpec((1, 2, D_K, w_buf), lambda i: (i, 0, 0, 0)),
                  row3(D_KV)],
        out_specs=row3(dq),
        out_shape=jax.ShapeDtypeStruct((b, 1, dq), BF16),
        scratch_shapes=[pltpu.VMEM((2, 2, D_K, past), F32),
                        pltpu.VMEM((DEC_ROWS, n_cmp), F32),
                        pltpu.VMEM((DEC_ROWS, past), F32),
                        pltpu.VMEM((DEC_ROWS, w_buf), F32),
                        pltpu.SemaphoreType.DMA((2,))],
        compiler_params=_cparams(1),
        name="nsa_step",
    )(tab, page_table, q, gl, comp, pool_t, slc_new, win_t, win_new)


def _block_diag_pairs(w):
    n, blk, _ = w.shape
    w = w.reshape(n // 2, 2, blk, blk)
    z = jnp.zeros((n // 2, blk, blk), w.dtype)
    top = jnp.concatenate([w[:, 0], z], axis=2)
    bot = jnp.concatenate([z, w[:, 1]], axis=2)
    return jnp.concatenate([top, bot], axis=1).astype(BF16)


def _prep_weights(w_in_a, conv_w, conv_b, w_rg, b_rg, w_ig, b_ig, lru_lambda, w_in_b, w_kv_shared, cmp_pe, cmp_w1,
                  cmp_b1, cmp_w2, w_mem_kv, w_out, ln1_g, ln1_b, ln2_g, ln2_b, w_router_g, b_router_g,
                  w_router_e, b_router_e, w_exp_gate, w_exp_up, w_exp_down):
    depth, d_model, _ = w_out.shape
    n_a = w_in_a.shape[0]
    nq = N_HEADS * HEAD_DIM
    n_gl = 3 * N_HEADS
    hid = EXPERTS_PER_GROUP * D_EXPERT
    p = {}
    p["w_in_a"] = w_in_a.astype(BF16)
    p["w_in_b"] = jnp.concatenate(
        [w_in_b[:, :, :nq], w_in_b[:, :, nq + n_gl:], w_in_b[:, :, nq:nq + n_gl],
         jnp.zeros((w_in_b.shape[0], d_model, LANES - n_gl), w_in_b.dtype)], axis=2).astype(BF16)
    p["w_kv"] = w_kv_shared.astype(BF16)
    p["w_mem"] = w_mem_kv.transpose(1, 0, 2).reshape(d_model, depth * 2 * D_MEM).astype(BF16)
    p["wo_a"] = w_out[:, :D_RNN].astype(BF16)
    p["wo_m"] = w_out[:, D_RNN:].astype(BF16)
    p["conv_w"] = conv_w
    p["conv_b"] = conv_b[:, None]
    p["w_rg"] = jnp.stack([_block_diag_pairs(w_rg[l]) for l in range(n_a)])
    p["w_ig"] = jnp.stack([_block_diag_pairs(w_ig[l]) for l in range(n_a)])
    p["b_rg"] = b_rg[:, None]
    p["b_ig"] = b_ig[:, None]
    p["lam"] = lru_lambda[:, None]
    p["pe"], p["cmp_w1"], p["cmp_b1"], p["cmp_w2"] = _compress_weights(cmp_pe, cmp_w1, cmp_b1, cmp_w2)
    p["ln1_g"], p["ln1_b"], p["ln2_g"], p["ln2_b"] = ln1_g[:, None], ln1_b[:, None], ln2_g[:, None], ln2_b[:, None]
    n_r = N_GROUPS + N_EXPERTS
    wr = jnp.concatenate([w_router_g, w_router_e, jnp.zeros((depth, d_model, LANES - n_r), F32)], axis=2)
    p["wr"] = wr.astype(BF16)
    p["rb"] = jnp.concatenate([b_router_g, b_router_e, jnp.zeros((depth, LANES - n_r), F32)], axis=1)[:, None]

    def by_group(w):
        w = w.astype(BF16).reshape(depth, N_GROUPS, EXPERTS_PER_GROUP, d_model, D_EXPERT)
        return w.transpose(0, 1, 3, 2, 4).reshape(depth, N_GROUPS, d_model, hid)
    p["w_gu"] = jnp.concatenate([by_group(w_exp_gate), by_group(w_exp_up)], axis=3)
    p["w_dn"] = w_exp_down.astype(BF16).reshape(depth, N_GROUPS, hid, d_model)
    return p


def _kv_layouts(rows):
    b, t_len, _ = rows.shape
    k = rows[:, :, :D_KV // 2].reshape(b, t_len, N_KV_GROUPS, HEAD_DIM).transpose(0, 2, 1, 3)
    return k, rows[:, :, D_KV // 2:].transpose(0, 2, 1)


def _trunk_tail(x, mix_a, mix_m, p, l, alpha, tm):
    x, grp = _out_ln(x, mix_a, mix_m, p["wo_a"][l], p["wo_m"][l], p["ln1_g"][l], p["ln1_b"][l],
                     p["wr"][l], p["rb"][l], alpha, tm)
    if x.shape[0] < 8 * MOE_TILE:
        return _moe_ln(x, p["wr"][l], p["rb"][l], p["w_gu"][l], p["w_dn"][l],
                       p["ln2_g"][l], p["ln2_b"][l], alpha, tm)
    dest, tile_grp = _moe_dispatch(grp[:, 0])
    xs = _moe_place(x, dest, tile_grp.shape[0] * MOE_TILE)
    ys = _moe_grouped(xs, tile_grp, p["wr"][l], p["rb"][l], p["w_gu"][l], p["w_dn"][l])
    return _moe_return_ln(x, ys, dest, p["ln2_g"][l], p["ln2_b"][l], alpha)


def kernel(x_prompt, x_sample, mem_prompt, cache_cmp_kv, cache_slc_kv, cache_win_kv, cache_mem_kv, state_lru_h, state_conv, page_table, w_in_a, conv_w, conv_b, w_rg, b_rg, w_ig, b_ig, lru_lambda, w_in_b, w_kv_shared, cmp_pe, cmp_w1, cmp_b1, cmp_w2, rel_bias, w_mem_kv, w_out, ln1_g, ln1_b, ln2_g, ln2_b, w_router_g, b_router_g, w_router_e, b_router_e, w_exp_gate, w_exp_up, w_exp_down):
    p = _prep_weights(w_in_a, conv_w, conv_b, w_rg, b_rg, w_ig, b_ig, lru_lambda, w_in_b, w_kv_shared, cmp_pe,
                      cmp_w1, cmp_b1, cmp_w2, w_mem_kv, w_out, ln1_g, ln1_b, ln2_g, ln2_b, w_router_g,
                      b_router_g, w_router_e, b_router_e, w_exp_gate, w_exp_up, w_exp_down)
    depth, d_model, _ = w_out.shape
    n_a = w_in_a.shape[0]
    alpha = (2 * depth) ** 0.25
    kv_shape = (2, N_KV_GROUPS, HEAD_DIM)
    a_outs = [(0, D_RNN, F32), (D_RNN, D_RNN, F32), (2 * D_RNN, D_MEM, BF16)]
    b_outs = [(0, D_RNN, BF16), (D_RNN, D_MEM, BF16), (D_RNN + D_MEM, LANES, F32)]
    kv_outs = [(0, D_KV, F32), (D_KV, D_KV, F32), (2 * D_KV, D_KV, F32)]

    bp, t_len, _ = x_prompt.shape
    n_mem = mem_prompt.shape[1]
    tm = 512
    mem_rows = _proj(mem_prompt.reshape(bp * n_mem, d_model), p["w_mem"],
                     [(l * 2 * D_MEM, 2 * D_MEM, F32) for l in range(depth)], tm, "mem_kv_proj")
    p_mem = jnp.stack(mem_rows).reshape(depth, bp, n_mem, 2 * D_MEM)
    x = x_prompt.reshape(bp * t_len, d_model)
    lru_p, conv_p = [], []
    for l in range(depth):
        if l < n_a:
            gate, xr, mq = _proj(x, p["w_in_a"][l], a_outs, tm, "in_proj_a")
            mix_a, h_t, buf = _rglru_prompt(gate.reshape(bp, t_len, D_RNN), xr.reshape(bp, t_len, D_RNN),
                                            p["conv_w"][l], p["conv_b"][l], p["w_rg"][l], p["w_ig"][l],
                                            p["b_rg"][l], p["b_ig"][l], p["lam"][l])
            lru_p.append(h_t[:, 0])
            conv_p.append(buf)
        else:
            if l == n_a:
                cmp_rows, slc_rows, win_rows, slc_bf, win_bf = _proj(
                    x, p["w_kv"], kv_outs + [(D_KV, D_KV, BF16), (2 * D_KV, D_KV, BF16)], tm, "kv_proj")
                comp = _compress_prompt(cmp_rows.reshape(bp, t_len, D_KV), p["pe"], p["cmp_w1"], p["cmp_b1"],
                                        p["cmp_w2"])
                comp_k, comp_vt = _kv_layouts(comp.astype(BF16))
                slc_k, slc_vt = _kv_layouts(slc_bf.reshape(bp, t_len, D_KV))
                win_k, win_vt = _kv_layouts(win_bf.reshape(bp, t_len, D_KV))
            q, mq, gl = _proj(x, p["w_in_b"][l - n_a], b_outs, tm, "in_proj_b")
            mix_t = _nsa_prompt(rel_bias, q.reshape(bp, t_len, D_RNN).transpose(0, 2, 1),
                                gl.reshape(bp, t_len, LANES).transpose(0, 2, 1),
                                comp_k, comp_vt, slc_k, slc_vt, win_k, win_vt)
            mix_a = mix_t.transpose(0, 2, 1)
        mix_m = _mem_attn(mq.reshape(bp, t_len, D_MEM), p_mem, l)
        x = _trunk_tail(x, mix_a.reshape(bp * t_len, D_RNN), mix_m.reshape(bp * t_len, D_MEM), p, l, alpha, tm)
    y_prompt = x.reshape(bp, t_len, d_model)
    p_cmp_kv = cmp_rows.reshape((bp, t_len) + kv_shape)
    p_slc_kv = slc_rows.reshape((bp, t_len) + kv_shape)
    w_keep = min(WINDOW, t_len)
    p_win_kv = win_rows.reshape((bp, t_len) + kv_shape)[:, t_len - w_keep:]
    p_mem_kv = p_mem.reshape(depth, bp, n_mem, 2, N_MEM_HEADS, HEAD_DIM)
    p_lru_h = jnp.stack(lru_p)
    p_conv = jnp.stack(conv_p)

    bd, s_len, _ = x_sample.shape
    assert s_len == 1
    n_pool, page = cache_cmp_kv.shape[:2]
    w_buf = cache_win_kv.shape[1]
    mem_t = _feature_major(cache_mem_kv)
    slc_pool_t = _feature_major(cache_slc_kv)
    win_t = _feature_major(cache_win_kv)
    x = x_sample.reshape(bd, d_model)
    lru_s, conv_s = [], []
    for l in range(depth):
        if l < n_a:
            gate, xr, mq = _proj(x, p["w_in_a"][l], a_outs, bd, "in_proj_a_step")
            mix_a, h_t, buf = _rglru_step(gate, xr, state_conv[l].transpose(1, 0, 2), state_lru_h[l],
                                          p["conv_w"][l], p["conv_b"][l], p["w_rg"][l], p["w_ig"][l],
                                          p["b_rg"][l], p["b_ig"][l], p["lam"][l])
            lru_s.append(h_t)
            conv_s.append(buf.transpose(1, 0, 2))
        else:
            if l == n_a:
                s_cmp, s_slc, s_win = _proj(x, p["w_kv"], kv_outs, bd, "kv_proj_step")
                comp_s = _compress_pages(page_table, _feature_major(cache_cmp_kv), p["pe"],
                                         p["cmp_w1"], p["cmp_b1"], p["cmp_w2"])
            q, mq, gl = _proj(x, p["w_in_b"][l - n_a], b_outs, bd, "in_proj_b_step")
            mix_a = _nsa_step(rel_bias, page_table, q.reshape(bd, 1, D_RNN), gl.reshape(bd, 1, LANES), comp_s,
                              slc_pool_t, s_slc.reshape(bd, 1, D_KV), win_t, s_win.reshape(bd, 1, D_KV))
            mix_a = mix_a.reshape(bd, D_RNN)
        mix_m = _mem_attn_step(mq.reshape(bd, 1, D_MEM), mem_t, l).reshape(bd, D_MEM)
        x = _trunk_tail(x, mix_a, mix_m, p, l, alpha, bd)
    y_sample = x.reshape(bd, 1, d_model)
    s_cmp_kv = s_cmp.reshape((bd, 1) + kv_shape)
    s_slc_kv = s_slc.reshape((bd, 1) + kv_shape)
    s_win_kv = jnp.concatenate([cache_win_kv, s_win.reshape((bd, 1) + kv_shape)], axis=1)[:, 1:]
    s_lru_h = jnp.stack(lru_s)
    s_conv = jnp.stack(conv_s)
    return (y_prompt, y_sample, p_cmp_kv, p_slc_kv, p_win_kv, p_mem_kv, p_lru_h, p_conv,
            s_cmp_kv, s_slc_kv, s_win_kv, s_lru_h, s_conv)
```

```python
import functools
import math

import jax
import jax.numpy as jnp
from jax import lax
from jax.experimental import pallas as pl
from jax.experimental.pallas import tpu as pltpu

F32 = jnp.float32
BF16 = jnp.bfloat16
I32 = jnp.int32

HEAD_DIM = 64
N_MEM_HEADS = 4
D_MEM = N_MEM_HEADS * HEAD_DIM
N_HEADS = 12
D_RNN = N_HEADS * HEAD_DIM
N_KV_GROUPS = 4
HEADS_PER_GROUP = N_HEADS // N_KV_GROUPS
D_KV = 2 * N_KV_GROUPS * HEAD_DIM
CONV_W = 4
LRU_C = 8.0
CMP_LEN = 32
CMP_STRIDE = 16
CMP_HID = 2 * HEAD_DIM
SEL_LEN = 64
N_SEL = 16
WINDOW = 512
N_BUCKETS = 32
MAX_DISTANCE = 1024
N_GROUPS = 4
EXPERTS_PER_GROUP = 8
N_EXPERTS = N_GROUPS * EXPERTS_PER_GROUP
D_EXPERT = 128
Q_BLOCK = 128
LN_EPS = 1e-5
NEG = -1e30
FORCE = 1e9
SCALE = HEAD_DIM ** -0.5

LANES = 128
SUBLANES = 8
VMEM_LIMIT_BYTES = 56 * 1024 * 1024


def _cparams(n_axes):
    return pltpu.CompilerParams(dimension_semantics=("arbitrary",) * n_axes,
                                vmem_limit_bytes=VMEM_LIMIT_BYTES)


def _dot(a, b):
    return jnp.dot(a, b, preferred_element_type=F32)


def _dot_nt(a, b):
    return lax.dot_general(a, b, (((1,), (1,)), ((), ())), preferred_element_type=F32)


def _layer_norm(v, g, b):
    mu = jnp.mean(v, axis=-1, keepdims=True)
    d = v - mu
    var = jnp.mean(d * d, axis=-1, keepdims=True)
    return d * lax.rsqrt(var + LN_EPS) * g + b


def _rel_bucket(dist):
    n = jnp.maximum(dist, 0)
    max_exact = N_BUCKETS // 2
    nf = jnp.maximum(n, 1).astype(F32)
    large = max_exact + (jnp.log(nf / max_exact) / math.log(MAX_DISTANCE / max_exact)
                         * (N_BUCKETS - max_exact)).astype(I32)
    large = jnp.minimum(large, N_BUCKETS - 1)
    return jnp.where(n < max_exact, n, large)


def _bias_from_bucket(bucket, tab_ref, heads):
    masks = [bucket == k for k in range(1, N_BUCKETS)]
    out = []
    for h in heads:
        acc = jnp.full(bucket.shape, tab_ref[0, h], F32)
        for k in range(1, N_BUCKETS):
            acc = jnp.where(masks[k - 1], tab_ref[k, h], acc)
        out.append(acc)
    return out


def _proj_kernel(x_ref, w_ref, *o_refs, cols):
    y = _dot(x_ref[...].astype(BF16), w_ref[...])
    for o_ref, (off, n) in zip(o_refs, cols):
        o_ref[...] = y[:, off:off + n].astype(o_ref.dtype)


def _proj(x, w, outs, tm, name):
    m, k = x.shape
    n = w.shape[1]
    tm = min(tm, m)
    assert m % tm == 0 and all(off + wd <= n for off, wd, _ in outs)
    return pl.pallas_call(
        functools.partial(_proj_kernel, cols=tuple((off, wd) for off, wd, _ in outs)),
        grid=(m // tm,),
        in_specs=[pl.BlockSpec((tm, k), lambda i: (i, 0)),
                  pl.BlockSpec((k, n), lambda i: (0, 0))],
        out_specs=[pl.BlockSpec((tm, wd), lambda i: (i, 0)) for _, wd, _ in outs],
        out_shape=[jax.ShapeDtypeStruct((m, wd), dt) for _, wd, dt in outs],
        compiler_params=_cparams(1),
        name=name,
    )(x, w)


def _softplus(z):
    return jnp.maximum(z, 0.0) + jnp.log1p(jnp.exp(-jnp.abs(z)))


def _lru_gates(xc, wr_ref, wi_ref, br, bi, lam):
    xcb = xc.astype(BF16)
    nb = D_RNN // LANES
    r_l = jnp.concatenate([_dot(xcb[:, j * LANES:(j + 1) * LANES], wr_ref[j]) for j in range(nb)], axis=1)
    i_l = jnp.concatenate([_dot(xcb[:, j * LANES:(j + 1) * LANES], wi_ref[j]) for j in range(nb)], axis=1)
    r = jax.nn.sigmoid(r_l + br)
    i = jax.nn.sigmoid(i_l + bi)
    log_a = -LRU_C * r * _softplus(-lam)
    a = jnp.exp(log_a)
    u = jnp.sqrt(1.0 - a * a) * i * xc
    return a, u


def _rglru_prompt_kernel(gate_ref, xr_ref, cw_ref, cb_ref, wr_ref, wi_ref, br_ref, bi_ref, lam_ref,
                         y_ref, ht_ref, buf_ref, xp_ref, *, t_len, chunk):
    d = D_RNN
    pad = SUBLANES
    xp_ref[0:pad, :] = jnp.zeros((pad, d), F32)
    xp_ref[pad:pad + t_len, :] = xr_ref[0]
    buf_ref[0] = xr_ref[0, t_len - (CONV_W - 1):t_len, :]
    cw = cw_ref[...]
    cb = cb_ref[...]
    br = br_ref[...]
    bi = bi_ref[...]
    lam = lam_ref[...]
    row_in_tile = lax.broadcasted_iota(I32, (chunk, d), 0) & (SUBLANES - 1)
    h = jnp.zeros((1, d), F32)
    for c in range(t_len // chunk):
        base = c * chunk
        xc = cb
        for k in range(CONV_W):
            lo = pad - (CONV_W - 1) + k + base
            xc = xc + xp_ref[lo:lo + chunk, :] * cw[k:k + 1, :]
        a, u = _lru_gates(xc, wr_ref, wi_ref, br, bi, lam)
        for k in (1, 2, 4):
            a_s = pltpu.roll(a, k, axis=0)
            u_s = pltpu.roll(u, k, axis=0)
            m = row_in_tile >= k
            u = jnp.where(m, a * u_s + u, u)
            a = jnp.where(m, a * a_s, a)
        hs = []
        for j in range(chunk // SUBLANES):
            h_t = a[j * SUBLANES:(j + 1) * SUBLANES] * h + u[j * SUBLANES:(j + 1) * SUBLANES]
            h = h_t[SUBLANES - 1:SUBLANES]
            hs.append(h_t)
        hfull = jnp.concatenate(hs, axis=0)
        g = jax.nn.gelu(gate_ref[0, base:base + chunk, :])
        y_ref[0, base:base + chunk, :] = (g * hfull).astype(y_ref.dtype)
    ht_ref[0] = h


def _rglru_prompt(gate, xr, cw, cb, wr_bd, wi_bd, br, bi, lam):
    b, t_len, d = xr.shape
    chunk = min(256, t_len)
    full = lambda shape: pl.BlockSpec(shape, lambda i: (0,) * len(shape))
    return pl.pallas_call(
        functools.partial(_rglru_prompt_kernel, t_len=t_len, chunk=chunk),
        grid=(b,),
        in_specs=[pl.BlockSpec((1, t_len, d), lambda i: (i, 0, 0)),
                  pl.BlockSpec((1, t_len, d), lambda i: (i, 0, 0)),
                  full((CONV_W, d)), full((1, d)), full(wr_bd.shape), full(wi_bd.shape),
                  full((1, d)), full((1, d)), full((1, d))],
        out_specs=[pl.BlockSpec((1, t_len, d), lambda i: (i, 0, 0)),
                   pl.BlockSpec((1, 1, d), lambda i: (i, 0, 0)),
                   pl.BlockSpec((1, CONV_W - 1, d), lambda i: (i, 0, 0))],
        out_shape=[jax.ShapeDtypeStruct((b, t_len, d), BF16),
                   jax.ShapeDtypeStruct((b, 1, d), F32),
                   jax.ShapeDtypeStruct((b, CONV_W - 1, d), F32)],
        scratch_shapes=[pltpu.VMEM((t_len + SUBLANES, d), F32)],
        compiler_params=_cparams(1),
        name="rglru_prompt",
    )(gate, xr, cw, cb, wr_bd, wi_bd, br, bi, lam)


def _rglru_step_kernel(gate_ref, xr_ref, conv_ref, h0_ref, cw_ref, cb_ref, wr_ref, wi_ref, br_ref, bi_ref,
                       lam_ref, y_ref, ht_ref, buf_ref):
    xr = xr_ref[...]
    cw = cw_ref[...]
    xc = cb_ref[...] + xr * cw[CONV_W - 1:CONV_W, :]
    for k in range(CONV_W - 1):
        xc = xc + conv_ref[k] * cw[k:k + 1, :]
    a, u = _lru_gates(xc, wr_ref, wi_ref, br_ref[...], bi_ref[...], lam_ref[...])
    h = a * h0_ref[...] + u
    ht_ref[...] = h
    y_ref[...] = (jax.nn.gelu(gate_ref[...]) * h).astype(y_ref.dtype)
    for k in range(CONV_W - 2):
        buf_ref[k] = conv_ref[k + 1]
    buf_ref[CONV_W - 2] = xr


def _rglru_step(gate, xr, conv_t, h0, cw, cb, wr_bd, wi_bd, br, bi, lam):
    b, d = xr.shape
    return pl.pallas_call(
        _rglru_step_kernel,
        out_shape=[jax.ShapeDtypeStruct((b, d), BF16),
                   jax.ShapeDtypeStruct((b, d), F32),
                   jax.ShapeDtypeStruct((CONV_W - 1, b, d), F32)],
        compiler_params=pltpu.CompilerParams(vmem_limit_bytes=VMEM_LIMIT_BYTES),
        name="rglru_step",
    )(gate, xr, conv_t, h0, cw, cb, wr_bd, wi_bd, br, bi, lam)


def _mem_attn_kernel(q_ref, kv_ref, o_ref, *, rows):
    q = q_ref[0]
    if rows < SUBLANES:
        q = jnp.broadcast_to(q, (SUBLANES, D_MEM))
    kv = kv_ref[0, 0].astype(BF16)
    outs = []
    for h in range(N_MEM_HEADS):
        qh = q[:, h * HEAD_DIM:(h + 1) * HEAD_DIM]
        kh = kv[:, h * HEAD_DIM:(h + 1) * HEAD_DIM]
        vh = kv[:, D_MEM + h * HEAD_DIM:D_MEM + (h + 1) * HEAD_DIM]
        s = _dot_nt(qh, kh) * SCALE
        m = jnp.max(s, axis=-1, keepdims=True)
        e = jnp.exp(s - m)
        p = e / jnp.sum(e, axis=-1, keepdims=True)
        outs.append(_dot(p.astype(BF16), vh))
    o = jnp.concatenate(outs, axis=1)
    o_ref[0] = o[:rows].astype(o_ref.dtype)


def _mem_attn(q, mem_kv, layer):
    b, t_len, _ = q.shape
    n_mem = mem_kv.shape[2]
    tq = min(t_len, 512)
    return pl.pallas_call(
        functools.partial(_mem_attn_kernel, rows=tq),
        grid=(b, t_len // tq),
        in_specs=[pl.BlockSpec((1, tq, D_MEM), lambda i, j: (i, j, 0)),
                  pl.BlockSpec((1, 1, n_mem, 2 * D_MEM), lambda i, j: (layer, i, 0, 0))],
        out_specs=pl.BlockSpec((1, tq, D_MEM), lambda i, j: (i, j, 0)),
        out_shape=jax.ShapeDtypeStruct((b, t_len, D_MEM), BF16),
        compiler_params=_cparams(2),
        name="mem_attn",
    )(q, mem_kv)


def _mem_attn_step_kernel(q_ref, kv_ref, o_ref):
    q_row = q_ref[0] * SCALE
    rows = []
    for h in range(N_MEM_HEADS):
        parts = []
        if h > 0:
            parts.append(jnp.zeros((1, h * HEAD_DIM), BF16))
        parts.append(q_row[:, h * HEAD_DIM:(h + 1) * HEAD_DIM])
        if h < N_MEM_HEADS - 1:
            parts.append(jnp.zeros((1, (N_MEM_HEADS - 1 - h) * HEAD_DIM), BF16))
        rows.append(jnp.concatenate(parts, axis=1))
    n_rows = 2 * SUBLANES
    q_blk = jnp.concatenate(rows + [jnp.zeros((n_rows - N_MEM_HEADS, D_MEM), BF16)], axis=0)
    s = _dot(q_blk, kv_ref[0, 0, 0].astype(BF16))
    e = jnp.exp(s - jnp.max(s, axis=-1, keepdims=True))
    p = e / jnp.sum(e, axis=-1, keepdims=True)
    o_all = _dot_nt(p.astype(BF16), kv_ref[0, 0, 1].astype(BF16))
    o = jnp.concatenate([o_all[h:h + 1, h * HEAD_DIM:(h + 1) * HEAD_DIM] for h in range(N_MEM_HEADS)], axis=1)
    o_ref[0] = o.astype(o_ref.dtype)


def _mem_attn_step(q, mem_t, layer):
    b = q.shape[0]
    n_mem = mem_t.shape[4]
    return pl.pallas_call(
        _mem_attn_step_kernel,
        grid=(b,),
        in_specs=[pl.BlockSpec((1, 1, D_MEM), lambda i: (i, 0, 0)),
                  pl.BlockSpec((1, 1, 2, D_MEM, n_mem), lambda i: (layer, i, 0, 0, 0))],
        out_specs=pl.BlockSpec((1, 1, D_MEM), lambda i: (i, 0, 0)),
        out_shape=jax.ShapeDtypeStruct((b, 1, D_MEM), BF16),
        compiler_params=_cparams(1),
        name="mem_attn_step",
    )(q, mem_t)


def _out_ln_kernel(x_ref, ma_ref, mm_ref, wa_ref, wm_ref, g_ref, b_ref, wr_ref, rb_ref, o_ref, grp_ref, *, alpha):
    sub = _dot(ma_ref[...], wa_ref[...]) + _dot(mm_ref[...], wm_ref[...])
    y = _layer_norm(alpha * x_ref[...] + sub, g_ref[...], b_ref[...])
    o_ref[...] = y
    grp, _ = _route_group(_dot(y.astype(BF16), wr_ref[...]) + rb_ref[...])
    grp_ref[...] = jnp.broadcast_to(grp, grp_ref.shape)


def _out_ln(x, mix_a, mix_m, wo_a, wo_m, g, b, wr, rb, alpha, tm):
    m, d = x.shape
    da = mix_a.shape[1]
    dm = mix_m.shape[1]
    return pl.pallas_call(
        functools.partial(_out_ln_kernel, alpha=alpha),
        grid=(m // tm,),
        in_specs=[pl.BlockSpec((tm, d), lambda i: (i, 0)),
                  pl.BlockSpec((tm, da), lambda i: (i, 0)),
                  pl.BlockSpec((tm, dm), lambda i: (i, 0)),
                  pl.BlockSpec((da, d), lambda i: (0, 0)),
                  pl.BlockSpec((dm, d), lambda i: (0, 0)),
                  pl.BlockSpec((1, d), lambda i: (0, 0)),
                  pl.BlockSpec((1, d), lambda i: (0, 0)),
                  pl.BlockSpec((d, LANES), lambda i: (0, 0)),
                  pl.BlockSpec((1, LANES), lambda i: (0, 0))],
        out_specs=[pl.BlockSpec((tm, d), lambda i: (i, 0)), pl.BlockSpec((tm, LANES), lambda i: (i, 0))],
        out_shape=[jax.ShapeDtypeStruct((m, d), F32), jax.ShapeDtypeStruct((m, LANES), I32)],
        compiler_params=_cparams(1),
        name="out_ln",
    )(x, mix_a, mix_m, wo_a, wo_m, g, b, wr, rb)


def _route_group(logits):
    lane = lax.broadcasted_iota(I32, logits.shape, 1)
    lg = jnp.where(lane < N_GROUPS, logits, -jnp.inf)
    gmax = jnp.max(lg, axis=-1, keepdims=True)
    return jnp.min(jnp.where(lg == gmax, lane, jnp.int32(4 * LANES)), axis=-1, keepdims=True), gmax


def _route(logits):
    lane = lax.broadcasted_iota(I32, logits.shape, 1)
    big = jnp.int32(4 * LANES)
    is_g = lane < N_GROUPS
    grp, gmax = _route_group(logits)
    gsum = jnp.sum(jnp.where(is_g, jnp.exp(logits - gmax), 0.0), axis=-1, keepdims=True)
    g_w = 1.0 / gsum
    e_idx = lane - N_GROUPS
    in_grp = (lane >= N_GROUPS) & (lane < N_GROUPS + N_EXPERTS) & ((e_idx >> 3) == grp)
    v = jnp.where(in_grp, logits, -jnp.inf)
    v1 = jnp.max(v, axis=-1, keepdims=True)
    i1 = jnp.min(jnp.where(v == v1, lane, big), axis=-1, keepdims=True)
    vr = jnp.where(lane == i1, -jnp.inf, v)
    v2 = jnp.max(vr, axis=-1, keepdims=True)
    i2 = jnp.min(jnp.where(vr == v2, lane, big), axis=-1, keepdims=True)
    e2 = jnp.exp(v2 - v1)
    w1 = g_w / (1.0 + e2)
    w2 = g_w * e2 / (1.0 + e2)
    return jnp.where(lane == i1, w1, 0.0) + jnp.where(lane == i2, w2, 0.0)


def _moe_kernel(x_ref, wr_ref, rb_ref, wgu_ref, wd_ref, g_ref, b_ref, o_ref,
                xb_ref, comb_ref, acc_ref, *, alpha):
    grp = pl.program_id(1)
    hid = EXPERTS_PER_GROUP * D_EXPERT

    @pl.when(grp == 0)
    def _():
        xh = x_ref[...].astype(BF16)
        logits = _dot(xh, wr_ref[...]) + rb_ref[...]
        xb_ref[...] = xh
        comb_ref[...] = _route(logits)
        acc_ref[...] = jnp.zeros(acc_ref.shape, F32)

    xb = xb_ref[...]
    gu = _dot(xb, wgu_ref[0])
    gate = gu[:, :hid]
    up = gu[:, hid:]
    hdn = gate * jax.nn.sigmoid(gate) * up
    comb = comb_ref[...]
    lane = lax.broadcasted_iota(I32, comb.shape, 1)
    scale = []
    for e in range(EXPERTS_PER_GROUP):
        col = jnp.sum(jnp.where(lane == N_GROUPS + grp * EXPERTS_PER_GROUP + e, comb, 0.0),
                      axis=-1, keepdims=True)
        scale.append(jnp.broadcast_to(col, (comb.shape[0], D_EXPERT)))
    hdn = hdn * jnp.concatenate(scale, axis=1)
    acc_ref[...] += _dot(hdn.astype(BF16), wd_ref[0])

    @pl.when(grp == N_GROUPS - 1)
    def _():
        o_ref[...] = _layer_norm(alpha * x_ref[...] + acc_ref[...], g_ref[...], b_ref[...])


def _moe_ln(x, wr, rb, wgu, wd, g, b, alpha, tm):
    m, d = x.shape
    hid = EXPERTS_PER_GROUP * D_EXPERT
    return pl.pallas_call(
        functools.partial(_moe_kernel, alpha=alpha),
        grid=(m // tm, N_GROUPS),
        in_specs=[pl.BlockSpec((tm, d), lambda i, j: (i, 0)),
                  pl.BlockSpec((d, LANES), lambda i, j: (0, 0)),
                  pl.BlockSpec((1, LANES), lambda i, j: (0, 0)),
                  pl.BlockSpec((1, d, 2 * hid), lambda i, j: (j, 0, 0)),
                  pl.BlockSpec((1, hid, d), lambda i, j: (j, 0, 0)),
                  pl.BlockSpec((1, d), lambda i, j: (0, 0)),
                  pl.BlockSpec((1, d), lambda i, j: (0, 0))],
        out_specs=pl.BlockSpec((tm, d), lambda i, j: (i, 0)),
        out_shape=jax.ShapeDtypeStruct((m, d), F32),
        scratch_shapes=[pltpu.VMEM((tm, d), BF16), pltpu.VMEM((tm, LANES), F32), pltpu.VMEM((tm, d), F32)],
        compiler_params=_cparams(2),
        name="moe_ln",
    )(x, wr, rb, wgu, wd, g, b)


MOE_TILE = 256


def _moe_dispatch(grp):
    n = grp.shape[0]
    onehot = (grp[:, None] == jnp.arange(N_GROUPS, dtype=I32)[None, :]).astype(I32)
    before = jnp.cumsum(onehot, axis=0) - onehot
    counts = jnp.sum(onehot, axis=0)
    padded = (counts + MOE_TILE - 1) // MOE_TILE * MOE_TILE
    ends = jnp.cumsum(padded)
    dest = jnp.sum(onehot * (before + (ends - padded)[None, :]), axis=1).astype(I32)
    n_slots = n + N_GROUPS * MOE_TILE
    tile_start = jnp.arange(n_slots // MOE_TILE, dtype=I32) * MOE_TILE
    tile_grp = jnp.minimum(jnp.sum((tile_start[:, None] >= ends[None, :]).astype(I32), axis=1), N_GROUPS - 1)
    return dest, tile_grp


def _moe_place_kernel(dest_ref, x_ref, spare_hbm, o_hbm, stage, sem, *, n_tiles):
    del spare_hbm
    i = pl.program_id(0)
    slot = i % 2

    def wait_tile(buf):
        pltpu.make_async_copy(stage.at[buf], o_hbm.at[pl.ds(0, MOE_TILE)], sem.at[buf]).wait()

    @pl.when(i >= 2)
    def _():
        wait_tile(slot)

    stage[slot] = x_ref[...]
    for r in range(MOE_TILE):
        pltpu.make_async_copy(stage.at[slot, pl.ds(r, 1)], o_hbm.at[pl.ds(dest_ref[i * MOE_TILE + r], 1)],
                              sem.at[slot]).start()

    @pl.when(i == n_tiles - 1)
    def _():
        @pl.when(i >= 1)
        def _():
            wait_tile(1 - slot)
        wait_tile(slot)


def _moe_place(x, dest, n_slots, spare):
    n, d = x.shape
    if spare is None:
        spare = jnp.zeros((n_slots, d), F32)
    assert n % MOE_TILE == 0
    n_tiles = n // MOE_TILE
    grid_spec = pltpu.PrefetchScalarGridSpec(
        num_scalar_prefetch=1,
        grid=(n_tiles,),
        in_specs=[pl.BlockSpec((MOE_TILE, d), lambda i, ds: (i, 0)), pl.BlockSpec(memory_space=pl.ANY)],
        out_specs=pl.BlockSpec(memory_space=pl.ANY),
        scratch_shapes=[pltpu.VMEM((2, MOE_TILE, d), F32), pltpu.SemaphoreType.DMA((2,))])
    return pl.pallas_call(
        functools.partial(_moe_place_kernel, n_tiles=n_tiles),
        grid_spec=grid_spec,
        out_shape=jax.ShapeDtypeStruct((n_slots, d), F32),
        input_output_aliases={2: 0},
        compiler_params=_cparams(1),
        name="moe_place",
    )(dest, x, spare)


def _moe_grouped_kernel(tgrp_ref, x_ref, wr_ref, rb_ref, wgu_ref, wd_ref, y_ref):
    i = pl.program_id(0)
    hid = EXPERTS_PER_GROUP * D_EXPERT
    xh = x_ref[...].astype(BF16)
    comb = _route(_dot(xh, wr_ref[...]) + rb_ref[...])
    gu = _dot(xh, wgu_ref[0])
    gate = gu[:, :hid]
    hdn = gate * jax.nn.sigmoid(gate) * gu[:, hid:]
    grp = tgrp_ref[i]
    lane = lax.broadcasted_iota(I32, comb.shape, 1)
    scale = []
    for e in range(EXPERTS_PER_GROUP):
        col = jnp.sum(jnp.where(lane == N_GROUPS + grp * EXPERTS_PER_GROUP + e, comb, 0.0),
                      axis=-1, keepdims=True)
        scale.append(jnp.broadcast_to(col, (comb.shape[0], D_EXPERT)))
    y_ref[...] = _dot((hdn * jnp.concatenate(scale, axis=1)).astype(BF16), wd_ref[0])


def _moe_grouped(xs, tile_grp, wr, rb, wgu, wd):
    n_slots, d = xs.shape
    hid = EXPERTS_PER_GROUP * D_EXPERT
    grid_spec = pltpu.PrefetchScalarGridSpec(
        num_scalar_prefetch=1,
        grid=(n_slots // MOE_TILE,),
        in_specs=[pl.BlockSpec((MOE_TILE, d), lambda i, tg: (i, 0)),
                  pl.BlockSpec((d, LANES), lambda i, tg: (0, 0)),
                  pl.BlockSpec((1, LANES), lambda i, tg: (0, 0)),
                  pl.BlockSpec((1, d, 2 * hid), lambda i, tg: (tg[i], 0, 0)),
                  pl.BlockSpec((1, hid, d), lambda i, tg: (tg[i], 0, 0))],
        out_specs=pl.BlockSpec((MOE_TILE, d), lambda i, tg: (i, 0)))
    return pl.pallas_call(
        _moe_grouped_kernel,
        grid_spec=grid_spec,
        out_shape=jax.ShapeDtypeStruct((n_slots, d), F32),
        compiler_params=_cparams(1),
        name="moe_grouped",
    )(tile_grp, xs, wr, rb, wgu, wd)


def _moe_return_ln_kernel(dest_ref, x_ref, y_hbm, g_ref, b_ref, o_ref, ybuf, sem, *, alpha, n_tiles):
    i = pl.program_id(0)
    slot = i % 2

    def start_tile(tile, buf):
        for r in range(MOE_TILE):
            pltpu.make_async_copy(y_hbm.at[pl.ds(dest_ref[tile * MOE_TILE + r], 1)],
                                  ybuf.at[buf, pl.ds(r, 1)], sem.at[buf]).start()

    @pl.when(i == 0)
    def _():
        start_tile(i, slot)

    @pl.when(i + 1 < n_tiles)
    def _():
        start_tile(i + 1, 1 - slot)

    pltpu.make_async_copy(y_hbm.at[pl.ds(0, MOE_TILE)], ybuf.at[slot], sem.at[slot]).wait()
    o_ref[...] = _layer_norm(alpha * x_ref[...] + ybuf[slot], g_ref[...], b_ref[...])


def _moe_return_ln(x, ys, dest, g, b, alpha):
    n, d = x.shape
    n_tiles = n // MOE_TILE
    grid_spec = pltpu.PrefetchScalarGridSpec(
        num_scalar_prefetch=1,
        grid=(n_tiles,),
        in_specs=[pl.BlockSpec((MOE_TILE, d), lambda i, ds: (i, 0)),
                  pl.BlockSpec(memory_space=pl.ANY),
                  pl.BlockSpec((1, d), lambda i, ds: (0, 0)),
                  pl.BlockSpec((1, d), lambda i, ds: (0, 0))],
        out_specs=pl.BlockSpec((MOE_TILE, d), lambda i, ds: (i, 0)),
        scratch_shapes=[pltpu.VMEM((2, MOE_TILE, d), F32), pltpu.SemaphoreType.DMA((2,))])
    return pl.pallas_call(
        functools.partial(_moe_return_ln_kernel, alpha=alpha, n_tiles=n_tiles),
        grid_spec=grid_spec,
        out_shape=jax.ShapeDtypeStruct((n, d), F32),
        compiler_params=_cparams(1),
        name="moe_return_ln",
    )(dest, x, ys, g, b)


CMP_MBLK = 128
ROW_TILES = D_KV // LANES
PAIR_HID = 2 * CMP_HID


def _compress_weights(pe, w1, b1, w2):
    w1 = w1.reshape(2, CMP_LEN, HEAD_DIM, CMP_HID)
    z1 = jnp.zeros_like(w1)
    w1p = jnp.concatenate([jnp.concatenate([w1, z1], axis=3), jnp.concatenate([z1, w1], axis=3)], axis=2)
    z2 = jnp.zeros_like(w2)
    w2p = jnp.concatenate([jnp.concatenate([w2, z2], axis=2), jnp.concatenate([z2, w2], axis=2)], axis=1)
    w1p = w1p.astype(BF16).reshape(2, 2, CMP_STRIDE * LANES, PAIR_HID)
    return (jnp.concatenate([pe, pe], axis=2), w1p,
            jnp.concatenate([b1, b1], axis=1)[:, None], w2p.astype(BF16))


def _load_interleaved(x_ref):
    step = CMP_STRIDE * ROW_TILES

    def load(tile, r, mb):
        row0 = pl.multiple_of(mb * (CMP_MBLK * step), CMP_MBLK * step)
        return x_ref[pl.ds(row0 + r * ROW_TILES + tile, CMP_MBLK, stride=step), :]
    return load


def _load_tiled(x_ref):
    def load(tile, r, mb):
        row0 = pl.multiple_of(mb * (CMP_MBLK * CMP_STRIDE), CMP_MBLK * CMP_STRIDE)
        return x_ref[tile, pl.ds(row0 + r, CMP_MBLK, stride=CMP_STRIDE), :]
    return load


def _compress_rows(load, n_chunk, pe_ref, w1_ref, b1_ref, w2_ref, out_ref, a_ref, b_ref):
    n_blk = n_chunk // CMP_MBLK
    for kv in range(2):
        for pair in range(N_KV_GROUPS // 2):
            tile = kv * (N_KV_GROUPS // 2) + pair

            def fill(mb, carry, kv=kv, tile=tile):
                top, bot = [], []
                for r in range(CMP_STRIDE):
                    xg = load(tile, r, mb)
                    top.append((xg + pe_ref[kv, r:r + 1, :]).astype(BF16))
                    bot.append((xg + pe_ref[kv, CMP_STRIDE + r:CMP_STRIDE + r + 1, :]).astype(BF16))
                m0 = pl.multiple_of(mb * CMP_MBLK, CMP_MBLK)
                a_ref[pl.ds(m0, CMP_MBLK), :] = _dot(jnp.concatenate(top, axis=1), w1_ref[kv, 0])
                b_ref[pl.ds(m0, CMP_MBLK), :] = _dot(jnp.concatenate(bot, axis=1), w1_ref[kv, 1])
                return carry

            lax.fori_loop(0, n_blk, fill, 0)
            b_ref[n_chunk:n_chunk + SUBLANES, :] = jnp.zeros((SUBLANES, PAIR_HID), F32)
            for mb in range(n_blk):
                m0 = mb * CMP_MBLK
                hid = jax.nn.gelu(a_ref[m0:m0 + CMP_MBLK, :] + b_ref[m0 + 1:m0 + 1 + CMP_MBLK, :] + b1_ref[kv])
                out_ref[m0:m0 + CMP_MBLK, tile * LANES:(tile + 1) * LANES] = _dot(hid.astype(BF16), w2_ref[kv])


def _compress_prompt_kernel(x_ref, pe_ref, w1_ref, b1_ref, w2_ref, o_ref, a_ref, b_ref, *, n_chunk):
    _compress_rows(_load_interleaved(x_ref.at[0]), n_chunk, pe_ref, w1_ref, b1_ref, w2_ref, o_ref.at[0],
                   a_ref, b_ref)


def _compress_prompt(rows, pe, w1, b1, w2):
    b, t_len, _ = rows.shape
    n_chunk = t_len // CMP_STRIDE
    assert n_chunk % CMP_MBLK == 0
    full = lambda a: pl.BlockSpec(a.shape, lambda i: (0,) * a.ndim)
    return pl.pallas_call(
        functools.partial(_compress_prompt_kernel, n_chunk=n_chunk),
        grid=(b,),
        in_specs=[pl.BlockSpec((1, t_len * ROW_TILES, LANES), lambda i: (i, 0, 0)),
                  full(pe), full(w1), full(b1), full(w2)],
        out_specs=pl.BlockSpec((1, n_chunk, D_KV), lambda i: (i, 0, 0)),
        out_shape=jax.ShapeDtypeStruct((b, n_chunk, D_KV), F32),
        scratch_shapes=[pltpu.VMEM((n_chunk, PAIR_HID), F32), pltpu.VMEM((n_chunk + SUBLANES, PAIR_HID), F32)],
        compiler_params=_cparams(1),
        name="compress_prompt",
    )(rows.reshape(b, t_len * ROW_TILES, LANES), pe, w1, b1, w2)


def _masked_softmax_rows(s, mask):
    sm = jnp.where(mask, s, NEG)
    m = jnp.max(sm, axis=-1, keepdims=True)
    e = jnp.where(mask, jnp.exp(sm - m), 0.0)
    tot = jnp.sum(e, axis=-1, keepdims=True)
    return e * jnp.where(tot > 0.0, 1.0 / tot, 0.0)


WIN_TILES = WINDOW // Q_BLOCK


KEY_TILE = 512
KEY_SUB = KEY_TILE // Q_BLOCK


def _attend_tiles_t(qts, k_ref, vt_ref, t_lo, t_hi, i_blk, bias_ref, bias_index, mask_fn):
    cols = qts[0].shape[1]

    def body(t, carry):
        k0 = pl.multiple_of(t * KEY_TILE, KEY_TILE)
        bases = [bias_index(i_blk - (t * KEY_SUB + u)) for u in range(KEY_SUB)]
        scores = [_dot(k_ref[0, g, pl.ds(k0, KEY_TILE), :], qt) for g, qt in enumerate(qts)]
        new = []
        for g in range(len(qts)):
            m, l, acc = carry[g]
            vt = vt_ref[0, g * HEAD_DIM:(g + 1) * HEAD_DIM, pl.ds(k0, KEY_TILE)]
            heads = [g * HEADS_PER_GROUP + hp for hp in range(HEADS_PER_GROUP)]
            bias = jnp.concatenate(
                [jnp.concatenate([bias_ref[base + h] for h in heads], axis=1) for base in bases], axis=0)
            s = scores[g] + bias
            extra = mask_fn(t, g)
            if extra is not None:
                s = s + jnp.concatenate([extra] * HEADS_PER_GROUP, axis=1)
            m_new = jnp.maximum(m, jnp.max(s, axis=0, keepdims=True))
            alpha = jnp.exp(m - m_new)
            p = jnp.exp(s - m_new)
            new.append((m_new, alpha * l + jnp.sum(p, axis=0, keepdims=True), alpha * acc + _dot(vt, p.astype(BF16))))
        return tuple(new)

    init = tuple((jnp.full((1, cols), NEG, F32), jnp.zeros((1, cols), F32), jnp.zeros((HEAD_DIM, cols), F32))
                 for _ in qts)
    fin = lax.fori_loop(t_lo, t_hi, body, init)
    return [acc / l for (_, l, acc) in fin]


def _nsa_prompt_kernel(tab_ref, q_ref, gl_ref, ck_ref, cvt_ref, sk_ref, svt_ref, wk_ref, wvt_ref, o_ref,
                       bias_ref, cbias_ref, sel_ref, *, n_tiles):
    b = pl.program_id(0)
    i = pl.program_id(1)
    kk = lax.broadcasted_iota(I32, (Q_BLOCK, Q_BLOCK), 0)
    qq = lax.broadcasted_iota(I32, (Q_BLOCK, Q_BLOCK), 1)
    n_cmp = ck_ref.shape[2]
    n_sel_blocks = 2 * n_tiles
    edge_base = n_tiles * N_HEADS
    none_base = (n_tiles + 1) * N_HEADS

    @pl.when((b == 0) & (i == 0))
    def _():
        def build(delta, carry):
            dist = delta * Q_BLOCK + qq - kk
            biases = _bias_from_bucket(_rel_bucket(dist), tab_ref, range(N_HEADS))
            for h in range(N_HEADS):
                bias_ref[delta * N_HEADS + h] = jnp.where(dist >= 0, biases[h], NEG)
            return carry
        lax.fori_loop(0, n_tiles, build, 0)
        dist = WIN_TILES * Q_BLOCK + qq - kk
        biases = _bias_from_bucket(_rel_bucket(dist), tab_ref, range(N_HEADS))
        for h in range(N_HEADS):
            bias_ref[edge_base + h] = jnp.where(dist < WINDOW, biases[h], NEG)
            bias_ref[none_base + h] = jnp.full((Q_BLOCK, Q_BLOCK), NEG, F32)

        def build_c(t, carry):
            dist_c = t * Q_BLOCK + qq - (kk * CMP_STRIDE + (CMP_LEN - 1))
            ok = (dist_c >= 0) & (kk < n_cmp - 1)
            biases_c = _bias_from_bucket(_rel_bucket(dist_c), tab_ref, range(N_HEADS))
            for h in range(N_HEADS):
                cbias_ref[t * N_HEADS + h] = jnp.where(ok, biases_c[h], NEG)
            return carry
        lax.fori_loop(0, n_tiles, build_c, 0)

    qs = i * Q_BLOCK
    q_all = q_ref[0] * SCALE
    gates = jax.nn.sigmoid(gl_ref[0])

    blk = lax.broadcasted_iota(I32, (n_sel_blocks, Q_BLOCK), 0)
    col = lax.broadcasted_iota(I32, (n_sel_blocks, Q_BLOCK), 1)
    c_start = col * CMP_STRIDE
    s_start = blk * SEL_LEN
    cover_t = jnp.where((c_start < s_start + SEL_LEN) & (c_start + CMP_LEN > s_start) & (col < n_cmp - 1),
                        1.0, 0.0).astype(BF16)
    q_pos = qs + col
    cur = q_pos // SEL_LEN
    forced = (blk == 0) | (blk == cur) | (blk == cur - 1)
    valid = blk * SEL_LEN <= q_pos
    upper = kk < SEL_LEN

    qts, o_cs, splits = [], [], []
    for g in range(N_KV_GROUPS):
        heads = [g * HEADS_PER_GROUP + hp for hp in range(HEADS_PER_GROUP)]
        qt = jnp.concatenate([q_all[h * HEAD_DIM:(h + 1) * HEAD_DIM, :] for h in heads], axis=1)
        qts.append(qt)

        s_c = _dot(ck_ref[0, g], qt) + jnp.concatenate([cbias_ref[i * N_HEADS + h] for h in heads], axis=1)
        m_c = jnp.max(s_c, axis=0, keepdims=True)
        e_c = jnp.exp(s_c - m_c)
        tot = jnp.sum(e_c, axis=0, keepdims=True)
        p_c = e_c * jnp.where(m_c > 0.5 * NEG, 1.0 / tot, 0.0)
        o_cs.append(_dot(cvt_ref[0, g * HEAD_DIM:(g + 1) * HEAD_DIM, :], p_c.astype(BF16)))
        p_sum = p_c[:, 0:Q_BLOCK]
        for hp in range(1, HEADS_PER_GROUP):
            p_sum = p_sum + p_c[:, hp * Q_BLOCK:(hp + 1) * Q_BLOCK]
        splits.append(p_sum.astype(BF16))

    imp_all = _dot(cover_t, jnp.concatenate(splits, axis=1))
    for g in range(N_KV_GROUPS):
        imp = imp_all[:, g * Q_BLOCK:(g + 1) * Q_BLOCK]
        score = jnp.where(forced, FORCE, imp)
        score = jnp.where(valid, score, NEG)
        cnt = jnp.zeros(score.shape, F32)
        for s in range(n_sel_blocks):
            row = score[s:s + 1, :]
            beats = (row > score) | ((row == score) & (blk > s))
            cnt = cnt + jnp.where(beats, 1.0, 0.0)
        sel = (cnt < float(N_SEL)) & (score > 0.5 * NEG)
        sel_ref[g] = jnp.where(sel, 0.0, NEG)

    def mask_sel(t, g):
        rows = sel_ref[g, pl.ds(pl.multiple_of(2 * KEY_SUB * t, 2 * KEY_SUB), 2 * KEY_SUB), :]
        return jnp.concatenate([jnp.where(upper, rows[2 * u:2 * u + 1], rows[2 * u + 1:2 * u + 2])
                                for u in range(KEY_SUB)], axis=0)
    last = i // KEY_SUB + 1
    o_ss = _attend_tiles_t(qts, sk_ref, svt_ref, 0, last, i, bias_ref,
                           lambda d: jnp.where(d < 0, none_base, d * N_HEADS), mask_sel)

    def win_index(d):
        return jnp.where((d < 0) | (d > WIN_TILES), none_base, jnp.where(d == WIN_TILES, edge_base, d * N_HEADS))
    o_ws = _attend_tiles_t(qts, wk_ref, wvt_ref, jnp.maximum(i - WIN_TILES, 0) // KEY_SUB, last, i, bias_ref,
                           win_index, lambda t, g: None)

    for h in range(N_HEADS):
        g, hp = divmod(h, HEADS_PER_GROUP)
        c0 = hp * Q_BLOCK
        o_h = (gates[3 * h:3 * h + 1, :] * o_cs[g][:, c0:c0 + Q_BLOCK]
               + gates[3 * h + 1:3 * h + 2, :] * o_ss[g][:, c0:c0 + Q_BLOCK]
               + gates[3 * h + 2:3 * h + 3, :] * o_ws[g][:, c0:c0 + Q_BLOCK])
        o_ref[0, h * HEAD_DIM:(h + 1) * HEAD_DIM, :] = o_h.astype(o_ref.dtype)


def _nsa_prompt(tab, q_t, gl_t, comp_k, comp_vt, slc_k, slc_vt, win_k, win_vt):
    b, dq, t_len = q_t.shape
    n_tiles = t_len // Q_BLOCK
    n_cmp = comp_k.shape[2]
    assert n_cmp == Q_BLOCK and t_len % KEY_TILE == 0 and n_tiles > WIN_TILES
    dv = N_KV_GROUPS * HEAD_DIM
    return pl.pallas_call(
        functools.partial(_nsa_prompt_kernel, n_tiles=n_tiles),
        grid=(b, n_tiles),
        in_specs=[pl.BlockSpec(memory_space=pltpu.SMEM),
                  pl.BlockSpec((1, dq, Q_BLOCK), lambda i, j: (i, 0, j)),
                  pl.BlockSpec((1, LANES, Q_BLOCK), lambda i, j: (i, 0, j)),
                  pl.BlockSpec((1, N_KV_GROUPS, n_cmp, HEAD_DIM), lambda i, j: (i, 0, 0, 0)),
                  pl.BlockSpec((1, dv, n_cmp), lambda i, j: (i, 0, 0)),
                  pl.BlockSpec((1, N_KV_GROUPS, t_len, HEAD_DIM), lambda i, j: (i, 0, 0, 0)),
                  pl.BlockSpec((1, dv, t_len), lambda i, j: (i, 0, 0)),
                  pl.BlockSpec((1, N_KV_GROUPS, t_len, HEAD_DIM), lambda i, j: (i, 0, 0, 0)),
                  pl.BlockSpec((1, dv, t_len), lambda i, j: (i, 0, 0))],
        out_specs=pl.BlockSpec((1, dq, Q_BLOCK), lambda i, j: (i, 0, j)),
        out_shape=jax.ShapeDtypeStruct((b, dq, t_len), BF16),
        scratch_shapes=[pltpu.VMEM(((n_tiles + 2) * N_HEADS, Q_BLOCK, Q_BLOCK), F32),
                        pltpu.VMEM((n_tiles * N_HEADS, Q_BLOCK, Q_BLOCK), F32),
                        pltpu.VMEM((N_KV_GROUPS, 2 * n_tiles, Q_BLOCK), F32)],
        compiler_params=_cparams(2),
        name="nsa_prompt",
    )(tab, q_t, gl_t, comp_k, comp_vt, slc_k, slc_vt, win_k, win_vt)


def _page_copy(pt_ref, pool_ref, buf_ref, sem, b, j, page):
    start = pl.multiple_of(j * page, page)
    if len(buf_ref.shape) == 2:
        dst = buf_ref.at[pl.ds(start, page)]
    else:
        dst = buf_ref.at[:, :, pl.ds(start, page)]
    return pltpu.make_async_copy(pool_ref.at[pt_ref[b, j]], dst, sem)


def _gather_pages_start(pt_ref, pool_ref, buf_ref, sem, b, n_pages, page):
    def go(j, carry):
        _page_copy(pt_ref, pool_ref, buf_ref, sem, b, j, page).start()
        return carry
    lax.fori_loop(0, n_pages, go, 0)


def _gather_pages_wait(pt_ref, pool_ref, buf_ref, sem, b, n_pages, page):
    def go(j, carry):
        _page_copy(pt_ref, pool_ref, buf_ref, sem, b, j, page).wait()
        return carry
    lax.fori_loop(0, n_pages, go, 0)


def _compress_pages_kernel(pt_ref, pool_ref, pe_ref, w1_ref, b1_ref, w2_ref, o_ref, raw_ref, x_ref, a_ref, b_ref,
                           sem, *, n_seq, n_pages, page, n_chunk):
    b = pl.program_id(0)

    @pl.when(b == 0)
    def _():
        _gather_pages_start(pt_ref, pool_ref, raw_ref, sem, b, n_pages, page)

    _gather_pages_wait(pt_ref, pool_ref, raw_ref, sem, b, n_pages, page)

    def relayout(j, carry):
        r0 = pl.multiple_of(j * page, page)
        for tile in range(ROW_TILES):
            kv, pair = divmod(tile, N_KV_GROUPS // 2)
            x_ref[tile, pl.ds(r0, page), :] = raw_ref[kv, pair * LANES:(pair + 1) * LANES, pl.ds(r0, page)].T
        return carry
    lax.fori_loop(0, n_pages, relayout, 0)

    @pl.when(b + 1 < n_seq)
    def _():
        _gather_pages_start(pt_ref, pool_ref, raw_ref, sem, b + 1, n_pages, page)

    _compress_rows(_load_tiled(x_ref), n_chunk, pe_ref, w1_ref, b1_ref, w2_ref, o_ref.at[0], a_ref, b_ref)


def _compress_pages(page_table, pool_t, pe, w1, b1, w2):
    b, n_pages = page_table.shape
    page = pool_t.shape[3]
    past = n_pages * page
    n_chunk = past // CMP_STRIDE
    assert n_chunk % CMP_MBLK == 0 and page == LANES
    full = lambda a: pl.BlockSpec(a.shape, lambda i: (0,) * a.ndim)
    return pl.pallas_call(
        functools.partial(_compress_pages_kernel, n_seq=b, n_pages=n_pages, page=page, n_chunk=n_chunk),
        grid=(b,),
        in_specs=[pl.BlockSpec(memory_space=pltpu.SMEM), pl.BlockSpec(memory_space=pl.ANY),
                  full(pe), full(w1), full(b1), full(w2)],
        out_specs=pl.BlockSpec((1, n_chunk, D_KV), lambda i: (i, 0, 0)),
        out_shape=jax.ShapeDtypeStruct((b, n_chunk, D_KV), F32),
        scratch_shapes=[pltpu.VMEM((2, D_K, past), F32),
                        pltpu.VMEM((ROW_TILES, past, LANES), F32),
                        pltpu.VMEM((n_chunk, PAIR_HID), F32), pltpu.VMEM((n_chunk + SUBLANES, PAIR_HID), F32),
                        pltpu.SemaphoreType.DMA(())],
        compiler_params=_cparams(1),
        name="compress_pages",
    )(page_table, pool_t, pe, w1, b1, w2)


DEC_TILE = 1024
DEC_ROWS = 16
D_K = D_KV // 2


def _row_bias(dist_row, tab_ref, bias_ref, col0):
    length = dist_row.shape[1]
    biases = _bias_from_bucket(_rel_bucket(dist_row), tab_ref, range(N_HEADS))
    rows = biases + [jnp.zeros((DEC_ROWS - N_HEADS, length), F32)]
    bias_ref[:, pl.ds(col0, length)] = jnp.concatenate(rows, axis=0)


def _head_block(o_all, h):
    g = h // HEADS_PER_GROUP
    return o_all[h:h + 1, g * HEAD_DIM:(g + 1) * HEAD_DIM]


def _new_token_scores(q_blk, new_row, bias0):
    k_new = new_row[:, :D_K].astype(BF16).astype(F32)
    v_new = new_row[:, D_K:].astype(BF16).astype(F32)
    return jnp.sum(q_blk.astype(F32) * k_new, axis=-1, keepdims=True) + bias0, v_new


def _nsa_step_kernel(tab_ref, pt_ref, q_ref, gl_ref, comp_ref, pool_ref, slc_new_ref, win_ref, win_new_ref, o_ref,
                     kv_buf, bias_c_ref, bias_s_ref, bias_w_ref, sems,
                     *, n_seq, n_pages, page, past, n_cmp, w_buf):
    b = pl.program_id(0)
    slot = b % 2
    n_tiles = past // DEC_TILE
    n_blocks = past // SEL_LEN + 1
    blk_lanes = -(-n_blocks // LANES) * LANES

    @pl.when(b == 0)
    def _():
        _gather_pages_start(pt_ref, pool_ref, kv_buf.at[0], sems.at[0], b, n_pages, page)

    @pl.when(b + 1 < n_seq)
    def _():
        _gather_pages_start(pt_ref, pool_ref, kv_buf.at[1 - slot], sems.at[1 - slot], b + 1, n_pages, page)

    lane_t = lax.broadcasted_iota(I32, (1, DEC_TILE), 1)

    @pl.when(b == 0)
    def _():
        lane_c = lax.broadcasted_iota(I32, (1, n_cmp), 1)
        _row_bias(past - (lane_c * CMP_STRIDE + (CMP_LEN - 1)), tab_ref, bias_c_ref, 0)

        def tile_bias(t, carry):
            k0 = pl.multiple_of(t * DEC_TILE, DEC_TILE)
            _row_bias(past - (k0 + lane_t), tab_ref, bias_s_ref, k0)
            return carry
        lax.fori_loop(0, n_tiles, tile_bias, 0)
        lane_w = lax.broadcasted_iota(I32, (1, w_buf), 1)
        _row_bias(w_buf - lane_w, tab_ref, bias_w_ref, 0)

    bias0 = jnp.concatenate([jnp.full((1, 1), tab_ref[0, h], F32) for h in range(N_HEADS)]
                            + [jnp.zeros((DEC_ROWS - N_HEADS, 1), F32)], axis=0)

    q_row = q_ref[0] * SCALE
    gates = jax.nn.sigmoid(gl_ref[0])
    rows = []
    for h in range(N_HEADS):
        g = h // HEADS_PER_GROUP
        parts = []
        if g > 0:
            parts.append(jnp.zeros((1, g * HEAD_DIM), BF16))
        parts.append(q_row[:, h * HEAD_DIM:(h + 1) * HEAD_DIM])
        if g < N_KV_GROUPS - 1:
            parts.append(jnp.zeros((1, (N_KV_GROUPS - 1 - g) * HEAD_DIM), BF16))
        rows.append(jnp.concatenate(parts, axis=1))
    q_blk = jnp.concatenate(rows + [jnp.zeros((DEC_ROWS - N_HEADS, D_K), BF16)], axis=0)

    comp = comp_ref[0].astype(BF16)
    mask_c = lax.broadcasted_iota(I32, (DEC_ROWS, n_cmp), 1) < n_cmp - 1
    p_c = _masked_softmax_rows(_dot_nt(q_blk, comp[:, :D_K]) + bias_c_ref[...], mask_c)
    o_c = _dot(p_c.astype(BF16), comp[:, D_K:])
    p_sums = [jnp.sum(p_c[g * HEADS_PER_GROUP:(g + 1) * HEADS_PER_GROUP], axis=0, keepdims=True)
              for g in range(N_KV_GROUPS)]
    p_sum = jnp.concatenate(p_sums + [jnp.zeros((DEC_ROWS - N_KV_GROUPS, n_cmp), F32)], axis=0)

    c_start = lax.broadcasted_iota(I32, (n_cmp, blk_lanes), 0) * CMP_STRIDE
    s_start = lax.broadcasted_iota(I32, (n_cmp, blk_lanes), 1) * SEL_LEN
    c_idx = lax.broadcasted_iota(I32, (n_cmp, blk_lanes), 0)
    cover = jnp.where((c_start < s_start + SEL_LEN) & (c_start + CMP_LEN > s_start) & (c_idx < n_cmp - 1),
                      1.0, 0.0).astype(BF16)
    imp = _dot(p_sum.astype(BF16), cover)
    blk = lax.broadcasted_iota(I32, (DEC_ROWS, blk_lanes), 1)
    cur = past // SEL_LEN
    forced = (blk == 0) | (blk == cur) | (blk == cur - 1)
    score = jnp.where(forced, FORCE, imp)
    score = jnp.where(blk < n_blocks, score, NEG)
    score_t = score.T
    j_idx = lax.broadcasted_iota(I32, (blk_lanes, blk_lanes), 0)
    i_idx = lax.broadcasted_iota(I32, (blk_lanes, blk_lanes), 1)
    sel_groups = []
    for g in range(N_KV_GROUPS):
        col = score_t[:, g:g + 1]
        row = score[g:g + 1, :]
        beats = (col > row) | ((col == row) & (j_idx < i_idx))
        rank = jnp.sum(jnp.where(beats, 1.0, 0.0), axis=0, keepdims=True)
        sel_groups.append(jnp.where((rank < float(N_SEL)) & (row > 0.5 * NEG), 1.0, 0.0))
    sel_rows = [sel_groups[h // HEADS_PER_GROUP] for h in range(N_HEADS)]
    sel_bf = jnp.concatenate(sel_rows + [jnp.zeros((DEC_ROWS - N_HEADS, blk_lanes), F32)], axis=0).astype(BF16)

    _gather_pages_wait(pt_ref, pool_ref, kv_buf.at[slot], sems.at[slot], b, n_pages, page)
    e_row = lax.broadcasted_iota(I32, (blk_lanes, DEC_TILE), 0)
    e_col = lax.broadcasted_iota(I32, (blk_lanes, DEC_TILE), 1) // SEL_LEN

    def sel_tile(t, carry):
        m, l, acc = carry
        k0 = pl.multiple_of(t * DEC_TILE, DEC_TILE)
        expand = jnp.where(e_row == e_col + t * (DEC_TILE // SEL_LEN), 1.0, 0.0).astype(BF16)
        key_sel = _dot(sel_bf, expand)
        kt = kv_buf[slot, 0, :, pl.ds(k0, DEC_TILE)].astype(BF16)
        vt = kv_buf[slot, 1, :, pl.ds(k0, DEC_TILE)].astype(BF16)
        s = _dot(q_blk, kt) + bias_s_ref[:, pl.ds(k0, DEC_TILE)] + jnp.where(key_sel > 0.5, 0.0, NEG)
        m_new = jnp.maximum(m, jnp.max(s, axis=-1, keepdims=True))
        alpha = jnp.exp(m - m_new)
        p = jnp.exp(s - m_new)
        return m_new, alpha * l + jnp.sum(p, axis=-1, keepdims=True), alpha * acc + _dot_nt(p.astype(BF16), vt)

    init = (jnp.full((DEC_ROWS, 1), NEG, F32), jnp.zeros((DEC_ROWS, 1), F32), jnp.zeros((DEC_ROWS, D_K), F32))
    m_s, l_s, acc_s = lax.fori_loop(0, n_tiles, sel_tile, init)
    s_new, v_new = _new_token_scores(q_blk, slc_new_ref[0], bias0)
    cur_sel = sel_bf[:, cur:cur + 1].astype(F32)
    s_new = s_new + jnp.where(cur_sel > 0.5, 0.0, NEG)
    m_fin = jnp.maximum(m_s, s_new)
    alpha = jnp.exp(m_s - m_fin)
    p_new = jnp.exp(s_new - m_fin)
    o_s = (alpha * acc_s + p_new.astype(BF16).astype(F32) * v_new) / (alpha * l_s + p_new)

    kw = win_ref[0, 0].astype(BF16)
    vw = win_ref[0, 1].astype(BF16)
    idx_w = lax.broadcasted_iota(I32, (DEC_ROWS, w_buf), 1)
    s_w = jnp.where(w_buf - idx_w < WINDOW, _dot(q_blk, kw) + bias_w_ref[...], NEG)
    s_wn, v_wn = _new_token_scores(q_blk, win_new_ref[0], bias0)
    m_w = jnp.maximum(jnp.max(s_w, axis=-1, keepdims=True), s_wn)
    e_w = jnp.exp(s_w - m_w)
    e_wn = jnp.exp(s_wn - m_w)
    o_w = ((_dot_nt(e_w.astype(BF16), vw) + e_wn.astype(BF16).astype(F32) * v_wn)
           / (jnp.sum(e_w, axis=-1, keepdims=True) + e_wn))

    outs = []
    for h in range(N_HEADS):
        outs.append(gates[:, 3 * h:3 * h + 1] * _head_block(o_c, h)
                    + gates[:, 3 * h + 1:3 * h + 2] * _head_block(o_s, h)
                    + gates[:, 3 * h + 2:3 * h + 3] * _head_block(o_w, h))
    o_ref[0] = jnp.concatenate(outs, axis=1).astype(o_ref.dtype)


def _feature_major(cache):
    nd = cache.ndim
    perm = tuple(range(nd - 4)) + (nd - 3, nd - 2, nd - 1, nd - 4)
    t = cache.transpose(perm)
    return t.reshape(t.shape[:nd - 3] + (t.shape[nd - 3] * t.shape[nd - 2], t.shape[nd - 1]))


def _nsa_step(tab, page_table, q, gl, comp, pool_t, slc_new, win_t, win_new):
    b, n_pages = page_table.shape
    page = pool_t.shape[3]
    past = n_pages * page
    n_cmp = comp.shape[1]
    w_buf = win_t.shape[3]
    dq = q.shape[2]
    assert past % DEC_TILE == 0 and past % SEL_LEN == 0 and w_buf == WINDOW and n_cmp == past // CMP_STRIDE
    row3 = lambda w: pl.BlockSpec((1, 1, w), lambda i: (i, 0, 0))
    return pl.pallas_call(
        functools.partial(_nsa_step_kernel, n_seq=b, n_pages=n_pages, page=page, past=past, n_cmp=n_cmp,
                          w_buf=w_buf),
        grid=(b,),
        in_specs=[pl.BlockSpec(memory_space=pltpu.SMEM), pl.BlockSpec(memory_space=pltpu.SMEM),
                  row3(dq), row3(LANES),
                  pl.BlockSpec((1, n_cmp, D_KV), lambda i: (i, 0, 0)),
                  pl.BlockSpec(memory_space=pl.ANY),
                  row3(D_KV),
                  pl.BlockSpec((1, 2, D_K, w_buf), lambda i: (i, 0, 0, 0)),
                  row3(D_KV)],
        out_specs=row3(dq),
        out_shape=jax.ShapeDtypeStruct((b, 1, dq), BF16),
        scratch_shapes=[pltpu.VMEM((2, 2, D_K, past), F32),
                        pltpu.VMEM((DEC_ROWS, n_cmp), F32),
                        pltpu.VMEM((DEC_ROWS, past), F32),
                        pltpu.VMEM((DEC_ROWS, w_buf), F32),
                        pltpu.SemaphoreType.DMA((2,))],
        compiler_params=_cparams(1),
        name="nsa_step",
    )(tab, page_table, q, gl, comp, pool_t, slc_new, win_t, win_new)


def _block_diag_pairs(w):
    n, blk, _ = w.shape
    w = w.reshape(n // 2, 2, blk, blk)
    z = jnp.zeros((n // 2, blk, blk), w.dtype)
    top = jnp.concatenate([w[:, 0], z], axis=2)
    bot = jnp.concatenate([z, w[:, 1]], axis=2)
    return jnp.concatenate([top, bot], axis=1).astype(BF16)


def _prep_weights(w_in_a, conv_w, conv_b, w_rg, b_rg, w_ig, b_ig, lru_lambda, w_in_b, w_kv_shared, cmp_pe, cmp_w1,
                  cmp_b1, cmp_w2, w_mem_kv, w_out, ln1_g, ln1_b, ln2_g, ln2_b, w_router_g, b_router_g,
                  w_router_e, b_router_e, w_exp_gate, w_exp_up, w_exp_down):
    depth, d_model, _ = w_out.shape
    n_a = w_in_a.shape[0]
    nq = N_HEADS * HEAD_DIM
    n_gl = 3 * N_HEADS
    hid = EXPERTS_PER_GROUP * D_EXPERT
    p = {}
    p["w_in_a"] = w_in_a.astype(BF16)
    p["w_in_b"] = jnp.concatenate(
        [w_in_b[:, :, :nq], w_in_b[:, :, nq + n_gl:], w_in_b[:, :, nq:nq + n_gl],
         jnp.zeros((w_in_b.shape[0], d_model, LANES - n_gl), w_in_b.dtype)], axis=2).astype(BF16)
    p["w_kv"] = w_kv_shared.astype(BF16)
    p["w_mem"] = w_mem_kv.transpose(1, 0, 2).reshape(d_model, depth * 2 * D_MEM).astype(BF16)
    p["wo_a"] = w_out[:, :D_RNN].astype(BF16)
    p["wo_m"] = w_out[:, D_RNN:].astype(BF16)
    p["conv_w"] = conv_w
    p["conv_b"] = conv_b[:, None]
    p["w_rg"] = jnp.stack([_block_diag_pairs(w_rg[l]) for l in range(n_a)])
    p["w_ig"] = jnp.stack([_block_diag_pairs(w_ig[l]) for l in range(n_a)])
    p["b_rg"] = b_rg[:, None]
    p["b_ig"] = b_ig[:, None]
    p["lam"] = lru_lambda[:, None]
    p["pe"], p["cmp_w1"], p["cmp_b1"], p["cmp_w2"] = _compress_weights(cmp_pe, cmp_w1, cmp_b1, cmp_w2)
    p["ln1_g"], p["ln1_b"], p["ln2_g"], p["ln2_b"] = ln1_g[:, None], ln1_b[:, None], ln2_g[:, None], ln2_b[:, None]
    n_r = N_GROUPS + N_EXPERTS
    wr = jnp.concatenate([w_router_g, w_router_e, jnp.zeros((depth, d_model, LANES - n_r), F32)], axis=2)
    p["wr"] = wr.astype(BF16)
    p["rb"] = jnp.concatenate([b_router_g, b_router_e, jnp.zeros((depth, LANES - n_r), F32)], axis=1)[:, None]

    def by_group(w):
        w = w.astype(BF16).reshape(depth, N_GROUPS, EXPERTS_PER_GROUP, d_model, D_EXPERT)
        return w.transpose(0, 1, 3, 2, 4).reshape(depth, N_GROUPS, d_model, hid)
    p["w_gu"] = jnp.concatenate([by_group(w_exp_gate), by_group(w_exp_up)], axis=3)
    p["w_dn"] = w_exp_down.astype(BF16).reshape(depth, N_GROUPS, hid, d_model)
    return p


def _kv_layouts(rows):
    b, t_len, _ = rows.shape
    k = rows[:, :, :D_KV // 2].reshape(b, t_len, N_KV_GROUPS, HEAD_DIM).transpose(0, 2, 1, 3)
    return k, rows[:, :, D_KV // 2:].transpose(0, 2, 1)


def _trunk_tail(x, mix_a, mix_m, p, l, alpha, tm, spare=None):
    x, grp = _out_ln(x, mix_a, mix_m, p["wo_a"][l], p["wo_m"][l], p["ln1_g"][l], p["ln1_b"][l],
                     p["wr"][l], p["rb"][l], alpha, tm)
    if x.shape[0] < 8 * MOE_TILE:
        return _moe_ln(x, p["wr"][l], p["rb"][l], p["w_gu"][l], p["w_dn"][l],
                       p["ln2_g"][l], p["ln2_b"][l], alpha, tm), None
    dest, tile_grp = _moe_dispatch(grp[:, 0])
    xs = _moe_place(x, dest, tile_grp.shape[0] * MOE_TILE, spare)
    ys = _moe_grouped(xs, tile_grp, p["wr"][l], p["rb"][l], p["w_gu"][l], p["w_dn"][l])
    return _moe_return_ln(x, ys, dest, p["ln2_g"][l], p["ln2_b"][l], alpha), xs


def kernel(x_prompt, x_sample, mem_prompt, cache_cmp_kv, cache_slc_kv, cache_win_kv, cache_mem_kv, state_lru_h, state_conv, page_table, w_in_a, conv_w, conv_b, w_rg, b_rg, w_ig, b_ig, lru_lambda, w_in_b, w_kv_shared, cmp_pe, cmp_w1, cmp_b1, cmp_w2, rel_bias, w_mem_kv, w_out, ln1_g, ln1_b, ln2_g, ln2_b, w_router_g, b_router_g, w_router_e, b_router_e, w_exp_gate, w_exp_up, w_exp_down):
    p = _prep_weights(w_in_a, conv_w, conv_b, w_rg, b_rg, w_ig, b_ig, lru_lambda, w_in_b, w_kv_shared, cmp_pe,
                      cmp_w1, cmp_b1, cmp_w2, w_mem_kv, w_out, ln1_g, ln1_b, ln2_g, ln2_b, w_router_g,
                      b_router_g, w_router_e, b_router_e, w_exp_gate, w_exp_up, w_exp_down)
    depth, d_model, _ = w_out.shape
    n_a = w_in_a.shape[0]
    alpha = (2 * depth) ** 0.25
    kv_shape = (2, N_KV_GROUPS, HEAD_DIM)
    a_outs = [(0, D_RNN, F32), (D_RNN, D_RNN, F32), (2 * D_RNN, D_MEM, BF16)]
    b_outs = [(0, D_RNN, BF16), (D_RNN, D_MEM, BF16), (D_RNN + D_MEM, LANES, F32)]
    kv_outs = [(0, D_KV, F32), (D_KV, D_KV, F32), (2 * D_KV, D_KV, F32)]

    bp, t_len, _ = x_prompt.shape
    n_mem = mem_prompt.shape[1]
    tm = 512
    mem_rows = _proj(mem_prompt.reshape(bp * n_mem, d_model), p["w_mem"],
                     [(l * 2 * D_MEM, 2 * D_MEM, F32) for l in range(depth)], tm, "mem_kv_proj")
    p_mem = jnp.stack(mem_rows).reshape(depth, bp, n_mem, 2 * D_MEM)
    x = x_prompt.reshape(bp * t_len, d_model)
    lru_p, conv_p = [], []
    spare = None
    for l in range(depth):
        if l < n_a:
            gate, xr, mq = _proj(x, p["w_in_a"][l], a_outs, tm, "in_proj_a")
            mix_a, h_t, buf = _rglru_prompt(gate.reshape(bp, t_len, D_RNN), xr.reshape(bp, t_len, D_RNN),
                                            p["conv_w"][l], p["conv_b"][l], p["w_rg"][l], p["w_ig"][l],
                                            p["b_rg"][l], p["b_ig"][l], p["lam"][l])
            lru_p.append(h_t[:, 0])
            conv_p.append(buf)
        else:
            if l == n_a:
                cmp_rows, slc_rows, win_rows, slc_bf, win_bf = _proj(
                    x, p["w_kv"], kv_outs + [(D_KV, D_KV, BF16), (2 * D_KV, D_KV, BF16)], tm, "kv_proj")
                comp = _compress_prompt(cmp_rows.reshape(bp, t_len, D_KV), p["pe"], p["cmp_w1"], p["cmp_b1"],
                                        p["cmp_w2"])
                comp_k, comp_vt = _kv_layouts(comp.astype(BF16))
                slc_k, slc_vt = _kv_layouts(slc_bf.reshape(bp, t_len, D_KV))
                win_k, win_vt = _kv_layouts(win_bf.reshape(bp, t_len, D_KV))
            q, mq, gl = _proj(x, p["w_in_b"][l - n_a], b_outs, tm, "in_proj_b")
            mix_t = _nsa_prompt(rel_bias, q.reshape(bp, t_len, D_RNN).transpose(0, 2, 1),
                                gl.reshape(bp, t_len, LANES).transpose(0, 2, 1),
                                comp_k, comp_vt, slc_k, slc_vt, win_k, win_vt)
            mix_a = mix_t.transpose(0, 2, 1)
        mix_m = _mem_attn(mq.reshape(bp, t_len, D_MEM), p_mem, l)
        x, spare = _trunk_tail(x, mix_a.reshape(bp * t_len, D_RNN), mix_m.reshape(bp * t_len, D_MEM), p, l,
                               alpha, tm, spare)
    y_prompt = x.reshape(bp, t_len, d_model)
    p_cmp_kv = cmp_rows.reshape((bp, t_len) + kv_shape)
    p_slc_kv = slc_rows.reshape((bp, t_len) + kv_shape)
    w_keep = min(WINDOW, t_len)
    p_win_kv = win_rows.reshape((bp, t_len) + kv_shape)[:, t_len - w_keep:]
    p_mem_kv = p_mem.reshape(depth, bp, n_mem, 2, N_MEM_HEADS, HEAD_DIM)
    p_lru_h = jnp.stack(lru_p)
    p_conv = jnp.stack(conv_p)

    bd, s_len, _ = x_sample.shape
    assert s_len == 1
    n_pool, page = cache_cmp_kv.shape[:2]
    w_buf = cache_win_kv.shape[1]
    mem_t = _feature_major(cache_mem_kv)
    slc_pool_t = _feature_major(cache_slc_kv)
    win_t = _feature_major(cache_win_kv)
    x = x_sample.reshape(bd, d_model)
    lru_s, conv_s = [], []
    for l in range(depth):
        if l < n_a:
            gate, xr, mq = _proj(x, p["w_in_a"][l], a_outs, bd, "in_proj_a_step")
            mix_a, h_t, buf = _rglru_step(gate, xr, state_conv[l].transpose(1, 0, 2), state_lru_h[l],
                                          p["conv_w"][l], p["conv_b"][l], p["w_rg"][l], p["w_ig"][l],
                                          p["b_rg"][l], p["b_ig"][l], p["lam"][l])
            lru_s.append(h_t)
            conv_s.append(buf.transpose(1, 0, 2))
        else:
            if l == n_a:
                s_cmp, s_slc, s_win = _proj(x, p["w_kv"], kv_outs, bd, "kv_proj_step")
                comp_s = _compress_pages(page_table, _feature_major(cache_cmp_kv), p["pe"],
                                         p["cmp_w1"], p["cmp_b1"], p["cmp_w2"])
            q, mq, gl = _proj(x, p["w_in_b"][l - n_a], b_outs, bd, "in_proj_b_step")
            mix_a = _nsa_step(rel_bias, page_table, q.reshape(bd, 1, D_RNN), gl.reshape(bd, 1, LANES), comp_s,
                              slc_pool_t, s_slc.reshape(bd, 1, D_KV), win_t, s_win.reshape(bd, 1, D_KV))
            mix_a = mix_a.reshape(bd, D_RNN)
        mix_m = _mem_attn_step(mq.reshape(bd, 1, D_MEM), mem_t, l).reshape(bd, D_MEM)
        x, _ = _trunk_tail(x, mix_a, mix_m, p, l, alpha, bd)
    y_sample = x.reshape(bd, 1, d_model)
    s_cmp_kv = s_cmp.reshape((bd, 1) + kv_shape)
    s_slc_kv = s_slc.reshape((bd, 1) + kv_shape)
    s_win_kv = jnp.concatenate([cache_win_kv, s_win.reshape((bd, 1) + kv_shape)], axis=1)[:, 1:]
    s_lru_h = jnp.stack(lru_s)
    s_conv = jnp.stack(conv_s)
    return (y_prompt, y_sample, p_cmp_kv, p_slc_kv, p_win_kv, p_mem_kv, p_lru_h, p_conv,
            s_cmp_kv, s_slc_kv, s_win_kv, s_lru_h, s_conv)
```

```python
import functools
import math

import jax
import jax.numpy as jnp
from jax import lax
from jax.experimental import pallas as pl
from jax.experimental.pallas import tpu as pltpu

F32 = jnp.float32
BF16 = jnp.bfloat16
I32 = jnp.int32

HEAD_DIM = 64
N_MEM_HEADS = 4
D_MEM = N_MEM_HEADS * HEAD_DIM
N_HEADS = 12
D_RNN = N_HEADS * HEAD_DIM
N_KV_GROUPS = 4
HEADS_PER_GROUP = N_HEADS // N_KV_GROUPS
D_KV = 2 * N_KV_GROUPS * HEAD_DIM
CONV_W = 4
LRU_C = 8.0
CMP_LEN = 32
CMP_STRIDE = 16
CMP_HID = 2 * HEAD_DIM
SEL_LEN = 64
N_SEL = 16
WINDOW = 512
N_BUCKETS = 32
MAX_DISTANCE = 1024
N_GROUPS = 4
EXPERTS_PER_GROUP = 8
N_EXPERTS = N_GROUPS * EXPERTS_PER_GROUP
D_EXPERT = 128
Q_BLOCK = 128
LN_EPS = 1e-5
NEG = -1e30
FORCE = 1e9
SCALE = HEAD_DIM ** -0.5

LANES = 128
SUBLANES = 8
VMEM_LIMIT_BYTES = 56 * 1024 * 1024


def _cparams(n_axes):
    return pltpu.CompilerParams(dimension_semantics=("arbitrary",) * n_axes,
                                vmem_limit_bytes=VMEM_LIMIT_BYTES)


def _dot(a, b):
    return jnp.dot(a, b, preferred_element_type=F32)


def _dot_nt(a, b):
    return lax.dot_general(a, b, (((1,), (1,)), ((), ())), preferred_element_type=F32)


def _layer_norm(v, g, b):
    mu = jnp.mean(v, axis=-1, keepdims=True)
    d = v - mu
    var = jnp.mean(d * d, axis=-1, keepdims=True)
    return d * lax.rsqrt(var + LN_EPS) * g + b


def _rel_bucket(dist):
    n = jnp.maximum(dist, 0)
    max_exact = N_BUCKETS // 2
    nf = jnp.maximum(n, 1).astype(F32)
    large = max_exact + (jnp.log(nf / max_exact) / math.log(MAX_DISTANCE / max_exact)
                         * (N_BUCKETS - max_exact)).astype(I32)
    large = jnp.minimum(large, N_BUCKETS - 1)
    return jnp.where(n < max_exact, n, large)


def _bias_from_bucket(bucket, tab_ref, heads):
    masks = [bucket == k for k in range(1, N_BUCKETS)]
    out = []
    for h in heads:
        acc = jnp.full(bucket.shape, tab_ref[0, h], F32)
        for k in range(1, N_BUCKETS):
            acc = jnp.where(masks[k - 1], tab_ref[k, h], acc)
        out.append(acc)
    return out


def _proj_kernel(x_ref, w_ref, *o_refs, cols):
    y = _dot(x_ref[...].astype(BF16), w_ref[...])
    for o_ref, (off, n) in zip(o_refs, cols):
        o_ref[...] = y[:, off:off + n].astype(o_ref.dtype)


def _proj(x, w, outs, tm, name):
    m, k = x.shape
    n = w.shape[1]
    tm = min(tm, m)
    assert m % tm == 0 and all(off + wd <= n for off, wd, _ in outs)
    return pl.pallas_call(
        functools.partial(_proj_kernel, cols=tuple((off, wd) for off, wd, _ in outs)),
        grid=(m // tm,),
        in_specs=[pl.BlockSpec((tm, k), lambda i: (i, 0)),
                  pl.BlockSpec((k, n), lambda i: (0, 0))],
        out_specs=[pl.BlockSpec((tm, wd), lambda i: (i, 0)) for _, wd, _ in outs],
        out_shape=[jax.ShapeDtypeStruct((m, wd), dt) for _, wd, dt in outs],
        compiler_params=_cparams(1),
        name=name,
    )(x, w)


def _softplus(z):
    return jnp.maximum(z, 0.0) + jnp.log1p(jnp.exp(-jnp.abs(z)))


def _lru_gates(xc, wr_ref, wi_ref, br, bi, lam):
    xcb = xc.astype(BF16)
    nb = D_RNN // LANES
    r_l = jnp.concatenate([_dot(xcb[:, j * LANES:(j + 1) * LANES], wr_ref[j]) for j in range(nb)], axis=1)
    i_l = jnp.concatenate([_dot(xcb[:, j * LANES:(j + 1) * LANES], wi_ref[j]) for j in range(nb)], axis=1)
    r = jax.nn.sigmoid(r_l + br)
    i = jax.nn.sigmoid(i_l + bi)
    log_a = -LRU_C * r * _softplus(-lam)
    a = jnp.exp(log_a)
    u = jnp.sqrt(1.0 - a * a) * i * xc
    return a, u


def _rglru_prompt_kernel(gate_ref, xr_ref, cw_ref, cb_ref, wr_ref, wi_ref, br_ref, bi_ref, lam_ref,
                         y_ref, ht_ref, buf_ref, xp_ref, *, t_len, chunk):
    d = D_RNN
    pad = SUBLANES
    xp_ref[0:pad, :] = jnp.zeros((pad, d), F32)
    xp_ref[pad:pad + t_len, :] = xr_ref[0]
    buf_ref[0] = xr_ref[0, t_len - (CONV_W - 1):t_len, :]
    cw = cw_ref[...]
    cb = cb_ref[...]
    br = br_ref[...]
    bi = bi_ref[...]
    lam = lam_ref[...]
    row_in_tile = lax.broadcasted_iota(I32, (chunk, d), 0) & (SUBLANES - 1)
    h = jnp.zeros((1, d), F32)
    for c in range(t_len // chunk):
        base = c * chunk
        xc = cb
        for k in range(CONV_W):
            lo = pad - (CONV_W - 1) + k + base
            xc = xc + xp_ref[lo:lo + chunk, :] * cw[k:k + 1, :]
        a, u = _lru_gates(xc, wr_ref, wi_ref, br, bi, lam)
        for k in (1, 2, 4):
            a_s = pltpu.roll(a, k, axis=0)
            u_s = pltpu.roll(u, k, axis=0)
            m = row_in_tile >= k
            u = jnp.where(m, a * u_s + u, u)
            a = jnp.where(m, a * a_s, a)
        hs = []
        for j in range(chunk // SUBLANES):
            h_t = a[j * SUBLANES:(j + 1) * SUBLANES] * h + u[j * SUBLANES:(j + 1) * SUBLANES]
            h = h_t[SUBLANES - 1:SUBLANES]
            hs.append(h_t)
        hfull = jnp.concatenate(hs, axis=0)
        g = jax.nn.gelu(gate_ref[0, base:base + chunk, :])
        y_ref[0, base:base + chunk, :] = (g * hfull).astype(y_ref.dtype)
    ht_ref[0] = h


def _rglru_prompt(gate, xr, cw, cb, wr_bd, wi_bd, br, bi, lam):
    b, t_len, d = xr.shape
    chunk = min(256, t_len)
    full = lambda shape: pl.BlockSpec(shape, lambda i: (0,) * len(shape))
    return pl.pallas_call(
        functools.partial(_rglru_prompt_kernel, t_len=t_len, chunk=chunk),
        grid=(b,),
        in_specs=[pl.BlockSpec((1, t_len, d), lambda i: (i, 0, 0)),
                  pl.BlockSpec((1, t_len, d), lambda i: (i, 0, 0)),
                  full((CONV_W, d)), full((1, d)), full(wr_bd.shape), full(wi_bd.shape),
                  full((1, d)), full((1, d)), full((1, d))],
        out_specs=[pl.BlockSpec((1, t_len, d), lambda i: (i, 0, 0)),
                   pl.BlockSpec((1, 1, d), lambda i: (i, 0, 0)),
                   pl.BlockSpec((1, CONV_W - 1, d), lambda i: (i, 0, 0))],
        out_shape=[jax.ShapeDtypeStruct((b, t_len, d), BF16),
                   jax.ShapeDtypeStruct((b, 1, d), F32),
                   jax.ShapeDtypeStruct((b, CONV_W - 1, d), F32)],
        scratch_shapes=[pltpu.VMEM((t_len + SUBLANES, d), F32)],
        compiler_params=_cparams(1),
        name="rglru_prompt",
    )(gate, xr, cw, cb, wr_bd, wi_bd, br, bi, lam)


def _rglru_step_kernel(gate_ref, xr_ref, conv_ref, h0_ref, cw_ref, cb_ref, wr_ref, wi_ref, br_ref, bi_ref,
                       lam_ref, y_ref, ht_ref, buf_ref):
    xr = xr_ref[...]
    cw = cw_ref[...]
    xc = cb_ref[...] + xr * cw[CONV_W - 1:CONV_W, :]
    for k in range(CONV_W - 1):
        xc = xc + conv_ref[k] * cw[k:k + 1, :]
    a, u = _lru_gates(xc, wr_ref, wi_ref, br_ref[...], bi_ref[...], lam_ref[...])
    h = a * h0_ref[...] + u
    ht_ref[...] = h
    y_ref[...] = (jax.nn.gelu(gate_ref[...]) * h).astype(y_ref.dtype)
    for k in range(CONV_W - 2):
        buf_ref[k] = conv_ref[k + 1]
    buf_ref[CONV_W - 2] = xr


def _rglru_step(gate, xr, conv_t, h0, cw, cb, wr_bd, wi_bd, br, bi, lam):
    b, d = xr.shape
    return pl.pallas_call(
        _rglru_step_kernel,
        out_shape=[jax.ShapeDtypeStruct((b, d), BF16),
                   jax.ShapeDtypeStruct((b, d), F32),
                   jax.ShapeDtypeStruct((CONV_W - 1, b, d), F32)],
        compiler_params=pltpu.CompilerParams(vmem_limit_bytes=VMEM_LIMIT_BYTES),
        name="rglru_step",
    )(gate, xr, conv_t, h0, cw, cb, wr_bd, wi_bd, br, bi, lam)


def _mem_attn_kernel(q_ref, kv_ref, o_ref, *, rows):
    q = q_ref[0]
    if rows < SUBLANES:
        q = jnp.broadcast_to(q, (SUBLANES, D_MEM))
    kv = kv_ref[0, 0].astype(BF16)
    outs = []
    for h in range(N_MEM_HEADS):
        qh = q[:, h * HEAD_DIM:(h + 1) * HEAD_DIM]
        kh = kv[:, h * HEAD_DIM:(h + 1) * HEAD_DIM]
        vh = kv[:, D_MEM + h * HEAD_DIM:D_MEM + (h + 1) * HEAD_DIM]
        s = _dot_nt(qh, kh) * SCALE
        m = jnp.max(s, axis=-1, keepdims=True)
        e = jnp.exp(s - m)
        p = e / jnp.sum(e, axis=-1, keepdims=True)
        outs.append(_dot(p.astype(BF16), vh))
    o = jnp.concatenate(outs, axis=1)
    o_ref[0] = o[:rows].astype(o_ref.dtype)


def _mem_attn(q, mem_kv, layer):
    b, t_len, _ = q.shape
    n_mem = mem_kv.shape[2]
    tq = min(t_len, 512)
    return pl.pallas_call(
        functools.partial(_mem_attn_kernel, rows=tq),
        grid=(b, t_len // tq),
        in_specs=[pl.BlockSpec((1, tq, D_MEM), lambda i, j: (i, j, 0)),
                  pl.BlockSpec((1, 1, n_mem, 2 * D_MEM), lambda i, j: (layer, i, 0, 0))],
        out_specs=pl.BlockSpec((1, tq, D_MEM), lambda i, j: (i, j, 0)),
        out_shape=jax.ShapeDtypeStruct((b, t_len, D_MEM), BF16),
        compiler_params=_cparams(2),
        name="mem_attn",
    )(q, mem_kv)


def _mem_attn_step_kernel(q_ref, kv_ref, o_ref):
    q_row = q_ref[0] * SCALE
    rows = []
    for h in range(N_MEM_HEADS):
        parts = []
        if h > 0:
            parts.append(jnp.zeros((1, h * HEAD_DIM), BF16))
        parts.append(q_row[:, h * HEAD_DIM:(h + 1) * HEAD_DIM])
        if h < N_MEM_HEADS - 1:
            parts.append(jnp.zeros((1, (N_MEM_HEADS - 1 - h) * HEAD_DIM), BF16))
        rows.append(jnp.concatenate(parts, axis=1))
    n_rows = 2 * SUBLANES
    q_blk = jnp.concatenate(rows + [jnp.zeros((n_rows - N_MEM_HEADS, D_MEM), BF16)], axis=0)
    s = _dot(q_blk, kv_ref[0, 0, 0].astype(BF16))
    e = jnp.exp(s - jnp.max(s, axis=-1, keepdims=True))
    p = e / jnp.sum(e, axis=-1, keepdims=True)
    o_all = _dot_nt(p.astype(BF16), kv_ref[0, 0, 1].astype(BF16))
    o = jnp.concatenate([o_all[h:h + 1, h * HEAD_DIM:(h + 1) * HEAD_DIM] for h in range(N_MEM_HEADS)], axis=1)
    o_ref[0] = o.astype(o_ref.dtype)


def _mem_attn_step(q, mem_t, layer):
    b = q.shape[0]
    n_mem = mem_t.shape[4]
    return pl.pallas_call(
        _mem_attn_step_kernel,
        grid=(b,),
        in_specs=[pl.BlockSpec((1, 1, D_MEM), lambda i: (i, 0, 0)),
                  pl.BlockSpec((1, 1, 2, D_MEM, n_mem), lambda i: (layer, i, 0, 0, 0))],
        out_specs=pl.BlockSpec((1, 1, D_MEM), lambda i: (i, 0, 0)),
        out_shape=jax.ShapeDtypeStruct((b, 1, D_MEM), BF16),
        compiler_params=_cparams(1),
        name="mem_attn_step",
    )(q, mem_t)


def _out_ln_kernel(x_ref, ma_ref, mm_ref, wa_ref, wm_ref, g_ref, b_ref, wr_ref, rb_ref, o_ref, grp_ref, *, alpha):
    sub = _dot(ma_ref[...], wa_ref[...]) + _dot(mm_ref[...], wm_ref[...])
    y = _layer_norm(alpha * x_ref[...] + sub, g_ref[...], b_ref[...])
    o_ref[...] = y
    grp, _ = _route_group(_dot(y.astype(BF16), wr_ref[...]) + rb_ref[...])
    grp_ref[...] = jnp.broadcast_to(grp, grp_ref.shape)


def _out_ln(x, mix_a, mix_m, wo_a, wo_m, g, b, wr, rb, alpha, tm):
    m, d = x.shape
    da = mix_a.shape[1]
    dm = mix_m.shape[1]
    return pl.pallas_call(
        functools.partial(_out_ln_kernel, alpha=alpha),
        grid=(m // tm,),
        in_specs=[pl.BlockSpec((tm, d), lambda i: (i, 0)),
                  pl.BlockSpec((tm, da), lambda i: (i, 0)),
                  pl.BlockSpec((tm, dm), lambda i: (i, 0)),
                  pl.BlockSpec((da, d), lambda i: (0, 0)),
                  pl.BlockSpec((dm, d), lambda i: (0, 0)),
                  pl.BlockSpec((1, d), lambda i: (0, 0)),
                  pl.BlockSpec((1, d), lambda i: (0, 0)),
                  pl.BlockSpec((d, LANES), lambda i: (0, 0)),
                  pl.BlockSpec((1, LANES), lambda i: (0, 0))],
        out_specs=[pl.BlockSpec((tm, d), lambda i: (i, 0)), pl.BlockSpec((tm, LANES), lambda i: (i, 0))],
        out_shape=[jax.ShapeDtypeStruct((m, d), F32), jax.ShapeDtypeStruct((m, LANES), I32)],
        compiler_params=_cparams(1),
        name="out_ln",
    )(x, mix_a, mix_m, wo_a, wo_m, g, b, wr, rb)


def _route_group(logits):
    lane = lax.broadcasted_iota(I32, logits.shape, 1)
    lg = jnp.where(lane < N_GROUPS, logits, -jnp.inf)
    gmax = jnp.max(lg, axis=-1, keepdims=True)
    return jnp.min(jnp.where(lg == gmax, lane, jnp.int32(4 * LANES)), axis=-1, keepdims=True), gmax


def _route(logits):
    lane = lax.broadcasted_iota(I32, logits.shape, 1)
    big = jnp.int32(4 * LANES)
    is_g = lane < N_GROUPS
    grp, gmax = _route_group(logits)
    gsum = jnp.sum(jnp.where(is_g, jnp.exp(logits - gmax), 0.0), axis=-1, keepdims=True)
    g_w = 1.0 / gsum
    e_idx = lane - N_GROUPS
    in_grp = (lane >= N_GROUPS) & (lane < N_GROUPS + N_EXPERTS) & ((e_idx >> 3) == grp)
    v = jnp.where(in_grp, logits, -jnp.inf)
    v1 = jnp.max(v, axis=-1, keepdims=True)
    i1 = jnp.min(jnp.where(v == v1, lane, big), axis=-1, keepdims=True)
    vr = jnp.where(lane == i1, -jnp.inf, v)
    v2 = jnp.max(vr, axis=-1, keepdims=True)
    i2 = jnp.min(jnp.where(vr == v2, lane, big), axis=-1, keepdims=True)
    e2 = jnp.exp(v2 - v1)
    w1 = g_w / (1.0 + e2)
    w2 = g_w * e2 / (1.0 + e2)
    return jnp.where(lane == i1, w1, 0.0) + jnp.where(lane == i2, w2, 0.0)


def _moe_kernel(x_ref, wr_ref, rb_ref, wgu_ref, wd_ref, g_ref, b_ref, o_ref,
                xb_ref, comb_ref, acc_ref, *, alpha):
    grp = pl.program_id(1)
    hid = EXPERTS_PER_GROUP * D_EXPERT

    @pl.when(grp == 0)
    def _():
        xh = x_ref[...].astype(BF16)
        logits = _dot(xh, wr_ref[...]) + rb_ref[...]
        xb_ref[...] = xh
        comb_ref[...] = _route(logits)
        acc_ref[...] = jnp.zeros(acc_ref.shape, F32)

    xb = xb_ref[...]
    gu = _dot(xb, wgu_ref[0])
    gate = gu[:, :hid]
    up = gu[:, hid:]
    hdn = gate * jax.nn.sigmoid(gate) * up
    comb = comb_ref[...]
    lane = lax.broadcasted_iota(I32, comb.shape, 1)
    scale = []
    for e in range(EXPERTS_PER_GROUP):
        col = jnp.sum(jnp.where(lane == N_GROUPS + grp * EXPERTS_PER_GROUP + e, comb, 0.0),
                      axis=-1, keepdims=True)
        scale.append(jnp.broadcast_to(col, (comb.shape[0], D_EXPERT)))
    hdn = hdn * jnp.concatenate(scale, axis=1)
    acc_ref[...] += _dot(hdn.astype(BF16), wd_ref[0])

    @pl.when(grp == N_GROUPS - 1)
    def _():
        o_ref[...] = _layer_norm(alpha * x_ref[...] + acc_ref[...], g_ref[...], b_ref[...])


def _moe_ln(x, wr, rb, wgu, wd, g, b, alpha, tm):
    m, d = x.shape
    hid = EXPERTS_PER_GROUP * D_EXPERT
    return pl.pallas_call(
        functools.partial(_moe_kernel, alpha=alpha),
        grid=(m // tm, N_GROUPS),
        in_specs=[pl.BlockSpec((tm, d), lambda i, j: (i, 0)),
                  pl.BlockSpec((d, LANES), lambda i, j: (0, 0)),
                  pl.BlockSpec((1, LANES), lambda i, j: (0, 0)),
                  pl.BlockSpec((1, d, 2 * hid), lambda i, j: (j, 0, 0)),
                  pl.BlockSpec((1, hid, d), lambda i, j: (j, 0, 0)),
                  pl.BlockSpec((1, d), lambda i, j: (0, 0)),
                  pl.BlockSpec((1, d), lambda i, j: (0, 0))],
        out_specs=pl.BlockSpec((tm, d), lambda i, j: (i, 0)),
        out_shape=jax.ShapeDtypeStruct((m, d), F32),
        scratch_shapes=[pltpu.VMEM((tm, d), BF16), pltpu.VMEM((tm, LANES), F32), pltpu.VMEM((tm, d), F32)],
        compiler_params=_cparams(2),
        name="moe_ln",
    )(x, wr, rb, wgu, wd, g, b)


MOE_TILE = 256


def _moe_dispatch(grp):
    n = grp.shape[0]
    onehot = (grp[:, None] == jnp.arange(N_GROUPS, dtype=I32)[None, :]).astype(I32)
    before = jnp.cumsum(onehot, axis=0) - onehot
    counts = jnp.sum(onehot, axis=0)
    padded = (counts + MOE_TILE - 1) // MOE_TILE * MOE_TILE
    ends = jnp.cumsum(padded)
    dest = jnp.sum(onehot * (before + (ends - padded)[None, :]), axis=1).astype(I32)
    n_slots = n + N_GROUPS * MOE_TILE
    tile_start = jnp.arange(n_slots // MOE_TILE, dtype=I32) * MOE_TILE
    tile_grp = jnp.minimum(jnp.sum((tile_start[:, None] >= ends[None, :]).astype(I32), axis=1), N_GROUPS - 1)
    return dest, tile_grp


def _moe_place_kernel(dest_ref, x_ref, spare_hbm, o_hbm, stage, sem, *, n_tiles):
    del spare_hbm
    i = pl.program_id(0)
    slot = i % 2

    def wait_tile(buf):
        pltpu.make_async_copy(stage.at[buf], o_hbm.at[pl.ds(0, MOE_TILE)], sem.at[buf]).wait()

    @pl.when(i >= 2)
    def _():
        wait_tile(slot)

    stage[slot] = x_ref[...]
    for r in range(MOE_TILE):
        pltpu.make_async_copy(stage.at[slot, pl.ds(r, 1)], o_hbm.at[pl.ds(dest_ref[i * MOE_TILE + r], 1)],
                              sem.at[slot]).start()

    @pl.when(i == n_tiles - 1)
    def _():
        @pl.when(i >= 1)
        def _():
            wait_tile(1 - slot)
        wait_tile(slot)


def _moe_place(x, dest, n_slots, spare):
    n, d = x.shape
    if spare is None:
        spare = jnp.zeros((n_slots, d), F32)
    assert n % MOE_TILE == 0
    n_tiles = n // MOE_TILE
    grid_spec = pltpu.PrefetchScalarGridSpec(
        num_scalar_prefetch=1,
        grid=(n_tiles,),
        in_specs=[pl.BlockSpec((MOE_TILE, d), lambda i, ds: (i, 0)), pl.BlockSpec(memory_space=pl.ANY)],
        out_specs=pl.BlockSpec(memory_space=pl.ANY),
        scratch_shapes=[pltpu.VMEM((2, MOE_TILE, d), F32), pltpu.SemaphoreType.DMA((2,))])
    return pl.pallas_call(
        functools.partial(_moe_place_kernel, n_tiles=n_tiles),
        grid_spec=grid_spec,
        out_shape=jax.ShapeDtypeStruct((n_slots, d), F32),
        input_output_aliases={2: 0},
        compiler_params=_cparams(1),
        name="moe_place",
    )(dest, x, spare)


def _moe_grouped_kernel(tgrp_ref, x_ref, wr_ref, rb_ref, wgu_ref, wd_ref, y_ref):
    i = pl.program_id(0)
    hid = EXPERTS_PER_GROUP * D_EXPERT
    xh = x_ref[...].astype(BF16)
    comb = _route(_dot(xh, wr_ref[...]) + rb_ref[...])
    gu = _dot(xh, wgu_ref[0])
    gate = gu[:, :hid]
    hdn = gate * jax.nn.sigmoid(gate) * gu[:, hid:]
    grp = tgrp_ref[i]
    lane = lax.broadcasted_iota(I32, comb.shape, 1)
    scale = []
    for e in range(EXPERTS_PER_GROUP):
        col = jnp.sum(jnp.where(lane == N_GROUPS + grp * EXPERTS_PER_GROUP + e, comb, 0.0),
                      axis=-1, keepdims=True)
        scale.append(jnp.broadcast_to(col, (comb.shape[0], D_EXPERT)))
    y_ref[...] = _dot((hdn * jnp.concatenate(scale, axis=1)).astype(BF16), wd_ref[0])


def _moe_grouped(xs, tile_grp, wr, rb, wgu, wd):
    n_slots, d = xs.shape
    hid = EXPERTS_PER_GROUP * D_EXPERT
    grid_spec = pltpu.PrefetchScalarGridSpec(
        num_scalar_prefetch=1,
        grid=(n_slots // MOE_TILE,),
        in_specs=[pl.BlockSpec((MOE_TILE, d), lambda i, tg: (i, 0)),
                  pl.BlockSpec((d, LANES), lambda i, tg: (0, 0)),
                  pl.BlockSpec((1, LANES), lambda i, tg: (0, 0)),
                  pl.BlockSpec((1, d, 2 * hid), lambda i, tg: (tg[i], 0, 0)),
                  pl.BlockSpec((1, hid, d), lambda i, tg: (tg[i], 0, 0))],
        out_specs=pl.BlockSpec((MOE_TILE, d), lambda i, tg: (i, 0)))
    return pl.pallas_call(
        _moe_grouped_kernel,
        grid_spec=grid_spec,
        out_shape=jax.ShapeDtypeStruct((n_slots, d), F32),
        compiler_params=_cparams(1),
        name="moe_grouped",
    )(tile_grp, xs, wr, rb, wgu, wd)


def _moe_return_ln_kernel(dest_ref, x_ref, y_hbm, g_ref, b_ref, o_ref, ybuf, sem, *, alpha, n_tiles):
    i = pl.program_id(0)
    slot = i % 2

    def start_tile(tile, buf):
        for r in range(MOE_TILE):
            pltpu.make_async_copy(y_hbm.at[pl.ds(dest_ref[tile * MOE_TILE + r], 1)],
                                  ybuf.at[buf, pl.ds(r, 1)], sem.at[buf]).start()

    @pl.when(i == 0)
    def _():
        start_tile(i, slot)

    @pl.when(i + 1 < n_tiles)
    def _():
        start_tile(i + 1, 1 - slot)

    pltpu.make_async_copy(y_hbm.at[pl.ds(0, MOE_TILE)], ybuf.at[slot], sem.at[slot]).wait()
    o_ref[...] = _layer_norm(alpha * x_ref[...] + ybuf[slot], g_ref[...], b_ref[...])


def _moe_return_ln(x, ys, dest, g, b, alpha):
    n, d = x.shape
    n_tiles = n // MOE_TILE
    grid_spec = pltpu.PrefetchScalarGridSpec(
        num_scalar_prefetch=1,
        grid=(n_tiles,),
        in_specs=[pl.BlockSpec((MOE_TILE, d), lambda i, ds: (i, 0)),
                  pl.BlockSpec(memory_space=pl.ANY),
                  pl.BlockSpec((1, d), lambda i, ds: (0, 0)),
                  pl.BlockSpec((1, d), lambda i, ds: (0, 0))],
        out_specs=pl.BlockSpec((MOE_TILE, d), lambda i, ds: (i, 0)),
        scratch_shapes=[pltpu.VMEM((2, MOE_TILE, d), F32), pltpu.SemaphoreType.DMA((2,))])
    return pl.pallas_call(
        functools.partial(_moe_return_ln_kernel, alpha=alpha, n_tiles=n_tiles),
        grid_spec=grid_spec,
        out_shape=jax.ShapeDtypeStruct((n, d), F32),
        compiler_params=_cparams(1),
        name="moe_return_ln",
    )(dest, x, ys, g, b)


CMP_MBLK = 128
ROW_TILES = D_KV // LANES
PAIR_HID = 2 * CMP_HID


def _compress_weights(pe, w1, b1, w2):
    w1 = w1.reshape(2, CMP_LEN, HEAD_DIM, CMP_HID)
    z1 = jnp.zeros_like(w1)
    w1p = jnp.concatenate([jnp.concatenate([w1, z1], axis=3), jnp.concatenate([z1, w1], axis=3)], axis=2)
    z2 = jnp.zeros_like(w2)
    w2p = jnp.concatenate([jnp.concatenate([w2, z2], axis=2), jnp.concatenate([z2, w2], axis=2)], axis=1)
    w1p = w1p.astype(BF16).reshape(2, 2, CMP_STRIDE * LANES, PAIR_HID)
    return (jnp.concatenate([pe, pe], axis=2), w1p,
            jnp.concatenate([b1, b1], axis=1)[:, None], w2p.astype(BF16))


def _load_interleaved(x_ref):
    step = CMP_STRIDE * ROW_TILES

    def load(tile, r, mb):
        row0 = pl.multiple_of(mb * (CMP_MBLK * step), CMP_MBLK * step)
        return x_ref[pl.ds(row0 + r * ROW_TILES + tile, CMP_MBLK, stride=step), :]
    return load


def _load_tiled(x_ref):
    def load(tile, r, mb):
        row0 = pl.multiple_of(mb * (CMP_MBLK * CMP_STRIDE), CMP_MBLK * CMP_STRIDE)
        return x_ref[tile, pl.ds(row0 + r, CMP_MBLK, stride=CMP_STRIDE), :]
    return load


def _compress_rows(load, n_chunk, pe_ref, w1_ref, b1_ref, w2_ref, out_ref, a_ref, b_ref):
    n_blk = n_chunk // CMP_MBLK
    for kv in range(2):
        for pair in range(N_KV_GROUPS // 2):
            tile = kv * (N_KV_GROUPS // 2) + pair

            def fill(mb, carry, kv=kv, tile=tile):
                top, bot = [], []
                for r in range(CMP_STRIDE):
                    xg = load(tile, r, mb)
                    top.append((xg + pe_ref[kv, r:r + 1, :]).astype(BF16))
                    bot.append((xg + pe_ref[kv, CMP_STRIDE + r:CMP_STRIDE + r + 1, :]).astype(BF16))
                m0 = pl.multiple_of(mb * CMP_MBLK, CMP_MBLK)
                a_ref[pl.ds(m0, CMP_MBLK), :] = _dot(jnp.concatenate(top, axis=1), w1_ref[kv, 0])
                b_ref[pl.ds(m0, CMP_MBLK), :] = _dot(jnp.concatenate(bot, axis=1), w1_ref[kv, 1])
                return carry

            lax.fori_loop(0, n_blk, fill, 0)
            b_ref[n_chunk:n_chunk + SUBLANES, :] = jnp.zeros((SUBLANES, PAIR_HID), F32)
            for mb in range(n_blk):
                m0 = mb * CMP_MBLK
                hid = jax.nn.gelu(a_ref[m0:m0 + CMP_MBLK, :] + b_ref[m0 + 1:m0 + 1 + CMP_MBLK, :] + b1_ref[kv])
                out_ref[m0:m0 + CMP_MBLK, tile * LANES:(tile + 1) * LANES] = _dot(hid.astype(BF16), w2_ref[kv])


def _compress_prompt_kernel(x_ref, pe_ref, w1_ref, b1_ref, w2_ref, o_ref, a_ref, b_ref, *, n_chunk):
    _compress_rows(_load_interleaved(x_ref.at[0]), n_chunk, pe_ref, w1_ref, b1_ref, w2_ref, o_ref.at[0],
                   a_ref, b_ref)


def _compress_prompt(rows, pe, w1, b1, w2):
    b, t_len, _ = rows.shape
    n_chunk = t_len // CMP_STRIDE
    assert n_chunk % CMP_MBLK == 0
    full = lambda a: pl.BlockSpec(a.shape, lambda i: (0,) * a.ndim)
    return pl.pallas_call(
        functools.partial(_compress_prompt_kernel, n_chunk=n_chunk),
        grid=(b,),
        in_specs=[pl.BlockSpec((1, t_len * ROW_TILES, LANES), lambda i: (i, 0, 0)),
                  full(pe), full(w1), full(b1), full(w2)],
        out_specs=pl.BlockSpec((1, n_chunk, D_KV), lambda i: (i, 0, 0)),
        out_shape=jax.ShapeDtypeStruct((b, n_chunk, D_KV), F32),
        scratch_shapes=[pltpu.VMEM((n_chunk, PAIR_HID), F32), pltpu.VMEM((n_chunk + SUBLANES, PAIR_HID), F32)],
        compiler_params=_cparams(1),
        name="compress_prompt",
    )(rows.reshape(b, t_len * ROW_TILES, LANES), pe, w1, b1, w2)


def _masked_softmax_rows(s, mask):
    sm = jnp.where(mask, s, NEG)
    m = jnp.max(sm, axis=-1, keepdims=True)
    e = jnp.where(mask, jnp.exp(sm - m), 0.0)
    tot = jnp.sum(e, axis=-1, keepdims=True)
    return e * jnp.where(tot > 0.0, 1.0 / tot, 0.0)


WIN_TILES = WINDOW // Q_BLOCK


KEY_TILE = 512
KEY_SUB = KEY_TILE // Q_BLOCK


def _attend_tiles_t(qts, k_ref, vt_ref, t_lo, t_hi, i_blk, bias_ref, bias_index, mask_fn):
    cols = qts[0].shape[1]

    def body(t, carry):
        k0 = pl.multiple_of(t * KEY_TILE, KEY_TILE)
        bases = [bias_index(i_blk - (t * KEY_SUB + u)) for u in range(KEY_SUB)]
        scores = [_dot(k_ref[0, g, pl.ds(k0, KEY_TILE), :], qt) for g, qt in enumerate(qts)]
        new = []
        for g in range(len(qts)):
            m, l, acc = carry[g]
            vt = vt_ref[0, g * HEAD_DIM:(g + 1) * HEAD_DIM, pl.ds(k0, KEY_TILE)]
            heads = [g * HEADS_PER_GROUP + hp for hp in range(HEADS_PER_GROUP)]
            bias = jnp.concatenate(
                [jnp.concatenate([bias_ref[base + h] for h in heads], axis=1) for base in bases], axis=0)
            s = scores[g] + bias
            extra = mask_fn(t, g)
            if extra is not None:
                s = s + jnp.concatenate([extra] * HEADS_PER_GROUP, axis=1)
            m_new = jnp.maximum(m, jnp.max(s, axis=0, keepdims=True))
            alpha = jnp.exp(m - m_new)
            p = jnp.exp(s - m_new)
            new.append((m_new, alpha * l + jnp.sum(p, axis=0, keepdims=True), alpha * acc + _dot(vt, p.astype(BF16))))
        return tuple(new)

    init = tuple((jnp.full((1, cols), NEG, F32), jnp.zeros((1, cols), F32), jnp.zeros((HEAD_DIM, cols), F32))
                 for _ in qts)
    fin = lax.fori_loop(t_lo, t_hi, body, init)
    return [acc / l for (_, l, acc) in fin]


def _nsa_prompt_kernel(tab_ref, q_ref, gl_ref, ck_ref, cvt_ref, sk_ref, svt_ref, wk_ref, wvt_ref, o_ref,
                       bias_ref, cbias_ref, sel_ref, *, n_tiles):
    b = pl.program_id(0)
    i = pl.program_id(1)
    kk = lax.broadcasted_iota(I32, (Q_BLOCK, Q_BLOCK), 0)
    qq = lax.broadcasted_iota(I32, (Q_BLOCK, Q_BLOCK), 1)
    n_cmp = ck_ref.shape[2]
    n_sel_blocks = 2 * n_tiles
    edge_base = n_tiles * N_HEADS
    none_base = (n_tiles + 1) * N_HEADS

    @pl.when((b == 0) & (i == 0))
    def _():
        def build(delta, carry):
            dist = delta * Q_BLOCK + qq - kk
            biases = _bias_from_bucket(_rel_bucket(dist), tab_ref, range(N_HEADS))
            for h in range(N_HEADS):
                bias_ref[delta * N_HEADS + h] = jnp.where(dist >= 0, biases[h], NEG)
            return carry
        lax.fori_loop(0, n_tiles, build, 0)
        dist = WIN_TILES * Q_BLOCK + qq - kk
        biases = _bias_from_bucket(_rel_bucket(dist), tab_ref, range(N_HEADS))
        for h in range(N_HEADS):
            bias_ref[edge_base + h] = jnp.where(dist < WINDOW, biases[h], NEG)
            bias_ref[none_base + h] = jnp.full((Q_BLOCK, Q_BLOCK), NEG, F32)

        def build_c(t, carry):
            dist_c = t * Q_BLOCK + qq - (kk * CMP_STRIDE + (CMP_LEN - 1))
            ok = (dist_c >= 0) & (kk < n_cmp - 1)
            biases_c = _bias_from_bucket(_rel_bucket(dist_c), tab_ref, range(N_HEADS))
            for h in range(N_HEADS):
                cbias_ref[t * N_HEADS + h] = jnp.where(ok, biases_c[h], NEG)
            return carry
        lax.fori_loop(0, n_tiles, build_c, 0)

    qs = i * Q_BLOCK
    q_all = q_ref[0] * SCALE
    gates = jax.nn.sigmoid(gl_ref[0])

    blk = lax.broadcasted_iota(I32, (n_sel_blocks, Q_BLOCK), 0)
    col = lax.broadcasted_iota(I32, (n_sel_blocks, Q_BLOCK), 1)
    c_start = col * CMP_STRIDE
    s_start = blk * SEL_LEN
    cover_t = jnp.where((c_start < s_start + SEL_LEN) & (c_start + CMP_LEN > s_start) & (col < n_cmp - 1),
                        1.0, 0.0).astype(BF16)
    q_pos = qs + col
    cur = q_pos // SEL_LEN
    forced = (blk == 0) | (blk == cur) | (blk == cur - 1)
    valid = blk * SEL_LEN <= q_pos
    upper = kk < SEL_LEN

    qts, o_cs, splits = [], [], []
    for g in range(N_KV_GROUPS):
        heads = [g * HEADS_PER_GROUP + hp for hp in range(HEADS_PER_GROUP)]
        qt = jnp.concatenate([q_all[h * HEAD_DIM:(h + 1) * HEAD_DIM, :] for h in heads], axis=1)
        qts.append(qt)

        s_c = _dot(ck_ref[0, g], qt) + jnp.concatenate([cbias_ref[i * N_HEADS + h] for h in heads], axis=1)
        m_c = jnp.max(s_c, axis=0, keepdims=True)
        e_c = jnp.exp(s_c - m_c)
        tot = jnp.sum(e_c, axis=0, keepdims=True)
        p_c = e_c * jnp.where(m_c > 0.5 * NEG, 1.0 / tot, 0.0)
        o_cs.append(_dot(cvt_ref[0, g * HEAD_DIM:(g + 1) * HEAD_DIM, :], p_c.astype(BF16)))
        p_sum = p_c[:, 0:Q_BLOCK]
        for hp in range(1, HEADS_PER_GROUP):
            p_sum = p_sum + p_c[:, hp * Q_BLOCK:(hp + 1) * Q_BLOCK]
        splits.append(p_sum.astype(BF16))

    imp_all = _dot(cover_t, jnp.concatenate(splits, axis=1))
    for g in range(N_KV_GROUPS):
        imp = imp_all[:, g * Q_BLOCK:(g + 1) * Q_BLOCK]
        score = jnp.where(forced, FORCE, imp)
        score = jnp.where(valid, score, NEG)
        cnt = jnp.zeros(score.shape, F32)
        for s in range(n_sel_blocks):
            row = score[s:s + 1, :]
            beats = (row > score) | ((row == score) & (blk > s))
            cnt = cnt + jnp.where(beats, 1.0, 0.0)
        sel = (cnt < float(N_SEL)) & (score > 0.5 * NEG)
        sel_ref[g] = jnp.where(sel, 0.0, NEG)

    def mask_sel(t, g):
        rows = sel_ref[g, pl.ds(pl.multiple_of(2 * KEY_SUB * t, 2 * KEY_SUB), 2 * KEY_SUB), :]
        return jnp.concatenate([jnp.where(upper, rows[2 * u:2 * u + 1], rows[2 * u + 1:2 * u + 2])
                                for u in range(KEY_SUB)], axis=0)
    last = i // KEY_SUB + 1
    o_ss = _attend_tiles_t(qts, sk_ref, svt_ref, 0, last, i, bias_ref,
                           lambda d: jnp.where(d < 0, none_base, d * N_HEADS), mask_sel)

    def win_index(d):
        return jnp.where((d < 0) | (d > WIN_TILES), none_base, jnp.where(d == WIN_TILES, edge_base, d * N_HEADS))
    o_ws = _attend_tiles_t(qts, wk_ref, wvt_ref, jnp.maximum(i - WIN_TILES, 0) // KEY_SUB, last, i, bias_ref,
                           win_index, lambda t, g: None)

    for h in range(N_HEADS):
        g, hp = divmod(h, HEADS_PER_GROUP)
        c0 = hp * Q_BLOCK
        o_h = (gates[3 * h:3 * h + 1, :] * o_cs[g][:, c0:c0 + Q_BLOCK]
               + gates[3 * h + 1:3 * h + 2, :] * o_ss[g][:, c0:c0 + Q_BLOCK]
               + gates[3 * h + 2:3 * h + 3, :] * o_ws[g][:, c0:c0 + Q_BLOCK])
        o_ref[0, h * HEAD_DIM:(h + 1) * HEAD_DIM, :] = o_h.astype(o_ref.dtype)


def _nsa_prompt(tab, q_t, gl_t, comp_k, comp_vt, slc_k, slc_vt, win_k, win_vt):
    b, dq, t_len = q_t.shape
    n_tiles = t_len // Q_BLOCK
    n_cmp = comp_k.shape[2]
    assert n_cmp == Q_BLOCK and t_len % KEY_TILE == 0 and n_tiles > WIN_TILES
    dv = N_KV_GROUPS * HEAD_DIM
    return pl.pallas_call(
        functools.partial(_nsa_prompt_kernel, n_tiles=n_tiles),
        grid=(b, n_tiles),
        in_specs=[pl.BlockSpec(memory_space=pltpu.SMEM),
                  pl.BlockSpec((1, dq, Q_BLOCK), lambda i, j: (i, 0, j)),
                  pl.BlockSpec((1, LANES, Q_BLOCK), lambda i, j: (i, 0, j)),
                  pl.BlockSpec((1, N_KV_GROUPS, n_cmp, HEAD_DIM), lambda i, j: (i, 0, 0, 0)),
                  pl.BlockSpec((1, dv, n_cmp), lambda i, j: (i, 0, 0)),
                  pl.BlockSpec((1, N_KV_GROUPS, t_len, HEAD_DIM), lambda i, j: (i, 0, 0, 0)),
                  pl.BlockSpec((1, dv, t_len), lambda i, j: (i, 0, 0)),
                  pl.BlockSpec((1, N_KV_GROUPS, t_len, HEAD_DIM), lambda i, j: (i, 0, 0, 0)),
                  pl.BlockSpec((1, dv, t_len), lambda i, j: (i, 0, 0))],
        out_specs=pl.BlockSpec((1, dq, Q_BLOCK), lambda i, j: (i, 0, j)),
        out_shape=jax.ShapeDtypeStruct((b, dq, t_len), BF16),
        scratch_shapes=[pltpu.VMEM(((n_tiles + 2) * N_HEADS, Q_BLOCK, Q_BLOCK), F32),
                        pltpu.VMEM((n_tiles * N_HEADS, Q_BLOCK, Q_BLOCK), F32),
                        pltpu.VMEM((N_KV_GROUPS, 2 * n_tiles, Q_BLOCK), F32)],
        compiler_params=_cparams(2),
        name="nsa_prompt",
    )(tab, q_t, gl_t, comp_k, comp_vt, slc_k, slc_vt, win_k, win_vt)


def _page_copy(pt_ref, pool_ref, buf_ref, sem, b, j, page):
    start = pl.multiple_of(j * page, page)
    if len(buf_ref.shape) == 2:
        dst = buf_ref.at[pl.ds(start, page)]
    else:
        dst = buf_ref.at[:, :, pl.ds(start, page)]
    return pltpu.make_async_copy(pool_ref.at[pt_ref[b, j]], dst, sem)


def _gather_pages_start(pt_ref, pool_ref, buf_ref, sem, b, n_pages, page):
    def go(j, carry):
        _page_copy(pt_ref, pool_ref, buf_ref, sem, b, j, page).start()
        return carry
    lax.fori_loop(0, n_pages, go, 0)


def _gather_pages_wait(pt_ref, pool_ref, buf_ref, sem, b, n_pages, page):
    def go(j, carry):
        _page_copy(pt_ref, pool_ref, buf_ref, sem, b, j, page).wait()
        return carry
    lax.fori_loop(0, n_pages, go, 0)


def _compress_pages_kernel(pt_ref, pool_ref, pe_ref, w1_ref, b1_ref, w2_ref, o_ref, raw_ref, x_ref, a_ref, b_ref,
                           sem, *, n_seq, n_pages, page, n_chunk):
    b = pl.program_id(0)

    @pl.when(b == 0)
    def _():
        _gather_pages_start(pt_ref, pool_ref, raw_ref, sem, b, n_pages, page)

    _gather_pages_wait(pt_ref, pool_ref, raw_ref, sem, b, n_pages, page)

    def relayout(j, carry):
        r0 = pl.multiple_of(j * page, page)
        for tile in range(ROW_TILES):
            kv, pair = divmod(tile, N_KV_GROUPS // 2)
            x_ref[tile, pl.ds(r0, page), :] = raw_ref[kv, pair * LANES:(pair + 1) * LANES, pl.ds(r0, page)].T
        return carry
    lax.fori_loop(0, n_pages, relayout, 0)

    @pl.when(b + 1 < n_seq)
    def _():
        _gather_pages_start(pt_ref, pool_ref, raw_ref, sem, b + 1, n_pages, page)

    _compress_rows(_load_tiled(x_ref), n_chunk, pe_ref, w1_ref, b1_ref, w2_ref, o_ref.at[0], a_ref, b_ref)


def _compress_pages(page_table, pool_t, pe, w1, b1, w2):
    b, n_pages = page_table.shape
    page = pool_t.shape[3]
    past = n_pages * page
    n_chunk = past // CMP_STRIDE
    assert n_chunk % CMP_MBLK == 0 and page == LANES
    full = lambda a: pl.BlockSpec(a.shape, lambda i: (0,) * a.ndim)
    return pl.pallas_call(
        functools.partial(_compress_pages_kernel, n_seq=b, n_pages=n_pages, page=page, n_chunk=n_chunk),
        grid=(b,),
        in_specs=[pl.BlockSpec(memory_space=pltpu.SMEM), pl.BlockSpec(memory_space=pl.ANY),
                  full(pe), full(w1), full(b1), full(w2)],
        out_specs=pl.BlockSpec((1, n_chunk, D_KV), lambda i: (i, 0, 0)),
        out_shape=jax.ShapeDtypeStruct((b, n_chunk, D_KV), F32),
        scratch_shapes=[pltpu.VMEM((2, D_K, past), F32),
                        pltpu.VMEM((ROW_TILES, past, LANES), F32),
                        pltpu.VMEM((n_chunk, PAIR_HID), F32), pltpu.VMEM((n_chunk + SUBLANES, PAIR_HID), F32),
                        pltpu.SemaphoreType.DMA(())],
        compiler_params=_cparams(1),
        name="compress_pages",
    )(page_table, pool_t, pe, w1, b1, w2)


DEC_TILE = 2048
DEC_ROWS = 16
D_K = D_KV // 2


def _row_bias(dist_row, tab_ref, bias_ref, col0):
    length = dist_row.shape[1]
    biases = _bias_from_bucket(_rel_bucket(dist_row), tab_ref, range(N_HEADS))
    rows = biases + [jnp.zeros((DEC_ROWS - N_HEADS, length), F32)]
    bias_ref[:, pl.ds(col0, length)] = jnp.concatenate(rows, axis=0)


def _head_block(o_all, h):
    g = h // HEADS_PER_GROUP
    return o_all[h:h + 1, g * HEAD_DIM:(g + 1) * HEAD_DIM]


def _new_token_scores(q_blk, new_row, bias0):
    k_new = new_row[:, :D_K].astype(BF16).astype(F32)
    v_new = new_row[:, D_K:].astype(BF16).astype(F32)
    return jnp.sum(q_blk.astype(F32) * k_new, axis=-1, keepdims=True) + bias0, v_new


def _nsa_step_kernel(tab_ref, pt_ref, q_ref, gl_ref, comp_ref, pool_ref, slc_new_ref, win_ref, win_new_ref, o_ref,
                     kv_buf, bias_c_ref, bias_s_ref, bias_w_ref, sems,
                     *, n_seq, n_pages, page, past, n_cmp, w_buf):
    b = pl.program_id(0)
    slot = b % 2
    n_tiles = past // DEC_TILE
    n_blocks = past // SEL_LEN + 1
    blk_lanes = -(-n_blocks // LANES) * LANES

    @pl.when(b == 0)
    def _():
        _gather_pages_start(pt_ref, pool_ref, kv_buf.at[0], sems.at[0], b, n_pages, page)

    @pl.when(b + 1 < n_seq)
    def _():
        _gather_pages_start(pt_ref, pool_ref, kv_buf.at[1 - slot], sems.at[1 - slot], b + 1, n_pages, page)

    lane_t = lax.broadcasted_iota(I32, (1, DEC_TILE), 1)

    @pl.when(b == 0)
    def _():
        lane_c = lax.broadcasted_iota(I32, (1, n_cmp), 1)
        _row_bias(past - (lane_c * CMP_STRIDE + (CMP_LEN - 1)), tab_ref, bias_c_ref, 0)

        def tile_bias(t, carry):
            k0 = pl.multiple_of(t * DEC_TILE, DEC_TILE)
            _row_bias(past - (k0 + lane_t), tab_ref, bias_s_ref, k0)
            return carry
        lax.fori_loop(0, n_tiles, tile_bias, 0)
        lane_w = lax.broadcasted_iota(I32, (1, w_buf), 1)
        _row_bias(w_buf - lane_w, tab_ref, bias_w_ref, 0)

    bias0 = jnp.concatenate([jnp.full((1, 1), tab_ref[0, h], F32) for h in range(N_HEADS)]
                            + [jnp.zeros((DEC_ROWS - N_HEADS, 1), F32)], axis=0)

    q_row = q_ref[0] * SCALE
    gates = jax.nn.sigmoid(gl_ref[0])
    rows = []
    for h in range(N_HEADS):
        g = h // HEADS_PER_GROUP
        parts = []
        if g > 0:
            parts.append(jnp.zeros((1, g * HEAD_DIM), BF16))
        parts.append(q_row[:, h * HEAD_DIM:(h + 1) * HEAD_DIM])
        if g < N_KV_GROUPS - 1:
            parts.append(jnp.zeros((1, (N_KV_GROUPS - 1 - g) * HEAD_DIM), BF16))
        rows.append(jnp.concatenate(parts, axis=1))
    q_blk = jnp.concatenate(rows + [jnp.zeros((DEC_ROWS - N_HEADS, D_K), BF16)], axis=0)

    comp = comp_ref[0].astype(BF16)
    mask_c = lax.broadcasted_iota(I32, (DEC_ROWS, n_cmp), 1) < n_cmp - 1
    p_c = _masked_softmax_rows(_dot_nt(q_blk, comp[:, :D_K]) + bias_c_ref[...], mask_c)
    o_c = _dot(p_c.astype(BF16), comp[:, D_K:])
    p_sums = [jnp.sum(p_c[g * HEADS_PER_GROUP:(g + 1) * HEADS_PER_GROUP], axis=0, keepdims=True)
              for g in range(N_KV_GROUPS)]
    p_sum = jnp.concatenate(p_sums + [jnp.zeros((DEC_ROWS - N_KV_GROUPS, n_cmp), F32)], axis=0)

    c_start = lax.broadcasted_iota(I32, (n_cmp, blk_lanes), 0) * CMP_STRIDE
    s_start = lax.broadcasted_iota(I32, (n_cmp, blk_lanes), 1) * SEL_LEN
    c_idx = lax.broadcasted_iota(I32, (n_cmp, blk_lanes), 0)
    cover = jnp.where((c_start < s_start + SEL_LEN) & (c_start + CMP_LEN > s_start) & (c_idx < n_cmp - 1),
                      1.0, 0.0).astype(BF16)
    imp = _dot(p_sum.astype(BF16), cover)
    blk = lax.broadcasted_iota(I32, (DEC_ROWS, blk_lanes), 1)
    cur = past // SEL_LEN
    forced = (blk == 0) | (blk == cur) | (blk == cur - 1)
    score = jnp.where(forced, FORCE, imp)
    score = jnp.where(blk < n_blocks, score, NEG)
    score_t = score.T
    j_idx = lax.broadcasted_iota(I32, (blk_lanes, blk_lanes), 0)
    i_idx = lax.broadcasted_iota(I32, (blk_lanes, blk_lanes), 1)
    sel_groups = []
    for g in range(N_KV_GROUPS):
        col = score_t[:, g:g + 1]
        row = score[g:g + 1, :]
        beats = (col > row) | ((col == row) & (j_idx < i_idx))
        rank = jnp.sum(jnp.where(beats, 1.0, 0.0), axis=0, keepdims=True)
        sel_groups.append(jnp.where((rank < float(N_SEL)) & (row > 0.5 * NEG), 1.0, 0.0))
    sel_rows = [sel_groups[h // HEADS_PER_GROUP] for h in range(N_HEADS)]
    sel_bf = jnp.concatenate(sel_rows + [jnp.zeros((DEC_ROWS - N_HEADS, blk_lanes), F32)], axis=0).astype(BF16)

    _gather_pages_wait(pt_ref, pool_ref, kv_buf.at[slot], sems.at[slot], b, n_pages, page)
    e_row = lax.broadcasted_iota(I32, (blk_lanes, DEC_TILE), 0)
    e_col = lax.broadcasted_iota(I32, (blk_lanes, DEC_TILE), 1) // SEL_LEN

    def sel_tile(t, carry):
        m, l, acc = carry
        k0 = pl.multiple_of(t * DEC_TILE, DEC_TILE)
        expand = jnp.where(e_row == e_col + t * (DEC_TILE // SEL_LEN), 1.0, 0.0).astype(BF16)
        key_sel = _dot(sel_bf, expand)
        kt = kv_buf[slot, 0, :, pl.ds(k0, DEC_TILE)].astype(BF16)
        vt = kv_buf[slot, 1, :, pl.ds(k0, DEC_TILE)].astype(BF16)
        s = _dot(q_blk, kt) + bias_s_ref[:, pl.ds(k0, DEC_TILE)] + jnp.where(key_sel > 0.5, 0.0, NEG)
        m_new = jnp.maximum(m, jnp.max(s, axis=-1, keepdims=True))
        alpha = jnp.exp(m - m_new)
        p = jnp.exp(s - m_new)
        return m_new, alpha * l + jnp.sum(p, axis=-1, keepdims=True), alpha * acc + _dot_nt(p.astype(BF16), vt)

    init = (jnp.full((DEC_ROWS, 1), NEG, F32), jnp.zeros((DEC_ROWS, 1), F32), jnp.zeros((DEC_ROWS, D_K), F32))
    m_s, l_s, acc_s = lax.fori_loop(0, n_tiles, sel_tile, init)
    s_new, v_new = _new_token_scores(q_blk, slc_new_ref[0], bias0)
    cur_sel = sel_bf[:, cur:cur + 1].astype(F32)
    s_new = s_new + jnp.where(cur_sel > 0.5, 0.0, NEG)
    m_fin = jnp.maximum(m_s, s_new)
    alpha = jnp.exp(m_s - m_fin)
    p_new = jnp.exp(s_new - m_fin)
    o_s = (alpha * acc_s + p_new.astype(BF16).astype(F32) * v_new) / (alpha * l_s + p_new)

    kw = win_ref[0, 0].astype(BF16)
    vw = win_ref[0, 1].astype(BF16)
    idx_w = lax.broadcasted_iota(I32, (DEC_ROWS, w_buf), 1)
    s_w = jnp.where(w_buf - idx_w < WINDOW, _dot(q_blk, kw) + bias_w_ref[...], NEG)
    s_wn, v_wn = _new_token_scores(q_blk, win_new_ref[0], bias0)
    m_w = jnp.maximum(jnp.max(s_w, axis=-1, keepdims=True), s_wn)
    e_w = jnp.exp(s_w - m_w)
    e_wn = jnp.exp(s_wn - m_w)
    o_w = ((_dot_nt(e_w.astype(BF16), vw) + e_wn.astype(BF16).astype(F32) * v_wn)
           / (jnp.sum(e_w, axis=-1, keepdims=True) + e_wn))

    outs = []
    for h in range(N_HEADS):
        outs.append(gates[:, 3 * h:3 * h + 1] * _head_block(o_c, h)
                    + gates[:, 3 * h + 1:3 * h + 2] * _head_block(o_s, h)
                    + gates[:, 3 * h + 2:3 * h + 3] * _head_block(o_w, h))
    o_ref[0] = jnp.concatenate(outs, axis=1).astype(o_ref.dtype)


def _feature_major(cache):
    nd = cache.ndim
    perm = tuple(range(nd - 4)) + (nd - 3, nd - 2, nd - 1, nd - 4)
    t = cache.transpose(perm)
    return t.reshape(t.shape[:nd - 3] + (t.shape[nd - 3] * t.shape[nd - 2], t.shape[nd - 1]))


def _nsa_step(tab, page_table, q, gl, comp, pool_t, slc_new, win_t, win_new):
    b, n_pages = page_table.shape
    page = pool_t.shape[3]
    past = n_pages * page
    n_cmp = comp.shape[1]
    w_buf = win_t.shape[3]
    dq = q.shape[2]
    assert past % DEC_TILE == 0 and past % SEL_LEN == 0 and w_buf == WINDOW and n_cmp == past // CMP_STRIDE
    row3 = lambda w: pl.BlockSpec((1, 1, w), lambda i: (i, 0, 0))
    return pl.pallas_call(
        functools.partial(_nsa_step_kernel, n_seq=b, n_pages=n_pages, page=page, past=past, n_cmp=n_cmp,
                          w_buf=w_buf),
        grid=(b,),
        in_specs=[pl.BlockSpec(memory_space=pltpu.SMEM), pl.BlockSpec(memory_space=pltpu.SMEM),
                  row3(dq), row3(LANES),
                  pl.BlockSpec((1, n_cmp, D_KV), lambda i: (i, 0, 0)),
                  pl.BlockSpec(memory_space=pl.ANY),
                  row3(D_KV),
                  pl.BlockSpec((1, 2, D_K, w_buf), lambda i: (i, 0, 0, 0)),
                  row3(D_KV)],
        out_specs=row3(dq),
        out_shape=jax.ShapeDtypeStruct((b, 1, dq), BF16),
        scratch_shapes=[pltpu.VMEM((2, 2, D_K, past), F32),
                        pltpu.VMEM((DEC_ROWS, n_cmp), F32),
                        pltpu.VMEM((DEC_ROWS, past), F32),
                        pltpu.VMEM((DEC_ROWS, w_buf), F32),
                        pltpu.SemaphoreType.DMA((2,))],
        compiler_params=_cparams(1),
        name="nsa_step",
    )(tab, page_table, q, gl, comp, pool_t, slc_new, win_t, win_new)


def _block_diag_pairs(w):
    n, blk, _ = w.shape
    w = w.reshape(n // 2, 2, blk, blk)
    z = jnp.zeros((n // 2, blk, blk), w.dtype)
    top = jnp.concatenate([w[:, 0], z], axis=2)
    bot = jnp.concatenate([z, w[:, 1]], axis=2)
    return jnp.concatenate([top, bot], axis=1).astype(BF16)


def _prep_weights(w_in_a, conv_w, conv_b, w_rg, b_rg, w_ig, b_ig, lru_lambda, w_in_b, w_kv_shared, cmp_pe, cmp_w1,
                  cmp_b1, cmp_w2, w_mem_kv, w_out, ln1_g, ln1_b, ln2_g, ln2_b, w_router_g, b_router_g,
                  w_router_e, b_router_e, w_exp_gate, w_exp_up, w_exp_down):
    depth, d_model, _ = w_out.shape
    n_a = w_in_a.shape[0]
    nq = N_HEADS * HEAD_DIM
    n_gl = 3 * N_HEADS
    hid = EXPERTS_PER_GROUP * D_EXPERT
    p = {}
    p["w_in_a"] = w_in_a.astype(BF16)
    p["w_in_b"] = jnp.concatenate(
        [w_in_b[:, :, :nq], w_in_b[:, :, nq + n_gl:], w_in_b[:, :, nq:nq + n_gl],
         jnp.zeros((w_in_b.shape[0], d_model, LANES - n_gl), w_in_b.dtype)], axis=2).astype(BF16)
    p["w_kv"] = w_kv_shared.astype(BF16)
    p["w_mem"] = w_mem_kv.transpose(1, 0, 2).reshape(d_model, depth * 2 * D_MEM).astype(BF16)
    p["wo_a"] = w_out[:, :D_RNN].astype(BF16)
    p["wo_m"] = w_out[:, D_RNN:].astype(BF16)
    p["conv_w"] = conv_w
    p["conv_b"] = conv_b[:, None]
    p["w_rg"] = jnp.stack([_block_diag_pairs(w_rg[l]) for l in range(n_a)])
    p["w_ig"] = jnp.stack([_block_diag_pairs(w_ig[l]) for l in range(n_a)])
    p["b_rg"] = b_rg[:, None]
    p["b_ig"] = b_ig[:, None]
    p["lam"] = lru_lambda[:, None]
    p["pe"], p["cmp_w1"], p["cmp_b1"], p["cmp_w2"] = _compress_weights(cmp_pe, cmp_w1, cmp_b1, cmp_w2)
    p["ln1_g"], p["ln1_b"], p["ln2_g"], p["ln2_b"] = ln1_g[:, None], ln1_b[:, None], ln2_g[:, None], ln2_b[:, None]
    n_r = N_GROUPS + N_EXPERTS
    wr = jnp.concatenate([w_router_g, w_router_e, jnp.zeros((depth, d_model, LANES - n_r), F32)], axis=2)
    p["wr"] = wr.astype(BF16)
    p["rb"] = jnp.concatenate([b_router_g, b_router_e, jnp.zeros((depth, LANES - n_r), F32)], axis=1)[:, None]

    def by_group(w):
        w = w.astype(BF16).reshape(depth, N_GROUPS, EXPERTS_PER_GROUP, d_model, D_EXPERT)
        return w.transpose(0, 1, 3, 2, 4).reshape(depth, N_GROUPS, d_model, hid)
    p["w_gu"] = jnp.concatenate([by_group(w_exp_gate), by_group(w_exp_up)], axis=3)
    p["w_dn"] = w_exp_down.astype(BF16).reshape(depth, N_GROUPS, hid, d_model)
    return p


def _kv_layouts(rows):
    b, t_len, _ = rows.shape
    k = rows[:, :, :D_KV // 2].reshape(b, t_len, N_KV_GROUPS, HEAD_DIM).transpose(0, 2, 1, 3)
    return k, rows[:, :, D_KV // 2:].transpose(0, 2, 1)


def _trunk_tail(x, mix_a, mix_m, p, l, alpha, tm, spare=None):
    x, grp = _out_ln(x, mix_a, mix_m, p["wo_a"][l], p["wo_m"][l], p["ln1_g"][l], p["ln1_b"][l],
                     p["wr"][l], p["rb"][l], alpha, tm)
    if x.shape[0] < 8 * MOE_TILE:
        return _moe_ln(x, p["wr"][l], p["rb"][l], p["w_gu"][l], p["w_dn"][l],
                       p["ln2_g"][l], p["ln2_b"][l], alpha, tm), None
    dest, tile_grp = _moe_dispatch(grp[:, 0])
    xs = _moe_place(x, dest, tile_grp.shape[0] * MOE_TILE, spare)
    ys = _moe_grouped(xs, tile_grp, p["wr"][l], p["rb"][l], p["w_gu"][l], p["w_dn"][l])
    return _moe_return_ln(x, ys, dest, p["ln2_g"][l], p["ln2_b"][l], alpha), xs


def kernel(x_prompt, x_sample, mem_prompt, cache_cmp_kv, cache_slc_kv, cache_win_kv, cache_mem_kv, state_lru_h, state_conv, page_table, w_in_a, conv_w, conv_b, w_rg, b_rg, w_ig, b_ig, lru_lambda, w_in_b, w_kv_shared, cmp_pe, cmp_w1, cmp_b1, cmp_w2, rel_bias, w_mem_kv, w_out, ln1_g, ln1_b, ln2_g, ln2_b, w_router_g, b_router_g, w_router_e, b_router_e, w_exp_gate, w_exp_up, w_exp_down):
    p = _prep_weights(w_in_a, conv_w, conv_b, w_rg, b_rg, w_ig, b_ig, lru_lambda, w_in_b, w_kv_shared, cmp_pe,
                      cmp_w1, cmp_b1, cmp_w2, w_mem_kv, w_out, ln1_g, ln1_b, ln2_g, ln2_b, w_router_g,
                      b_router_g, w_router_e, b_router_e, w_exp_gate, w_exp_up, w_exp_down)
    depth, d_model, _ = w_out.shape
    n_a = w_in_a.shape[0]
    alpha = (2 * depth) ** 0.25
    kv_shape = (2, N_KV_GROUPS, HEAD_DIM)
    a_outs = [(0, D_RNN, F32), (D_RNN, D_RNN, F32), (2 * D_RNN, D_MEM, BF16)]
    b_outs = [(0, D_RNN, BF16), (D_RNN, D_MEM, BF16), (D_RNN + D_MEM, LANES, F32)]
    kv_outs = [(0, D_KV, F32), (D_KV, D_KV, F32), (2 * D_KV, D_KV, F32)]

    bp, t_len, _ = x_prompt.shape
    n_mem = mem_prompt.shape[1]
    tm = 512
    mem_rows = _proj(mem_prompt.reshape(bp * n_mem, d_model), p["w_mem"],
                     [(l * 2 * D_MEM, 2 * D_MEM, F32) for l in range(depth)], tm, "mem_kv_proj")
    p_mem = jnp.stack(mem_rows).reshape(depth, bp, n_mem, 2 * D_MEM)
    x = x_prompt.reshape(bp * t_len, d_model)
    lru_p, conv_p = [], []
    spare = None
    for l in range(depth):
        if l < n_a:
            gate, xr, mq = _proj(x, p["w_in_a"][l], a_outs, tm, "in_proj_a")
            mix_a, h_t, buf = _rglru_prompt(gate.reshape(bp, t_len, D_RNN), xr.reshape(bp, t_len, D_RNN),
                                            p["conv_w"][l], p["conv_b"][l], p["w_rg"][l], p["w_ig"][l],
                                            p["b_rg"][l], p["b_ig"][l], p["lam"][l])
            lru_p.append(h_t[:, 0])
            conv_p.append(buf)
        else:
            if l == n_a:
                cmp_rows, slc_rows, win_rows, slc_bf, win_bf = _proj(
                    x, p["w_kv"], kv_outs + [(D_KV, D_KV, BF16), (2 * D_KV, D_KV, BF16)], tm, "kv_proj")
                comp = _compress_prompt(cmp_rows.reshape(bp, t_len, D_KV), p["pe"], p["cmp_w1"], p["cmp_b1"],
                                        p["cmp_w2"])
                comp_k, comp_vt = _kv_layouts(comp.astype(BF16))
                slc_k, slc_vt = _kv_layouts(slc_bf.reshape(bp, t_len, D_KV))
                win_k, win_vt = _kv_layouts(win_bf.reshape(bp, t_len, D_KV))
            q, mq, gl = _proj(x, p["w_in_b"][l - n_a], b_outs, tm, "in_proj_b")
            mix_t = _nsa_prompt(rel_bias, q.reshape(bp, t_len, D_RNN).transpose(0, 2, 1),
                                gl.reshape(bp, t_len, LANES).transpose(0, 2, 1),
                                comp_k, comp_vt, slc_k, slc_vt, win_k, win_vt)
            mix_a = mix_t.transpose(0, 2, 1)
        mix_m = _mem_attn(mq.reshape(bp, t_len, D_MEM), p_mem, l)
        x, spare = _trunk_tail(x, mix_a.reshape(bp * t_len, D_RNN), mix_m.reshape(bp * t_len, D_MEM), p, l,
                               alpha, tm, spare)
    y_prompt = x.reshape(bp, t_len, d_model)
    p_cmp_kv = cmp_rows.reshape((bp, t_len) + kv_shape)
    p_slc_kv = slc_rows.reshape((bp, t_len) + kv_shape)
    w_keep = min(WINDOW, t_len)
    p_win_kv = win_rows.reshape((bp, t_len) + kv_shape)[:, t_len - w_keep:]
    p_mem_kv = p_mem.reshape(depth, bp, n_mem, 2, N_MEM_HEADS, HEAD_DIM)
    p_lru_h = jnp.stack(lru_p)
    p_conv = jnp.stack(conv_p)

    bd, s_len, _ = x_sample.shape
    assert s_len == 1
    n_pool, page = cache_cmp_kv.shape[:2]
    w_buf = cache_win_kv.shape[1]
    mem_t = _feature_major(cache_mem_kv)
    slc_pool_t = _feature_major(cache_slc_kv)
    win_t = _feature_major(cache_win_kv)
    x = x_sample.reshape(bd, d_model)
    lru_s, conv_s = [], []
    for l in range(depth):
        if l < n_a:
            gate, xr, mq = _proj(x, p["w_in_a"][l], a_outs, bd, "in_proj_a_step")
            mix_a, h_t, buf = _rglru_step(gate, xr, state_conv[l].transpose(1, 0, 2), state_lru_h[l],
                                          p["conv_w"][l], p["conv_b"][l], p["w_rg"][l], p["w_ig"][l],
                                          p["b_rg"][l], p["b_ig"][l], p["lam"][l])
            lru_s.append(h_t)
            conv_s.append(buf.transpose(1, 0, 2))
        else:
            if l == n_a:
                s_cmp, s_slc, s_win = _proj(x, p["w_kv"], kv_outs, bd, "kv_proj_step")
                comp_s = _compress_pages(page_table, _feature_major(cache_cmp_kv), p["pe"],
                                         p["cmp_w1"], p["cmp_b1"], p["cmp_w2"])
            q, mq, gl = _proj(x, p["w_in_b"][l - n_a], b_outs, bd, "in_proj_b_step")
            mix_a = _nsa_step(rel_bias, page_table, q.reshape(bd, 1, D_RNN), gl.reshape(bd, 1, LANES), comp_s,
                              slc_pool_t, s_slc.reshape(bd, 1, D_KV), win_t, s_win.reshape(bd, 1, D_KV))
            mix_a = mix_a.reshape(bd, D_RNN)
        mix_m = _mem_attn_step(mq.reshape(bd, 1, D_MEM), mem_t, l).reshape(bd, D_MEM)
        x, _ = _trunk_tail(x, mix_a, mix_m, p, l, alpha, bd)
    y_sample = x.reshape(bd, 1, d_model)
    s_cmp_kv = s_cmp.reshape((bd, 1) + kv_shape)
    s_slc_kv = s_slc.reshape((bd, 1) + kv_shape)
    s_win_kv = jnp.concatenate([cache_win_kv, s_win.reshape((bd, 1) + kv_shape)], axis=1)[:, 1:]
    s_lru_h = jnp.stack(lru_s)
    s_conv = jnp.stack(conv_s)
    return (y_prompt, y_sample, p_cmp_kv, p_slc_kv, p_win_kv, p_mem_kv, p_lru_h, p_conv,
            s_cmp_kv, s_slc_kv, s_win_kv, s_lru_h, s_conv)
```

```python
import functools
import math

import jax
import jax.numpy as jnp
from jax import lax
from jax.experimental import pallas as pl
from jax.experimental.pallas import tpu as pltpu

F32 = jnp.float32
BF16 = jnp.bfloat16
I32 = jnp.int32

HEAD_DIM = 64
N_MEM_HEADS = 4
D_MEM = N_MEM_HEADS * HEAD_DIM
N_HEADS = 12
D_RNN = N_HEADS * HEAD_DIM
N_KV_GROUPS = 4
HEADS_PER_GROUP = N_HEADS // N_KV_GROUPS
D_KV = 2 * N_KV_GROUPS * HEAD_DIM
CONV_W = 4
LRU_C = 8.0
CMP_LEN = 32
CMP_STRIDE = 16
CMP_HID = 2 * HEAD_DIM
SEL_LEN = 64
N_SEL = 16
WINDOW = 512
N_BUCKETS = 32
MAX_DISTANCE = 1024
N_GROUPS = 4
EXPERTS_PER_GROUP = 8
N_EXPERTS = N_GROUPS * EXPERTS_PER_GROUP
D_EXPERT = 128
Q_BLOCK = 128
LN_EPS = 1e-5
NEG = -1e30
FORCE = 1e9
SCALE = HEAD_DIM ** -0.5

LANES = 128
SUBLANES = 8
VMEM_LIMIT_BYTES = 56 * 1024 * 1024


def _cparams(n_axes):
    return pltpu.CompilerParams(dimension_semantics=("arbitrary",) * n_axes,
                                vmem_limit_bytes=VMEM_LIMIT_BYTES)


def _dot(a, b):
    return jnp.dot(a, b, preferred_element_type=F32)


def _dot_nt(a, b):
    return lax.dot_general(a, b, (((1,), (1,)), ((), ())), preferred_element_type=F32)


def _layer_norm(v, g, b):
    mu = jnp.mean(v, axis=-1, keepdims=True)
    d = v - mu
    var = jnp.mean(d * d, axis=-1, keepdims=True)
    return d * lax.rsqrt(var + LN_EPS) * g + b


def _rel_bucket(dist):
    n = jnp.maximum(dist, 0)
    max_exact = N_BUCKETS // 2
    nf = jnp.maximum(n, 1).astype(F32)
    large = max_exact + (jnp.log(nf / max_exact) / math.log(MAX_DISTANCE / max_exact)
                         * (N_BUCKETS - max_exact)).astype(I32)
    large = jnp.minimum(large, N_BUCKETS - 1)
    return jnp.where(n < max_exact, n, large)


def _bias_from_bucket(bucket, tab_ref, heads):
    masks = [bucket == k for k in range(1, N_BUCKETS)]
    out = []
    for h in heads:
        acc = jnp.full(bucket.shape, tab_ref[0, h], F32)
        for k in range(1, N_BUCKETS):
            acc = jnp.where(masks[k - 1], tab_ref[k, h], acc)
        out.append(acc)
    return out


def _proj_kernel(x_ref, w_ref, *o_refs, cols):
    y = _dot(x_ref[...].astype(BF16), w_ref[...])
    for o_ref, (off, n) in zip(o_refs, cols):
        o_ref[...] = y[:, off:off + n].astype(o_ref.dtype)


def _proj(x, w, outs, tm, name):
    m, k = x.shape
    n = w.shape[1]
    tm = min(tm, m)
    assert m % tm == 0 and all(off + wd <= n for off, wd, _ in outs)
    return pl.pallas_call(
        functools.partial(_proj_kernel, cols=tuple((off, wd) for off, wd, _ in outs)),
        grid=(m // tm,),
        in_specs=[pl.BlockSpec((tm, k), lambda i: (i, 0)),
                  pl.BlockSpec((k, n), lambda i: (0, 0))],
        out_specs=[pl.BlockSpec((tm, wd), lambda i: (i, 0)) for _, wd, _ in outs],
        out_shape=[jax.ShapeDtypeStruct((m, wd), dt) for _, wd, dt in outs],
        compiler_params=_cparams(1),
        name=name,
    )(x, w)


def _softplus(z):
    return jnp.maximum(z, 0.0) + jnp.log1p(jnp.exp(-jnp.abs(z)))


def _lru_gates(xc, wr_ref, wi_ref, br, bi, lam):
    xcb = xc.astype(BF16)
    nb = D_RNN // LANES
    r_l = jnp.concatenate([_dot(xcb[:, j * LANES:(j + 1) * LANES], wr_ref[j]) for j in range(nb)], axis=1)
    i_l = jnp.concatenate([_dot(xcb[:, j * LANES:(j + 1) * LANES], wi_ref[j]) for j in range(nb)], axis=1)
    r = jax.nn.sigmoid(r_l + br)
    i = jax.nn.sigmoid(i_l + bi)
    log_a = -LRU_C * r * _softplus(-lam)
    a = jnp.exp(log_a)
    u = jnp.sqrt(1.0 - a * a) * i * xc
    return a, u


def _rglru_prompt_kernel(gate_ref, xr_ref, cw_ref, cb_ref, wr_ref, wi_ref, br_ref, bi_ref, lam_ref,
                         y_ref, ht_ref, buf_ref, xp_ref, *, t_len, chunk):
    d = D_RNN
    pad = SUBLANES
    xp_ref[0:pad, :] = jnp.zeros((pad, d), F32)
    xp_ref[pad:pad + t_len, :] = xr_ref[0]
    buf_ref[0] = xr_ref[0, t_len - (CONV_W - 1):t_len, :]
    cw = cw_ref[...]
    cb = cb_ref[...]
    br = br_ref[...]
    bi = bi_ref[...]
    lam = lam_ref[...]
    row_in_tile = lax.broadcasted_iota(I32, (chunk, d), 0) & (SUBLANES - 1)
    h = jnp.zeros((1, d), F32)
    for c in range(t_len // chunk):
        base = c * chunk
        xc = cb
        for k in range(CONV_W):
            lo = pad - (CONV_W - 1) + k + base
            xc = xc + xp_ref[lo:lo + chunk, :] * cw[k:k + 1, :]
        a, u = _lru_gates(xc, wr_ref, wi_ref, br, bi, lam)
        for k in (1, 2, 4):
            a_s = pltpu.roll(a, k, axis=0)
            u_s = pltpu.roll(u, k, axis=0)
            m = row_in_tile >= k
            u = jnp.where(m, a * u_s + u, u)
            a = jnp.where(m, a * a_s, a)
        hs = []
        for j in range(chunk // SUBLANES):
            h_t = a[j * SUBLANES:(j + 1) * SUBLANES] * h + u[j * SUBLANES:(j + 1) * SUBLANES]
            h = h_t[SUBLANES - 1:SUBLANES]
            hs.append(h_t)
        hfull = jnp.concatenate(hs, axis=0)
        g = jax.nn.gelu(gate_ref[0, base:base + chunk, :])
        y_ref[0, base:base + chunk, :] = (g * hfull).astype(y_ref.dtype)
    ht_ref[0] = h


def _rglru_prompt(gate, xr, cw, cb, wr_bd, wi_bd, br, bi, lam):
    b, t_len, d = xr.shape
    chunk = min(256, t_len)
    full = lambda shape: pl.BlockSpec(shape, lambda i: (0,) * len(shape))
    return pl.pallas_call(
        functools.partial(_rglru_prompt_kernel, t_len=t_len, chunk=chunk),
        grid=(b,),
        in_specs=[pl.BlockSpec((1, t_len, d), lambda i: (i, 0, 0)),
                  pl.BlockSpec((1, t_len, d), lambda i: (i, 0, 0)),
                  full((CONV_W, d)), full((1, d)), full(wr_bd.shape), full(wi_bd.shape),
                  full((1, d)), full((1, d)), full((1, d))],
        out_specs=[pl.BlockSpec((1, t_len, d), lambda i: (i, 0, 0)),
                   pl.BlockSpec((1, 1, d), lambda i: (i, 0, 0)),
                   pl.BlockSpec((1, CONV_W - 1, d), lambda i: (i, 0, 0))],
        out_shape=[jax.ShapeDtypeStruct((b, t_len, d), BF16),
                   jax.ShapeDtypeStruct((b, 1, d), F32),
                   jax.ShapeDtypeStruct((b, CONV_W - 1, d), F32)],
        scratch_shapes=[pltpu.VMEM((t_len + SUBLANES, d), F32)],
        compiler_params=_cparams(1),
        name="rglru_prompt",
    )(gate, xr, cw, cb, wr_bd, wi_bd, br, bi, lam)


def _rglru_step_kernel(gate_ref, xr_ref, conv_ref, h0_ref, cw_ref, cb_ref, wr_ref, wi_ref, br_ref, bi_ref,
                       lam_ref, y_ref, ht_ref, buf_ref):
    xr = xr_ref[...]
    cw = cw_ref[...]
    xc = cb_ref[...] + xr * cw[CONV_W - 1:CONV_W, :]
    for k in range(CONV_W - 1):
        xc = xc + conv_ref[k] * cw[k:k + 1, :]
    a, u = _lru_gates(xc, wr_ref, wi_ref, br_ref[...], bi_ref[...], lam_ref[...])
    h = a * h0_ref[...] + u
    ht_ref[...] = h
    y_ref[...] = (jax.nn.gelu(gate_ref[...]) * h).astype(y_ref.dtype)
    for k in range(CONV_W - 2):
        buf_ref[k] = conv_ref[k + 1]
    buf_ref[CONV_W - 2] = xr


def _rglru_step(gate, xr, conv_t, h0, cw, cb, wr_bd, wi_bd, br, bi, lam):
    b, d = xr.shape
    return pl.pallas_call(
        _rglru_step_kernel,
        out_shape=[jax.ShapeDtypeStruct((b, d), BF16),
                   jax.ShapeDtypeStruct((b, d), F32),
                   jax.ShapeDtypeStruct((CONV_W - 1, b, d), F32)],
        compiler_params=pltpu.CompilerParams(vmem_limit_bytes=VMEM_LIMIT_BYTES),
        name="rglru_step",
    )(gate, xr, conv_t, h0, cw, cb, wr_bd, wi_bd, br, bi, lam)


def _mem_attn_kernel(q_ref, kv_ref, o_ref, *, rows):
    q = q_ref[0]
    if rows < SUBLANES:
        q = jnp.broadcast_to(q, (SUBLANES, D_MEM))
    kv = kv_ref[0, 0].astype(BF16)
    outs = []
    for h in range(N_MEM_HEADS):
        qh = q[:, h * HEAD_DIM:(h + 1) * HEAD_DIM]
        kh = kv[:, h * HEAD_DIM:(h + 1) * HEAD_DIM]
        vh = kv[:, D_MEM + h * HEAD_DIM:D_MEM + (h + 1) * HEAD_DIM]
        s = _dot_nt(qh, kh) * SCALE
        m = jnp.max(s, axis=-1, keepdims=True)
        e = jnp.exp(s - m)
        p = e / jnp.sum(e, axis=-1, keepdims=True)
        outs.append(_dot(p.astype(BF16), vh))
    o = jnp.concatenate(outs, axis=1)
    o_ref[0] = o[:rows].astype(o_ref.dtype)


def _mem_attn(q, mem_kv, layer):
    b, t_len, _ = q.shape
    n_mem = mem_kv.shape[2]
    tq = min(t_len, 512)
    return pl.pallas_call(
        functools.partial(_mem_attn_kernel, rows=tq),
        grid=(b, t_len // tq),
        in_specs=[pl.BlockSpec((1, tq, D_MEM), lambda i, j: (i, j, 0)),
                  pl.BlockSpec((1, 1, n_mem, 2 * D_MEM), lambda i, j: (layer, i, 0, 0))],
        out_specs=pl.BlockSpec((1, tq, D_MEM), lambda i, j: (i, j, 0)),
        out_shape=jax.ShapeDtypeStruct((b, t_len, D_MEM), BF16),
        compiler_params=_cparams(2),
        name="mem_attn",
    )(q, mem_kv)


def _mem_attn_step_kernel(q_ref, kv_ref, o_ref):
    q_row = q_ref[0] * SCALE
    rows = []
    for h in range(N_MEM_HEADS):
        parts = []
        if h > 0:
            parts.append(jnp.zeros((1, h * HEAD_DIM), BF16))
        parts.append(q_row[:, h * HEAD_DIM:(h + 1) * HEAD_DIM])
        if h < N_MEM_HEADS - 1:
            parts.append(jnp.zeros((1, (N_MEM_HEADS - 1 - h) * HEAD_DIM), BF16))
        rows.append(jnp.concatenate(parts, axis=1))
    n_rows = 2 * SUBLANES
    q_blk = jnp.concatenate(rows + [jnp.zeros((n_rows - N_MEM_HEADS, D_MEM), BF16)], axis=0)
    s = _dot(q_blk, kv_ref[0, 0, 0].astype(BF16))
    e = jnp.exp(s - jnp.max(s, axis=-1, keepdims=True))
    p = e / jnp.sum(e, axis=-1, keepdims=True)
    o_all = _dot_nt(p.astype(BF16), kv_ref[0, 0, 1].astype(BF16))
    o = jnp.concatenate([o_all[h:h + 1, h * HEAD_DIM:(h + 1) * HEAD_DIM] for h in range(N_MEM_HEADS)], axis=1)
    o_ref[0] = o.astype(o_ref.dtype)


def _mem_attn_step(q, mem_t, layer):
    b = q.shape[0]
    n_mem = mem_t.shape[4]
    return pl.pallas_call(
        _mem_attn_step_kernel,
        grid=(b,),
        in_specs=[pl.BlockSpec((1, 1, D_MEM), lambda i: (i, 0, 0)),
                  pl.BlockSpec((1, 1, 2, D_MEM, n_mem), lambda i: (layer, i, 0, 0, 0))],
        out_specs=pl.BlockSpec((1, 1, D_MEM), lambda i: (i, 0, 0)),
        out_shape=jax.ShapeDtypeStruct((b, 1, D_MEM), BF16),
        compiler_params=_cparams(1),
        name="mem_attn_step",
    )(q, mem_t)


def _out_ln_kernel(x_ref, ma_ref, mm_ref, wa_ref, wm_ref, g_ref, b_ref, wr_ref, rb_ref, o_ref, grp_ref, *, alpha):
    sub = _dot(ma_ref[...], wa_ref[...]) + _dot(mm_ref[...], wm_ref[...])
    y = _layer_norm(alpha * x_ref[...] + sub, g_ref[...], b_ref[...])
    o_ref[...] = y
    grp, _ = _route_group(_dot(y.astype(BF16), wr_ref[...]) + rb_ref[...])
    grp_ref[...] = jnp.broadcast_to(grp, grp_ref.shape)


def _out_ln(x, mix_a, mix_m, wo_a, wo_m, g, b, wr, rb, alpha, tm):
    m, d = x.shape
    da = mix_a.shape[1]
    dm = mix_m.shape[1]
    return pl.pallas_call(
        functools.partial(_out_ln_kernel, alpha=alpha),
        grid=(m // tm,),
        in_specs=[pl.BlockSpec((tm, d), lambda i: (i, 0)),
                  pl.BlockSpec((tm, da), lambda i: (i, 0)),
                  pl.BlockSpec((tm, dm), lambda i: (i, 0)),
                  pl.BlockSpec((da, d), lambda i: (0, 0)),
                  pl.BlockSpec((dm, d), lambda i: (0, 0)),
                  pl.BlockSpec((1, d), lambda i: (0, 0)),
                  pl.BlockSpec((1, d), lambda i: (0, 0)),
                  pl.BlockSpec((d, LANES), lambda i: (0, 0)),
                  pl.BlockSpec((1, LANES), lambda i: (0, 0))],
        out_specs=[pl.BlockSpec((tm, d), lambda i: (i, 0)), pl.BlockSpec((tm, LANES), lambda i: (i, 0))],
        out_shape=[jax.ShapeDtypeStruct((m, d), F32), jax.ShapeDtypeStruct((m, LANES), I32)],
        compiler_params=_cparams(1),
        name="out_ln",
    )(x, mix_a, mix_m, wo_a, wo_m, g, b, wr, rb)


def _route_group(logits):
    lane = lax.broadcasted_iota(I32, logits.shape, 1)
    lg = jnp.where(lane < N_GROUPS, logits, -jnp.inf)
    gmax = jnp.max(lg, axis=-1, keepdims=True)
    return jnp.min(jnp.where(lg == gmax, lane, jnp.int32(4 * LANES)), axis=-1, keepdims=True), gmax


def _route(logits):
    lane = lax.broadcasted_iota(I32, logits.shape, 1)
    big = jnp.int32(4 * LANES)
    is_g = lane < N_GROUPS
    grp, gmax = _route_group(logits)
    gsum = jnp.sum(jnp.where(is_g, jnp.exp(logits - gmax), 0.0), axis=-1, keepdims=True)
    g_w = 1.0 / gsum
    e_idx = lane - N_GROUPS
    in_grp = (lane >= N_GROUPS) & (lane < N_GROUPS + N_EXPERTS) & ((e_idx >> 3) == grp)
    v = jnp.where(in_grp, logits, -jnp.inf)
    v1 = jnp.max(v, axis=-1, keepdims=True)
    i1 = jnp.min(jnp.where(v == v1, lane, big), axis=-1, keepdims=True)
    vr = jnp.where(lane == i1, -jnp.inf, v)
    v2 = jnp.max(vr, axis=-1, keepdims=True)
    i2 = jnp.min(jnp.where(vr == v2, lane, big), axis=-1, keepdims=True)
    e2 = jnp.exp(v2 - v1)
    w1 = g_w / (1.0 + e2)
    w2 = g_w * e2 / (1.0 + e2)
    return jnp.where(lane == i1, w1, 0.0) + jnp.where(lane == i2, w2, 0.0)


def _moe_kernel(x_ref, wr_ref, rb_ref, wgu_ref, wd_ref, g_ref, b_ref, o_ref,
                xb_ref, comb_ref, acc_ref, *, alpha):
    grp = pl.program_id(1)
    hid = EXPERTS_PER_GROUP * D_EXPERT

    @pl.when(grp == 0)
    def _():
        xh = x_ref[...].astype(BF16)
        logits = _dot(xh, wr_ref[...]) + rb_ref[...]
        xb_ref[...] = xh
        comb_ref[...] = _route(logits)
        acc_ref[...] = jnp.zeros(acc_ref.shape, F32)

    xb = xb_ref[...]
    gu = _dot(xb, wgu_ref[0])
    gate = gu[:, :hid]
    up = gu[:, hid:]
    hdn = gate * jax.nn.sigmoid(gate) * up
    comb = comb_ref[...]
    lane = lax.broadcasted_iota(I32, comb.shape, 1)
    scale = []
    for e in range(EXPERTS_PER_GROUP):
        col = jnp.sum(jnp.where(lane == N_GROUPS + grp * EXPERTS_PER_GROUP + e, comb, 0.0),
                      axis=-1, keepdims=True)
        scale.append(jnp.broadcast_to(col, (comb.shape[0], D_EXPERT)))
    hdn = hdn * jnp.concatenate(scale, axis=1)
    acc_ref[...] += _dot(hdn.astype(BF16), wd_ref[0])

    @pl.when(grp == N_GROUPS - 1)
    def _():
        o_ref[...] = _layer_norm(alpha * x_ref[...] + acc_ref[...], g_ref[...], b_ref[...])


def _moe_ln(x, wr, rb, wgu, wd, g, b, alpha, tm):
    m, d = x.shape
    hid = EXPERTS_PER_GROUP * D_EXPERT
    return pl.pallas_call(
        functools.partial(_moe_kernel, alpha=alpha),
        grid=(m // tm, N_GROUPS),
        in_specs=[pl.BlockSpec((tm, d), lambda i, j: (i, 0)),
                  pl.BlockSpec((d, LANES), lambda i, j: (0, 0)),
                  pl.BlockSpec((1, LANES), lambda i, j: (0, 0)),
                  pl.BlockSpec((1, d, 2 * hid), lambda i, j: (j, 0, 0)),
                  pl.BlockSpec((1, hid, d), lambda i, j: (j, 0, 0)),
                  pl.BlockSpec((1, d), lambda i, j: (0, 0)),
                  pl.BlockSpec((1, d), lambda i, j: (0, 0))],
        out_specs=pl.BlockSpec((tm, d), lambda i, j: (i, 0)),
        out_shape=jax.ShapeDtypeStruct((m, d), F32),
        scratch_shapes=[pltpu.VMEM((tm, d), BF16), pltpu.VMEM((tm, LANES), F32), pltpu.VMEM((tm, d), F32)],
        compiler_params=_cparams(2),
        name="moe_ln",
    )(x, wr, rb, wgu, wd, g, b)


MOE_TILE = 256


def _moe_dispatch(grp):
    n = grp.shape[0]
    onehot = (grp[:, None] == jnp.arange(N_GROUPS, dtype=I32)[None, :]).astype(I32)
    before = jnp.cumsum(onehot, axis=0) - onehot
    counts = jnp.sum(onehot, axis=0)
    padded = (counts + MOE_TILE - 1) // MOE_TILE * MOE_TILE
    ends = jnp.cumsum(padded)
    dest = jnp.sum(onehot * (before + (ends - padded)[None, :]), axis=1).astype(I32)
    n_slots = n + N_GROUPS * MOE_TILE
    tile_start = jnp.arange(n_slots // MOE_TILE, dtype=I32) * MOE_TILE
    tile_grp = jnp.minimum(jnp.sum((tile_start[:, None] >= ends[None, :]).astype(I32), axis=1), N_GROUPS - 1)
    return dest, tile_grp


def _moe_place_kernel(dest_ref, x_ref, spare_hbm, o_hbm, stage, sem, *, n_tiles):
    del spare_hbm
    i = pl.program_id(0)
    slot = i % 2

    def wait_tile(buf):
        pltpu.make_async_copy(stage.at[buf], o_hbm.at[pl.ds(0, MOE_TILE)], sem.at[buf]).wait()

    @pl.when(i >= 2)
    def _():
        wait_tile(slot)

    stage[slot] = x_ref[...]
    for r in range(MOE_TILE):
        pltpu.make_async_copy(stage.at[slot, pl.ds(r, 1)], o_hbm.at[pl.ds(dest_ref[i * MOE_TILE + r], 1)],
                              sem.at[slot]).start(priority=r % 2)

    @pl.when(i == n_tiles - 1)
    def _():
        @pl.when(i >= 1)
        def _():
            wait_tile(1 - slot)
        wait_tile(slot)


def _moe_place(x, dest, n_slots, spare):
    n, d = x.shape
    if spare is None:
        spare = jnp.zeros((n_slots, d), F32)
    assert n % MOE_TILE == 0
    n_tiles = n // MOE_TILE
    grid_spec = pltpu.PrefetchScalarGridSpec(
        num_scalar_prefetch=1,
        grid=(n_tiles,),
        in_specs=[pl.BlockSpec((MOE_TILE, d), lambda i, ds: (i, 0)), pl.BlockSpec(memory_space=pl.ANY)],
        out_specs=pl.BlockSpec(memory_space=pl.ANY),
        scratch_shapes=[pltpu.VMEM((2, MOE_TILE, d), F32), pltpu.SemaphoreType.DMA((2,))])
    return pl.pallas_call(
        functools.partial(_moe_place_kernel, n_tiles=n_tiles),
        grid_spec=grid_spec,
        out_shape=jax.ShapeDtypeStruct((n_slots, d), F32),
        input_output_aliases={2: 0},
        compiler_params=_cparams(1),
        name="moe_place",
    )(dest, x, spare)


def _moe_grouped_kernel(tgrp_ref, x_ref, wr_ref, rb_ref, wgu_ref, wd_ref, y_ref):
    i = pl.program_id(0)
    hid = EXPERTS_PER_GROUP * D_EXPERT
    xh = x_ref[...].astype(BF16)
    comb = _route(_dot(xh, wr_ref[...]) + rb_ref[...])
    gu = _dot(xh, wgu_ref[0])
    gate = gu[:, :hid]
    hdn = gate * jax.nn.sigmoid(gate) * gu[:, hid:]
    grp = tgrp_ref[i]
    lane = lax.broadcasted_iota(I32, comb.shape, 1)
    scale = []
    for e in range(EXPERTS_PER_GROUP):
        col = jnp.sum(jnp.where(lane == N_GROUPS + grp * EXPERTS_PER_GROUP + e, comb, 0.0),
                      axis=-1, keepdims=True)
        scale.append(jnp.broadcast_to(col, (comb.shape[0], D_EXPERT)))
    y_ref[...] = _dot((hdn * jnp.concatenate(scale, axis=1)).astype(BF16), wd_ref[0])


def _moe_grouped(xs, tile_grp, wr, rb, wgu, wd):
    n_slots, d = xs.shape
    hid = EXPERTS_PER_GROUP * D_EXPERT
    grid_spec = pltpu.PrefetchScalarGridSpec(
        num_scalar_prefetch=1,
        grid=(n_slots // MOE_TILE,),
        in_specs=[pl.BlockSpec((MOE_TILE, d), lambda i, tg: (i, 0)),
                  pl.BlockSpec((d, LANES), lambda i, tg: (0, 0)),
                  pl.BlockSpec((1, LANES), lambda i, tg: (0, 0)),
                  pl.BlockSpec((1, d, 2 * hid), lambda i, tg: (tg[i], 0, 0)),
                  pl.BlockSpec((1, hid, d), lambda i, tg: (tg[i], 0, 0))],
        out_specs=pl.BlockSpec((MOE_TILE, d), lambda i, tg: (i, 0)))
    return pl.pallas_call(
        _moe_grouped_kernel,
        grid_spec=grid_spec,
        out_shape=jax.ShapeDtypeStruct((n_slots, d), F32),
        compiler_params=_cparams(1),
        name="moe_grouped",
    )(tile_grp, xs, wr, rb, wgu, wd)


def _moe_return_ln_kernel(dest_ref, x_ref, y_hbm, g_ref, b_ref, o_ref, ybuf, sem, *, alpha, n_tiles):
    i = pl.program_id(0)
    slot = i % 2

    def start_tile(tile, buf):
        for r in range(MOE_TILE):
            pltpu.make_async_copy(y_hbm.at[pl.ds(dest_ref[tile * MOE_TILE + r], 1)],
                                  ybuf.at[buf, pl.ds(r, 1)], sem.at[buf]).start(priority=r % 2)

    @pl.when(i == 0)
    def _():
        start_tile(i, slot)

    @pl.when(i + 1 < n_tiles)
    def _():
        start_tile(i + 1, 1 - slot)

    pltpu.make_async_copy(y_hbm.at[pl.ds(0, MOE_TILE)], ybuf.at[slot], sem.at[slot]).wait()
    o_ref[...] = _layer_norm(alpha * x_ref[...] + ybuf[slot], g_ref[...], b_ref[...])


def _moe_return_ln(x, ys, dest, g, b, alpha):
    n, d = x.shape
    n_tiles = n // MOE_TILE
    grid_spec = pltpu.PrefetchScalarGridSpec(
        num_scalar_prefetch=1,
        grid=(n_tiles,),
        in_specs=[pl.BlockSpec((MOE_TILE, d), lambda i, ds: (i, 0)),
                  pl.BlockSpec(memory_space=pl.ANY),
                  pl.BlockSpec((1, d), lambda i, ds: (0, 0)),
                  pl.BlockSpec((1, d), lambda i, ds: (0, 0))],
        out_specs=pl.BlockSpec((MOE_TILE, d), lambda i, ds: (i, 0)),
        scratch_shapes=[pltpu.VMEM((2, MOE_TILE, d), F32), pltpu.SemaphoreType.DMA((2,))])
    return pl.pallas_call(
        functools.partial(_moe_return_ln_kernel, alpha=alpha, n_tiles=n_tiles),
        grid_spec=grid_spec,
        out_shape=jax.ShapeDtypeStruct((n, d), F32),
        compiler_params=_cparams(1),
        name="moe_return_ln",
    )(dest, x, ys, g, b)


CMP_MBLK = 128
ROW_TILES = D_KV // LANES
PAIR_HID = 2 * CMP_HID


def _compress_weights(pe, w1, b1, w2):
    w1 = w1.reshape(2, CMP_LEN, HEAD_DIM, CMP_HID)
    z1 = jnp.zeros_like(w1)
    w1p = jnp.concatenate([jnp.concatenate([w1, z1], axis=3), jnp.concatenate([z1, w1], axis=3)], axis=2)
    z2 = jnp.zeros_like(w2)
    w2p = jnp.concatenate([jnp.concatenate([w2, z2], axis=2), jnp.concatenate([z2, w2], axis=2)], axis=1)
    w1p = w1p.astype(BF16).reshape(2, 2, CMP_STRIDE * LANES, PAIR_HID)
    return (jnp.concatenate([pe, pe], axis=2), w1p,
            jnp.concatenate([b1, b1], axis=1)[:, None], w2p.astype(BF16))


def _load_interleaved(x_ref):
    step = CMP_STRIDE * ROW_TILES

    def load(tile, r, mb):
        row0 = pl.multiple_of(mb * (CMP_MBLK * step), CMP_MBLK * step)
        return x_ref[pl.ds(row0 + r * ROW_TILES + tile, CMP_MBLK, stride=step), :]
    return load


def _load_tiled(x_ref):
    def load(tile, r, mb):
        row0 = pl.multiple_of(mb * (CMP_MBLK * CMP_STRIDE), CMP_MBLK * CMP_STRIDE)
        return x_ref[tile, pl.ds(row0 + r, CMP_MBLK, stride=CMP_STRIDE), :]
    return load


def _compress_rows(load, n_chunk, pe_ref, w1_ref, b1_ref, w2_ref, out_ref, a_ref, b_ref):
    n_blk = n_chunk // CMP_MBLK
    for kv in range(2):
        for pair in range(N_KV_GROUPS // 2):
            tile = kv * (N_KV_GROUPS // 2) + pair

            def fill(mb, carry, kv=kv, tile=tile):
                top, bot = [], []
                for r in range(CMP_STRIDE):
                    xg = load(tile, r, mb)
                    top.append((xg + pe_ref[kv, r:r + 1, :]).astype(BF16))
                    bot.append((xg + pe_ref[kv, CMP_STRIDE + r:CMP_STRIDE + r + 1, :]).astype(BF16))
                m0 = pl.multiple_of(mb * CMP_MBLK, CMP_MBLK)
                a_ref[pl.ds(m0, CMP_MBLK), :] = _dot(jnp.concatenate(top, axis=1), w1_ref[kv, 0])
                b_ref[pl.ds(m0, CMP_MBLK), :] = _dot(jnp.concatenate(bot, axis=1), w1_ref[kv, 1])
                return carry

            lax.fori_loop(0, n_blk, fill, 0)
            b_ref[n_chunk:n_chunk + SUBLANES, :] = jnp.zeros((SUBLANES, PAIR_HID), F32)
            for mb in range(n_blk):
                m0 = mb * CMP_MBLK
                hid = jax.nn.gelu(a_ref[m0:m0 + CMP_MBLK, :] + b_ref[m0 + 1:m0 + 1 + CMP_MBLK, :] + b1_ref[kv])
                out_ref[m0:m0 + CMP_MBLK, tile * LANES:(tile + 1) * LANES] = _dot(hid.astype(BF16), w2_ref[kv])


def _compress_prompt_kernel(x_ref, pe_ref, w1_ref, b1_ref, w2_ref, o_ref, a_ref, b_ref, *, n_chunk):
    _compress_rows(_load_interleaved(x_ref.at[0]), n_chunk, pe_ref, w1_ref, b1_ref, w2_ref, o_ref.at[0],
                   a_ref, b_ref)


def _compress_prompt(rows, pe, w1, b1, w2):
    b, t_len, _ = rows.shape
    n_chunk = t_len // CMP_STRIDE
    assert n_chunk % CMP_MBLK == 0
    full = lambda a: pl.BlockSpec(a.shape, lambda i: (0,) * a.ndim)
    return pl.pallas_call(
        functools.partial(_compress_prompt_kernel, n_chunk=n_chunk),
        grid=(b,),
        in_specs=[pl.BlockSpec((1, t_len * ROW_TILES, LANES), lambda i: (i, 0, 0)),
                  full(pe), full(w1), full(b1), full(w2)],
        out_specs=pl.BlockSpec((1, n_chunk, D_KV), lambda i: (i, 0, 0)),
        out_shape=jax.ShapeDtypeStruct((b, n_chunk, D_KV), F32),
        scratch_shapes=[pltpu.VMEM((n_chunk, PAIR_HID), F32), pltpu.VMEM((n_chunk + SUBLANES, PAIR_HID), F32)],
        compiler_params=_cparams(1),
        name="compress_prompt",
    )(rows.reshape(b, t_len * ROW_TILES, LANES), pe, w1, b1, w2)


def _masked_softmax_rows(s, mask):
    sm = jnp.where(mask, s, NEG)
    m = jnp.max(sm, axis=-1, keepdims=True)
    e = jnp.where(mask, jnp.exp(sm - m), 0.0)
    tot = jnp.sum(e, axis=-1, keepdims=True)
    return e * jnp.where(tot > 0.0, 1.0 / tot, 0.0)


WIN_TILES = WINDOW // Q_BLOCK


KEY_TILE = 512
KEY_SUB = KEY_TILE // Q_BLOCK


def _attend_tiles_t(qts, k_ref, vt_ref, t_lo, t_hi, i_blk, bias_ref, bias_index, mask_fn):
    cols = qts[0].shape[1]

    def body(t, carry):
        k0 = pl.multiple_of(t * KEY_TILE, KEY_TILE)
        bases = [bias_index(i_blk - (t * KEY_SUB + u)) for u in range(KEY_SUB)]
        scores = [_dot(k_ref[0, g, pl.ds(k0, KEY_TILE), :], qt) for g, qt in enumerate(qts)]
        new = []
        for g in range(len(qts)):
            m, l, acc = carry[g]
            vt = vt_ref[0, g * HEAD_DIM:(g + 1) * HEAD_DIM, pl.ds(k0, KEY_TILE)]
            heads = [g * HEADS_PER_GROUP + hp for hp in range(HEADS_PER_GROUP)]
            bias = jnp.concatenate(
                [jnp.concatenate([bias_ref[base + h] for h in heads], axis=1) for base in bases], axis=0)
            s = scores[g] + bias
            extra = mask_fn(t, g)
            if extra is not None:
                s = s + jnp.concatenate([extra] * HEADS_PER_GROUP, axis=1)
            m_new = jnp.maximum(m, jnp.max(s, axis=0, keepdims=True))
            alpha = jnp.exp(m - m_new)
            p = jnp.exp(s - m_new)
            new.append((m_new, alpha * l + jnp.sum(p, axis=0, keepdims=True), alpha * acc + _dot(vt, p.astype(BF16))))
        return tuple(new)

    init = tuple((jnp.full((1, cols), NEG, F32), jnp.zeros((1, cols), F32), jnp.zeros((HEAD_DIM, cols), F32))
                 for _ in qts)
    fin = lax.fori_loop(t_lo, t_hi, body, init)
    return [acc / l for (_, l, acc) in fin]


def _nsa_prompt_kernel(tab_ref, q_ref, gl_ref, ck_ref, cvt_ref, sk_ref, svt_ref, wk_ref, wvt_ref, o_ref,
                       bias_ref, cbias_ref, sel_ref, *, n_tiles):
    b = pl.program_id(0)
    i = pl.program_id(1)
    kk = lax.broadcasted_iota(I32, (Q_BLOCK, Q_BLOCK), 0)
    qq = lax.broadcasted_iota(I32, (Q_BLOCK, Q_BLOCK), 1)
    n_cmp = ck_ref.shape[2]
    n_sel_blocks = 2 * n_tiles
    edge_base = n_tiles * N_HEADS
    none_base = (n_tiles + 1) * N_HEADS

    @pl.when((b == 0) & (i == 0))
    def _():
        def build(delta, carry):
            dist = delta * Q_BLOCK + qq - kk
            biases = _bias_from_bucket(_rel_bucket(dist), tab_ref, range(N_HEADS))
            for h in range(N_HEADS):
                bias_ref[delta * N_HEADS + h] = jnp.where(dist >= 0, biases[h], NEG)
            return carry
        lax.fori_loop(0, n_tiles, build, 0)
        dist = WIN_TILES * Q_BLOCK + qq - kk
        biases = _bias_from_bucket(_rel_bucket(dist), tab_ref, range(N_HEADS))
        for h in range(N_HEADS):
            bias_ref[edge_base + h] = jnp.where(dist < WINDOW, biases[h], NEG)
            bias_ref[none_base + h] = jnp.full((Q_BLOCK, Q_BLOCK), NEG, F32)

        def build_c(t, carry):
            dist_c = t * Q_BLOCK + qq - (kk * CMP_STRIDE + (CMP_LEN - 1))
            ok = (dist_c >= 0) & (kk < n_cmp - 1)
            biases_c = _bias_from_bucket(_rel_bucket(dist_c), tab_ref, range(N_HEADS))
            for h in range(N_HEADS):
                cbias_ref[t * N_HEADS + h] = jnp.where(ok, biases_c[h], NEG)
            return carry
        lax.fori_loop(0, n_tiles, build_c, 0)

    qs = i * Q_BLOCK
    q_all = q_ref[0] * SCALE
    gates = jax.nn.sigmoid(gl_ref[0])

    blk = lax.broadcasted_iota(I32, (n_sel_blocks, Q_BLOCK), 0)
    col = lax.broadcasted_iota(I32, (n_sel_blocks, Q_BLOCK), 1)
    c_start = col * CMP_STRIDE
    s_start = blk * SEL_LEN
    cover_t = jnp.where((c_start < s_start + SEL_LEN) & (c_start + CMP_LEN > s_start) & (col < n_cmp - 1),
                        1.0, 0.0).astype(BF16)
    q_pos = qs + col
    cur = q_pos // SEL_LEN
    forced = (blk == 0) | (blk == cur) | (blk == cur - 1)
    valid = blk * SEL_LEN <= q_pos
    upper = kk < SEL_LEN

    qts, o_cs, splits = [], [], []
    for g in range(N_KV_GROUPS):
        heads = [g * HEADS_PER_GROUP + hp for hp in range(HEADS_PER_GROUP)]
        qt = jnp.concatenate([q_all[h * HEAD_DIM:(h + 1) * HEAD_DIM, :] for h in heads], axis=1)
        qts.append(qt)

        s_c = _dot(ck_ref[0, g], qt) + jnp.concatenate([cbias_ref[i * N_HEADS + h] for h in heads], axis=1)
        m_c = jnp.max(s_c, axis=0, keepdims=True)
        e_c = jnp.exp(s_c - m_c)
        tot = jnp.sum(e_c, axis=0, keepdims=True)
        p_c = e_c * jnp.where(m_c > 0.5 * NEG, 1.0 / tot, 0.0)
        o_cs.append(_dot(cvt_ref[0, g * HEAD_DIM:(g + 1) * HEAD_DIM, :], p_c.astype(BF16)))
        p_sum = p_c[:, 0:Q_BLOCK]
        for hp in range(1, HEADS_PER_GROUP):
            p_sum = p_sum + p_c[:, hp * Q_BLOCK:(hp + 1) * Q_BLOCK]
        splits.append(p_sum.astype(BF16))

    imp_all = _dot(cover_t, jnp.concatenate(splits, axis=1))
    for g in range(N_KV_GROUPS):
        imp = imp_all[:, g * Q_BLOCK:(g + 1) * Q_BLOCK]
        score = jnp.where(forced, FORCE, imp)
        score = jnp.where(valid, score, NEG)
        cnt = jnp.zeros(score.shape, F32)
        for s in range(n_sel_blocks):
            row = score[s:s + 1, :]
            beats = (row > score) | ((row == score) & (blk > s))
            cnt = cnt + jnp.where(beats, 1.0, 0.0)
        sel = (cnt < float(N_SEL)) & (score > 0.5 * NEG)
        sel_ref[g] = jnp.where(sel, 0.0, NEG)

    def mask_sel(t, g):
        rows = sel_ref[g, pl.ds(pl.multiple_of(2 * KEY_SUB * t, 2 * KEY_SUB), 2 * KEY_SUB), :]
        return jnp.concatenate([jnp.where(upper, rows[2 * u:2 * u + 1], rows[2 * u + 1:2 * u + 2])
                                for u in range(KEY_SUB)], axis=0)
    last = i // KEY_SUB + 1
    o_ss = _attend_tiles_t(qts, sk_ref, svt_ref, 0, last, i, bias_ref,
                           lambda d: jnp.where(d < 0, none_base, d * N_HEADS), mask_sel)

    def win_index(d):
        return jnp.where((d < 0) | (d > WIN_TILES), none_base, jnp.where(d == WIN_TILES, edge_base, d * N_HEADS))
    o_ws = _attend_tiles_t(qts, wk_ref, wvt_ref, jnp.maximum(i - WIN_TILES, 0) // KEY_SUB, last, i, bias_ref,
                           win_index, lambda t, g: None)

    for h in range(N_HEADS):
        g, hp = divmod(h, HEADS_PER_GROUP)
        c0 = hp * Q_BLOCK
        o_h = (gates[3 * h:3 * h + 1, :] * o_cs[g][:, c0:c0 + Q_BLOCK]
               + gates[3 * h + 1:3 * h + 2, :] * o_ss[g][:, c0:c0 + Q_BLOCK]
               + gates[3 * h + 2:3 * h + 3, :] * o_ws[g][:, c0:c0 + Q_BLOCK])
        o_ref[0, h * HEAD_DIM:(h + 1) * HEAD_DIM, :] = o_h.astype(o_ref.dtype)


def _nsa_prompt(tab, q_t, gl_t, comp_k, comp_vt, slc_k, slc_vt, win_k, win_vt):
    b, dq, t_len = q_t.shape
    n_tiles = t_len // Q_BLOCK
    n_cmp = comp_k.shape[2]
    assert n_cmp == Q_BLOCK and t_len % KEY_TILE == 0 and n_tiles > WIN_TILES
    dv = N_KV_GROUPS * HEAD_DIM
    return pl.pallas_call(
        functools.partial(_nsa_prompt_kernel, n_tiles=n_tiles),
        grid=(b, n_tiles),
        in_specs=[pl.BlockSpec(memory_space=pltpu.SMEM),
                  pl.BlockSpec((1, dq, Q_BLOCK), lambda i, j: (i, 0, j)),
                  pl.BlockSpec((1, LANES, Q_BLOCK), lambda i, j: (i, 0, j)),
                  pl.BlockSpec((1, N_KV_GROUPS, n_cmp, HEAD_DIM), lambda i, j: (i, 0, 0, 0)),
                  pl.BlockSpec((1, dv, n_cmp), lambda i, j: (i, 0, 0)),
                  pl.BlockSpec((1, N_KV_GROUPS, t_len, HEAD_DIM), lambda i, j: (i, 0, 0, 0)),
                  pl.BlockSpec((1, dv, t_len), lambda i, j: (i, 0, 0)),
                  pl.BlockSpec((1, N_KV_GROUPS, t_len, HEAD_DIM), lambda i, j: (i, 0, 0, 0)),
                  pl.BlockSpec((1, dv, t_len), lambda i, j: (i, 0, 0))],
        out_specs=pl.BlockSpec((1, dq, Q_BLOCK), lambda i, j: (i, 0, j)),
        out_shape=jax.ShapeDtypeStruct((b, dq, t_len), BF16),
        scratch_shapes=[pltpu.VMEM(((n_tiles + 2) * N_HEADS, Q_BLOCK, Q_BLOCK), F32),
                        pltpu.VMEM((n_tiles * N_HEADS, Q_BLOCK, Q_BLOCK), F32),
                        pltpu.VMEM((N_KV_GROUPS, 2 * n_tiles, Q_BLOCK), F32)],
        compiler_params=_cparams(2),
        name="nsa_prompt",
    )(tab, q_t, gl_t, comp_k, comp_vt, slc_k, slc_vt, win_k, win_vt)


def _page_copy(pt_ref, pool_ref, buf_ref, sem, b, j, page):
    start = pl.multiple_of(j * page, page)
    if len(buf_ref.shape) == 2:
        dst = buf_ref.at[pl.ds(start, page)]
    else:
        dst = buf_ref.at[:, :, pl.ds(start, page)]
    return pltpu.make_async_copy(pool_ref.at[pt_ref[b, j]], dst, sem)


def _gather_pages_start(pt_ref, pool_ref, buf_ref, sem, b, n_pages, page):
    def go(j, carry):
        _page_copy(pt_ref, pool_ref, buf_ref, sem, b, j, page).start()
        return carry
    lax.fori_loop(0, n_pages, go, 0)


def _gather_pages_wait(pt_ref, pool_ref, buf_ref, sem, b, n_pages, page):
    def go(j, carry):
        _page_copy(pt_ref, pool_ref, buf_ref, sem, b, j, page).wait()
        return carry
    lax.fori_loop(0, n_pages, go, 0)


def _compress_pages_kernel(pt_ref, pool_ref, pe_ref, w1_ref, b1_ref, w2_ref, o_ref, raw_ref, x_ref, a_ref, b_ref,
                           sem, *, n_seq, n_pages, page, n_chunk):
    b = pl.program_id(0)

    @pl.when(b == 0)
    def _():
        _gather_pages_start(pt_ref, pool_ref, raw_ref, sem, b, n_pages, page)

    _gather_pages_wait(pt_ref, pool_ref, raw_ref, sem, b, n_pages, page)

    def relayout(j, carry):
        r0 = pl.multiple_of(j * page, page)
        for tile in range(ROW_TILES):
            kv, pair = divmod(tile, N_KV_GROUPS // 2)
            x_ref[tile, pl.ds(r0, page), :] = raw_ref[kv, pair * LANES:(pair + 1) * LANES, pl.ds(r0, page)].T
        return carry
    lax.fori_loop(0, n_pages, relayout, 0)

    @pl.when(b + 1 < n_seq)
    def _():
        _gather_pages_start(pt_ref, pool_ref, raw_ref, sem, b + 1, n_pages, page)

    _compress_rows(_load_tiled(x_ref), n_chunk, pe_ref, w1_ref, b1_ref, w2_ref, o_ref.at[0], a_ref, b_ref)


def _compress_pages(page_table, pool_t, pe, w1, b1, w2):
    b, n_pages = page_table.shape
    page = pool_t.shape[3]
    past = n_pages * page
    n_chunk = past // CMP_STRIDE
    assert n_chunk % CMP_MBLK == 0 and page == LANES
    full = lambda a: pl.BlockSpec(a.shape, lambda i: (0,) * a.ndim)
    return pl.pallas_call(
        functools.partial(_compress_pages_kernel, n_seq=b, n_pages=n_pages, page=page, n_chunk=n_chunk),
        grid=(b,),
        in_specs=[pl.BlockSpec(memory_space=pltpu.SMEM), pl.BlockSpec(memory_space=pl.ANY),
                  full(pe), full(w1), full(b1), full(w2)],
        out_specs=pl.BlockSpec((1, n_chunk, D_KV), lambda i: (i, 0, 0)),
        out_shape=jax.ShapeDtypeStruct((b, n_chunk, D_KV), F32),
        scratch_shapes=[pltpu.VMEM((2, D_K, past), F32),
                        pltpu.VMEM((ROW_TILES, past, LANES), F32),
                        pltpu.VMEM((n_chunk, PAIR_HID), F32), pltpu.VMEM((n_chunk + SUBLANES, PAIR_HID), F32),
                        pltpu.SemaphoreType.DMA(())],
        compiler_params=_cparams(1),
        name="compress_pages",
    )(page_table, pool_t, pe, w1, b1, w2)


DEC_TILE = 2048
DEC_ROWS = 16
D_K = D_KV // 2


def _row_bias(dist_row, tab_ref, bias_ref, col0):
    length = dist_row.shape[1]
    biases = _bias_from_bucket(_rel_bucket(dist_row), tab_ref, range(N_HEADS))
    rows = biases + [jnp.zeros((DEC_ROWS - N_HEADS, length), F32)]
    bias_ref[:, pl.ds(col0, length)] = jnp.concatenate(rows, axis=0)


def _head_block(o_all, h):
    g = h // HEADS_PER_GROUP
    return o_all[h:h + 1, g * HEAD_DIM:(g + 1) * HEAD_DIM]


def _new_token_scores(q_blk, new_row, bias0):
    k_new = new_row[:, :D_K].astype(BF16).astype(F32)
    v_new = new_row[:, D_K:].astype(BF16).astype(F32)
    return jnp.sum(q_blk.astype(F32) * k_new, axis=-1, keepdims=True) + bias0, v_new


def _nsa_step_kernel(tab_ref, pt_ref, q_ref, gl_ref, comp_ref, pool_ref, slc_new_ref, win_ref, win_new_ref, o_ref,
                     kv_buf, bias_c_ref, bias_s_ref, bias_w_ref, sems,
                     *, n_seq, n_pages, page, past, n_cmp, w_buf):
    b = pl.program_id(0)
    slot = b % 2
    n_tiles = past // DEC_TILE
    n_blocks = past // SEL_LEN + 1
    blk_lanes = -(-n_blocks // LANES) * LANES

    @pl.when(b == 0)
    def _():
        _gather_pages_start(pt_ref, pool_ref, kv_buf.at[0], sems.at[0], b, n_pages, page)

    @pl.when(b + 1 < n_seq)
    def _():
        _gather_pages_start(pt_ref, pool_ref, kv_buf.at[1 - slot], sems.at[1 - slot], b + 1, n_pages, page)

    lane_t = lax.broadcasted_iota(I32, (1, DEC_TILE), 1)

    @pl.when(b == 0)
    def _():
        lane_c = lax.broadcasted_iota(I32, (1, n_cmp), 1)
        _row_bias(past - (lane_c * CMP_STRIDE + (CMP_LEN - 1)), tab_ref, bias_c_ref, 0)

        def tile_bias(t, carry):
            k0 = pl.multiple_of(t * DEC_TILE, DEC_TILE)
            _row_bias(past - (k0 + lane_t), tab_ref, bias_s_ref, k0)
            return carry
        lax.fori_loop(0, n_tiles, tile_bias, 0)
        lane_w = lax.broadcasted_iota(I32, (1, w_buf), 1)
        _row_bias(w_buf - lane_w, tab_ref, bias_w_ref, 0)

    bias0 = jnp.concatenate([jnp.full((1, 1), tab_ref[0, h], F32) for h in range(N_HEADS)]
                            + [jnp.zeros((DEC_ROWS - N_HEADS, 1), F32)], axis=0)

    q_row = q_ref[0] * SCALE
    gates = jax.nn.sigmoid(gl_ref[0])
    rows = []
    for h in range(N_HEADS):
        g = h // HEADS_PER_GROUP
        parts = []
        if g > 0:
            parts.append(jnp.zeros((1, g * HEAD_DIM), BF16))
        parts.append(q_row[:, h * HEAD_DIM:(h + 1) * HEAD_DIM])
        if g < N_KV_GROUPS - 1:
            parts.append(jnp.zeros((1, (N_KV_GROUPS - 1 - g) * HEAD_DIM), BF16))
        rows.append(jnp.concatenate(parts, axis=1))
    q_blk = jnp.concatenate(rows + [jnp.zeros((DEC_ROWS - N_HEADS, D_K), BF16)], axis=0)

    comp = comp_ref[0].astype(BF16)
    mask_c = lax.broadcasted_iota(I32, (DEC_ROWS, n_cmp), 1) < n_cmp - 1
    p_c = _masked_softmax_rows(_dot_nt(q_blk, comp[:, :D_K]) + bias_c_ref[...], mask_c)
    o_c = _dot(p_c.astype(BF16), comp[:, D_K:])
    p_sums = [jnp.sum(p_c[g * HEADS_PER_GROUP:(g + 1) * HEADS_PER_GROUP], axis=0, keepdims=True)
              for g in range(N_KV_GROUPS)]
    p_sum = jnp.concatenate(p_sums + [jnp.zeros((DEC_ROWS - N_KV_GROUPS, n_cmp), F32)], axis=0)

    c_start = lax.broadcasted_iota(I32, (n_cmp, blk_lanes), 0) * CMP_STRIDE
    s_start = lax.broadcasted_iota(I32, (n_cmp, blk_lanes), 1) * SEL_LEN
    c_idx = lax.broadcasted_iota(I32, (n_cmp, blk_lanes), 0)
    cover = jnp.where((c_start < s_start + SEL_LEN) & (c_start + CMP_LEN > s_start) & (c_idx < n_cmp - 1),
                      1.0, 0.0).astype(BF16)
    imp = _dot(p_sum.astype(BF16), cover)
    blk = lax.broadcasted_iota(I32, (DEC_ROWS, blk_lanes), 1)
    cur = past // SEL_LEN
    forced = (blk == 0) | (blk == cur) | (blk == cur - 1)
    score = jnp.where(forced, FORCE, imp)
    score = jnp.where(blk < n_blocks, score, NEG)
    score_t = score.T
    j_idx = lax.broadcasted_iota(I32, (blk_lanes, blk_lanes), 0)
    i_idx = lax.broadcasted_iota(I32, (blk_lanes, blk_lanes), 1)
    sel_groups = []
    for g in range(N_KV_GROUPS):
        col = score_t[:, g:g + 1]
        row = score[g:g + 1, :]
        beats = (col > row) | ((col == row) & (j_idx < i_idx))
        rank = jnp.sum(jnp.where(beats, 1.0, 0.0), axis=0, keepdims=True)
        sel_groups.append(jnp.where((rank < float(N_SEL)) & (row > 0.5 * NEG), 1.0, 0.0))
    sel_rows = [sel_groups[h // HEADS_PER_GROUP] for h in range(N_HEADS)]
    sel_bf = jnp.concatenate(sel_rows + [jnp.zeros((DEC_ROWS - N_HEADS, blk_lanes), F32)], axis=0).astype(BF16)

    _gather_pages_wait(pt_ref, pool_ref, kv_buf.at[slot], sems.at[slot], b, n_pages, page)
    e_row = lax.broadcasted_iota(I32, (blk_lanes, DEC_TILE), 0)
    e_col = lax.broadcasted_iota(I32, (blk_lanes, DEC_TILE), 1) // SEL_LEN

    def sel_tile(t, carry):
        m, l, acc = carry
        k0 = pl.multiple_of(t * DEC_TILE, DEC_TILE)
        expand = jnp.where(e_row == e_col + t * (DEC_TILE // SEL_LEN), 1.0, 0.0).astype(BF16)
        key_sel = _dot(sel_bf, expand)
        kt = kv_buf[slot, 0, :, pl.ds(k0, DEC_TILE)].astype(BF16)
        vt = kv_buf[slot, 1, :, pl.ds(k0, DEC_TILE)].astype(BF16)
        s = _dot(q_blk, kt) + bias_s_ref[:, pl.ds(k0, DEC_TILE)] + jnp.where(key_sel > 0.5, 0.0, NEG)
        m_new = jnp.maximum(m, jnp.max(s, axis=-1, keepdims=True))
        alpha = jnp.exp(m - m_new)
        p = jnp.exp(s - m_new)
        return m_new, alpha * l + jnp.sum(p, axis=-1, keepdims=True), alpha * acc + _dot_nt(p.astype(BF16), vt)

    init = (jnp.full((DEC_ROWS, 1), NEG, F32), jnp.zeros((DEC_ROWS, 1), F32), jnp.zeros((DEC_ROWS, D_K), F32))
    m_s, l_s, acc_s = lax.fori_loop(0, n_tiles, sel_tile, init)
    s_new, v_new = _new_token_scores(q_blk, slc_new_ref[0], bias0)
    cur_sel = sel_bf[:, cur:cur + 1].astype(F32)
    s_new = s_new + jnp.where(cur_sel > 0.5, 0.0, NEG)
    m_fin = jnp.maximum(m_s, s_new)
    alpha = jnp.exp(m_s - m_fin)
    p_new = jnp.exp(s_new - m_fin)
    o_s = (alpha * acc_s + p_new.astype(BF16).astype(F32) * v_new) / (alpha * l_s + p_new)

    kw = win_ref[0, 0].astype(BF16)
    vw = win_ref[0, 1].astype(BF16)
    idx_w = lax.broadcasted_iota(I32, (DEC_ROWS, w_buf), 1)
    s_w = jnp.where(w_buf - idx_w < WINDOW, _dot(q_blk, kw) + bias_w_ref[...], NEG)
    s_wn, v_wn = _new_token_scores(q_blk, win_new_ref[0], bias0)
    m_w = jnp.maximum(jnp.max(s_w, axis=-1, keepdims=True), s_wn)
    e_w = jnp.exp(s_w - m_w)
    e_wn = jnp.exp(s_wn - m_w)
    o_w = ((_dot_nt(e_w.astype(BF16), vw) + e_wn.astype(BF16).astype(F32) * v_wn)
           / (jnp.sum(e_w, axis=-1, keepdims=True) + e_wn))

    outs = []
    for h in range(N_HEADS):
        outs.append(gates[:, 3 * h:3 * h + 1] * _head_block(o_c, h)
                    + gates[:, 3 * h + 1:3 * h + 2] * _head_block(o_s, h)
                    + gates[:, 3 * h + 2:3 * h + 3] * _head_block(o_w, h))
    o_ref[0] = jnp.concatenate(outs, axis=1).astype(o_ref.dtype)


def _feature_major(cache):
    nd = cache.ndim
    perm = tuple(range(nd - 4)) + (nd - 3, nd - 2, nd - 1, nd - 4)
    t = cache.transpose(perm)
    return t.reshape(t.shape[:nd - 3] + (t.shape[nd - 3] * t.shape[nd - 2], t.shape[nd - 1]))


def _nsa_step(tab, page_table, q, gl, comp, pool_t, slc_new, win_t, win_new):
    b, n_pages = page_table.shape
    page = pool_t.shape[3]
    past = n_pages * page
    n_cmp = comp.shape[1]
    w_buf = win_t.shape[3]
    dq = q.shape[2]
    assert past % DEC_TILE == 0 and past % SEL_LEN == 0 and w_buf == WINDOW and n_cmp == past // CMP_STRIDE
    row3 = lambda w: pl.BlockSpec((1, 1, w), lambda i: (i, 0, 0))
    return pl.pallas_call(
        functools.partial(_nsa_step_kernel, n_seq=b, n_pages=n_pages, page=page, past=past, n_cmp=n_cmp,
                          w_buf=w_buf),
        grid=(b,),
        in_specs=[pl.BlockSpec(memory_space=pltpu.SMEM), pl.BlockSpec(memory_space=pltpu.SMEM),
                  row3(dq), row3(LANES),
                  pl.BlockSpec((1, n_cmp, D_KV), lambda i: (i, 0, 0)),
                  pl.BlockSpec(memory_space=pl.ANY),
                  row3(D_KV),
                  pl.BlockSpec((1, 2, D_K, w_buf), lambda i: (i, 0, 0, 0)),
                  row3(D_KV)],
        out_specs=row3(dq),
        out_shape=jax.ShapeDtypeStruct((b, 1, dq), BF16),
        scratch_shapes=[pltpu.VMEM((2, 2, D_K, past), F32),
                        pltpu.VMEM((DEC_ROWS, n_cmp), F32),
                        pltpu.VMEM((DEC_ROWS, past), F32),
                        pltpu.VMEM((DEC_ROWS, w_buf), F32),
                        pltpu.SemaphoreType.DMA((2,))],
        compiler_params=_cparams(1),
        name="nsa_step",
    )(tab, page_table, q, gl, comp, pool_t, slc_new, win_t, win_new)


def _block_diag_pairs(w):
    n, blk, _ = w.shape
    w = w.reshape(n // 2, 2, blk, blk)
    z = jnp.zeros((n // 2, blk, blk), w.dtype)
    top = jnp.concatenate([w[:, 0], z], axis=2)
    bot = jnp.concatenate([z, w[:, 1]], axis=2)
    return jnp.concatenate([top, bot], axis=1).astype(BF16)


def _prep_weights(w_in_a, conv_w, conv_b, w_rg, b_rg, w_ig, b_ig, lru_lambda, w_in_b, w_kv_shared, cmp_pe, cmp_w1,
                  cmp_b1, cmp_w2, w_mem_kv, w_out, ln1_g, ln1_b, ln2_g, ln2_b, w_router_g, b_router_g,
                  w_router_e, b_router_e, w_exp_gate, w_exp_up, w_exp_down):
    depth, d_model, _ = w_out.shape
    n_a = w_in_a.shape[0]
    nq = N_HEADS * HEAD_DIM
    n_gl = 3 * N_HEADS
    hid = EXPERTS_PER_GROUP * D_EXPERT
    p = {}
    p["w_in_a"] = w_in_a.astype(BF16)
    p["w_in_b"] = jnp.concatenate(
        [w_in_b[:, :, :nq], w_in_b[:, :, nq + n_gl:], w_in_b[:, :, nq:nq + n_gl],
         jnp.zeros((w_in_b.shape[0], d_model, LANES - n_gl), w_in_b.dtype)], axis=2).astype(BF16)
    p["w_kv"] = w_kv_shared.astype(BF16)
    p["w_mem"] = w_mem_kv.transpose(1, 0, 2).reshape(d_model, depth * 2 * D_MEM).astype(BF16)
    p["wo_a"] = w_out[:, :D_RNN].astype(BF16)
    p["wo_m"] = w_out[:, D_RNN:].astype(BF16)
    p["conv_w"] = conv_w
    p["conv_b"] = conv_b[:, None]
    p["w_rg"] = jnp.stack([_block_diag_pairs(w_rg[l]) for l in range(n_a)])
    p["w_ig"] = jnp.stack([_block_diag_pairs(w_ig[l]) for l in range(n_a)])
    p["b_rg"] = b_rg[:, None]
    p["b_ig"] = b_ig[:, None]
    p["lam"] = lru_lambda[:, None]
    p["pe"], p["cmp_w1"], p["cmp_b1"], p["cmp_w2"] = _compress_weights(cmp_pe, cmp_w1, cmp_b1, cmp_w2)
    p["ln1_g"], p["ln1_b"], p["ln2_g"], p["ln2_b"] = ln1_g[:, None], ln1_b[:, None], ln2_g[:, None], ln2_b[:, None]
    n_r = N_GROUPS + N_EXPERTS
    wr = jnp.concatenate([w_router_g, w_router_e, jnp.zeros((depth, d_model, LANES - n_r), F32)], axis=2)
    p["wr"] = wr.astype(BF16)
    p["rb"] = jnp.concatenate([b_router_g, b_router_e, jnp.zeros((depth, LANES - n_r), F32)], axis=1)[:, None]

    def by_group(w):
        w = w.astype(BF16).reshape(depth, N_GROUPS, EXPERTS_PER_GROUP, d_model, D_EXPERT)
        return w.transpose(0, 1, 3, 2, 4).reshape(depth, N_GROUPS, d_model, hid)
    p["w_gu"] = jnp.concatenate([by_group(w_exp_gate), by_group(w_exp_up)], axis=3)
    p["w_dn"] = w_exp_down.astype(BF16).reshape(depth, N_GROUPS, hid, d_model)
    return p


def _kv_layouts(rows):
    b, t_len, _ = rows.shape
    k = rows[:, :, :D_KV // 2].reshape(b, t_len, N_KV_GROUPS, HEAD_DIM).transpose(0, 2, 1, 3)
    return k, rows[:, :, D_KV // 2:].transpose(0, 2, 1)


def _trunk_tail(x, mix_a, mix_m, p, l, alpha, tm, spare=None):
    x, grp = _out_ln(x, mix_a, mix_m, p["wo_a"][l], p["wo_m"][l], p["ln1_g"][l], p["ln1_b"][l],
                     p["wr"][l], p["rb"][l], alpha, tm)
    if x.shape[0] < 8 * MOE_TILE:
        return _moe_ln(x, p["wr"][l], p["rb"][l], p["w_gu"][l], p["w_dn"][l],
                       p["ln2_g"][l], p["ln2_b"][l], alpha, tm), None
    dest, tile_grp = _moe_dispatch(grp[:, 0])
    xs = _moe_place(x, dest, tile_grp.shape[0] * MOE_TILE, spare)
    ys = _moe_grouped(xs, tile_grp, p["wr"][l], p["rb"][l], p["w_gu"][l], p["w_dn"][l])
    return _moe_return_ln(x, ys, dest, p["ln2_g"][l], p["ln2_b"][l], alpha), xs


def kernel(x_prompt, x_sample, mem_prompt, cache_cmp_kv, cache_slc_kv, cache_win_kv, cache_mem_kv, state_lru_h, state_conv, page_table, w_in_a, conv_w, conv_b, w_rg, b_rg, w_ig, b_ig, lru_lambda, w_in_b, w_kv_shared, cmp_pe, cmp_w1, cmp_b1, cmp_w2, rel_bias, w_mem_kv, w_out, ln1_g, ln1_b, ln2_g, ln2_b, w_router_g, b_router_g, w_router_e, b_router_e, w_exp_gate, w_exp_up, w_exp_down):
    p = _prep_weights(w_in_a, conv_w, conv_b, w_rg, b_rg, w_ig, b_ig, lru_lambda, w_in_b, w_kv_shared, cmp_pe,
                      cmp_w1, cmp_b1, cmp_w2, w_mem_kv, w_out, ln1_g, ln1_b, ln2_g, ln2_b, w_router_g,
                      b_router_g, w_router_e, b_router_e, w_exp_gate, w_exp_up, w_exp_down)
    depth, d_model, _ = w_out.shape
    n_a = w_in_a.shape[0]
    alpha = (2 * depth) ** 0.25
    kv_shape = (2, N_KV_GROUPS, HEAD_DIM)
    a_outs = [(0, D_RNN, F32), (D_RNN, D_RNN, F32), (2 * D_RNN, D_MEM, BF16)]
    b_outs = [(0, D_RNN, BF16), (D_RNN, D_MEM, BF16), (D_RNN + D_MEM, LANES, F32)]
    kv_outs = [(0, D_KV, F32), (D_KV, D_KV, F32), (2 * D_KV, D_KV, F32)]

    bp, t_len, _ = x_prompt.shape
    n_mem = mem_prompt.shape[1]
    tm = 512
    mem_rows = _proj(mem_prompt.reshape(bp * n_mem, d_model), p["w_mem"],
                     [(l * 2 * D_MEM, 2 * D_MEM, F32) for l in range(depth)], tm, "mem_kv_proj")
    p_mem = jnp.stack(mem_rows).reshape(depth, bp, n_mem, 2 * D_MEM)
    x = x_prompt.reshape(bp * t_len, d_model)
    lru_p, conv_p = [], []
    spare = None
    for l in range(depth):
        if l < n_a:
            gate, xr, mq = _proj(x, p["w_in_a"][l], a_outs, tm, "in_proj_a")
            mix_a, h_t, buf = _rglru_prompt(gate.reshape(bp, t_len, D_RNN), xr.reshape(bp, t_len, D_RNN),
                                            p["conv_w"][l], p["conv_b"][l], p["w_rg"][l], p["w_ig"][l],
                                            p["b_rg"][l], p["b_ig"][l], p["lam"][l])
            lru_p.append(h_t[:, 0])
            conv_p.append(buf)
        else:
            if l == n_a:
                cmp_rows, slc_rows, win_rows, slc_bf, win_bf = _proj(
                    x, p["w_kv"], kv_outs + [(D_KV, D_KV, BF16), (2 * D_KV, D_KV, BF16)], tm, "kv_proj")
                comp = _compress_prompt(cmp_rows.reshape(bp, t_len, D_KV), p["pe"], p["cmp_w1"], p["cmp_b1"],
                                        p["cmp_w2"])
                comp_k, comp_vt = _kv_layouts(comp.astype(BF16))
                slc_k, slc_vt = _kv_layouts(slc_bf.reshape(bp, t_len, D_KV))
                win_k, win_vt = _kv_layouts(win_bf.reshape(bp, t_len, D_KV))
            q, mq, gl = _proj(x, p["w_in_b"][l - n_a], b_outs, tm, "in_proj_b")
            mix_t = _nsa_prompt(rel_bias, q.reshape(bp, t_len, D_RNN).transpose(0, 2, 1),
                                gl.reshape(bp, t_len, LANES).transpose(0, 2, 1),
                                comp_k, comp_vt, slc_k, slc_vt, win_k, win_vt)
            mix_a = mix_t.transpose(0, 2, 1)
        mix_m = _mem_attn(mq.reshape(bp, t_len, D_MEM), p_mem, l)
        x, spare = _trunk_tail(x, mix_a.reshape(bp * t_len, D_RNN), mix_m.reshape(bp * t_len, D_MEM), p, l,
                               alpha, tm, spare)
    y_prompt = x.reshape(bp, t_len, d_model)
    p_cmp_kv = cmp_rows.reshape((bp, t_len) + kv_shape)
    p_slc_kv = slc_rows.reshape((bp, t_len) + kv_shape)
    w_keep = min(WINDOW, t_len)
    p_win_kv = win_rows.reshape((bp, t_len) + kv_shape)[:, t_len - w_keep:]
    p_mem_kv = p_mem.reshape(depth, bp, n_mem, 2, N_MEM_HEADS, HEAD_DIM)
    p_lru_h = jnp.stack(lru_p)
    p_conv = jnp.stack(conv_p)

    bd, s_len, _ = x_sample.shape
    assert s_len == 1
    n_pool, page = cache_cmp_kv.shape[:2]
    w_buf = cache_win_kv.shape[1]
    mem_t = _feature_major(cache_mem_kv)
    slc_pool_t = _feature_major(cache_slc_kv)
    win_t = _feature_major(cache_win_kv)
    x = x_sample.reshape(bd, d_model)
    lru_s, conv_s = [], []
    for l in range(depth):
        if l < n_a:
            gate, xr, mq = _proj(x, p["w_in_a"][l], a_outs, bd, "in_proj_a_step")
            mix_a, h_t, buf = _rglru_step(gate, xr, state_conv[l].transpose(1, 0, 2), state_lru_h[l],
                                          p["conv_w"][l], p["conv_b"][l], p["w_rg"][l], p["w_ig"][l],
                                          p["b_rg"][l], p["b_ig"][l], p["lam"][l])
            lru_s.append(h_t)
            conv_s.append(buf.transpose(1, 0, 2))
        else:
            if l == n_a:
                s_cmp, s_slc, s_win = _proj(x, p["w_kv"], kv_outs, bd, "kv_proj_step")
                comp_s = _compress_pages(page_table, _feature_major(cache_cmp_kv), p["pe"],
                                         p["cmp_w1"], p["cmp_b1"], p["cmp_w2"])
            q, mq, gl = _proj(x, p["w_in_b"][l - n_a], b_outs, bd, "in_proj_b_step")
            mix_a = _nsa_step(rel_bias, page_table, q.reshape(bd, 1, D_RNN), gl.reshape(bd, 1, LANES), comp_s,
                              slc_pool_t, s_slc.reshape(bd, 1, D_KV), win_t, s_win.reshape(bd, 1, D_KV))
            mix_a = mix_a.reshape(bd, D_RNN)
        mix_m = _mem_attn_step(mq.reshape(bd, 1, D_MEM), mem_t, l).reshape(bd, D_MEM)
        x, _ = _trunk_tail(x, mix_a, mix_m, p, l, alpha, bd)
    y_sample = x.reshape(bd, 1, d_model)
    s_cmp_kv = s_cmp.reshape((bd, 1) + kv_shape)
    s_slc_kv = s_slc.reshape((bd, 1) + kv_shape)
    s_win_kv = jnp.concatenate([cache_win_kv, s_win.reshape((bd, 1) + kv_shape)], axis=1)[:, 1:]
    s_lru_h = jnp.stack(lru_s)
    s_conv = jnp.stack(conv_s)
    return (y_prompt, y_sample, p_cmp_kv, p_slc_kv, p_win_kv, p_mem_kv, p_lru_h, p_conv,
            s_cmp_kv, s_slc_kv, s_win_kv, s_lru_h, s_conv)
```
